```python
import math
import jax
import jax.numpy as jnp
from jax import lax
import numpy as np

D_MODEL = 2048
BATCH = 8
SEQ = 4096
DEPTH = 2

CHUNK = 64
N_LEFT_CHUNKS = 8
BAND = (N_LEFT_CHUNKS + 1) * CHUNK
HEAD_DIM = 64
D_ATT = D_MODEL // 2
N_HEADS_ATT = D_ATT // HEAD_DIM
REL_CLIP = 128
N_REL = (CHUNK - 1) + REL_CLIP + 1
D_RWKV = D_MODEL // 2
N_HEADS_RWKV = D_RWKV // HEAD_DIM
DECAY_LORA = 64
AAA_LORA = 64
GATE_LORA = 128
N_B_IN = 3 * D_RWKV + DECAY_LORA + AAA_LORA + GATE_LORA
N_IN_AB = 3 * D_ATT + N_B_IN
D_SSM = D_MODEL // 2
SSM_GROUP = 16
N_SSM_GROUPS = D_SSM // SSM_GROUP
SSM_STATE = 64
D_FF = 5632
D_PLE = 256
RMS_EPS = 1e-6
GN_EPS = 64e-5

kernel_name = 'hybrid_chunk_causal_encoder'


def rms_norm(x, g):
    xf = x.astype(jnp.float32)
    y = xf * lax.rsqrt(jnp.mean(xf * xf, axis=-1, keepdims=True) + RMS_EPS)
    return (y * g.astype(jnp.float32)).astype(x.dtype)


def swiglu_ffn(x, w_gate, w_up, w_down):
    return (jax.nn.silu(x @ w_gate) * (x @ w_up)) @ w_down


def rel_bias_index():
    i = np.arange(CHUNK)[:, None]
    j = np.arange(BAND)[None, :]
    dist = i + N_LEFT_CHUNKS * CHUNK - j
    return np.clip(dist, -(CHUNK - 1), REL_CLIP) + (CHUNK - 1)


def chunked_band_attention(q, k, v, q_gain, k_gain, rel_bias):
    bsz, t, h, dh = q.shape
    n_chunks = t // CHUNK
    pad = N_LEFT_CHUNKS * CHUNK
    q = rms_norm(q, q_gain) * (dh ** -0.5)
    k = rms_norm(k, k_gain)
    k_pad = jnp.pad(k, ((0, 0), (pad, 0), (0, 0), (0, 0)))
    v_pad = jnp.pad(v, ((0, 0), (pad, 0), (0, 0), (0, 0)))
    bias = rel_bias.astype(jnp.float32)[:, rel_bias_index()]
    q_chunks = jnp.swapaxes(q.reshape(bsz, n_chunks, CHUNK, h, dh), 0, 1)
    band_pos = jnp.arange(BAND)

    def one_chunk(args):
        c, q_c = args
        start = c * CHUNK
        k_b = lax.dynamic_slice_in_dim(k_pad, start, BAND, axis=1)
        v_b = lax.dynamic_slice_in_dim(v_pad, start, BAND, axis=1)
        s = jnp.einsum('bqhd,bkhd->bhqk', q_c, k_b).astype(jnp.float32) + bias
        valid = (start + band_pos) >= pad
        s = jnp.where(valid, s, -jnp.inf)
        prob = jax.nn.softmax(s, axis=-1).astype(v.dtype)
        return jnp.einsum('bhqk,bkhd->bqhd', prob, v_b)

    out = lax.map(one_chunk, (jnp.arange(n_chunks), q_chunks))
    return jnp.swapaxes(out, 0, 1).reshape(bsz, t, h, dh)


def token_shift(z, mu):
    prev = jnp.pad(z[:, :-1], ((0, 0), (1, 0), (0, 0)))
    return z + (prev - z) * mu


def rwkv7_time_mix(z, mu, w0, w_up, a0, a_up, g_up, k_k, k_a, r_k, lnx_w, lnx_b):
    f32 = jnp.float32
    bsz, t, _ = z.shape
    z = token_shift(z.astype(f32), mu.astype(f32))
    o1, o2, o3 = D_RWKV, 2 * D_RWKV, 3 * D_RWKV
    o4 = o3 + DECAY_LORA
    o5 = o4 + AAA_LORA
    r, k, v = z[..., :o1], z[..., o1:o2], z[..., o2:o3]
    xw, xa, xg = z[..., o3:o4], z[..., o4:o5], z[..., o5:]
    w_log = -jax.nn.softplus(-(w0.astype(f32) + jnp.tanh(xw) @ w_up.astype(f32))) - 0.5
    decay = jnp.exp(-jnp.exp(w_log))
    a = jax.nn.sigmoid(a0.astype(f32) + xa @ a_up.astype(f32))
    g = jax.nn.sigmoid(xg) @ g_up.astype(f32)

    def heads(u):
        return u.reshape(bsz, t, N_HEADS_RWKV, HEAD_DIM)

    kk = heads(k * k_k.astype(f32))
    kk = kk / jnp.maximum(jnp.sqrt(jnp.sum(kk * kk, axis=-1, keepdims=True)), 1e-12)
    k = k * (1.0 + (a - 1.0) * k_a.astype(f32))
    r_h, k_h, v_h, w_h, a_h = heads(r), heads(k), heads(v), heads(decay), heads(a)

    def step(state, inp):
        r_t, w_t, k_t, v_t, ia_t, ib_t = inp
        sa = jnp.einsum('bhvk,bhk->bhv', state, ia_t)
        state = (state * w_t[:, :, None, :] + sa[..., None] * ib_t[:, :, None, :]
                 + v_t[..., None] * k_t[:, :, None, :])
        return state, jnp.einsum('bhvk,bhk->bhv', state, r_t)

    def tm(u):
        return jnp.swapaxes(u, 0, 1)

    s0 = jnp.zeros((bsz, N_HEADS_RWKV, HEAD_DIM, HEAD_DIM), f32)
    _, y = lax.scan(step, s0, (tm(r_h), tm(w_h), tm(k_h), tm(v_h), tm(-kk), tm(kk * a_h)))
    y = tm(y)
    mean = jnp.mean(y, axis=-1, keepdims=True)
    var = jnp.mean(jnp.square(y - mean), axis=-1, keepdims=True)
    y = ((y - mean) * lax.rsqrt(var + GN_EPS)).reshape(bsz, t, D_RWKV)
    y = y * lnx_w.astype(f32) + lnx_b.astype(f32)
    bonus = jnp.sum(r_h * k_h * r_k.astype(f32), axis=-1, keepdims=True) * v_h
    return (y + bonus.reshape(bsz, t, D_RWKV)) * g


def attn_rwkv_mixer(h, w_in, q_gain, k_gain, rel_bias, mu, w0, w_up, a0, a_up, g_up,
                    k_k, k_a, r_k, lnx_w, lnx_b, w_out):
    bsz, t, _ = h.shape
    proj = h @ w_in

    def heads(u):
        return u.reshape(bsz, t, N_HEADS_ATT, HEAD_DIM)

    q = heads(proj[..., :D_ATT])
    k = heads(proj[..., D_ATT:2 * D_ATT])
    v = heads(proj[..., 2 * D_ATT:3 * D_ATT])
    att = chunked_band_attention(q, k, v, q_gain, k_gain, rel_bias).reshape(bsz, t, D_ATT)
    rw = rwkv7_time_mix(proj[..., 3 * D_ATT:], mu, w0, w_up, a0, a_up, g_up,
                        k_k, k_a, r_k, lnx_w, lnx_b).astype(att.dtype)
    return jnp.concatenate([att, rw], axis=-1) @ w_out


def s5_ssm(u, lam_re, lam_im, log_dt, b_re, b_im, c_re, c_im, d_skip):
    f32 = jnp.float32
    bsz, t, _ = u.shape
    G, P, GS = N_SSM_GROUPS, SSM_STATE, SSM_GROUP
    uf = u.astype(f32).reshape(bsz, t, G, GS)
    lr, li = lam_re.astype(f32), lam_im.astype(f32)
    dt = jnp.exp(log_dt.astype(f32))[:, None]
    mag = jnp.exp(lr * dt)
    ab_re, ab_im = mag * jnp.cos(li * dt), mag * jnp.sin(li * dt)
    denom = lr * lr + li * li
    z_re = ((ab_re - 1.0) * lr + ab_im * li) / denom
    z_im = (ab_im * lr - (ab_re - 1.0) * li) / denom
    br, bi = b_re.astype(f32), b_im.astype(f32)
    bb_re = z_re[..., None] * br - z_im[..., None] * bi
    bb_im = z_re[..., None] * bi + z_im[..., None] * br
    bu_re = jnp.einsum('gpc,btgc->btgp', bb_re, uf)
    bu_im = jnp.einsum('gpc,btgc->btgp', bb_im, uf)
    a_re = jnp.broadcast_to(ab_re[None, None], (1, t, G, P))
    a_im = jnp.broadcast_to(ab_im[None, None], (1, t, G, P))

    def combine(left, right):
        al_re, al_im, bl_re, bl_im = left
        ar_re, ar_im, br_re, br_im = right
        return (ar_re * al_re - ar_im * al_im,
                ar_re * al_im + ar_im * al_re,
                ar_re * bl_re - ar_im * bl_im + br_re,
                ar_re * bl_im + ar_im * bl_re + br_im)

    _, _, h_re, h_im = lax.associative_scan(combine, (a_re, a_im, bu_re, bu_im), axis=1)
    y = (jnp.einsum('gcp,btgp->btgc', c_re.astype(f32), h_re)
         - jnp.einsum('gcp,btgp->btgc', c_im.astype(f32), h_im))
    y = y + d_skip.astype(f32).reshape(G, GS) * uf
    return y.reshape(bsz, t, D_SSM).astype(u.dtype)


def s5_mixer(h, w_in, lam_re, lam_im, log_dt, b_re, b_im, c_re, c_im, d_skip, w_out):
    y = jax.nn.gelu(s5_ssm(h @ w_in, lam_re, lam_im, log_dt, b_re, b_im, c_re, c_im, d_skip))
    z = y @ w_out
    return z[..., :D_MODEL] * jax.nn.sigmoid(z[..., D_MODEL:])


def _fwd_setup_inputs(seed: int = 0) -> dict:
    key = jax.random.key(seed)
    ks = iter(jax.random.split(key, 48))
    f32 = jnp.float32
    ne, no = (DEPTH + 1) // 2, DEPTH // 2
    G, P, GS = N_SSM_GROUPS, SSM_STATE, SSM_GROUP

    def normal(shape, scale):
        return jax.random.normal(next(ks), shape, f32) * scale

    def gain(shape):
        return 1.0 + normal(shape, 0.02)

    x = normal((BATCH, SEQ, D_MODEL), 1.0)
    p = normal((DEPTH, BATCH, SEQ, D_PLE), 1.0)
    ffn1_norm = gain((DEPTH, D_MODEL))
    ffn1_w_gate = normal((DEPTH, D_MODEL, D_FF), D_MODEL ** -0.5)
    ffn1_w_up = normal((DEPTH, D_MODEL, D_FF), D_MODEL ** -0.5)
    ffn1_w_down = normal((DEPTH, D_FF, D_MODEL), D_FF ** -0.5)
    mix_norm = gain((DEPTH, D_MODEL))
    ffn2_norm = gain((DEPTH, D_MODEL))
    ffn2_w_gate = normal((DEPTH, D_MODEL, D_FF), D_MODEL ** -0.5)
    ffn2_w_up = normal((DEPTH, D_MODEL, D_FF), D_MODEL ** -0.5)
    ffn2_w_down = normal((DEPTH, D_FF, D_MODEL), D_FF ** -0.5)
    ple_norm = gain((DEPTH, D_MODEL))
    ple_w_gate = normal((DEPTH, D_MODEL, D_MODEL), D_MODEL ** -0.5)
    ple_w_proj = normal((DEPTH, D_PLE, D_MODEL), D_PLE ** -0.5)
    ab_w_in = normal((ne, D_MODEL, N_IN_AB), D_MODEL ** -0.5)
    att_q_gain = gain((ne, HEAD_DIM))
    att_k_gain = gain((ne, HEAD_DIM))
    att_rel_bias = normal((ne, N_HEADS_ATT, N_REL), 0.1)
    rwkv_mu = jax.random.uniform(next(ks), (ne, N_B_IN), f32)
    rwkv_w0 = jnp.linspace(-6.0, -1.0, D_RWKV, dtype=f32) + normal((ne, D_RWKV), 0.1)
    rwkv_w_up = normal((ne, DECAY_LORA, D_RWKV), 0.1 * DECAY_LORA ** -0.5)
    rwkv_a0 = normal((ne, D_RWKV), 0.1)
    rwkv_a_up = normal((ne, AAA_LORA, D_RWKV), 0.5 * AAA_LORA ** -0.5)
    rwkv_g_up = normal((ne, GATE_LORA, D_RWKV), GATE_LORA ** -0.5)
    rwkv_k_k = 0.85 + normal((ne, D_RWKV), 0.02)
    rwkv_k_a = gain((ne, D_RWKV))
    rwkv_r_k = normal((ne, N_HEADS_RWKV, HEAD_DIM), 0.1)
    rwkv_lnx_w = gain((ne, D_RWKV))
    rwkv_lnx_b = normal((ne, D_RWKV), 0.02)
    ab_w_out = normal((ne, D_ATT + D_RWKV, D_MODEL), (D_ATT + D_RWKV) ** -0.5)
    ssm_w_in = normal((no, D_MODEL, D_SSM), D_MODEL ** -0.5)
    ssm_lambda_re = -0.5 + normal((no, G, P), 0.01)
    ssm_lambda_im = math.pi * jnp.arange(P, dtype=f32) + normal((no, G, P), 0.01)
    ssm_log_dt = jax.random.uniform(next(ks), (no, G), f32, math.log(1e-3), math.log(1e-1))
    ssm_b_re = normal((no, G, P, GS), (2 * GS) ** -0.5)
    ssm_b_im = normal((no, G, P, GS), (2 * GS) ** -0.5)
    ssm_c_re = normal((no, G, GS, P), (2 * P) ** -0.5)
    ssm_c_im = normal((no, G, GS, P), (2 * P) ** -0.5)
    ssm_d = normal((no, D_SSM), 1.0)
    ssm_w_out = normal((no, D_SSM, 2 * D_MODEL), D_SSM ** -0.5)
    return {
        'x': x, 'p': p,
        'ffn1_norm': ffn1_norm, 'ffn1_w_gate': ffn1_w_gate, 'ffn1_w_up': ffn1_w_up,
        'ffn1_w_down': ffn1_w_down, 'mix_norm': mix_norm,
        'ffn2_norm': ffn2_norm, 'ffn2_w_gate': ffn2_w_gate, 'ffn2_w_up': ffn2_w_up,
        'ffn2_w_down': ffn2_w_down,
        'ple_norm': ple_norm, 'ple_w_gate': ple_w_gate, 'ple_w_proj': ple_w_proj,
        'ab_w_in': ab_w_in, 'att_q_gain': att_q_gain, 'att_k_gain': att_k_gain,
        'att_rel_bias': att_rel_bias, 'rwkv_mu': rwkv_mu, 'rwkv_w0': rwkv_w0,
        'rwkv_w_up': rwkv_w_up, 'rwkv_a0': rwkv_a0, 'rwkv_a_up': rwkv_a_up,
        'rwkv_g_up': rwkv_g_up, 'rwkv_k_k': rwkv_k_k, 'rwkv_k_a': rwkv_k_a,
        'rwkv_r_k': rwkv_r_k, 'rwkv_lnx_w': rwkv_lnx_w, 'rwkv_lnx_b': rwkv_lnx_b,
        'ab_w_out': ab_w_out,
        'ssm_w_in': ssm_w_in, 'ssm_lambda_re': ssm_lambda_re, 'ssm_lambda_im': ssm_lambda_im,
        'ssm_log_dt': ssm_log_dt, 'ssm_b_re': ssm_b_re, 'ssm_b_im': ssm_b_im,
        'ssm_c_re': ssm_c_re, 'ssm_c_im': ssm_c_im, 'ssm_d': ssm_d, 'ssm_w_out': ssm_w_out,
    }


def _fwd_reference(x, p, ffn1_norm, ffn1_w_gate, ffn1_w_up, ffn1_w_down, mix_norm,
              ffn2_norm, ffn2_w_gate, ffn2_w_up, ffn2_w_down,
              ple_norm, ple_w_gate, ple_w_proj,
              ab_w_in, att_q_gain, att_k_gain, att_rel_bias, rwkv_mu, rwkv_w0,
              rwkv_w_up, rwkv_a0, rwkv_a_up, rwkv_g_up, rwkv_k_k, rwkv_k_a,
              rwkv_r_k, rwkv_lnx_w, rwkv_lnx_b, ab_w_out,
              ssm_w_in, ssm_lambda_re, ssm_lambda_im, ssm_log_dt, ssm_b_re, ssm_b_im,
              ssm_c_re, ssm_c_im, ssm_d, ssm_w_out):
    h = x
    for i in range(DEPTH):
        j = i // 2
        h = h + 0.5 * swiglu_ffn(rms_norm(h, ffn1_norm[i]), ffn1_w_gate[i],
                                 ffn1_w_up[i], ffn1_w_down[i])
        hn = rms_norm(h, mix_norm[i])
        if i % 2 == 0:
            mix = attn_rwkv_mixer(hn, ab_w_in[j], att_q_gain[j], att_k_gain[j],
                                  att_rel_bias[j], rwkv_mu[j], rwkv_w0[j], rwkv_w_up[j],
                                  rwkv_a0[j], rwkv_a_up[j], rwkv_g_up[j], rwkv_k_k[j],
                                  rwkv_k_a[j], rwkv_r_k[j], rwkv_lnx_w[j], rwkv_lnx_b[j],
                                  ab_w_out[j])
        else:
            mix = s5_mixer(hn, ssm_w_in[j], ssm_lambda_re[j], ssm_lambda_im[j],
                           ssm_log_dt[j], ssm_b_re[j], ssm_b_im[j], ssm_c_re[j],
                           ssm_c_im[j], ssm_d[j], ssm_w_out[j])
        h = h + mix
        h = h + 0.5 * swiglu_ffn(rms_norm(h, ffn2_norm[i]), ffn2_w_gate[i],
                                 ffn2_w_up[i], ffn2_w_down[i])
        gate = jax.nn.sigmoid(rms_norm(h, ple_norm[i]) @ ple_w_gate[i])
        h = h + gate * (p[i] @ ple_w_proj[i])
    return h


import jax as _jax
import jax.numpy as _jnp

TWIN_FORMAT = 'train_step'
FWD_PARAMS = ['x', 'p', 'ffn1_norm', 'ffn1_w_gate', 'ffn1_w_up', 'ffn1_w_down', 'mix_norm', 'ffn2_norm', 'ffn2_w_gate', 'ffn2_w_up', 'ffn2_w_down', 'ple_norm', 'ple_w_gate', 'ple_w_proj', 'ab_w_in', 'att_q_gain', 'att_k_gain', 'att_rel_bias', 'rwkv_mu', 'rwkv_w0', 'rwkv_w_up', 'rwkv_a0', 'rwkv_a_up', 'rwkv_g_up', 'rwkv_k_k', 'rwkv_k_a', 'rwkv_r_k', 'rwkv_lnx_w', 'rwkv_lnx_b', 'ab_w_out', 'ssm_w_in', 'ssm_lambda_re', 'ssm_lambda_im', 'ssm_log_dt', 'ssm_b_re', 'ssm_b_im', 'ssm_c_re', 'ssm_c_im', 'ssm_d', 'ssm_w_out']
TWIN_WEIGHTS = ['ffn1_norm', 'ffn1_w_gate', 'ffn1_w_up', 'ffn1_w_down', 'mix_norm', 'ffn2_norm', 'ffn2_w_gate', 'ffn2_w_up', 'ffn2_w_down', 'ple_norm', 'ple_w_gate', 'ple_w_proj', 'ab_w_in', 'att_q_gain', 'att_k_gain', 'att_rel_bias', 'rwkv_mu', 'rwkv_w0', 'rwkv_w_up', 'rwkv_a0', 'rwkv_a_up', 'rwkv_g_up', 'rwkv_k_k', 'rwkv_k_a', 'rwkv_r_k', 'rwkv_lnx_w', 'rwkv_lnx_b', 'ab_w_out', 'ssm_w_in', 'ssm_lambda_re', 'ssm_lambda_im', 'ssm_log_dt', 'ssm_b_re', 'ssm_b_im', 'ssm_c_re', 'ssm_c_im', 'ssm_d', 'ssm_w_out']
TWIN_DIFF_INPUT = 'x'
TWIN_INPUTS = ['x', 'p', 'ffn1_norm', 'ffn1_w_gate', 'ffn1_w_up', 'ffn1_w_down', 'mix_norm', 'ffn2_norm', 'ffn2_w_gate', 'ffn2_w_up', 'ffn2_w_down', 'ple_norm', 'ple_w_gate', 'ple_w_proj', 'ab_w_in', 'att_q_gain', 'att_k_gain', 'att_rel_bias', 'rwkv_mu', 'rwkv_w0', 'rwkv_w_up', 'rwkv_a0', 'rwkv_a_up', 'rwkv_g_up', 'rwkv_k_k', 'rwkv_k_a', 'rwkv_r_k', 'rwkv_lnx_w', 'rwkv_lnx_b', 'ab_w_out', 'ssm_w_in', 'ssm_lambda_re', 'ssm_lambda_im', 'ssm_log_dt', 'ssm_b_re', 'ssm_b_im', 'ssm_c_re', 'ssm_c_im', 'ssm_d', 'ssm_w_out', 'loss_target', 'm_ffn1_norm', 'm_ffn1_w_gate', 'm_ffn1_w_up', 'm_ffn1_w_down', 'm_mix_norm', 'm_ffn2_norm', 'm_ffn2_w_gate', 'm_ffn2_w_up', 'm_ffn2_w_down', 'm_ple_norm', 'm_ple_w_gate', 'm_ple_w_proj', 'm_ab_w_in', 'm_att_q_gain', 'm_att_k_gain', 'm_att_rel_bias', 'm_rwkv_mu', 'm_rwkv_w0', 'm_rwkv_w_up', 'm_rwkv_a0', 'm_rwkv_a_up', 'm_rwkv_g_up', 'm_rwkv_k_k', 'm_rwkv_k_a', 'm_rwkv_r_k', 'm_rwkv_lnx_w', 'm_rwkv_lnx_b', 'm_ab_w_out', 'm_ssm_w_in', 'm_ssm_lambda_re', 'm_ssm_lambda_im', 'm_ssm_log_dt', 'm_ssm_b_re', 'm_ssm_b_im', 'm_ssm_c_re', 'm_ssm_c_im', 'm_ssm_d', 'm_ssm_w_out', 'v_ffn1_norm', 'v_ffn1_w_gate', 'v_ffn1_w_up', 'v_ffn1_w_down', 'v_mix_norm', 'v_ffn2_norm', 'v_ffn2_w_gate', 'v_ffn2_w_up', 'v_ffn2_w_down', 'v_ple_norm', 'v_ple_w_gate', 'v_ple_w_proj', 'v_ab_w_in', 'v_att_q_gain', 'v_att_k_gain', 'v_att_rel_bias', 'v_rwkv_mu', 'v_rwkv_w0', 'v_rwkv_w_up', 'v_rwkv_a0', 'v_rwkv_a_up', 'v_rwkv_g_up', 'v_rwkv_k_k', 'v_rwkv_k_a', 'v_rwkv_r_k', 'v_rwkv_lnx_w', 'v_rwkv_lnx_b', 'v_ab_w_out', 'v_ssm_w_in', 'v_ssm_lambda_re', 'v_ssm_lambda_im', 'v_ssm_log_dt', 'v_ssm_b_re', 'v_ssm_b_im', 'v_ssm_c_re', 'v_ssm_c_im', 'v_ssm_d', 'v_ssm_w_out']
TWIN_OUTPUTS = ['loss', 'grad_x', 'grad_ffn1_norm', 'grad_ffn1_w_gate', 'grad_ffn1_w_up', 'grad_ffn1_w_down', 'grad_mix_norm', 'grad_ffn2_norm', 'grad_ffn2_w_gate', 'grad_ffn2_w_up', 'grad_ffn2_w_down', 'grad_ple_norm', 'grad_ple_w_gate', 'grad_ple_w_proj', 'grad_ab_w_in', 'grad_att_q_gain', 'grad_att_k_gain', 'grad_att_rel_bias', 'grad_rwkv_mu', 'grad_rwkv_w0', 'grad_rwkv_w_up', 'grad_rwkv_a0', 'grad_rwkv_a_up', 'grad_rwkv_g_up', 'grad_rwkv_k_k', 'grad_rwkv_k_a', 'grad_rwkv_r_k', 'grad_rwkv_lnx_w', 'grad_rwkv_lnx_b', 'grad_ab_w_out', 'grad_ssm_w_in', 'grad_ssm_lambda_re', 'grad_ssm_lambda_im', 'grad_ssm_log_dt', 'grad_ssm_b_re', 'grad_ssm_b_im', 'grad_ssm_c_re', 'grad_ssm_c_im', 'grad_ssm_d', 'grad_ssm_w_out', 'delta_ffn1_norm', 'delta_ffn1_w_gate', 'delta_ffn1_w_up', 'delta_ffn1_w_down', 'delta_mix_norm', 'delta_ffn2_norm', 'delta_ffn2_w_gate', 'delta_ffn2_w_up', 'delta_ffn2_w_down', 'delta_ple_norm', 'delta_ple_w_gate', 'delta_ple_w_proj', 'delta_ab_w_in', 'delta_att_q_gain', 'delta_att_k_gain', 'delta_att_rel_bias', 'delta_rwkv_mu', 'delta_rwkv_w0', 'delta_rwkv_w_up', 'delta_rwkv_a0', 'delta_rwkv_a_up', 'delta_rwkv_g_up', 'delta_rwkv_k_k', 'delta_rwkv_k_a', 'delta_rwkv_r_k', 'delta_rwkv_lnx_w', 'delta_rwkv_lnx_b', 'delta_ab_w_out', 'delta_ssm_w_in', 'delta_ssm_lambda_re', 'delta_ssm_lambda_im', 'delta_ssm_log_dt', 'delta_ssm_b_re', 'delta_ssm_b_im', 'delta_ssm_c_re', 'delta_ssm_c_im', 'delta_ssm_d', 'delta_ssm_w_out', 'new_m_ffn1_norm', 'new_m_ffn1_w_gate', 'new_m_ffn1_w_up', 'new_m_ffn1_w_down', 'new_m_mix_norm', 'new_m_ffn2_norm', 'new_m_ffn2_w_gate', 'new_m_ffn2_w_up', 'new_m_ffn2_w_down', 'new_m_ple_norm', 'new_m_ple_w_gate', 'new_m_ple_w_proj', 'new_m_ab_w_in', 'new_m_att_q_gain', 'new_m_att_k_gain', 'new_m_att_rel_bias', 'new_m_rwkv_mu', 'new_m_rwkv_w0', 'new_m_rwkv_w_up', 'new_m_rwkv_a0', 'new_m_rwkv_a_up', 'new_m_rwkv_g_up', 'new_m_rwkv_k_k', 'new_m_rwkv_k_a', 'new_m_rwkv_r_k', 'new_m_rwkv_lnx_w', 'new_m_rwkv_lnx_b', 'new_m_ab_w_out', 'new_m_ssm_w_in', 'new_m_ssm_lambda_re', 'new_m_ssm_lambda_im', 'new_m_ssm_log_dt', 'new_m_ssm_b_re', 'new_m_ssm_b_im', 'new_m_ssm_c_re', 'new_m_ssm_c_im', 'new_m_ssm_d', 'new_m_ssm_w_out', 'new_v_ffn1_norm', 'new_v_ffn1_w_gate', 'new_v_ffn1_w_up', 'new_v_ffn1_w_down', 'new_v_mix_norm', 'new_v_ffn2_norm', 'new_v_ffn2_w_gate', 'new_v_ffn2_w_up', 'new_v_ffn2_w_down', 'new_v_ple_norm', 'new_v_ple_w_gate', 'new_v_ple_w_proj', 'new_v_ab_w_in', 'new_v_att_q_gain', 'new_v_att_k_gain', 'new_v_att_rel_bias', 'new_v_rwkv_mu', 'new_v_rwkv_w0', 'new_v_rwkv_w_up', 'new_v_rwkv_a0', 'new_v_rwkv_a_up', 'new_v_rwkv_g_up', 'new_v_rwkv_k_k', 'new_v_rwkv_k_a', 'new_v_rwkv_r_k', 'new_v_rwkv_lnx_w', 'new_v_rwkv_lnx_b', 'new_v_ab_w_out', 'new_v_ssm_w_in', 'new_v_ssm_lambda_re', 'new_v_ssm_lambda_im', 'new_v_ssm_log_dt', 'new_v_ssm_b_re', 'new_v_ssm_b_im', 'new_v_ssm_c_re', 'new_v_ssm_c_im', 'new_v_ssm_d', 'new_v_ssm_w_out']
TWIN_LEAF_KINDS = {'loss': 'loss', 'grad_x': 'grad_x', 'grad_ffn1_norm': 'grad_w', 'grad_ffn1_w_gate': 'grad_w', 'grad_ffn1_w_up': 'grad_w', 'grad_ffn1_w_down': 'grad_w', 'grad_mix_norm': 'grad_w', 'grad_ffn2_norm': 'grad_w', 'grad_ffn2_w_gate': 'grad_w', 'grad_ffn2_w_up': 'grad_w', 'grad_ffn2_w_down': 'grad_w', 'grad_ple_norm': 'grad_w', 'grad_ple_w_gate': 'grad_w', 'grad_ple_w_proj': 'grad_w', 'grad_ab_w_in': 'grad_w', 'grad_att_q_gain': 'grad_w', 'grad_att_k_gain': 'grad_w', 'grad_att_rel_bias': 'grad_w', 'grad_rwkv_mu': 'grad_w', 'grad_rwkv_w0': 'grad_w', 'grad_rwkv_w_up': 'grad_w', 'grad_rwkv_a0': 'grad_w', 'grad_rwkv_a_up': 'grad_w', 'grad_rwkv_g_up': 'grad_w', 'grad_rwkv_k_k': 'grad_w', 'grad_rwkv_k_a': 'grad_w', 'grad_rwkv_r_k': 'grad_w', 'grad_rwkv_lnx_w': 'grad_w', 'grad_rwkv_lnx_b': 'grad_w', 'grad_ab_w_out': 'grad_w', 'grad_ssm_w_in': 'grad_w', 'grad_ssm_lambda_re': 'grad_w', 'grad_ssm_lambda_im': 'grad_w', 'grad_ssm_log_dt': 'grad_w', 'grad_ssm_b_re': 'grad_w', 'grad_ssm_b_im': 'grad_w', 'grad_ssm_c_re': 'grad_w', 'grad_ssm_c_im': 'grad_w', 'grad_ssm_d': 'grad_w', 'grad_ssm_w_out': 'grad_w', 'delta_ffn1_norm': 'delta_w', 'delta_ffn1_w_gate': 'delta_w', 'delta_ffn1_w_up': 'delta_w', 'delta_ffn1_w_down': 'delta_w', 'delta_mix_norm': 'delta_w', 'delta_ffn2_norm': 'delta_w', 'delta_ffn2_w_gate': 'delta_w', 'delta_ffn2_w_up': 'delta_w', 'delta_ffn2_w_down': 'delta_w', 'delta_ple_norm': 'delta_w', 'delta_ple_w_gate': 'delta_w', 'delta_ple_w_proj': 'delta_w', 'delta_ab_w_in': 'delta_w', 'delta_att_q_gain': 'delta_w', 'delta_att_k_gain': 'delta_w', 'delta_att_rel_bias': 'delta_w', 'delta_rwkv_mu': 'delta_w', 'delta_rwkv_w0': 'delta_w', 'delta_rwkv_w_up': 'delta_w', 'delta_rwkv_a0': 'delta_w', 'delta_rwkv_a_up': 'delta_w', 'delta_rwkv_g_up': 'delta_w', 'delta_rwkv_k_k': 'delta_w', 'delta_rwkv_k_a': 'delta_w', 'delta_rwkv_r_k': 'delta_w', 'delta_rwkv_lnx_w': 'delta_w', 'delta_rwkv_lnx_b': 'delta_w', 'delta_ab_w_out': 'delta_w', 'delta_ssm_w_in': 'delta_w', 'delta_ssm_lambda_re': 'delta_w', 'delta_ssm_lambda_im': 'delta_w', 'delta_ssm_log_dt': 'delta_w', 'delta_ssm_b_re': 'delta_w', 'delta_ssm_b_im': 'delta_w', 'delta_ssm_c_re': 'delta_w', 'delta_ssm_c_im': 'delta_w', 'delta_ssm_d': 'delta_w', 'delta_ssm_w_out': 'delta_w', 'new_m_ffn1_norm': 'new_m', 'new_m_ffn1_w_gate': 'new_m', 'new_m_ffn1_w_up': 'new_m', 'new_m_ffn1_w_down': 'new_m', 'new_m_mix_norm': 'new_m', 'new_m_ffn2_norm': 'new_m', 'new_m_ffn2_w_gate': 'new_m', 'new_m_ffn2_w_up': 'new_m', 'new_m_ffn2_w_down': 'new_m', 'new_m_ple_norm': 'new_m', 'new_m_ple_w_gate': 'new_m', 'new_m_ple_w_proj': 'new_m', 'new_m_ab_w_in': 'new_m', 'new_m_att_q_gain': 'new_m', 'new_m_att_k_gain': 'new_m', 'new_m_att_rel_bias': 'new_m', 'new_m_rwkv_mu': 'new_m', 'new_m_rwkv_w0': 'new_m', 'new_m_rwkv_w_up': 'new_m', 'new_m_rwkv_a0': 'new_m', 'new_m_rwkv_a_up': 'new_m', 'new_m_rwkv_g_up': 'new_m', 'new_m_rwkv_k_k': 'new_m', 'new_m_rwkv_k_a': 'new_m', 'new_m_rwkv_r_k': 'new_m', 'new_m_rwkv_lnx_w': 'new_m', 'new_m_rwkv_lnx_b': 'new_m', 'new_m_ab_w_out': 'new_m', 'new_m_ssm_w_in': 'new_m', 'new_m_ssm_lambda_re': 'new_m', 'new_m_ssm_lambda_im': 'new_m', 'new_m_ssm_log_dt': 'new_m', 'new_m_ssm_b_re': 'new_m', 'new_m_ssm_b_im': 'new_m', 'new_m_ssm_c_re': 'new_m', 'new_m_ssm_c_im': 'new_m', 'new_m_ssm_d': 'new_m', 'new_m_ssm_w_out': 'new_m', 'new_v_ffn1_norm': 'new_v', 'new_v_ffn1_w_gate': 'new_v', 'new_v_ffn1_w_up': 'new_v', 'new_v_ffn1_w_down': 'new_v', 'new_v_mix_norm': 'new_v', 'new_v_ffn2_norm': 'new_v', 'new_v_ffn2_w_gate': 'new_v', 'new_v_ffn2_w_up': 'new_v', 'new_v_ffn2_w_down': 'new_v', 'new_v_ple_norm': 'new_v', 'new_v_ple_w_gate': 'new_v', 'new_v_ple_w_proj': 'new_v', 'new_v_ab_w_in': 'new_v', 'new_v_att_q_gain': 'new_v', 'new_v_att_k_gain': 'new_v', 'new_v_att_rel_bias': 'new_v', 'new_v_rwkv_mu': 'new_v', 'new_v_rwkv_w0': 'new_v', 'new_v_rwkv_w_up': 'new_v', 'new_v_rwkv_a0': 'new_v', 'new_v_rwkv_a_up': 'new_v', 'new_v_rwkv_g_up': 'new_v', 'new_v_rwkv_k_k': 'new_v', 'new_v_rwkv_k_a': 'new_v', 'new_v_rwkv_r_k': 'new_v', 'new_v_rwkv_lnx_w': 'new_v', 'new_v_rwkv_lnx_b': 'new_v', 'new_v_ab_w_out': 'new_v', 'new_v_ssm_w_in': 'new_v', 'new_v_ssm_lambda_re': 'new_v', 'new_v_ssm_lambda_im': 'new_v', 'new_v_ssm_log_dt': 'new_v', 'new_v_ssm_b_re': 'new_v', 'new_v_ssm_b_im': 'new_v', 'new_v_ssm_c_re': 'new_v', 'new_v_ssm_c_im': 'new_v', 'new_v_ssm_d': 'new_v', 'new_v_ssm_w_out': 'new_v'}


def _forward(args):
    return _fwd_reference(*[args[k] for k in FWD_PARAMS])


def _output_shape():
    def fwd():
        inp = _fwd_setup_inputs(0)
        return _fwd_reference(*[inp[k] for k in FWD_PARAMS])
    out = _jax.eval_shape(fwd)
    return out.shape, out.dtype

N_MICROBATCH = 1
ADAM_LR = 0.001
ADAM_B1 = 0.9
ADAM_B2 = 0.999
ADAM_EPS = 1e-08
ADAM_WD = 0.01
ADAM_STEP = 10
PER_EXAMPLE_BATCH_AXIS = {'x': 0, 'p': 1, 'loss_target': 0}
SHARED_INPUTS = []
_WEIGHT_DTYPES = {'ffn1_norm': _jnp.float32, 'ffn1_w_gate': _jnp.float32, 'ffn1_w_up': _jnp.float32, 'ffn1_w_down': _jnp.float32, 'mix_norm': _jnp.float32, 'ffn2_norm': _jnp.float32, 'ffn2_w_gate': _jnp.float32, 'ffn2_w_up': _jnp.float32, 'ffn2_w_down': _jnp.float32, 'ple_norm': _jnp.float32, 'ple_w_gate': _jnp.float32, 'ple_w_proj': _jnp.float32, 'ab_w_in': _jnp.float32, 'att_q_gain': _jnp.float32, 'att_k_gain': _jnp.float32, 'att_rel_bias': _jnp.float32, 'rwkv_mu': _jnp.float32, 'rwkv_w0': _jnp.float32, 'rwkv_w_up': _jnp.float32, 'rwkv_a0': _jnp.float32, 'rwkv_a_up': _jnp.float32, 'rwkv_g_up': _jnp.float32, 'rwkv_k_k': _jnp.float32, 'rwkv_k_a': _jnp.float32, 'rwkv_r_k': _jnp.float32, 'rwkv_lnx_w': _jnp.float32, 'rwkv_lnx_b': _jnp.float32, 'ab_w_out': _jnp.float32, 'ssm_w_in': _jnp.float32, 'ssm_lambda_re': _jnp.float32, 'ssm_lambda_im': _jnp.float32, 'ssm_log_dt': _jnp.float32, 'ssm_b_re': _jnp.float32, 'ssm_b_im': _jnp.float32, 'ssm_c_re': _jnp.float32, 'ssm_c_im': _jnp.float32, 'ssm_d': _jnp.float32, 'ssm_w_out': _jnp.float32}
MOMENT_SCALE = {'ffn1_norm': 3.083402e+00, 'ffn1_w_gate': 4.574735e-02, 'ffn1_w_up': 4.859904e-02, 'ffn1_w_down': 7.977054e-02, 'mix_norm': 1.535639e+00, 'ffn2_norm': 3.078810e+00, 'ffn2_w_gate': 4.407754e-02, 'ffn2_w_up': 4.328754e-02, 'ffn2_w_down': 7.066039e-02, 'ple_norm': 5.036998e-01, 'ple_w_gate': 1.139483e-01, 'ple_w_proj': 2.616517e-01, 'ab_w_in': 1.050201e-01, 'att_q_gain': 1.083002e+00, 'att_k_gain': 1.082993e+00, 'att_rel_bias': 1.752185e-02, 'rwkv_mu': 2.350280e+00, 'rwkv_w0': 6.741685e-02, 'rwkv_w_up': 9.291413e-03, 'rwkv_a0': 3.038801e-01, 'rwkv_a_up': 5.223469e-02, 'rwkv_g_up': 4.412150e+00, 'rwkv_k_k': 1.418699e-01, 'rwkv_k_a': 6.113980e-01, 'rwkv_r_k': 3.547327e+00, 'rwkv_lnx_w': 7.545203e+00, 'rwkv_lnx_b': 2.472403e+00, 'ab_w_out': 1.332293e-01, 'ssm_w_in': 2.367147e-01, 'ssm_lambda_re': 1.145236e-02, 'ssm_lambda_im': 7.623751e-03, 'ssm_log_dt': 4.235319e+00, 'ssm_b_re': 7.324898e-03, 'ssm_b_im': 6.905895e-03, 'ssm_c_re': 1.419755e-02, 'ssm_c_im': 1.461579e-02, 'ssm_d': 4.489553e+00, 'ssm_w_out': 3.911980e-01}


def _to_microbatches(a, axis):
    t = _jnp.moveaxis(a, axis, 0)
    t = t.reshape((N_MICROBATCH, t.shape[0] // N_MICROBATCH) + t.shape[1:])
    return _jnp.moveaxis(t, 1, axis + 1)


def setup_inputs(seed: int = 0) -> dict:
    inp = _fwd_setup_inputs(seed)
    key = _jax.random.fold_in(_jax.random.key(seed), 7919)
    shape, _ = _output_shape()
    out = dict(inp)
    out["loss_target"] = _jax.random.normal(_jax.random.fold_in(key, 0), shape, _jnp.float32)
    for i, name in enumerate(TWIN_WEIGHTS):
        w = inp[name].astype(_jnp.float32)
        if MOMENT_SCALE is None:
            s = _jnp.sqrt(_jnp.mean(_jnp.square(w)) + 1e-30)
        else:
            s = MOMENT_SCALE[name]
        km, kv = _jax.random.split(_jax.random.fold_in(key, i + 1))
        out[name] = w
        out["m_" + name] = s * _jax.random.normal(km, w.shape, _jnp.float32)
        out["v_" + name] = (s * s) * _jax.random.uniform(kv, w.shape, _jnp.float32, 0.5, 1.5)
    if N_MICROBATCH > 1:
        for name, axis in PER_EXAMPLE_BATCH_AXIS.items():
            out[name] = _to_microbatches(out[name], axis)
    return {'x': out['x'], 'p': out['p'], 'ffn1_norm': out['ffn1_norm'], 'ffn1_w_gate': out['ffn1_w_gate'], 'ffn1_w_up': out['ffn1_w_up'], 'ffn1_w_down': out['ffn1_w_down'], 'mix_norm': out['mix_norm'], 'ffn2_norm': out['ffn2_norm'], 'ffn2_w_gate': out['ffn2_w_gate'], 'ffn2_w_up': out['ffn2_w_up'], 'ffn2_w_down': out['ffn2_w_down'], 'ple_norm': out['ple_norm'], 'ple_w_gate': out['ple_w_gate'], 'ple_w_proj': out['ple_w_proj'], 'ab_w_in': out['ab_w_in'], 'att_q_gain': out['att_q_gain'], 'att_k_gain': out['att_k_gain'], 'att_rel_bias': out['att_rel_bias'], 'rwkv_mu': out['rwkv_mu'], 'rwkv_w0': out['rwkv_w0'], 'rwkv_w_up': out['rwkv_w_up'], 'rwkv_a0': out['rwkv_a0'], 'rwkv_a_up': out['rwkv_a_up'], 'rwkv_g_up': out['rwkv_g_up'], 'rwkv_k_k': out['rwkv_k_k'], 'rwkv_k_a': out['rwkv_k_a'], 'rwkv_r_k': out['rwkv_r_k'], 'rwkv_lnx_w': out['rwkv_lnx_w'], 'rwkv_lnx_b': out['rwkv_lnx_b'], 'ab_w_out': out['ab_w_out'], 'ssm_w_in': out['ssm_w_in'], 'ssm_lambda_re': out['ssm_lambda_re'], 'ssm_lambda_im': out['ssm_lambda_im'], 'ssm_log_dt': out['ssm_log_dt'], 'ssm_b_re': out['ssm_b_re'], 'ssm_b_im': out['ssm_b_im'], 'ssm_c_re': out['ssm_c_re'], 'ssm_c_im': out['ssm_c_im'], 'ssm_d': out['ssm_d'], 'ssm_w_out': out['ssm_w_out'], 'loss_target': out['loss_target'], 'm_ffn1_norm': out['m_ffn1_norm'], 'm_ffn1_w_gate': out['m_ffn1_w_gate'], 'm_ffn1_w_up': out['m_ffn1_w_up'], 'm_ffn1_w_down': out['m_ffn1_w_down'], 'm_mix_norm': out['m_mix_norm'], 'm_ffn2_norm': out['m_ffn2_norm'], 'm_ffn2_w_gate': out['m_ffn2_w_gate'], 'm_ffn2_w_up': out['m_ffn2_w_up'], 'm_ffn2_w_down': out['m_ffn2_w_down'], 'm_ple_norm': out['m_ple_norm'], 'm_ple_w_gate': out['m_ple_w_gate'], 'm_ple_w_proj': out['m_ple_w_proj'], 'm_ab_w_in': out['m_ab_w_in'], 'm_att_q_gain': out['m_att_q_gain'], 'm_att_k_gain': out['m_att_k_gain'], 'm_att_rel_bias': out['m_att_rel_bias'], 'm_rwkv_mu': out['m_rwkv_mu'], 'm_rwkv_w0': out['m_rwkv_w0'], 'm_rwkv_w_up': out['m_rwkv_w_up'], 'm_rwkv_a0': out['m_rwkv_a0'], 'm_rwkv_a_up': out['m_rwkv_a_up'], 'm_rwkv_g_up': out['m_rwkv_g_up'], 'm_rwkv_k_k': out['m_rwkv_k_k'], 'm_rwkv_k_a': out['m_rwkv_k_a'], 'm_rwkv_r_k': out['m_rwkv_r_k'], 'm_rwkv_lnx_w': out['m_rwkv_lnx_w'], 'm_rwkv_lnx_b': out['m_rwkv_lnx_b'], 'm_ab_w_out': out['m_ab_w_out'], 'm_ssm_w_in': out['m_ssm_w_in'], 'm_ssm_lambda_re': out['m_ssm_lambda_re'], 'm_ssm_lambda_im': out['m_ssm_lambda_im'], 'm_ssm_log_dt': out['m_ssm_log_dt'], 'm_ssm_b_re': out['m_ssm_b_re'], 'm_ssm_b_im': out['m_ssm_b_im'], 'm_ssm_c_re': out['m_ssm_c_re'], 'm_ssm_c_im': out['m_ssm_c_im'], 'm_ssm_d': out['m_ssm_d'], 'm_ssm_w_out': out['m_ssm_w_out'], 'v_ffn1_norm': out['v_ffn1_norm'], 'v_ffn1_w_gate': out['v_ffn1_w_gate'], 'v_ffn1_w_up': out['v_ffn1_w_up'], 'v_ffn1_w_down': out['v_ffn1_w_down'], 'v_mix_norm': out['v_mix_norm'], 'v_ffn2_norm': out['v_ffn2_norm'], 'v_ffn2_w_gate': out['v_ffn2_w_gate'], 'v_ffn2_w_up': out['v_ffn2_w_up'], 'v_ffn2_w_down': out['v_ffn2_w_down'], 'v_ple_norm': out['v_ple_norm'], 'v_ple_w_gate': out['v_ple_w_gate'], 'v_ple_w_proj': out['v_ple_w_proj'], 'v_ab_w_in': out['v_ab_w_in'], 'v_att_q_gain': out['v_att_q_gain'], 'v_att_k_gain': out['v_att_k_gain'], 'v_att_rel_bias': out['v_att_rel_bias'], 'v_rwkv_mu': out['v_rwkv_mu'], 'v_rwkv_w0': out['v_rwkv_w0'], 'v_rwkv_w_up': out['v_rwkv_w_up'], 'v_rwkv_a0': out['v_rwkv_a0'], 'v_rwkv_a_up': out['v_rwkv_a_up'], 'v_rwkv_g_up': out['v_rwkv_g_up'], 'v_rwkv_k_k': out['v_rwkv_k_k'], 'v_rwkv_k_a': out['v_rwkv_k_a'], 'v_rwkv_r_k': out['v_rwkv_r_k'], 'v_rwkv_lnx_w': out['v_rwkv_lnx_w'], 'v_rwkv_lnx_b': out['v_rwkv_lnx_b'], 'v_ab_w_out': out['v_ab_w_out'], 'v_ssm_w_in': out['v_ssm_w_in'], 'v_ssm_lambda_re': out['v_ssm_lambda_re'], 'v_ssm_lambda_im': out['v_ssm_lambda_im'], 'v_ssm_log_dt': out['v_ssm_log_dt'], 'v_ssm_b_re': out['v_ssm_b_re'], 'v_ssm_b_im': out['v_ssm_b_im'], 'v_ssm_c_re': out['v_ssm_c_re'], 'v_ssm_c_im': out['v_ssm_c_im'], 'v_ssm_d': out['v_ssm_d'], 'v_ssm_w_out': out['v_ssm_w_out']}


def _loss(weights, diff, rest, loss_target):
    with _jax.named_scope("forward"):
        args = {**rest, TWIN_DIFF_INPUT: diff, **{k: w.astype(_WEIGHT_DTYPES[k]) for k, w in weights.items()}}
        y = _forward(args)
    with _jax.named_scope("loss_head"):
        err = _jnp.square(y.astype(_jnp.float32) - loss_target)
        return 0.5 * _jnp.sum(_jnp.mean(err, axis=-1)) if err.ndim else 0.5 * err


def _adamw(w, g, m, v):
    m = ADAM_B1 * m + (1.0 - ADAM_B1) * g
    v = ADAM_B2 * v + (1.0 - ADAM_B2) * _jnp.square(g)
    m_hat = m / (1.0 - ADAM_B1 ** ADAM_STEP)
    v_hat = v / (1.0 - ADAM_B2 ** ADAM_STEP)
    delta = -ADAM_LR * (m_hat / (_jnp.sqrt(v_hat) + ADAM_EPS) + ADAM_WD * w)
    return delta, m, v


def reference(x, p, ffn1_norm, ffn1_w_gate, ffn1_w_up, ffn1_w_down, mix_norm, ffn2_norm, ffn2_w_gate, ffn2_w_up, ffn2_w_down, ple_norm, ple_w_gate, ple_w_proj, ab_w_in, att_q_gain, att_k_gain, att_rel_bias, rwkv_mu, rwkv_w0, rwkv_w_up, rwkv_a0, rwkv_a_up, rwkv_g_up, rwkv_k_k, rwkv_k_a, rwkv_r_k, rwkv_lnx_w, rwkv_lnx_b, ab_w_out, ssm_w_in, ssm_lambda_re, ssm_lambda_im, ssm_log_dt, ssm_b_re, ssm_b_im, ssm_c_re, ssm_c_im, ssm_d, ssm_w_out, loss_target, m_ffn1_norm, m_ffn1_w_gate, m_ffn1_w_up, m_ffn1_w_down, m_mix_norm, m_ffn2_norm, m_ffn2_w_gate, m_ffn2_w_up, m_ffn2_w_down, m_ple_norm, m_ple_w_gate, m_ple_w_proj, m_ab_w_in, m_att_q_gain, m_att_k_gain, m_att_rel_bias, m_rwkv_mu, m_rwkv_w0, m_rwkv_w_up, m_rwkv_a0, m_rwkv_a_up, m_rwkv_g_up, m_rwkv_k_k, m_rwkv_k_a, m_rwkv_r_k, m_rwkv_lnx_w, m_rwkv_lnx_b, m_ab_w_out, m_ssm_w_in, m_ssm_lambda_re, m_ssm_lambda_im, m_ssm_log_dt, m_ssm_b_re, m_ssm_b_im, m_ssm_c_re, m_ssm_c_im, m_ssm_d, m_ssm_w_out, v_ffn1_norm, v_ffn1_w_gate, v_ffn1_w_up, v_ffn1_w_down, v_mix_norm, v_ffn2_norm, v_ffn2_w_gate, v_ffn2_w_up, v_ffn2_w_down, v_ple_norm, v_ple_w_gate, v_ple_w_proj, v_ab_w_in, v_att_q_gain, v_att_k_gain, v_att_rel_bias, v_rwkv_mu, v_rwkv_w0, v_rwkv_w_up, v_rwkv_a0, v_rwkv_a_up, v_rwkv_g_up, v_rwkv_k_k, v_rwkv_k_a, v_rwkv_r_k, v_rwkv_lnx_w, v_rwkv_lnx_b, v_ab_w_out, v_ssm_w_in, v_ssm_lambda_re, v_ssm_lambda_im, v_ssm_log_dt, v_ssm_b_re, v_ssm_b_im, v_ssm_c_re, v_ssm_c_im, v_ssm_d, v_ssm_w_out):
    given = dict(x=x, p=p, ffn1_norm=ffn1_norm, ffn1_w_gate=ffn1_w_gate, ffn1_w_up=ffn1_w_up, ffn1_w_down=ffn1_w_down, mix_norm=mix_norm, ffn2_norm=ffn2_norm, ffn2_w_gate=ffn2_w_gate, ffn2_w_up=ffn2_w_up, ffn2_w_down=ffn2_w_down, ple_norm=ple_norm, ple_w_gate=ple_w_gate, ple_w_proj=ple_w_proj, ab_w_in=ab_w_in, att_q_gain=att_q_gain, att_k_gain=att_k_gain, att_rel_bias=att_rel_bias, rwkv_mu=rwkv_mu, rwkv_w0=rwkv_w0, rwkv_w_up=rwkv_w_up, rwkv_a0=rwkv_a0, rwkv_a_up=rwkv_a_up, rwkv_g_up=rwkv_g_up, rwkv_k_k=rwkv_k_k, rwkv_k_a=rwkv_k_a, rwkv_r_k=rwkv_r_k, rwkv_lnx_w=rwkv_lnx_w, rwkv_lnx_b=rwkv_lnx_b, ab_w_out=ab_w_out, ssm_w_in=ssm_w_in, ssm_lambda_re=ssm_lambda_re, ssm_lambda_im=ssm_lambda_im, ssm_log_dt=ssm_log_dt, ssm_b_re=ssm_b_re, ssm_b_im=ssm_b_im, ssm_c_re=ssm_c_re, ssm_c_im=ssm_c_im, ssm_d=ssm_d, ssm_w_out=ssm_w_out, loss_target=loss_target, m_ffn1_norm=m_ffn1_norm, m_ffn1_w_gate=m_ffn1_w_gate, m_ffn1_w_up=m_ffn1_w_up, m_ffn1_w_down=m_ffn1_w_down, m_mix_norm=m_mix_norm, m_ffn2_norm=m_ffn2_norm, m_ffn2_w_gate=m_ffn2_w_gate, m_ffn2_w_up=m_ffn2_w_up, m_ffn2_w_down=m_ffn2_w_down, m_ple_norm=m_ple_norm, m_ple_w_gate=m_ple_w_gate, m_ple_w_proj=m_ple_w_proj, m_ab_w_in=m_ab_w_in, m_att_q_gain=m_att_q_gain, m_att_k_gain=m_att_k_gain, m_att_rel_bias=m_att_rel_bias, m_rwkv_mu=m_rwkv_mu, m_rwkv_w0=m_rwkv_w0, m_rwkv_w_up=m_rwkv_w_up, m_rwkv_a0=m_rwkv_a0, m_rwkv_a_up=m_rwkv_a_up, m_rwkv_g_up=m_rwkv_g_up, m_rwkv_k_k=m_rwkv_k_k, m_rwkv_k_a=m_rwkv_k_a, m_rwkv_r_k=m_rwkv_r_k, m_rwkv_lnx_w=m_rwkv_lnx_w, m_rwkv_lnx_b=m_rwkv_lnx_b, m_ab_w_out=m_ab_w_out, m_ssm_w_in=m_ssm_w_in, m_ssm_lambda_re=m_ssm_lambda_re, m_ssm_lambda_im=m_ssm_lambda_im, m_ssm_log_dt=m_ssm_log_dt, m_ssm_b_re=m_ssm_b_re, m_ssm_b_im=m_ssm_b_im, m_ssm_c_re=m_ssm_c_re, m_ssm_c_im=m_ssm_c_im, m_ssm_d=m_ssm_d, m_ssm_w_out=m_ssm_w_out, v_ffn1_norm=v_ffn1_norm, v_ffn1_w_gate=v_ffn1_w_gate, v_ffn1_w_up=v_ffn1_w_up, v_ffn1_w_down=v_ffn1_w_down, v_mix_norm=v_mix_norm, v_ffn2_norm=v_ffn2_norm, v_ffn2_w_gate=v_ffn2_w_gate, v_ffn2_w_up=v_ffn2_w_up, v_ffn2_w_down=v_ffn2_w_down, v_ple_norm=v_ple_norm, v_ple_w_gate=v_ple_w_gate, v_ple_w_proj=v_ple_w_proj, v_ab_w_in=v_ab_w_in, v_att_q_gain=v_att_q_gain, v_att_k_gain=v_att_k_gain, v_att_rel_bias=v_att_rel_bias, v_rwkv_mu=v_rwkv_mu, v_rwkv_w0=v_rwkv_w0, v_rwkv_w_up=v_rwkv_w_up, v_rwkv_a0=v_rwkv_a0, v_rwkv_a_up=v_rwkv_a_up, v_rwkv_g_up=v_rwkv_g_up, v_rwkv_k_k=v_rwkv_k_k, v_rwkv_k_a=v_rwkv_k_a, v_rwkv_r_k=v_rwkv_r_k, v_rwkv_lnx_w=v_rwkv_lnx_w, v_rwkv_lnx_b=v_rwkv_lnx_b, v_ab_w_out=v_ab_w_out, v_ssm_w_in=v_ssm_w_in, v_ssm_lambda_re=v_ssm_lambda_re, v_ssm_lambda_im=v_ssm_lambda_im, v_ssm_log_dt=v_ssm_log_dt, v_ssm_b_re=v_ssm_b_re, v_ssm_b_im=v_ssm_b_im, v_ssm_c_re=v_ssm_c_re, v_ssm_c_im=v_ssm_c_im, v_ssm_d=v_ssm_d, v_ssm_w_out=v_ssm_w_out)
    weights = {n: given[n] for n in TWIN_WEIGHTS}
    shared = {n: given[n] for n in SHARED_INPUTS}
    per_example = {n: given[n] for n in ['x', 'p']}
    grad_fn = _jax.value_and_grad(_loss, argnums=(0, 1))

    def one_microbatch(ex, loss_target):
        ex = dict(ex)
        diff = ex.pop(TWIN_DIFF_INPUT)
        return grad_fn(weights, diff, {**shared, **ex}, loss_target)

    if N_MICROBATCH == 1:
        loss, (grad_w, grad_x) = one_microbatch(per_example, given["loss_target"])
    else:
        def body(carry, xs):
            loss_sum, grad_sum = carry
            l_k, (gw_k, gx_k) = one_microbatch(xs[0], xs[1])
            with _jax.named_scope("update"):
                return (loss_sum + l_k, _jax.tree.map(_jnp.add, grad_sum, gw_k)), gx_k

        init = (_jnp.zeros((), _jnp.float32), _jax.tree.map(_jnp.zeros_like, weights))
        (loss, grad_w), grad_x = _jax.lax.scan(body, init, (per_example, given["loss_target"]))
    with _jax.named_scope("update"):
        delta_w, new_m, new_v = {}, {}, {}
        for n in TWIN_WEIGHTS:
            delta_w[n], new_m[n], new_v[n] = _adamw(weights[n], grad_w[n], given["m_" + n], given["v_" + n])
    return (loss, grad_x, *[grad_w[n] for n in TWIN_WEIGHTS], *[delta_w[n] for n in TWIN_WEIGHTS],
            *[new_m[n] for n in TWIN_WEIGHTS], *[new_v[n] for n in TWIN_WEIGHTS])
```

```python
import functools
import math

import numpy as np
import jax
import jax.numpy as jnp
from jax import lax
from jax.experimental import pallas as pl
from jax.experimental.pallas import tpu as pltpu

F32 = jnp.float32
BF16 = jnp.bfloat16
HI = lax.Precision.HIGHEST
MESH = pl.DeviceIdType.MESH

CHUNK = 64
N_LEFT = 8
BAND = (N_LEFT + 1) * CHUNK
PAD = N_LEFT * CHUNK
HEAD = 64
REL_CLIP = 128
N_REL = (CHUNK - 1) + REL_CLIP + 1
DECAY_LORA = 64
AAA_LORA = 64
GATE_LORA = 128
SSM_GROUP = 16
SSM_STATE = 64
RMS_EPS = 1e-6
GN_EPS = 64e-5
NEG_BIG = -1e30

ADAM_LR = 0.001
ADAM_B1 = 0.9
ADAM_B2 = 0.999
ADAM_EPS = 1e-08
ADAM_WD = 0.01
ADAM_STEP = 10

RW_CHUNK = 64
RW_HEADS = 4
ATT_HEADS = 2
VMEM_LIMIT = 56 * 1024 * 1024
ROW_BLOCK_BYTES = 6 * 1024 * 1024
GRAD_XFER = F32

W_NAMES = ['ffn1_norm', 'ffn1_w_gate', 'ffn1_w_up', 'ffn1_w_down', 'mix_norm', 'ffn2_norm', 'ffn2_w_gate',
           'ffn2_w_up', 'ffn2_w_down', 'ple_norm', 'ple_w_gate', 'ple_w_proj', 'ab_w_in', 'att_q_gain',
           'att_k_gain', 'att_rel_bias', 'rwkv_mu', 'rwkv_w0', 'rwkv_w_up', 'rwkv_a0', 'rwkv_a_up',
           'rwkv_g_up', 'rwkv_k_k', 'rwkv_k_a', 'rwkv_r_k', 'rwkv_lnx_w', 'rwkv_lnx_b', 'ab_w_out',
           'ssm_w_in', 'ssm_lambda_re', 'ssm_lambda_im', 'ssm_log_dt', 'ssm_b_re', 'ssm_b_im', 'ssm_c_re',
           'ssm_c_im', 'ssm_d', 'ssm_w_out']
BIG = ['ffn1_w_gate', 'ffn1_w_up', 'ffn1_w_down', 'ffn2_w_gate', 'ffn2_w_up', 'ffn2_w_down',
       'ple_w_gate', 'ple_w_proj', 'ab_w_in', 'ab_w_out', 'ssm_w_in', 'ssm_w_out']
SMALL_SHARDED = ['rwkv_w_up', 'rwkv_a_up', 'rwkv_g_up', 'ssm_d']


def _cparams(n_axes):
    return pltpu.CompilerParams(dimension_semantics=("arbitrary",) * n_axes, vmem_limit_bytes=VMEM_LIMIT)


def _pick(n, prefs):
    for p in prefs:
        if n % p == 0:
            return p
    return n


def mm(a, b, *, ta=False, tb=False, nshard=None, out3=False, out_dtype=F32, res=None, alpha=1.0, name):
    a3, b3 = a.ndim == 3, b.ndim == 3
    if a3:
        assert not ta
        sk, m, ks = a.shape
        k = sk * ks
    elif ta:
        k, m = a.shape
    else:
        m, k = a.shape
    kshard = None
    if b3 and not tb:
        s, kb, ns = b.shape
        n = s * ns
        nshard = s
    elif b3 and tb:
        sk2, n, ks2 = b.shape
        kb = sk2 * ks2
        kshard = (sk2, ks2)
    elif tb:
        n, kb = b.shape
    else:
        kb, n = b.shape
    assert k == kb, (a.shape, b.shape, ta, tb)
    if a3:
        assert kshard is None or kshard == (sk, ks)
        kshard = (sk, ks)
    if nshard is not None:
        tn, nj = n // nshard, nshard
    else:
        assert not out3
        tn = _pick(n, (1024, 1408, 1280, 512, 640, 256, 128))
        nj = n // tn
    if kshard is not None:
        nk, tk = kshard
    else:
        tk = _pick(k, (512, 256, 128))
        nk = k // tk
    tm = _pick(m, (512, 256, 128)) if tn > 1024 else _pick(m, (1024, 512, 256, 128))
    ni = m // tm

    if a3:
        a_spec = pl.BlockSpec((None, tm, tk), lambda i, j, kk: (kk, i, 0))
    elif ta:
        a_spec = pl.BlockSpec((tk, tm), lambda i, j, kk: (kk, i))
    else:
        a_spec = pl.BlockSpec((tm, tk), lambda i, j, kk: (i, kk))
    if b3 and not tb:
        b_spec = pl.BlockSpec((None, tk, tn), lambda i, j, kk: (j, kk, 0))
    elif b3 and tb:
        b_spec = pl.BlockSpec((None, tn, tk), lambda i, j, kk: (kk, j, 0))
    elif tb:
        b_spec = pl.BlockSpec((tn, tk), lambda i, j, kk: (j, kk))
    else:
        b_spec = pl.BlockSpec((tk, tn), lambda i, j, kk: (kk, j))
    if out3:
        o_spec = pl.BlockSpec((None, tm, tn), lambda i, j, kk: (j, i, 0))
        o_shape = (nj, m, tn)
    else:
        o_spec = pl.BlockSpec((tm, tn), lambda i, j, kk: (i, j))
        o_shape = (m, n)
    has_res = res is not None
    dn = (((0 if ta else 1,), (1 if tb else 0,)), ((), ()))

    def body(*refs):
        if has_res:
            a_ref, b_ref, r_ref, o_ref, acc_ref = refs
        else:
            a_ref, b_ref, o_ref, acc_ref = refs
        kk = pl.program_id(2)

        @pl.when(kk == 0)
        def _():
            acc_ref[...] = jnp.zeros_like(acc_ref)

        acc_ref[...] += lax.dot_general(a_ref[...].astype(BF16), b_ref[...].astype(BF16), dn,
                                        preferred_element_type=F32)

        @pl.when(kk == nk - 1)
        def _():
            val = acc_ref[...] * alpha if alpha != 1.0 else acc_ref[...]
            if has_res:
                val = val + r_ref[...].astype(F32)
            o_ref[...] = val.astype(o_ref.dtype)

    in_specs = [a_spec, b_spec] + ([o_spec] if has_res else [])
    args = (a, b) + ((res,) if has_res else ())
    return pl.pallas_call(
        body, name=name, grid=(ni, nj, nk), in_specs=in_specs, out_specs=o_spec,
        out_shape=jax.ShapeDtypeStruct(o_shape, out_dtype),
        scratch_shapes=[pltpu.VMEM((tm, tn), F32)], compiler_params=_cparams(3))(*args)


def rowk(fn, rows, consts, out_rows, out_accs=(), *, name, tb=None):
    t = rows[0].shape[0]
    nr, nc, no, na = len(rows), len(consts), len(out_rows), len(out_accs)
    if tb is None:
        per_row = sum(r.shape[1] * 4 for r in rows) + sum(n * 4 for n, _ in out_rows)
        tb = 8
        while tb * 2 <= min(t, 1024) and tb * 2 * per_row <= ROW_BLOCK_BYTES and t % (tb * 2) == 0:
            tb *= 2
    assert t % tb == 0
    nb = t // tb

    def body(*refs):
        r_in, c_in = refs[:nr], refs[nr:nr + nc]
        o_rows, o_accs = refs[nr + nc:nr + nc + no], refs[nr + nc + no:]
        outs = fn(*[r[...] for r in r_in], *[c[...] for c in c_in])
        if not isinstance(outs, (tuple, list)):
            outs = (outs,)
        assert len(outs) == no + na, (name, len(outs), no, na)
        for ref, v in zip(o_rows, outs[:no]):
            ref[...] = v.astype(ref.dtype)
        if na:
            @pl.when(pl.program_id(0) == 0)
            def _():
                for ref in o_accs:
                    ref[...] = jnp.zeros_like(ref)
            for ref, v in zip(o_accs, outs[no:]):
                ref[...] += v.astype(F32)

    in_specs = [pl.BlockSpec((tb, r.shape[1]), lambda i: (i, 0)) for r in rows]
    in_specs += [pl.BlockSpec(c.shape, lambda i, nd=c.ndim: (0,) * nd) for c in consts]
    out_specs = [pl.BlockSpec((tb, n), lambda i: (i, 0)) for n, _ in out_rows]
    out_specs += [pl.BlockSpec(s, lambda i, nd=len(s): (0,) * nd) for s in out_accs]
    out_shape = [jax.ShapeDtypeStruct((t, n), d) for n, d in out_rows]
    out_shape += [jax.ShapeDtypeStruct(s, F32) for s in out_accs]
    res = pl.pallas_call(body, name=name, grid=(nb,), in_specs=in_specs, out_specs=out_specs,
                         out_shape=out_shape, compiler_params=_cparams(1))(*rows, *consts)
    return res


def _f32(*xs):
    return [x.astype(F32) for x in xs]


def vjp_rows(f, n_rows, n_cots):
    def fn(*args):
        rows = _f32(*args[:n_rows])
        cots = _f32(*args[n_rows:n_rows + n_cots])
        consts = _f32(*args[n_rows + n_cots:])
        outs, pull = jax.vjp(f, *rows, *consts)
        if not isinstance(outs, (tuple, list)):
            cots = cots[0]
        else:
            cots = tuple(cots)
        return pull(cots)
    return fn


def hdot(x, y):
    return jnp.dot(x, y, precision=HI, preferred_element_type=F32)


def f_rms(h, g):
    return h * lax.rsqrt(jnp.mean(h * h, axis=-1, keepdims=True) + RMS_EPS) * g


def f_sigmoid(x):
    return 1.0 / (1.0 + jnp.exp(-x))


def f_swiglu(a, b):
    return a * f_sigmoid(a) * b


def f_softplus(x):
    return jnp.maximum(x, 0.0) + jnp.log(1.0 + jnp.exp(-jnp.abs(x)))


def f_gelu(x):
    return 0.5 * x * (1.0 + jnp.tanh(math.sqrt(2.0 / math.pi) * (x + 0.044715 * (x * x * x))))


def f_rwkv_pre(zs, w0, wup_p, a0, aup_p, g_up, k_k, k_a, e, et):
    d = w0.shape[1]
    r, k, v = zs[:, :d], zs[:, d:2 * d], zs[:, 2 * d:3 * d]
    xwa = zs[:, 3 * d:3 * d + DECAY_LORA + AAA_LORA]
    xg = zs[:, 3 * d + DECAY_LORA + AAA_LORA:]
    w_log = -f_softplus(-(w0 + hdot(jnp.tanh(xwa), wup_p))) - 0.5
    logw = -jnp.exp(w_log)
    a = f_sigmoid(a0 + hdot(xwa, aup_p))
    g = hdot(f_sigmoid(xg), g_up)
    kk = k * k_k
    nrm = jnp.maximum(jnp.sqrt(hdot(kk * kk, e)), 1e-12)
    kk = kk * hdot(1.0 / nrm, et)
    k2 = k * (1.0 + (a - 1.0) * k_a)
    return r, logw, k2, v, -kk, kk * a, g


def f_rwkv_post(y, r, k2, v, g, r_k, lnx_w, lnx_b, e, et):
    inv = 1.0 / HEAD
    mean = hdot(hdot(y, e) * inv, et)
    yc = y - mean
    var = hdot(yc * yc, e) * inv
    yn = yc * hdot(lax.rsqrt(var + GN_EPS), et) * lnx_w + lnx_b
    bonus = hdot(hdot(r * k2 * r_k, e), et) * v
    return (yn + bonus) * g


def f_ssm_ab(lr, li, log_dt):
    dt = jnp.exp(log_dt)
    mag = jnp.exp(lr * dt)
    ab_re, ab_im = mag * jnp.cos(li * dt), mag * jnp.sin(li * dt)
    denom = lr * lr + li * li
    z_re = ((ab_re - 1.0) * lr + ab_im * li) / denom
    z_im = (ab_im * lr - (ab_re - 1.0) * li) / denom
    return ab_re, ab_im, z_re, z_im


def f_ssm_bb(br, bi, z_re, z_im):
    return z_re * br - z_im * bi, z_re * bi + z_im * br


def _col_block(n):
    return _pick(n, (256, 128))


def ts_fwd(proj, col0, width, mu, *, name):
    t = proj.shape[0]
    cb = _col_block(width)
    assert col0 % cb == 0 and width % cb == 0
    off = col0 // cb

    def body(z_ref, mu_ref, o_ref):
        z = z_ref[...]
        row = lax.broadcasted_iota(jnp.int32, z.shape, 0)
        prev = jnp.where(row == 0, 0.0, pltpu.roll(z, 1, 0))
        o_ref[...] = z + (prev - z) * mu_ref[...]

    return pl.pallas_call(
        body, name=name, grid=(width // cb,),
        in_specs=[pl.BlockSpec((t, cb), lambda j: (0, j + off)), pl.BlockSpec((1, cb), lambda j: (0, j))],
        out_specs=pl.BlockSpec((t, cb), lambda j: (0, j)),
        out_shape=jax.ShapeDtypeStruct((t, width), F32), compiler_params=_cparams(1))(proj, mu)


def ts_bwd(proj, col0, dzs, mu, *, name):
    t, width = dzs.shape
    cb = _col_block(width)
    off = col0 // cb

    def body(z_ref, d_ref, mu_ref, dz_ref, dmu_ref):
        z, d, m = z_ref[...], d_ref[...], mu_ref[...]
        row = lax.broadcasted_iota(jnp.int32, z.shape, 0)
        prev = jnp.where(row == 0, 0.0, pltpu.roll(z, 1, 0))
        dm = d * m
        nxt = jnp.where(row == t - 1, 0.0, pltpu.roll(dm, t - 1, 0))
        dz_ref[...] = d - dm + nxt
        dmu_ref[...] = jnp.sum(d * (prev - z), axis=0, keepdims=True)

    return pl.pallas_call(
        body, name=name, grid=(width // cb,),
        in_specs=[pl.BlockSpec((t, cb), lambda j: (0, j + off)), pl.BlockSpec((t, cb), lambda j: (0, j)),
                  pl.BlockSpec((1, cb), lambda j: (0, j))],
        out_specs=[pl.BlockSpec((t, cb), lambda j: (0, j)), pl.BlockSpec((1, cb), lambda j: (0, j))],
        out_shape=[jax.ShapeDtypeStruct((t, width), F32), jax.ShapeDtypeStruct((1, width), F32)],
        compiler_params=_cparams(1))(proj, dzs, mu)


def _att_scores(qn, kb, bias, c):
    s = jnp.einsum('hqd,hkd->hqk', qn, kb, preferred_element_type=F32) + bias
    col = lax.broadcasted_iota(jnp.int32, s.shape, 2)
    s = jnp.where(col >= PAD - c * CHUNK, s, NEG_BIG)
    s = s - jnp.max(s, axis=-1, keepdims=True)
    e = jnp.exp(s)
    return e / jnp.sum(e, axis=-1, keepdims=True)


def att_fwd(qn, knp, vp, bias, *, name):
    h, t, _ = qn.shape
    hb = ATT_HEADS
    nc = t // CHUNK

    def body(q_ref, k_ref, v_ref, b_ref, o_ref):
        c = pl.program_id(1)
        start = pl.multiple_of(c * CHUNK, CHUNK)
        kb = k_ref[:, pl.ds(start, BAND), :]
        vb = v_ref[:, pl.ds(start, BAND), :]
        p = _att_scores(q_ref[...], kb, b_ref[...], c)
        o_ref[...] = jnp.einsum('hqk,hkd->hqd', p.astype(BF16), vb, preferred_element_type=F32).astype(o_ref.dtype)

    return pl.pallas_call(
        body, name=name, grid=(h // hb, nc),
        in_specs=[pl.BlockSpec((hb, CHUNK, HEAD), lambda g, c: (g, c, 0)),
                  pl.BlockSpec((hb, t + PAD, HEAD), lambda g, c: (g, 0, 0)),
                  pl.BlockSpec((hb, t + PAD, HEAD), lambda g, c: (g, 0, 0)),
                  pl.BlockSpec((hb, CHUNK, BAND), lambda g, c: (g, 0, 0))],
        out_specs=pl.BlockSpec((hb, CHUNK, HEAD), lambda g, c: (g, c, 0)),
        out_shape=jax.ShapeDtypeStruct((h, t, HEAD), BF16), compiler_params=_cparams(2))(qn, knp, vp, bias)


def att_bwd(qn, knp, vp, bias, do, *, name):
    h, t, _ = qn.shape
    hb = ATT_HEADS
    nc = t // CHUNK

    def body(q_ref, k_ref, v_ref, b_ref, do_ref, dq_ref, dk_ref, dv_ref, db_ref):
        c = pl.program_id(1)

        @pl.when(c == 0)
        def _():
            dk_ref[...] = jnp.zeros_like(dk_ref)
            dv_ref[...] = jnp.zeros_like(dv_ref)
            db_ref[...] = jnp.zeros_like(db_ref)

        start = pl.multiple_of(c * CHUNK, CHUNK)
        qv = q_ref[...]
        kb = k_ref[:, pl.ds(start, BAND), :]
        vb = v_ref[:, pl.ds(start, BAND), :]
        p = _att_scores(qv, kb, b_ref[...], c)
        dov = do_ref[...]
        dp = jnp.einsum('hqd,hkd->hqk', dov, vb, preferred_element_type=F32)
        ds = p * (dp - jnp.sum(p * dp, axis=-1, keepdims=True))
        db_ref[...] += ds
        dsb = ds.astype(BF16)
        dq_ref[...] = jnp.einsum('hqk,hkd->hqd', dsb, kb, preferred_element_type=F32)
        dst = jnp.swapaxes(dsb, 1, 2)
        pt = jnp.swapaxes(p.astype(BF16), 1, 2)
        dk_ref[:, pl.ds(start, BAND), :] += jnp.einsum('hkq,hqd->hkd', dst, qv, preferred_element_type=F32)
        dv_ref[:, pl.ds(start, BAND), :] += jnp.einsum('hkq,hqd->hkd', pt, dov, preferred_element_type=F32)

    blk_q = pl.BlockSpec((hb, CHUNK, HEAD), lambda g, c: (g, c, 0))
    blk_k = pl.BlockSpec((hb, t + PAD, HEAD), lambda g, c: (g, 0, 0))
    blk_b = pl.BlockSpec((hb, CHUNK, BAND), lambda g, c: (g, 0, 0))
    return pl.pallas_call(
        body, name=name, grid=(h // hb, nc),
        in_specs=[blk_q, blk_k, blk_k, blk_b, blk_q],
        out_specs=[blk_q, blk_k, blk_k, blk_b],
        out_shape=[jax.ShapeDtypeStruct((h, t, HEAD), F32), jax.ShapeDtypeStruct((h, t + PAD, HEAD), F32),
                   jax.ShapeDtypeStruct((h, t + PAD, HEAD), F32), jax.ShapeDtypeStruct((h, CHUNK, BAND), F32)],
        compiler_params=_cparams(2))(qn, knp, vp, bias, do)


def _rel_index():
    i = np.arange(CHUNK)[:, None]
    j = np.arange(BAND)[None, :]
    return np.clip(i + PAD - j, -(CHUNK - 1), REL_CLIP) + (CHUNK - 1)


def relbias_reduce(dbias, *, name):
    h = dbias.shape[0]
    onehot = jnp.asarray((_rel_index()[:, :, None] == np.arange(N_REL)[None, None, :]).astype(np.float32))
    dbt = jnp.swapaxes(dbias, 0, 1)

    def body(d_ref, oh_ref, o_ref):
        @pl.when(pl.program_id(0) == 0)
        def _():
            o_ref[...] = jnp.zeros_like(o_ref)
        o_ref[...] += hdot(d_ref[...], oh_ref[...])

    return pl.pallas_call(
        body, name=name, grid=(CHUNK,),
        in_specs=[pl.BlockSpec((None, h, BAND), lambda i: (i, 0, 0)),
                  pl.BlockSpec((None, BAND, N_REL), lambda i: (i, 0, 0))],
        out_specs=pl.BlockSpec((h, N_REL), lambda i: (0, 0)),
        out_shape=jax.ShapeDtypeStruct((h, N_REL), F32), compiler_params=_cparams(1))(dbt, onehot)


def _bt(x):
    return jnp.swapaxes(x, 1, 2)


def _bmm_raw(x, y):
    return lax.dot_general(x, y, (((2,), (1,)), ((0,), (0,))), precision=HI, preferred_element_type=F32)


@jax.custom_vjp
def bmm(x, y):
    return _bmm_raw(x, y)


def _bmm_fwd(x, y):
    return _bmm_raw(x, y), (x, y)


def _bmm_bwd(saved, dz):
    x, y = saved
    return _bmm_raw(dz, _bt(y)), _bmm_raw(_bt(x), dz)


bmm.defvjp(_bmm_fwd, _bmm_bwd)


def rwkv_chunk(p0, r, lw, k, v, a, b):
    g, c, n = r.shape
    row = lax.broadcasted_iota(jnp.int32, (g, c, c), 1)
    col = lax.broadcasted_iota(jnp.int32, (g, c, c), 2)
    incl, strict = row >= col, row > col
    cs = bmm(incl.astype(F32), lw)
    cs_end = cs[:, c - 1:c, :]
    e_cs = jnp.exp(cs)
    e_neg = jnp.exp(-cs)
    at = a * jnp.exp(cs - lw)
    rt = r * e_cs
    bt_, kt = b * e_neg, k * e_neg
    e_tail = jnp.exp(cs_end - cs)
    bh, kh = b * e_tail, k * e_tail
    btt, ktt = _bt(bt_), _bt(kt)
    a_ab = jnp.where(strict, bmm(at, btt), 0.0)
    a_ak = jnp.where(strict, bmm(at, ktt), 0.0)
    a_rb = jnp.where(incl, bmm(rt, btt), 0.0)
    a_rk = jnp.where(incl, bmm(rt, ktt), 0.0)
    tinv = jnp.where(row == col, 1.0, 0.0) + a_ab
    npow = a_ab
    for _ in range(int(math.log2(c)) - 1):
        npow = bmm(npow, npow)
        tinv = tinv + bmm(tinv, npow)
    u = bmm(tinv, bmm(at, p0) + bmm(a_ak, v))
    y = bmm(rt, p0) + bmm(a_rb, u) + bmm(a_rk, v)
    rown = lax.broadcasted_iota(jnp.int32, (g, n, n), 1)
    coln = lax.broadcasted_iota(jnp.int32, (g, n, n), 2)
    dg = jnp.where(rown == coln, jnp.exp(cs_end), 0.0)
    p1 = bmm(dg, p0) + bmm(_bt(bh), u) + bmm(_bt(kh), v)
    return y, p1


def rwkv_fwd(r, lw, k, v, a, b, *, name):
    h, t, n = r.shape
    g, c = min(RW_HEADS, h), RW_CHUNK
    nch = t // c

    def body(r_ref, lw_ref, k_ref, v_ref, a_ref, b_ref, y_ref, p_ref, st_ref):
        @pl.when(pl.program_id(1) == 0)
        def _():
            st_ref[...] = jnp.zeros_like(st_ref)
        p0 = st_ref[...]
        p_ref[...] = p0
        y, p1 = rwkv_chunk(p0, r_ref[...], lw_ref[...], k_ref[...], v_ref[...], a_ref[...], b_ref[...])
        y_ref[...] = y
        st_ref[...] = p1

    blk = pl.BlockSpec((g, c, n), lambda i, j: (i, j, 0))
    pblk = pl.BlockSpec((g, None, n, n), lambda i, j: (i, j, 0, 0))
    return pl.pallas_call(
        body, name=name, grid=(h // g, nch), in_specs=[blk] * 6, out_specs=[blk, pblk],
        out_shape=[jax.ShapeDtypeStruct((h, t, n), F32), jax.ShapeDtypeStruct((h, nch, n, n), F32)],
        scratch_shapes=[pltpu.VMEM((g, n, n), F32)], compiler_params=_cparams(2))(r, lw, k, v, a, b)


def rwkv_bwd(r, lw, k, v, a, b, p0s, dy, *, name):
    h, t, n = r.shape
    g, c = min(RW_HEADS, h), RW_CHUNK
    nch = t // c

    def body(r_ref, lw_ref, k_ref, v_ref, a_ref, b_ref, p_ref, dy_ref,
             dr_ref, dlw_ref, dk_ref, dv_ref, da_ref, db_ref, dp_ref):
        @pl.when(pl.program_id(1) == 0)
        def _():
            dp_ref[...] = jnp.zeros_like(dp_ref)
        _, pull = jax.vjp(rwkv_chunk, p_ref[...], r_ref[...], lw_ref[...], k_ref[...], v_ref[...],
                          a_ref[...], b_ref[...])
        dp0, dr, dlw, dk, dv, da, db = pull((dy_ref[...], dp_ref[...]))
        dr_ref[...] = dr
        dlw_ref[...] = dlw
        dk_ref[...] = dk
        dv_ref[...] = dv
        da_ref[...] = da
        db_ref[...] = db
        dp_ref[...] = dp0

    blk = pl.BlockSpec((g, c, n), lambda i, j: (i, nch - 1 - j, 0))
    pblk = pl.BlockSpec((g, None, n, n), lambda i, j: (i, nch - 1 - j, 0, 0))
    return pl.pallas_call(
        body, name=name, grid=(h // g, nch), in_specs=[blk] * 6 + [pblk, blk], out_specs=[blk] * 6,
        out_shape=[jax.ShapeDtypeStruct((h, t, n), F32)] * 6,
        scratch_shapes=[pltpu.VMEM((g, n, n), F32)], compiler_params=_cparams(2))(r, lw, k, v, a, b, p0s, dy)


def _time_block(t):
    return _pick(t, (256, 128, 64))


def ssm_scan_fwd(bu_re, bu_im, a_re, a_im, *, name):
    t, rr, ln = bu_re.shape
    tb = _time_block(t)

    def body(br_ref, bi_ref, ar_ref, ai_ref, hr_ref, hi_ref, sr_ref, si_ref):
        @pl.when(pl.program_id(0) == 0)
        def _():
            sr_ref[...] = jnp.zeros_like(sr_ref)
            si_ref[...] = jnp.zeros_like(si_ref)
        ar, ai = ar_ref[...], ai_ref[...]

        def step(i, carry):
            hr, hi = carry
            nr = ar * hr - ai * hi + br_ref[i]
            ni = ar * hi + ai * hr + bi_ref[i]
            hr_ref[i] = nr
            hi_ref[i] = ni
            return nr, ni

        hr, hi = lax.fori_loop(0, tb, step, (sr_ref[...], si_ref[...]))
        sr_ref[...] = hr
        si_ref[...] = hi

    blk = pl.BlockSpec((tb, rr, ln), lambda i: (i, 0, 0))
    cblk = pl.BlockSpec((rr, ln), lambda i: (0, 0))
    return pl.pallas_call(
        body, name=name, grid=(t // tb,), in_specs=[blk, blk, cblk, cblk], out_specs=[blk, blk],
        out_shape=[jax.ShapeDtypeStruct((t, rr, ln), F32)] * 2,
        scratch_shapes=[pltpu.VMEM((rr, ln), F32)] * 2, compiler_params=_cparams(1))(bu_re, bu_im, a_re, a_im)


def ssm_scan_bwd(dh_re, dh_im, hp_re, hp_im, a_re, a_im, *, name):
    t, rr, ln = dh_re.shape
    tb = _time_block(t)
    nb = t // tb

    def body(dr_ref, di_ref, pr_ref, pi_ref, ar_ref, ai_ref, gr_ref, gi_ref, dar_ref, dai_ref, sr_ref, si_ref):
        @pl.when(pl.program_id(0) == 0)
        def _():
            sr_ref[...] = jnp.zeros_like(sr_ref)
            si_ref[...] = jnp.zeros_like(si_ref)
            dar_ref[...] = jnp.zeros_like(dar_ref)
            dai_ref[...] = jnp.zeros_like(dai_ref)
        ar, ai = ar_ref[...], ai_ref[...]

        def step(ii, carry):
            gr, gi, dar, dai = carry
            i = tb - 1 - ii
            nr = dr_ref[i] + ar * gr + ai * gi
            ni = di_ref[i] - ai * gr + ar * gi
            gr_ref[i] = nr
            gi_ref[i] = ni
            pr, pi = pr_ref[i], pi_ref[i]
            dar = dar + nr * pr + ni * pi
            dai = dai - nr * pi + ni * pr
            return nr, ni, dar, dai

        gr, gi, dar, dai = lax.fori_loop(0, tb, step, (sr_ref[...], si_ref[...], dar_ref[...], dai_ref[...]))
        sr_ref[...] = gr
        si_ref[...] = gi
        dar_ref[...] = dar
        dai_ref[...] = dai

    blk = pl.BlockSpec((tb, rr, ln), lambda i: (nb - 1 - i, 0, 0))
    cblk = pl.BlockSpec((rr, ln), lambda i: (0, 0))
    return pl.pallas_call(
        body, name=name, grid=(nb,), in_specs=[blk] * 4 + [cblk, cblk], out_specs=[blk, blk, cblk, cblk],
        out_shape=[jax.ShapeDtypeStruct((t, rr, ln), F32)] * 2 + [jax.ShapeDtypeStruct((rr, ln), F32)] * 2,
        scratch_shapes=[pltpu.VMEM((rr, ln), F32)] * 2,
        compiler_params=_cparams(1))(dh_re, dh_im, hp_re, hp_im, a_re, a_im)


ANY = pl.BlockSpec(memory_space=pl.ANY)


def _chip_peers():
    x, y, c = lax.axis_index("x"), lax.axis_index("y"), lax.axis_index("c")
    return x, y, c, [(1 - x, y), (x, 1 - y), (1 - x, 1 - y)]


def all_gather_chips(w, *, name):
    def body(w_ref, o_ref, send_sems, recv_sems, lsem):
        x, y, c, peers = _chip_peers()
        me = 2 * x + y
        loc = pltpu.make_async_copy(w_ref, o_ref.at[me], lsem)
        loc.start()
        sends = []
        for j, (px, py) in enumerate(peers):
            cp = pltpu.make_async_remote_copy(src_ref=w_ref, dst_ref=o_ref.at[me], send_sem=send_sems.at[j],
                                              recv_sem=recv_sems.at[j], device_id=(px, py, c), device_id_type=MESH)
            cp.start()
            sends.append(cp)
        for j, (px, py) in enumerate(peers):
            pltpu.make_async_remote_copy(src_ref=w_ref, dst_ref=o_ref.at[2 * px + py], send_sem=send_sems.at[j],
                                         recv_sem=recv_sems.at[j], device_id=(px, py, c),
                                         device_id_type=MESH).wait_recv()
        for cp in sends:
            cp.wait_send()
        loc.wait()

    return pl.pallas_call(
        body, name=name, in_specs=[ANY], out_specs=ANY,
        out_shape=jax.ShapeDtypeStruct((4,) + w.shape, w.dtype),
        scratch_shapes=[pltpu.SemaphoreType.DMA((3,)), pltpu.SemaphoreType.DMA((3,)), pltpu.SemaphoreType.DMA(())],
        compiler_params=pltpu.CompilerParams(has_side_effects=True))(w)


def scatter_to_chips(g4, *, name):
    def body(g_ref, o_ref, send_sems, recv_sems, lsem):
        x, y, c, peers = _chip_peers()
        me = 2 * x + y
        loc = pltpu.make_async_copy(g_ref.at[me], o_ref.at[me], lsem)
        loc.start()
        sends = []
        for j, (px, py) in enumerate(peers):
            cp = pltpu.make_async_remote_copy(src_ref=g_ref.at[2 * px + py], dst_ref=o_ref.at[me],
                                              send_sem=send_sems.at[j], recv_sem=recv_sems.at[j],
                                              device_id=(px, py, c), device_id_type=MESH)
            cp.start()
            sends.append(cp)
        for j, (px, py) in enumerate(peers):
            pltpu.make_async_remote_copy(src_ref=g_ref.at[me], dst_ref=o_ref.at[2 * px + py],
                                         send_sem=send_sems.at[j], recv_sem=recv_sems.at[j],
                                         device_id=(px, py, c), device_id_type=MESH).wait_recv()
        for cp in sends:
            cp.wait_send()
        loc.wait()

    return pl.pallas_call(
        body, name=name, in_specs=[ANY], out_specs=ANY,
        out_shape=jax.ShapeDtypeStruct(g4.shape, g4.dtype),
        scratch_shapes=[pltpu.SemaphoreType.DMA((3,)), pltpu.SemaphoreType.DMA((3,)), pltpu.SemaphoreType.DMA(())],
        compiler_params=pltpu.CompilerParams(has_side_effects=True))(g4)


def swap_cores(s, *, name):
    def body(s_ref, o_ref, send_sem, recv_sem, lsem):
        x, y, c = lax.axis_index("x"), lax.axis_index("y"), lax.axis_index("c")
        loc = pltpu.make_async_copy(s_ref, o_ref.at[c], lsem)
        loc.start()
        cp = pltpu.make_async_remote_copy(src_ref=s_ref, dst_ref=o_ref.at[c], send_sem=send_sem, recv_sem=recv_sem,
                                          device_id=(x, y, 1 - c), device_id_type=MESH)
        cp.start()
        pltpu.make_async_remote_copy(src_ref=s_ref, dst_ref=o_ref.at[1 - c], send_sem=send_sem, recv_sem=recv_sem,
                                     device_id=(x, y, 1 - c), device_id_type=MESH).wait_recv()
        cp.wait_send()
        loc.wait()

    return pl.pallas_call(
        body, name=name, in_specs=[ANY], out_specs=ANY,
        out_shape=jax.ShapeDtypeStruct((2,) + s.shape, s.dtype),
        scratch_shapes=[pltpu.SemaphoreType.DMA(()), pltpu.SemaphoreType.DMA(()), pltpu.SemaphoreType.DMA(())],
        compiler_params=pltpu.CompilerParams(has_side_effects=True))(s)


def all_gather_devices(v, *, name):
    def body(v_ref, o_ref, send_sems, recv_sems, lsem):
        x, y, c = lax.axis_index("x"), lax.axis_index("y"), lax.axis_index("c")
        me = 4 * x + 2 * y + c
        loc = pltpu.make_async_copy(v_ref, o_ref.at[me], lsem)
        loc.start()
        peers = [((x + dx) % 2, (y + dy) % 2, (c + dc) % 2)
                 for dx in (0, 1) for dy in (0, 1) for dc in (0, 1) if (dx, dy, dc) != (0, 0, 0)]
        sends = []
        for j, peer in enumerate(peers):
            cp = pltpu.make_async_remote_copy(src_ref=v_ref, dst_ref=o_ref.at[me], send_sem=send_sems.at[j],
                                              recv_sem=recv_sems.at[j], device_id=peer, device_id_type=MESH)
            cp.start()
            sends.append(cp)
        for j, (px, py, pc) in enumerate(peers):
            pltpu.make_async_remote_copy(src_ref=v_ref, dst_ref=o_ref.at[4 * px + 2 * py + pc],
                                         send_sem=send_sems.at[j], recv_sem=recv_sems.at[j],
                                         device_id=(px, py, pc), device_id_type=MESH).wait_recv()
        for cp in sends:
            cp.wait_send()
        loc.wait()

    return pl.pallas_call(
        body, name=name, in_specs=[ANY], out_specs=ANY,
        out_shape=jax.ShapeDtypeStruct((8,) + v.shape, v.dtype),
        scratch_shapes=[pltpu.SemaphoreType.DMA((7,)), pltpu.SemaphoreType.DMA((7,)), pltpu.SemaphoreType.DMA(())],
        compiler_params=pltpu.CompilerParams(has_side_effects=True))(v)


def sum_slots(x, *, name):
    s, r, c = x.shape
    tb = 8
    while tb * 2 <= min(r, 512) and r % (tb * 2) == 0 and tb * 2 * c * 4 * (s + 1) <= ROW_BLOCK_BYTES:
        tb *= 2

    def body(x_ref, o_ref):
        acc = x_ref[0].astype(F32)
        for i in range(1, s):
            acc = acc + x_ref[i].astype(F32)
        o_ref[...] = acc

    return pl.pallas_call(
        body, name=name, grid=(r // tb,), in_specs=[pl.BlockSpec((s, tb, c), lambda i: (0, i, 0))],
        out_specs=pl.BlockSpec((tb, c), lambda i: (i, 0)),
        out_shape=jax.ShapeDtypeStruct((r, c), F32), compiler_params=_cparams(1))(x)


def _adam_math(w, g, m, v):
    m = ADAM_B1 * m + (1.0 - ADAM_B1) * g
    v = ADAM_B2 * v + (1.0 - ADAM_B2) * (g * g)
    m_hat = m / (1.0 - ADAM_B1 ** ADAM_STEP)
    v_hat = v / (1.0 - ADAM_B2 ** ADAM_STEP)
    delta = -ADAM_LR * (m_hat / (jnp.sqrt(v_hat) + ADAM_EPS) + ADAM_WD * w)
    return delta, m, v


def adamw_pair(w, g2, m, v, *, name):
    r, c = w.shape
    tb = 8
    while tb * 2 <= min(r, 512) and r % (tb * 2) == 0 and tb * 2 * c * 4 * 9 <= 2 * ROW_BLOCK_BYTES:
        tb *= 2

    def body(w_ref, g_ref, m_ref, v_ref, go_ref, d_ref, mo_ref, vo_ref):
        g = g_ref[0] + g_ref[1]
        d, mn, vn = _adam_math(w_ref[...], g, m_ref[...], v_ref[...])
        go_ref[...] = g
        d_ref[...] = d
        mo_ref[...] = mn
        vo_ref[...] = vn

    blk = pl.BlockSpec((tb, c), lambda i: (i, 0))
    return pl.pallas_call(
        body, name=name, grid=(r // tb,), in_specs=[blk, pl.BlockSpec((2, tb, c), lambda i: (0, i, 0)), blk, blk],
        out_specs=[blk] * 4, out_shape=[jax.ShapeDtypeStruct((r, c), F32)] * 4,
        compiler_params=_cparams(1))(w, g2, m, v)


def adamw_flat(w, g, m, v, *, name):
    def fn(w_, g_, m_, v_):
        return _adam_math(w_, g_, m_, v_)
    return rowk(fn, [w, g, m, v], [], [(w.shape[1], F32)] * 3, name=name)


def to_heads(x):
    t, d = x.shape
    return x.reshape(t, d // HEAD, HEAD).transpose(1, 0, 2)


def from_heads(x):
    h, t, n = x.shape
    return x.transpose(1, 0, 2).reshape(t, h * n)


def pack_flat(arrs, lanes=128, row_mult=8):
    flat = jnp.concatenate([a.reshape(-1).astype(F32) for a in arrs])
    n = flat.shape[0]
    rows = -(-n // lanes)
    rows = -(-rows // row_mult) * row_mult
    return jnp.pad(flat, (0, rows * lanes - n)).reshape(rows, lanes)


def unpack_flat(buf, shapes):
    flat = buf.reshape(-1)
    outs, off = [], 0
    for s in shapes:
        n = int(np.prod(s))
        outs.append(flat[off:off + n].reshape(s))
        off += n
    return outs


def block_diag_from(w_gab):
    g, a, b = w_gab.shape
    eye = jnp.eye(g, dtype=w_gab.dtype)
    return (w_gab[:, :, None, :] * eye[:, None, :, None]).reshape(g * a, g * b)


def block_diag_extract(m, g):
    a, b = m.shape[0] // g, m.shape[1] // g
    idx = jnp.arange(g)
    return m.reshape(g, a, g, b)[idx, :, idx, :]


def kernel(x, p, ffn1_norm, ffn1_w_gate, ffn1_w_up, ffn1_w_down, mix_norm, ffn2_norm, ffn2_w_gate, ffn2_w_up, ffn2_w_down, ple_norm, ple_w_gate, ple_w_proj, ab_w_in, att_q_gain, att_k_gain, att_rel_bias, rwkv_mu, rwkv_w0, rwkv_w_up, rwkv_a0, rwkv_a_up, rwkv_g_up, rwkv_k_k, rwkv_k_a, rwkv_r_k, rwkv_lnx_w, rwkv_lnx_b, ab_w_out, ssm_w_in, ssm_lambda_re, ssm_lambda_im, ssm_log_dt, ssm_b_re, ssm_b_im, ssm_c_re, ssm_c_im, ssm_d, ssm_w_out, loss_target, m_ffn1_norm, m_ffn1_w_gate, m_ffn1_w_up, m_ffn1_w_down, m_mix_norm, m_ffn2_norm, m_ffn2_w_gate, m_ffn2_w_up, m_ffn2_w_down, m_ple_norm, m_ple_w_gate, m_ple_w_proj, m_ab_w_in, m_att_q_gain, m_att_k_gain, m_att_rel_bias, m_rwkv_mu, m_rwkv_w0, m_rwkv_w_up, m_rwkv_a0, m_rwkv_a_up, m_rwkv_g_up, m_rwkv_k_k, m_rwkv_k_a, m_rwkv_r_k, m_rwkv_lnx_w, m_rwkv_lnx_b, m_ab_w_out, m_ssm_w_in, m_ssm_lambda_re, m_ssm_lambda_im, m_ssm_log_dt, m_ssm_b_re, m_ssm_b_im, m_ssm_c_re, m_ssm_c_im, m_ssm_d, m_ssm_w_out, v_ffn1_norm, v_ffn1_w_gate, v_ffn1_w_up, v_ffn1_w_down, v_mix_norm, v_ffn2_norm, v_ffn2_w_gate, v_ffn2_w_up, v_ffn2_w_down, v_ple_norm, v_ple_w_gate, v_ple_w_proj, v_ab_w_in, v_att_q_gain, v_att_k_gain, v_att_rel_bias, v_rwkv_mu, v_rwkv_w0, v_rwkv_w_up, v_rwkv_a0, v_rwkv_a_up, v_rwkv_g_up, v_rwkv_k_k, v_rwkv_k_a, v_rwkv_r_k, v_rwkv_lnx_w, v_rwkv_lnx_b, v_ab_w_out, v_ssm_w_in, v_ssm_lambda_re, v_ssm_lambda_im, v_ssm_log_dt, v_ssm_b_re, v_ssm_b_im, v_ssm_c_re, v_ssm_c_im, v_ssm_d, v_ssm_w_out):
    A = dict(locals())
    W = {n: A[n] for n in W_NAMES}
    return _step(A['x'], A['p'], A['loss_target'], W, {n: A['m_' + n] for n in W_NAMES},
                 {n: A['v_' + n] for n in W_NAMES})


def _step(x, p, target, W, M, V):
    assert x.shape[0] == 1
    t, d = x.shape[1], x.shape[2]
    depth = p.shape[0]
    h0 = x[0]
    tgt = target[0]
    qchip = 2 * lax.axis_index("x") + lax.axis_index("y")
    d_rw = W['rwkv_w0'].shape[1]
    d_att = W['ab_w_out'].shape[1] * 4 - d_rw
    n_h_att, n_h_rw = d_att // HEAD, d_rw // HEAD
    n_bin = 3 * d_rw + DECAY_LORA + AAA_LORA + GATE_LORA
    d_ssm = W['ssm_w_in'].shape[2]
    n_grp = d_ssm // SSM_GROUP
    gp = n_grp * SSM_STATE

    def gather(name, layer):
        return all_gather_chips(W[name][layer].astype(BF16), name=f"ag_{name}")

    grads = {}
    small = {}

    def add_small(name, val, layer=None, nl=1):
        if layer is None:
            small[name] = val
        else:
            small.setdefault(name, [None] * nl)[layer] = val

    def ffn_fwd(h, pre, i):
        g = W[pre + '_norm'][i][None]
        wg, wu, wd = gather(pre + '_w_gate', i), gather(pre + '_w_up', i), gather(pre + '_w_down', i)
        wd2 = wd.reshape(-1, d)
        n = rowk(lambda hh, gg: f_rms(hh, gg), [h], [g], [(d, BF16)], name=f"{pre}_rms")[0]
        a = mm(n, wg, out_dtype=BF16, name=f"{pre}_gate")
        b = mm(n, wu, out_dtype=BF16, name=f"{pre}_up")
        f = a.shape[1]
        u = rowk(lambda aa, bb: f_swiglu(*_f32(aa, bb)), [a, b], [], [(f, BF16)], name=f"{pre}_swiglu")[0]
        h_out = mm(u, wd2, res=h, alpha=0.5, name=f"{pre}_down")
        return h_out, dict(h=h, g=g, n=n, a=a, b=b, u=u, wg=wg, wu=wu, wd2=wd2)

    def ffn_bwd(dh, sv, pre, i):
        f = sv['a'].shape[1]
        dwd = mm(sv['u'], dh, ta=True, alpha=0.5, out_dtype=GRAD_XFER, name=f"{pre}_d_wdown")
        du = mm(dh, sv['wd2'], tb=True, alpha=0.5, out_dtype=BF16, name=f"{pre}_d_u")
        da, db = rowk(vjp_rows(f_swiglu, 2, 1), [sv['a'], sv['b'], du], [], [(f, BF16), (f, BF16)],
                      name=f"{pre}_d_swiglu")
        dwg = mm(sv['n'], da, ta=True, nshard=4, out3=True, out_dtype=GRAD_XFER, name=f"{pre}_d_wgate")
        dwu = mm(sv['n'], db, ta=True, nshard=4, out3=True, out_dtype=GRAD_XFER, name=f"{pre}_d_wup")
        dn = mm(da, sv['wg'], tb=True, name=f"{pre}_d_n1")
        dn = mm(db, sv['wu'], tb=True, res=dn, name=f"{pre}_d_n2")
        dh_in, dg = rms_bwd(sv['h'], dn, dh, sv['g'], name=f"{pre}_d_rms")
        grads.setdefault(pre + '_w_gate', [None] * depth)[i] = dwg
        grads.setdefault(pre + '_w_up', [None] * depth)[i] = dwu
        grads.setdefault(pre + '_w_down', [None] * depth)[i] = dwd.reshape(4, -1, d)
        add_small(pre + '_norm', dg[0], i, depth)
        return dh_in

    def rms_bwd(h, dn, dh_res, g, *, name):
        def fn(hh, dnn, dres, gg):
            _, pull = jax.vjp(f_rms, hh, gg)
            dh_, dg_ = pull(dnn)
            return dh_ + dres, dg_
        return rowk(fn, [h, dn, dh_res], [g], [(d, F32)], [(1, d)], name=name)

    def head_consts(nh):
        e = np.kron(np.eye(nh, dtype=np.float32), np.ones((HEAD, 1), np.float32))
        return jnp.asarray(e), jnp.asarray(e.T)

    def mixer_ab_fwd(h):
        g = W['mix_norm'][0][None]
        win, wout = gather('ab_w_in', 0), gather('ab_w_out', 0).reshape(-1, d)
        hn = rowk(lambda hh, gg: f_rms(hh, gg), [h], [g], [(d, BF16)], name="mixab_rms")[0]
        proj4 = mm(hn, win, out3=True, name="mixab_proj")
        proj = proj4.transpose(1, 0, 2).reshape(t, -1)
        q2, k2, v2 = [to_heads(proj[:, j * d_att:(j + 1) * d_att]).reshape(n_h_att * t, HEAD) for j in range(3)]
        qg, kg = W['att_q_gain'], W['att_k_gain']
        f_qn = lambda qq, gg: f_rms(qq, gg) * (HEAD ** -0.5)
        qn = rowk(f_qn, [q2], [qg], [(HEAD, BF16)], name="att_qnorm")[0].reshape(n_h_att, t, HEAD)
        kn = rowk(f_rms, [k2], [kg], [(HEAD, BF16)], name="att_knorm")[0].reshape(n_h_att, t, HEAD)
        knp = jnp.pad(kn, ((0, 0), (PAD, 0), (0, 0)))
        vp = jnp.pad(v2.astype(BF16).reshape(n_h_att, t, HEAD), ((0, 0), (PAD, 0), (0, 0)))
        bias = W['att_rel_bias'][0][:, _rel_index()]
        o = att_fwd(qn, knp, vp, bias, name="att_fwd")
        att = from_heads(o)
        mu = W['rwkv_mu']
        zs = ts_fwd(proj, 3 * d_att, n_bin, mu, name="rwkv_shift")
        e, et = head_consts(n_h_rw)
        zpad = jnp.zeros((AAA_LORA, d_rw), F32)
        wup_p = jnp.concatenate([W['rwkv_w_up_full'], zpad], 0)
        aup_p = jnp.concatenate([zpad, W['rwkv_a_up_full']], 0)
        pre_c = [W['rwkv_w0'], wup_p, W['rwkv_a0'], aup_p, W['rwkv_g_up_full'], W['rwkv_k_k'], W['rwkv_k_a'], e, et]
        pre = rowk(f_rwkv_pre, [zs], pre_c, [(d_rw, F32)] * 7, name="rwkv_pre")
        r_, lw_, kk_, vv_, ia_, ib_, gg_ = pre
        hm = [to_heads(u_) for u_ in (r_, lw_, kk_, vv_, ia_, ib_)]
        y_h, p0s = rwkv_fwd(*hm, name="rwkv_scan")
        y = from_heads(y_h)
        post_c = [W['rwkv_r_k'].reshape(1, d_rw), W['rwkv_lnx_w'], W['rwkv_lnx_b'], e, et]
        rw = rowk(f_rwkv_post, [y, r_, kk_, vv_, gg_], post_c, [(d_rw, BF16)], name="rwkv_post")[0]
        cat = jnp.concatenate([att, rw], axis=1)
        h_out = mm(cat, wout, res=h, name="mixab_out")
        sv = dict(h=h, g=g, hn=hn, win=win, wout=wout, proj=proj, q2=q2, k2=k2, qn=qn, knp=knp, vp=vp, bias=bias,
                  zs=zs, pre_c=pre_c, pre=pre, hm=hm, p0s=p0s, y=y, post_c=post_c, cat=cat, qg=qg, kg=kg, mu=mu)
        return h_out, sv

    def mixer_ab_bwd(dh, sv):
        dwout = mm(sv['cat'], dh, ta=True, out_dtype=GRAD_XFER, name="mixab_d_wout")
        grads['ab_w_out'] = [dwout.reshape(4, -1, d)]
        dcat = mm(dh, sv['wout'], tb=True, name="mixab_d_cat")
        datt, drw = dcat[:, :d_att], dcat[:, d_att:]
        r_, lw_, kk_, vv_, ia_, ib_, gg_ = sv['pre']
        post = rowk(vjp_rows(f_rwkv_post, 5, 1), [sv['y'], r_, kk_, vv_, gg_, drw], sv['post_c'],
                    [(d_rw, F32)] * 5, [(1, d_rw)] * 3 + [sv['post_c'][3].shape, sv['post_c'][4].shape],
                    name="rwkv_d_post")
        dy, dr1, dk1, dv1, dg1 = post[:5]
        add_small('rwkv_r_k', post[5].reshape(W['rwkv_r_k'].shape))
        add_small('rwkv_lnx_w', post[6])
        add_small('rwkv_lnx_b', post[7])
        dscan = rwkv_bwd(*sv['hm'], sv['p0s'], to_heads(dy), name="rwkv_d_scan")
        dr2, dlw, dk2, dv2, dia, dib = [from_heads(u_) for u_ in dscan]

        def pre_bwd(zs, dra, drb, dlw_, dka, dkb, dva, dvb, dia_, dib_, dg_, *consts):
            _, pull = jax.vjp(f_rwkv_pre, zs, *consts)
            return pull((dra + drb, dlw_, dka + dkb, dva + dvb, dia_, dib_, dg_))

        pc = sv['pre_c']
        preb = rowk(pre_bwd, [sv['zs'], dr1, dr2, dlw, dk1, dk2, dv1, dv2, dia, dib, dg1], pc,
                    [(n_bin, F32)], [c.shape for c in pc], name="rwkv_d_pre")
        dzs = preb[0]
        add_small('rwkv_w0', preb[1])
        add_small('rwkv_w_up', preb[2][:DECAY_LORA])
        add_small('rwkv_a0', preb[3])
        add_small('rwkv_a_up', preb[4][DECAY_LORA:])
        add_small('rwkv_g_up', preb[5])
        add_small('rwkv_k_k', preb[6])
        add_small('rwkv_k_a', preb[7])
        dz, dmu = ts_bwd(sv['proj'], 3 * d_att, dzs, sv['mu'], name="rwkv_d_shift")
        add_small('rwkv_mu', dmu)
        do = to_heads(datt).astype(BF16)
        dqn, dknp, dvp, dbias = att_bwd(sv['qn'], sv['knp'], sv['vp'], sv['bias'], do, name="att_bwd")
        add_small('att_rel_bias', relbias_reduce(dbias, name="att_d_relbias")[None])
        f_qn = lambda qq, gg: f_rms(qq, gg) * (HEAD ** -0.5)
        dq2, dqg = rowk(vjp_rows(f_qn, 1, 1), [sv['q2'], dqn.reshape(-1, HEAD)], [sv['qg']], [(HEAD, F32)],
                        [(1, HEAD)], name="att_d_qnorm")
        dk2_, dkg = rowk(vjp_rows(f_rms, 1, 1), [sv['k2'], dknp[:, PAD:].reshape(-1, HEAD)], [sv['kg']],
                         [(HEAD, F32)], [(1, HEAD)], name="att_d_knorm")
        add_small('att_q_gain', dqg)
        add_small('att_k_gain', dkg)
        dproj = jnp.concatenate([from_heads(dq2.reshape(n_h_att, t, HEAD)), from_heads(dk2_.reshape(n_h_att, t, HEAD)),
                                 from_heads(dvp[:, PAD:]), dz], axis=1)
        dproj4 = dproj.reshape(t, 4, -1).transpose(1, 0, 2).astype(BF16)
        grads['ab_w_in'] = [mm(sv['hn'], dproj4, ta=True, out3=True, out_dtype=GRAD_XFER, name="mixab_d_win")]
        dhn = mm(dproj4, sv['win'], tb=True, name="mixab_d_hn")
        dh_in, dg = rms_bwd(sv['h'], dhn, dh, sv['g'], name="mixab_d_rms")
        add_small('mix_norm', dg[0], 0, depth)
        return dh_in

    def ssm_params():
        lr, li = W['ssm_lambda_re'][0], W['ssm_lambda_im'][0]
        ldt = W['ssm_log_dt'][0][:, None]
        ab = rowk(f_ssm_ab, [lr, li, ldt], [], [(SSM_STATE, F32)] * 4, name="ssm_ab", tb=n_grp)
        br = W['ssm_b_re'][0].reshape(gp, SSM_GROUP)
        bi = W['ssm_b_im'][0].reshape(gp, SSM_GROUP)
        z_re, z_im = ab[2].reshape(gp, 1), ab[3].reshape(gp, 1)
        bb = rowk(f_ssm_bb, [br, bi, z_re, z_im], [], [(SSM_GROUP, F32)] * 2, name="ssm_bb", tb=gp)
        return dict(lr=lr, li=li, ldt=ldt, ab=ab, br=br, bi=bi, z_re=z_re, z_im=z_im, bb=bb)

    def mixer_s5_fwd(h):
        g = W['mix_norm'][1][None]
        win, wout = gather('ssm_w_in', 0).reshape(d, d_ssm), gather('ssm_w_out', 0)
        hn = rowk(lambda hh, gg: f_rms(hh, gg), [h], [g], [(d, BF16)], name="s5_rms")[0]
        u = mm(hn, win, name="s5_in")
        sp = ssm_params()
        bbd_re = block_diag_from(sp['bb'][0].reshape(n_grp, SSM_STATE, SSM_GROUP).transpose(0, 2, 1)).astype(BF16)
        bbd_im = block_diag_from(sp['bb'][1].reshape(n_grp, SSM_STATE, SSM_GROUP).transpose(0, 2, 1)).astype(BF16)
        cbd_re = block_diag_from(W['ssm_c_re'][0].transpose(0, 2, 1)).astype(BF16)
        cbd_im = block_diag_from(W['ssm_c_im'][0].transpose(0, 2, 1)).astype(BF16)
        ub = u.astype(BF16)
        bu_re = mm(ub, bbd_re, name="s5_bu_re").reshape(t, gp // 128, 128)
        bu_im = mm(ub, bbd_im, name="s5_bu_im").reshape(t, gp // 128, 128)
        a_re, a_im = sp['ab'][0].reshape(gp // 128, 128), sp['ab'][1].reshape(gp // 128, 128)
        h_re, h_im = ssm_scan_fwd(bu_re, bu_im, a_re, a_im, name="s5_scan")
        hb_re, hb_im = h_re.reshape(t, gp).astype(BF16), h_im.reshape(t, gp).astype(BF16)
        y = mm(hb_re, cbd_re, name="s5_y_re")
        y = mm(hb_im, cbd_im, res=y, alpha=-1.0, name="s5_y_im")
        dsk = W['ssm_d_full']
        f_act = lambda yy, uu, dd: f_gelu(yy + dd * uu)
        yg = rowk(f_act, [y, u], [dsk], [(d_ssm, BF16)], name="s5_gelu")[0]
        z = mm(yg, wout, name="s5_out")
        f_glu = lambda zz, hh: hh + zz[:, :d] * f_sigmoid(zz[:, d:])
        h_out = rowk(f_glu, [z, h], [], [(d, F32)], name="s5_glu")[0]
        sv = dict(h=h, g=g, hn=hn, win=win, wout=wout, u=u, ub=ub, sp=sp, bbd_re=bbd_re, bbd_im=bbd_im,
                  cbd_re=cbd_re, cbd_im=cbd_im, a_re=a_re, a_im=a_im, h_re=h_re, h_im=h_im, hb_re=hb_re,
                  hb_im=hb_im, y=y, dsk=dsk, yg=yg, z=z)
        return h_out, sv

    def mixer_s5_bwd(dh, sv):
        f_glu = lambda zz: zz[:, :d] * f_sigmoid(zz[:, d:])
        dz = rowk(vjp_rows(f_glu, 1, 1), [sv['z'], dh], [], [(2 * d, BF16)], name="s5_d_glu")[0]
        grads['ssm_w_out'] = [mm(sv['yg'], dz, ta=True, nshard=4, out3=True, out_dtype=GRAD_XFER, name="s5_d_wout")]
        dyg = mm(dz, sv['wout'], tb=True, name="s5_d_yg")
        f_act = lambda yy, uu, dd: f_gelu(yy + dd * uu)
        dy, du1, ddsk = rowk(vjp_rows(f_act, 2, 1), [sv['y'], sv['u'], dyg], [sv['dsk']],
                             [(d_ssm, F32), (d_ssm, F32)], [(1, d_ssm)], name="s5_d_gelu")
        add_small('ssm_d', ddsk)
        dyb = dy.astype(BF16)
        dcbd_re = mm(sv['hb_re'], dyb, ta=True, name="s5_d_c_re")
        dcbd_im = mm(sv['hb_im'], dyb, ta=True, alpha=-1.0, name="s5_d_c_im")
        add_small('ssm_c_re', block_diag_extract(dcbd_re, n_grp).transpose(0, 2, 1)[None])
        add_small('ssm_c_im', block_diag_extract(dcbd_im, n_grp).transpose(0, 2, 1)[None])
        dh_re = mm(dyb, sv['cbd_re'], tb=True, name="s5_d_h_re").reshape(t, gp // 128, 128)
        dh_im = mm(dyb, sv['cbd_im'], tb=True, alpha=-1.0, name="s5_d_h_im").reshape(t, gp // 128, 128)
        hp_re = jnp.pad(sv['h_re'][:-1], ((1, 0), (0, 0), (0, 0)))
        hp_im = jnp.pad(sv['h_im'][:-1], ((1, 0), (0, 0), (0, 0)))
        g_re, g_im, da_re, da_im = ssm_scan_bwd(dh_re, dh_im, hp_re, hp_im, sv['a_re'], sv['a_im'], name="s5_d_scan")
        gb_re, gb_im = g_re.reshape(t, gp).astype(BF16), g_im.reshape(t, gp).astype(BF16)
        dbbd_re = mm(sv['ub'], gb_re, ta=True, name="s5_d_bb_re")
        dbbd_im = mm(sv['ub'], gb_im, ta=True, name="s5_d_bb_im")
        du = mm(gb_re, sv['bbd_re'], tb=True, res=du1, name="s5_d_u_re")
        du = mm(gb_im, sv['bbd_im'], tb=True, res=du, name="s5_d_u_im")
        sp = sv['sp']
        dbb_re = block_diag_extract(dbbd_re, n_grp).transpose(0, 2, 1).reshape(gp, SSM_GROUP)
        dbb_im = block_diag_extract(dbbd_im, n_grp).transpose(0, 2, 1).reshape(gp, SSM_GROUP)
        dbr, dbi, dz_re, dz_im = rowk(vjp_rows(f_ssm_bb, 4, 2),
                                      [sp['br'], sp['bi'], sp['z_re'], sp['z_im'], dbb_re, dbb_im], [],
                                      [(SSM_GROUP, F32)] * 2 + [(1, F32)] * 2, name="ssm_d_bb", tb=gp)
        add_small('ssm_b_re', dbr.reshape(W['ssm_b_re'].shape))
        add_small('ssm_b_im', dbi.reshape(W['ssm_b_im'].shape))
        dlr, dli, dldt = rowk(vjp_rows(f_ssm_ab, 3, 4),
                              [sp['lr'], sp['li'], sp['ldt'], da_re.reshape(n_grp, SSM_STATE),
                               da_im.reshape(n_grp, SSM_STATE), dz_re.reshape(n_grp, SSM_STATE),
                               dz_im.reshape(n_grp, SSM_STATE)], [],
                              [(SSM_STATE, F32)] * 2 + [(1, F32)], name="ssm_d_ab", tb=n_grp)
        add_small('ssm_lambda_re', dlr[None])
        add_small('ssm_lambda_im', dli[None])
        add_small('ssm_log_dt', dldt.reshape(1, n_grp))
        grads['ssm_w_in'] = [mm(sv['hn'], du, ta=True, out_dtype=GRAD_XFER, name="s5_d_win").reshape(4, -1, d_ssm)]
        dhn = mm(du, sv['win'], tb=True, name="s5_d_hn")
        dh_in, dg = rms_bwd(sv['h'], dhn, dh, sv['g'], name="s5_d_rms")
        add_small('mix_norm', dg[0], 1, depth)
        return dh_in

    def ple_fwd(h, i):
        g = W['ple_norm'][i][None]
        wpg = gather('ple_w_gate', i).reshape(d, d)
        wpp = gather('ple_w_proj', i)
        n = rowk(lambda hh, gg: f_rms(hh, gg), [h], [g], [(d, BF16)], name="ple_rms")[0]
        zg = mm(n, wpg, name="ple_gate")
        pb = p[i, 0].astype(BF16)
        pp = mm(pb, wpp, name="ple_proj")
        f_ple = lambda zz, pq, hh: hh + f_sigmoid(zz) * pq
        h_out = rowk(f_ple, [zg, pp, h], [], [(d, F32)], name="ple_mix")[0]
        return h_out, dict(h=h, g=g, n=n, zg=zg, pp=pp, pb=pb, wpg=wpg, wpp=wpp)

    def ple_bwd(dh, sv, i):
        f_ple = lambda zz, pq: f_sigmoid(zz) * pq
        dzg, dpp = rowk(vjp_rows(f_ple, 2, 1), [sv['zg'], sv['pp'], dh], [], [(d, BF16), (d, BF16)],
                        name="ple_d_mix")
        grads.setdefault('ple_w_proj', [None] * depth)[i] = mm(sv['pb'], dpp, ta=True, nshard=4, out3=True,
                                                               out_dtype=GRAD_XFER, name="ple_d_wproj")
        grads.setdefault('ple_w_gate', [None] * depth)[i] = mm(sv['n'], dzg, ta=True, out_dtype=GRAD_XFER,
                                                               name="ple_d_wgate").reshape(4, -1, d)
        dn = mm(dzg, sv['wpg'], tb=True, name="ple_d_n")
        dh_in, dg = rms_bwd(sv['h'], dn, dh, sv['g'], name="ple_d_rms")
        add_small('ple_norm', dg[0], i, depth)
        return dh_in

    def gather_small(name):
        w = W[name]
        full = all_gather_chips(w, name=f"ag_{name}")
        return jnp.moveaxis(full, 0, -2).reshape(w.shape[1:-1] + (4 * w.shape[-1],))

    W = dict(W)
    W['rwkv_w_up_full'] = gather_small('rwkv_w_up')
    W['rwkv_a_up_full'] = gather_small('rwkv_a_up')
    W['rwkv_g_up_full'] = gather_small('rwkv_g_up')
    W['ssm_d_full'] = gather_small('ssm_d')[None]

    h = h0
    saved = []
    for i in range(depth):
        sv = {}
        h, sv['ffn1'] = ffn_fwd(h, 'ffn1', i)
        if i % 2 == 0:
            h, sv['mix'] = mixer_ab_fwd(h)
        else:
            h, sv['mix'] = mixer_s5_fwd(h)
        h, sv['ffn2'] = ffn_fwd(h, 'ffn2', i)
        h, sv['ple'] = ple_fwd(h, i)
        saved.append(sv)

    def f_loss(y, tg):
        e = y - tg
        part = 0.5 * jnp.sum(jnp.sum(e * e, axis=-1, keepdims=True) * (1.0 / d), axis=0, keepdims=True)
        return e * (1.0 / d), jnp.broadcast_to(part, (1, 128))
    dh, loss_part = rowk(f_loss, [h, tgt], [], [(d, F32)], [(1, 128)], name="loss")
    loss = lax.psum(loss_part[0, 0], ("x", "y", "c"))

    for i in reversed(range(depth)):
        sv = saved[i]
        dh = ple_bwd(dh, sv['ple'], i)
        dh = ffn_bwd(dh, sv['ffn2'], 'ffn2', i)
        if i % 2 == 0:
            dh = mixer_ab_bwd(dh, sv['mix'])
        else:
            dh = mixer_s5_bwd(dh, sv['mix'])
        dh = ffn_bwd(dh, sv['ffn1'], 'ffn1', i)
    grad_x = dh[None]

    small_names = [n for n in W_NAMES if n not in BIG]
    small_full = []
    for n in small_names:
        v_ = small[n]
        if isinstance(v_, list):
            v_ = jnp.stack(v_)
        full_shape = W[n].shape[:-1] + (4 * W[n].shape[-1],) if n in SMALL_SHARDED else W[n].shape
        small_full.append(v_.reshape(full_shape))
    packed = pack_flat(small_full)
    gathered = all_gather_devices(packed, name="ar_small")
    summed = sum_slots(gathered, name="ar_small_sum")
    small_tot = unpack_flat(summed, [a.shape for a in small_full])
    g_small = {}
    for n, a in zip(small_names, small_tot):
        if n in SMALL_SHARDED:
            ns = W[n].shape[-1]
            a = lax.dynamic_slice_in_dim(a, qchip * ns, ns, axis=a.ndim - 1)
        g_small[n] = a

    out = {}
    pk = lambda dct: pack_flat([dct[n] for n in small_names])
    res = adamw_flat(pk(W), pk(g_small), pk(M), pk(V), name="adamw_small")
    shapes = [W[n].shape for n in small_names]
    for kind, buf in zip(('delta', 'm', 'v'), res):
        for n, a in zip(small_names, unpack_flat(buf, shapes)):
            out[(kind, n)] = a
    for n in small_names:
        out[('grad', n)] = g_small[n]

    for n in BIG:
        per_layer = []
        for li_, g4 in enumerate(grads[n]):
            shard_shape = W[n].shape[1:]
            if n in ('ffn1_w_gate', 'ffn1_w_up', 'ffn2_w_gate', 'ffn2_w_up', 'ple_w_proj', 'ab_w_in', 'ssm_w_out'):
                g4 = g4.reshape((4,) + shard_shape)
            else:
                g4 = g4.reshape((4,) + shard_shape)
            r4 = scatter_to_chips(g4, name=f"rs_{n}")
            rows, cols = int(np.prod(shard_shape[:-1])), shard_shape[-1]
            s_c = sum_slots(r4.reshape(4, rows, cols), name=f"rs_sum_{n}")
            s2 = swap_cores(s_c, name=f"swap_{n}")
            per_layer.append(adamw_pair(W[n][li_].reshape(rows, cols), s2, M[n][li_].reshape(rows, cols),
                                        V[n][li_].reshape(rows, cols), name=f"adamw_{n}"))
        for kind, idx in zip(('grad', 'delta', 'm', 'v'), range(4)):
            out[(kind, n)] = jnp.stack([pl_[idx].reshape(W[n].shape[1:]) for pl_ in per_layer])

    return (loss, grad_x, *[out[('grad', n)] for n in W_NAMES], *[out[('delta', n)] for n in W_NAMES],
            *[out[('m', n)] for n in W_NAMES], *[out[('v', n)] for n in W_NAMES])
```

```python
import functools
import math

import numpy as np
import jax
import jax.numpy as jnp
from jax import lax
from jax.experimental import pallas as pl
from jax.experimental.pallas import tpu as pltpu

F32 = jnp.float32
BF16 = jnp.bfloat16
HI = lax.Precision.HIGHEST
MESH = pl.DeviceIdType.MESH

CHUNK = 64
N_LEFT = 8
BAND = (N_LEFT + 1) * CHUNK
PAD = N_LEFT * CHUNK
HEAD = 64
REL_CLIP = 128
N_REL = (CHUNK - 1) + REL_CLIP + 1
DECAY_LORA = 64
AAA_LORA = 64
GATE_LORA = 128
SSM_GROUP = 16
SSM_STATE = 64
RMS_EPS = 1e-6
GN_EPS = 64e-5
NEG_BIG = -1e30

ADAM_LR = 0.001
ADAM_B1 = 0.9
ADAM_B2 = 0.999
ADAM_EPS = 1e-08
ADAM_WD = 0.01
ADAM_STEP = 10

RW_CHUNK = 64
RW_HEADS = 4
ATT_HEADS = 2
VMEM_LIMIT = 56 * 1024 * 1024
ROW_BLOCK_BYTES = 6 * 1024 * 1024
GRAD_XFER = BF16

W_NAMES = ['ffn1_norm', 'ffn1_w_gate', 'ffn1_w_up', 'ffn1_w_down', 'mix_norm', 'ffn2_norm', 'ffn2_w_gate',
           'ffn2_w_up', 'ffn2_w_down', 'ple_norm', 'ple_w_gate', 'ple_w_proj', 'ab_w_in', 'att_q_gain',
           'att_k_gain', 'att_rel_bias', 'rwkv_mu', 'rwkv_w0', 'rwkv_w_up', 'rwkv_a0', 'rwkv_a_up',
           'rwkv_g_up', 'rwkv_k_k', 'rwkv_k_a', 'rwkv_r_k', 'rwkv_lnx_w', 'rwkv_lnx_b', 'ab_w_out',
           'ssm_w_in', 'ssm_lambda_re', 'ssm_lambda_im', 'ssm_log_dt', 'ssm_b_re', 'ssm_b_im', 'ssm_c_re',
           'ssm_c_im', 'ssm_d', 'ssm_w_out']
BIG = ['ffn1_w_gate', 'ffn1_w_up', 'ffn1_w_down', 'ffn2_w_gate', 'ffn2_w_up', 'ffn2_w_down',
       'ple_w_gate', 'ple_w_proj', 'ab_w_in', 'ab_w_out', 'ssm_w_in', 'ssm_w_out']
SMALL_SHARDED = ['rwkv_w_up', 'rwkv_a_up', 'rwkv_g_up', 'ssm_d']


def _cparams(n_axes):
    return pltpu.CompilerParams(dimension_semantics=("arbitrary",) * n_axes, vmem_limit_bytes=VMEM_LIMIT)


def _pick(n, prefs):
    for p in prefs:
        if n % p == 0:
            return p
    return n


def mm(a, b, *, ta=False, tb=False, nshard=None, out3=False, out_dtype=F32, res=None, alpha=1.0, comm=None, name):
    a3, b3 = a.ndim == 3, b.ndim == 3
    if a3:
        assert not ta
        sk, m, ks = a.shape
        k = sk * ks
    elif ta:
        k, m = a.shape
    else:
        m, k = a.shape
    kshard = None
    if b3 and not tb:
        s, kb, ns = b.shape
        n = s * ns
        nshard = s
    elif b3 and tb:
        sk2, n, ks2 = b.shape
        kb = sk2 * ks2
        kshard = (sk2, ks2)
    elif tb:
        n, kb = b.shape
    else:
        kb, n = b.shape
    assert k == kb, (a.shape, b.shape, ta, tb)
    if a3:
        assert kshard is None or kshard == (sk, ks)
        kshard = (sk, ks)
    if nshard is not None:
        tn, nj = n // nshard, nshard
    else:
        assert not out3
        tn = _pick(n, (1024, 1408, 1280, 512, 640, 256, 128))
        nj = n // tn
    if kshard is not None:
        nk, tk = kshard
    else:
        tk = _pick(k, (2048, 1408, 1024, 512, 256, 128))
        nk = k // tk
    tm = _pick(m, (512, 256, 128))
    ni = m // tm

    if a3:
        a_spec = pl.BlockSpec((None, tm, tk), lambda i, j, kk: (kk, i, 0))
    elif ta:
        a_spec = pl.BlockSpec((tk, tm), lambda i, j, kk: (kk, i))
    else:
        a_spec = pl.BlockSpec((tm, tk), lambda i, j, kk: (i, kk))
    if b3 and not tb:
        b_spec = pl.BlockSpec((None, tk, tn), lambda i, j, kk: (j, kk, 0))
    elif b3 and tb:
        b_spec = pl.BlockSpec((None, tn, tk), lambda i, j, kk: (kk, j, 0))
    elif tb:
        b_spec = pl.BlockSpec((tn, tk), lambda i, j, kk: (j, kk))
    else:
        b_spec = pl.BlockSpec((tk, tn), lambda i, j, kk: (kk, j))
    if out3:
        o_spec = pl.BlockSpec((None, tm, tn), lambda i, j, kk: (j, i, 0))
        o_shape = (nj, m, tn)
    else:
        o_spec = pl.BlockSpec((tm, tn), lambda i, j, kk: (i, j))
        o_shape = (m, n)
    has_res = res is not None
    dn = (((0 if ta else 1,), (1 if tb else 0,)), ((), ()))

    n_cin = len(comm.ins) if comm else 0
    n_cout = len(comm.outs) if comm else 0
    n_in = 2 + has_res

    def body(*refs):
        a_ref, b_ref = refs[0], refs[1]
        r_ref = refs[2] if has_res else None
        c_in = refs[n_in:n_in + n_cin]
        o_ref = refs[n_in + n_cin]
        c_out = refs[n_in + n_cin + 1:n_in + n_cin + 1 + n_cout]
        scratch = refs[n_in + n_cin + 1 + n_cout:]
        acc_ref = scratch[0] if nk > 1 else None
        sems = scratch[1:] if nk > 1 else scratch
        i, j, kk = pl.program_id(0), pl.program_id(1), pl.program_id(2)

        if comm:
            @pl.when((i == 0) & (j == 0) & (kk == 0))
            def _():
                comm.start(c_in, c_out, *sems)

        def finish(acc):
            val = acc * alpha if alpha != 1.0 else acc
            if has_res:
                val = val + r_ref[...].astype(F32)
            o_ref[...] = val.astype(o_ref.dtype)

        part = lax.dot_general(a_ref[...].astype(BF16), b_ref[...].astype(BF16), dn, preferred_element_type=F32)
        if nk == 1:
            finish(part)
        else:
            @pl.when(kk == 0)
            def _():
                acc_ref[...] = part

            @pl.when(kk > 0)
            def _():
                acc_ref[...] += part

            @pl.when(kk == nk - 1)
            def _():
                finish(acc_ref[...])

        if comm:
            @pl.when((i == ni - 1) & (j == nj - 1) & (kk == nk - 1))
            def _():
                comm.wait(c_in, c_out, *sems)

    in_specs = [a_spec, b_spec] + ([o_spec] if has_res else []) + [ANY] * n_cin
    args = (a, b) + ((res,) if has_res else ()) + (tuple(comm.ins) if comm else ())
    out_specs = [o_spec] + [ANY] * n_cout
    out_shape = [jax.ShapeDtypeStruct(o_shape, out_dtype)] + (list(comm.outs) if comm else [])
    scratch_shapes = ([pltpu.VMEM((tm, tn), F32)] if nk > 1 else []) + (comm.sem_shapes() if comm else [])
    outs = pl.pallas_call(
        body, name=name, grid=(ni, nj, nk), in_specs=in_specs, out_specs=out_specs, out_shape=out_shape,
        scratch_shapes=scratch_shapes, compiler_params=_cparams(3))(*args)
    if comm:
        return outs[0], list(outs[1:])
    return outs[0]


def rowk(fn, rows, consts, out_rows, out_accs=(), *, name, tb=None):
    t = rows[0].shape[0]
    nr, nc, no, na = len(rows), len(consts), len(out_rows), len(out_accs)
    if tb is None:
        per_row = sum(r.shape[1] * 4 for r in rows) + sum(n * 4 for n, _ in out_rows)
        tb = 8
        while tb * 2 <= min(t, 1024) and tb * 2 * per_row <= ROW_BLOCK_BYTES and t % (tb * 2) == 0:
            tb *= 2
    assert t % tb == 0
    nb = t // tb

    def body(*refs):
        r_in, c_in = refs[:nr], refs[nr:nr + nc]
        o_rows, o_accs = refs[nr + nc:nr + nc + no], refs[nr + nc + no:]
        outs = fn(*[r[...] for r in r_in], *[c[...] for c in c_in])
        if not isinstance(outs, (tuple, list)):
            outs = (outs,)
        assert len(outs) == no + na, (name, len(outs), no, na)
        for ref, v in zip(o_rows, outs[:no]):
            ref[...] = v.astype(ref.dtype)
        if na:
            @pl.when(pl.program_id(0) == 0)
            def _():
                for ref in o_accs:
                    ref[...] = jnp.zeros_like(ref)
            for ref, v in zip(o_accs, outs[no:]):
                ref[...] += v.astype(F32)

    in_specs = [pl.BlockSpec((tb, r.shape[1]), lambda i: (i, 0)) for r in rows]
    in_specs += [pl.BlockSpec(c.shape, lambda i, nd=c.ndim: (0,) * nd) for c in consts]
    out_specs = [pl.BlockSpec((tb, n), lambda i: (i, 0)) for n, _ in out_rows]
    out_specs += [pl.BlockSpec(s, lambda i, nd=len(s): (0,) * nd) for s in out_accs]
    out_shape = [jax.ShapeDtypeStruct((t, n), d) for n, d in out_rows]
    out_shape += [jax.ShapeDtypeStruct(s, F32) for s in out_accs]
    res = pl.pallas_call(body, name=name, grid=(nb,), in_specs=in_specs, out_specs=out_specs,
                         out_shape=out_shape, compiler_params=_cparams(1))(*rows, *consts)
    return res


def _f32(*xs):
    return [x.astype(F32) for x in xs]


def vjp_rows(f, n_rows, n_cots):
    def fn(*args):
        rows = _f32(*args[:n_rows])
        cots = _f32(*args[n_rows:n_rows + n_cots])
        consts = _f32(*args[n_rows + n_cots:])
        outs, pull = jax.vjp(f, *rows, *consts)
        if not isinstance(outs, (tuple, list)):
            cots = cots[0]
        else:
            cots = tuple(cots)
        return pull(cots)
    return fn


def hdot(x, y):
    return jnp.dot(x, y, precision=HI, preferred_element_type=F32)


def f_rms(h, g):
    return h * lax.rsqrt(jnp.mean(h * h, axis=-1, keepdims=True) + RMS_EPS) * g


def f_sigmoid(x):
    return 1.0 / (1.0 + jnp.exp(-x))


def f_swiglu(a, b):
    return a * f_sigmoid(a) * b


def f_softplus(x):
    return jnp.maximum(x, 0.0) + jnp.log(1.0 + jnp.exp(-jnp.abs(x)))


def f_gelu(x):
    return 0.5 * x * (1.0 + jnp.tanh(math.sqrt(2.0 / math.pi) * (x + 0.044715 * (x * x * x))))


def f_rwkv_pre(zs, w0, wup_p, a0, aup_p, g_up, k_k, k_a, e, et):
    d = w0.shape[1]
    r, k, v = zs[:, :d], zs[:, d:2 * d], zs[:, 2 * d:3 * d]
    xwa = zs[:, 3 * d:3 * d + DECAY_LORA + AAA_LORA]
    xg = zs[:, 3 * d + DECAY_LORA + AAA_LORA:]
    w_log = -f_softplus(-(w0 + hdot(jnp.tanh(xwa), wup_p))) - 0.5
    logw = -jnp.exp(w_log)
    a = f_sigmoid(a0 + hdot(xwa, aup_p))
    g = hdot(f_sigmoid(xg), g_up)
    kk = k * k_k
    nrm = jnp.maximum(jnp.sqrt(hdot(kk * kk, e)), 1e-12)
    kk = kk * hdot(1.0 / nrm, et)
    k2 = k * (1.0 + (a - 1.0) * k_a)
    return r, logw, k2, v, -kk, kk * a, g


def f_rwkv_post(y, r, k2, v, g, r_k, lnx_w, lnx_b, e, et):
    inv = 1.0 / HEAD
    mean = hdot(hdot(y, e) * inv, et)
    yc = y - mean
    var = hdot(yc * yc, e) * inv
    yn = yc * hdot(lax.rsqrt(var + GN_EPS), et) * lnx_w + lnx_b
    bonus = hdot(hdot(r * k2 * r_k, e), et) * v
    return (yn + bonus) * g


def f_ssm_ab(lr, li, log_dt):
    dt = jnp.exp(log_dt)
    mag = jnp.exp(lr * dt)
    ab_re, ab_im = mag * jnp.cos(li * dt), mag * jnp.sin(li * dt)
    denom = lr * lr + li * li
    z_re = ((ab_re - 1.0) * lr + ab_im * li) / denom
    z_im = (ab_im * lr - (ab_re - 1.0) * li) / denom
    return ab_re, ab_im, z_re, z_im


def f_ssm_bb(br, bi, z_re, z_im):
    return z_re * br - z_im * bi, z_re * bi + z_im * br


def _col_block(n):
    return _pick(n, (256, 128))


def ts_fwd(proj, col0, width, mu, *, name):
    t = proj.shape[0]
    cb = _col_block(width)
    assert col0 % cb == 0 and width % cb == 0
    off = col0 // cb

    def body(z_ref, mu_ref, o_ref):
        z = z_ref[...]
        row = lax.broadcasted_iota(jnp.int32, z.shape, 0)
        prev = jnp.where(row == 0, 0.0, pltpu.roll(z, 1, 0))
        o_ref[...] = z + (prev - z) * mu_ref[...]

    return pl.pallas_call(
        body, name=name, grid=(width // cb,),
        in_specs=[pl.BlockSpec((t, cb), lambda j: (0, j + off)), pl.BlockSpec((1, cb), lambda j: (0, j))],
        out_specs=pl.BlockSpec((t, cb), lambda j: (0, j)),
        out_shape=jax.ShapeDtypeStruct((t, width), F32), compiler_params=_cparams(1))(proj, mu)


def ts_bwd(proj, col0, dzs, mu, *, name):
    t, width = dzs.shape
    cb = _col_block(width)
    off = col0 // cb

    def body(z_ref, d_ref, mu_ref, dz_ref, dmu_ref):
        z, d, m = z_ref[...], d_ref[...], mu_ref[...]
        row = lax.broadcasted_iota(jnp.int32, z.shape, 0)
        prev = jnp.where(row == 0, 0.0, pltpu.roll(z, 1, 0))
        dm = d * m
        nxt = jnp.where(row == t - 1, 0.0, pltpu.roll(dm, t - 1, 0))
        dz_ref[...] = d - dm + nxt
        dmu_ref[...] = jnp.sum(d * (prev - z), axis=0, keepdims=True)

    return pl.pallas_call(
        body, name=name, grid=(width // cb,),
        in_specs=[pl.BlockSpec((t, cb), lambda j: (0, j + off)), pl.BlockSpec((t, cb), lambda j: (0, j)),
                  pl.BlockSpec((1, cb), lambda j: (0, j))],
        out_specs=[pl.BlockSpec((t, cb), lambda j: (0, j)), pl.BlockSpec((1, cb), lambda j: (0, j))],
        out_shape=[jax.ShapeDtypeStruct((t, width), F32), jax.ShapeDtypeStruct((1, width), F32)],
        compiler_params=_cparams(1))(proj, dzs, mu)


def _att_scores(qn, kb, bias, c):
    s = jnp.einsum('hqd,hkd->hqk', qn, kb, preferred_element_type=F32) + bias
    col = lax.broadcasted_iota(jnp.int32, s.shape, 2)
    s = jnp.where(col >= PAD - c * CHUNK, s, NEG_BIG)
    s = s - jnp.max(s, axis=-1, keepdims=True)
    e = jnp.exp(s)
    return e / jnp.sum(e, axis=-1, keepdims=True)


def att_fwd(qn, knp, vp, bias, *, name):
    h, t, _ = qn.shape
    hb = ATT_HEADS
    nc = t // CHUNK

    def body(q_ref, k_ref, v_ref, b_ref, o_ref):
        c = pl.program_id(1)
        start = pl.multiple_of(c * CHUNK, CHUNK)
        kb = k_ref[:, pl.ds(start, BAND), :]
        vb = v_ref[:, pl.ds(start, BAND), :]
        p = _att_scores(q_ref[...], kb, b_ref[...], c)
        o_ref[...] = jnp.einsum('hqk,hkd->hqd', p.astype(BF16), vb, preferred_element_type=F32).astype(o_ref.dtype)

    return pl.pallas_call(
        body, name=name, grid=(h // hb, nc),
        in_specs=[pl.BlockSpec((hb, CHUNK, HEAD), lambda g, c: (g, c, 0)),
                  pl.BlockSpec((hb, t + PAD, HEAD), lambda g, c: (g, 0, 0)),
                  pl.BlockSpec((hb, t + PAD, HEAD), lambda g, c: (g, 0, 0)),
                  pl.BlockSpec((hb, CHUNK, BAND), lambda g, c: (g, 0, 0))],
        out_specs=pl.BlockSpec((hb, CHUNK, HEAD), lambda g, c: (g, c, 0)),
        out_shape=jax.ShapeDtypeStruct((h, t, HEAD), BF16), compiler_params=_cparams(2))(qn, knp, vp, bias)


def att_bwd(qn, knp, vp, bias, do, *, name):
    h, t, _ = qn.shape
    hb = ATT_HEADS
    nc = t // CHUNK

    def body(q_ref, k_ref, v_ref, b_ref, do_ref, dq_ref, dk_ref, dv_ref, db_ref):
        c = pl.program_id(1)

        @pl.when(c == 0)
        def _():
            dk_ref[...] = jnp.zeros_like(dk_ref)
            dv_ref[...] = jnp.zeros_like(dv_ref)
            db_ref[...] = jnp.zeros_like(db_ref)

        start = pl.multiple_of(c * CHUNK, CHUNK)
        qv = q_ref[...]
        kb = k_ref[:, pl.ds(start, BAND), :]
        vb = v_ref[:, pl.ds(start, BAND), :]
        p = _att_scores(qv, kb, b_ref[...], c)
        dov = do_ref[...]
        dp = jnp.einsum('hqd,hkd->hqk', dov, vb, preferred_element_type=F32)
        ds = p * (dp - jnp.sum(p * dp, axis=-1, keepdims=True))
        db_ref[...] += ds
        dsb = ds.astype(BF16)
        dq_ref[...] = jnp.einsum('hqk,hkd->hqd', dsb, kb, preferred_element_type=F32)
        dst = jnp.swapaxes(dsb, 1, 2)
        pt = jnp.swapaxes(p.astype(BF16), 1, 2)
        dk_ref[:, pl.ds(start, BAND), :] += jnp.einsum('hkq,hqd->hkd', dst, qv, preferred_element_type=F32)
        dv_ref[:, pl.ds(start, BAND), :] += jnp.einsum('hkq,hqd->hkd', pt, dov, preferred_element_type=F32)

    blk_q = pl.BlockSpec((hb, CHUNK, HEAD), lambda g, c: (g, c, 0))
    blk_k = pl.BlockSpec((hb, t + PAD, HEAD), lambda g, c: (g, 0, 0))
    blk_b = pl.BlockSpec((hb, CHUNK, BAND), lambda g, c: (g, 0, 0))
    return pl.pallas_call(
        body, name=name, grid=(h // hb, nc),
        in_specs=[blk_q, blk_k, blk_k, blk_b, blk_q],
        out_specs=[blk_q, blk_k, blk_k, blk_b],
        out_shape=[jax.ShapeDtypeStruct((h, t, HEAD), F32), jax.ShapeDtypeStruct((h, t + PAD, HEAD), F32),
                   jax.ShapeDtypeStruct((h, t + PAD, HEAD), F32), jax.ShapeDtypeStruct((h, CHUNK, BAND), F32)],
        compiler_params=_cparams(2))(qn, knp, vp, bias, do)


def _rel_index():
    i = np.arange(CHUNK)[:, None]
    j = np.arange(BAND)[None, :]
    return np.clip(i + PAD - j, -(CHUNK - 1), REL_CLIP) + (CHUNK - 1)


def relbias_reduce(dbias, *, name):
    h = dbias.shape[0]
    onehot = jnp.asarray((_rel_index()[:, :, None] == np.arange(N_REL)[None, None, :]).astype(np.float32))
    dbt = jnp.swapaxes(dbias, 0, 1)

    def body(d_ref, oh_ref, o_ref):
        @pl.when(pl.program_id(0) == 0)
        def _():
            o_ref[...] = jnp.zeros_like(o_ref)
        o_ref[...] += hdot(d_ref[...], oh_ref[...])

    return pl.pallas_call(
        body, name=name, grid=(CHUNK,),
        in_specs=[pl.BlockSpec((None, h, BAND), lambda i: (i, 0, 0)),
                  pl.BlockSpec((None, BAND, N_REL), lambda i: (i, 0, 0))],
        out_specs=pl.BlockSpec((h, N_REL), lambda i: (0, 0)),
        out_shape=jax.ShapeDtypeStruct((h, N_REL), F32), compiler_params=_cparams(1))(dbt, onehot)


def _bt(x):
    return jnp.swapaxes(x, 1, 2)


def _bmm_raw(x, y):
    return lax.dot_general(x, y, (((2,), (1,)), ((0,), (0,))), precision=HI, preferred_element_type=F32)


@jax.custom_vjp
def bmm(x, y):
    return _bmm_raw(x, y)


def _bmm_fwd(x, y):
    return _bmm_raw(x, y), (x, y)


def _bmm_bwd(saved, dz):
    x, y = saved
    return _bmm_raw(dz, _bt(y)), _bmm_raw(_bt(x), dz)


bmm.defvjp(_bmm_fwd, _bmm_bwd)


def rwkv_chunk(p0, r, lw, k, v, a, b):
    g, c, n = r.shape
    row = lax.broadcasted_iota(jnp.int32, (g, c, c), 1)
    col = lax.broadcasted_iota(jnp.int32, (g, c, c), 2)
    incl, strict = row >= col, row > col
    cs = bmm(incl.astype(F32), lw)
    cs_end = cs[:, c - 1:c, :]
    e_cs = jnp.exp(cs)
    e_neg = jnp.exp(-cs)
    at = a * jnp.exp(cs - lw)
    rt = r * e_cs
    bt_, kt = b * e_neg, k * e_neg
    e_tail = jnp.exp(cs_end - cs)
    bh, kh = b * e_tail, k * e_tail
    btt, ktt = _bt(bt_), _bt(kt)
    a_ab = jnp.where(strict, bmm(at, btt), 0.0)
    a_ak = jnp.where(strict, bmm(at, ktt), 0.0)
    a_rb = jnp.where(incl, bmm(rt, btt), 0.0)
    a_rk = jnp.where(incl, bmm(rt, ktt), 0.0)
    tinv = jnp.where(row == col, 1.0, 0.0) + a_ab
    npow = a_ab
    for _ in range(int(math.log2(c)) - 1):
        npow = bmm(npow, npow)
        tinv = tinv + bmm(tinv, npow)
    u = bmm(tinv, bmm(at, p0) + bmm(a_ak, v))
    y = bmm(rt, p0) + bmm(a_rb, u) + bmm(a_rk, v)
    rown = lax.broadcasted_iota(jnp.int32, (g, n, n), 1)
    coln = lax.broadcasted_iota(jnp.int32, (g, n, n), 2)
    dg = jnp.where(rown == coln, jnp.exp(cs_end), 0.0)
    p1 = bmm(dg, p0) + bmm(_bt(bh), u) + bmm(_bt(kh), v)
    return y, p1


def rwkv_fwd(r, lw, k, v, a, b, *, name):
    h, t, n = r.shape
    g, c = min(RW_HEADS, h), RW_CHUNK
    nch = t // c

    def body(r_ref, lw_ref, k_ref, v_ref, a_ref, b_ref, y_ref, p_ref, st_ref):
        @pl.when(pl.program_id(1) == 0)
        def _():
            st_ref[...] = jnp.zeros_like(st_ref)
        p0 = st_ref[...]
        p_ref[...] = p0
        y, p1 = rwkv_chunk(p0, r_ref[...], lw_ref[...], k_ref[...], v_ref[...], a_ref[...], b_ref[...])
        y_ref[...] = y
        st_ref[...] = p1

    blk = pl.BlockSpec((g, c, n), lambda i, j: (i, j, 0))
    pblk = pl.BlockSpec((g, None, n, n), lambda i, j: (i, j, 0, 0))
    return pl.pallas_call(
        body, name=name, grid=(h // g, nch), in_specs=[blk] * 6, out_specs=[blk, pblk],
        out_shape=[jax.ShapeDtypeStruct((h, t, n), F32), jax.ShapeDtypeStruct((h, nch, n, n), F32)],
        scratch_shapes=[pltpu.VMEM((g, n, n), F32)], compiler_params=_cparams(2))(r, lw, k, v, a, b)


def rwkv_bwd(r, lw, k, v, a, b, p0s, dy, *, name):
    h, t, n = r.shape
    g, c = min(RW_HEADS, h), RW_CHUNK
    nch = t // c

    def body(r_ref, lw_ref, k_ref, v_ref, a_ref, b_ref, p_ref, dy_ref,
             dr_ref, dlw_ref, dk_ref, dv_ref, da_ref, db_ref, dp_ref):
        @pl.when(pl.program_id(1) == 0)
        def _():
            dp_ref[...] = jnp.zeros_like(dp_ref)
        _, pull = jax.vjp(rwkv_chunk, p_ref[...], r_ref[...], lw_ref[...], k_ref[...], v_ref[...],
                          a_ref[...], b_ref[...])
        dp0, dr, dlw, dk, dv, da, db = pull((dy_ref[...], dp_ref[...]))
        dr_ref[...] = dr
        dlw_ref[...] = dlw
        dk_ref[...] = dk
        dv_ref[...] = dv
        da_ref[...] = da
        db_ref[...] = db
        dp_ref[...] = dp0

    blk = pl.BlockSpec((g, c, n), lambda i, j: (i, nch - 1 - j, 0))
    pblk = pl.BlockSpec((g, None, n, n), lambda i, j: (i, nch - 1 - j, 0, 0))
    return pl.pallas_call(
        body, name=name, grid=(h // g, nch), in_specs=[blk] * 6 + [pblk, blk], out_specs=[blk] * 6,
        out_shape=[jax.ShapeDtypeStruct((h, t, n), F32)] * 6,
        scratch_shapes=[pltpu.VMEM((g, n, n), F32)], compiler_params=_cparams(2))(r, lw, k, v, a, b, p0s, dy)


def _time_block(t):
    return _pick(t, (256, 128, 64))


def ssm_scan_fwd(bu_re, bu_im, a_re, a_im, *, name):
    t, rr, ln = bu_re.shape
    tb = _time_block(t)

    def body(br_ref, bi_ref, ar_ref, ai_ref, hr_ref, hi_ref, sr_ref, si_ref):
        @pl.when(pl.program_id(0) == 0)
        def _():
            sr_ref[...] = jnp.zeros_like(sr_ref)
            si_ref[...] = jnp.zeros_like(si_ref)
        ar, ai = ar_ref[...], ai_ref[...]

        def step(i, carry):
            hr, hi = carry
            nr = ar * hr - ai * hi + br_ref[i]
            ni = ar * hi + ai * hr + bi_ref[i]
            hr_ref[i] = nr
            hi_ref[i] = ni
            return nr, ni

        hr, hi = lax.fori_loop(0, tb, step, (sr_ref[...], si_ref[...]))
        sr_ref[...] = hr
        si_ref[...] = hi

    blk = pl.BlockSpec((tb, rr, ln), lambda i: (i, 0, 0))
    cblk = pl.BlockSpec((rr, ln), lambda i: (0, 0))
    return pl.pallas_call(
        body, name=name, grid=(t // tb,), in_specs=[blk, blk, cblk, cblk], out_specs=[blk, blk],
        out_shape=[jax.ShapeDtypeStruct((t, rr, ln), F32)] * 2,
        scratch_shapes=[pltpu.VMEM((rr, ln), F32)] * 2, compiler_params=_cparams(1))(bu_re, bu_im, a_re, a_im)


def ssm_scan_bwd(dh_re, dh_im, hp_re, hp_im, a_re, a_im, *, name):
    t, rr, ln = dh_re.shape
    tb = _time_block(t)
    nb = t // tb

    def body(dr_ref, di_ref, pr_ref, pi_ref, ar_ref, ai_ref, gr_ref, gi_ref, dar_ref, dai_ref, sr_ref, si_ref):
        @pl.when(pl.program_id(0) == 0)
        def _():
            sr_ref[...] = jnp.zeros_like(sr_ref)
            si_ref[...] = jnp.zeros_like(si_ref)
            dar_ref[...] = jnp.zeros_like(dar_ref)
            dai_ref[...] = jnp.zeros_like(dai_ref)
        ar, ai = ar_ref[...], ai_ref[...]

        def step(ii, carry):
            gr, gi, dar, dai = carry
            i = tb - 1 - ii
            nr = dr_ref[i] + ar * gr + ai * gi
            ni = di_ref[i] - ai * gr + ar * gi
            gr_ref[i] = nr
            gi_ref[i] = ni
            pr, pi = pr_ref[i], pi_ref[i]
            dar = dar + nr * pr + ni * pi
            dai = dai - nr * pi + ni * pr
            return nr, ni, dar, dai

        gr, gi, dar, dai = lax.fori_loop(0, tb, step, (sr_ref[...], si_ref[...], dar_ref[...], dai_ref[...]))
        sr_ref[...] = gr
        si_ref[...] = gi
        dar_ref[...] = dar
        dai_ref[...] = dai

    blk = pl.BlockSpec((tb, rr, ln), lambda i: (nb - 1 - i, 0, 0))
    cblk = pl.BlockSpec((rr, ln), lambda i: (0, 0))
    return pl.pallas_call(
        body, name=name, grid=(nb,), in_specs=[blk] * 4 + [cblk, cblk], out_specs=[blk, blk, cblk, cblk],
        out_shape=[jax.ShapeDtypeStruct((t, rr, ln), F32)] * 2 + [jax.ShapeDtypeStruct((rr, ln), F32)] * 2,
        scratch_shapes=[pltpu.VMEM((rr, ln), F32)] * 2,
        compiler_params=_cparams(1))(dh_re, dh_im, hp_re, hp_im, a_re, a_im)


ANY = pl.BlockSpec(memory_space=pl.ANY)


def _rows(ref, lead, ch, nchunk):
    base = ref if lead is None else ref.at[lead]
    if nchunk == 1:
        return base
    n = base.shape[0] // nchunk
    return base.at[pl.ds(ch * n, n)]


class Comm:
    def __init__(self):
        self.ins, self.outs, self.ops = [], [], []
        self.n_remote, self.n_local = 0, 0

    def _add(self, kind, src, out_shape, n_peers, nchunk):
        lead_len = src.shape[1] if kind == "scatter" else src.shape[0]
        while lead_len % nchunk:
            nchunk //= 2
        self.ops.append((kind, len(self.ins), len(self.outs), self.n_remote, self.n_local, nchunk))
        self.ins.append(src)
        self.outs.append(jax.ShapeDtypeStruct(out_shape, src.dtype))
        self.n_remote += n_peers * nchunk
        self.n_local += 1
        return len(self.outs) - 1

    def all_gather(self, w, nchunk=2):
        return self._add("gather", w, (4,) + w.shape, 3, nchunk)

    def scatter(self, g4, nchunk=2):
        return self._add("scatter", g4, g4.shape, 3, nchunk)

    def swap(self, s, nchunk=8):
        return self._add("swap", s, (2,) + s.shape, 1, nchunk)

    def gather_all(self, v, nchunk=1):
        return self._add("gather_all", v, (8,) + v.shape, 7, nchunk)

    def sem_shapes(self):
        return [pltpu.SemaphoreType.DMA((self.n_remote,)), pltpu.SemaphoreType.DMA((self.n_remote,)),
                pltpu.SemaphoreType.DMA((self.n_local,))]

    def _descs(self, c_in, c_out, send, recv, lsem):
        x, y, c = lax.axis_index("x"), lax.axis_index("y"), lax.axis_index("c")
        sends, recvs, locs = [], [], []
        for kind, ii, oi, r0, l0, nchunk in self.ops:
            src, dst = c_in[ii], c_out[oi]
            if kind == "swap":
                me = c
                peers = [((x, y, 1 - c), 1 - c)]
            elif kind == "gather_all":
                me = 4 * x + 2 * y + c
                flips = [(dx, dy, dc) for dx in (0, 1) for dy in (0, 1) for dc in (0, 1) if dx + dy + dc]
                peers = []
                for dx, dy, dc in flips:
                    px, py, pc = (x + dx) % 2, (y + dy) % 2, (c + dc) % 2
                    peers.append(((px, py, pc), 4 * px + 2 * py + pc))
            else:
                me = 2 * x + y
                peers = [((px, py, c), 2 * px + py) for px, py in ((1 - x, y), (x, 1 - y), (1 - x, 1 - y))]
            if kind == "scatter":
                locs.append(pltpu.make_async_copy(src.at[me], dst.at[me], lsem.at[l0]))
            else:
                locs.append(pltpu.make_async_copy(src, dst.at[me], lsem.at[l0]))
            for pj, (dev, peer_slot) in enumerate(peers):
                for ch in range(nchunk):
                    k = r0 + pj * nchunk + ch
                    s_src = _rows(src, peer_slot if kind == "scatter" else None, ch, nchunk)
                    mk = functools.partial(pltpu.make_async_remote_copy, send_sem=send.at[k], recv_sem=recv.at[k],
                                           device_id=dev, device_id_type=MESH)
                    sends.append(mk(src_ref=s_src, dst_ref=_rows(dst, me, ch, nchunk)))
                    recvs.append(mk(src_ref=s_src, dst_ref=_rows(dst, peer_slot, ch, nchunk)))
        return sends, recvs, locs

    def start(self, c_in, c_out, send, recv, lsem):
        sends, _, locs = self._descs(c_in, c_out, send, recv, lsem)
        for d in locs + sends:
            d.start()

    def wait(self, c_in, c_out, send, recv, lsem):
        sends, recvs, locs = self._descs(c_in, c_out, send, recv, lsem)
        for d in recvs:
            d.wait_recv()
        for d in sends:
            d.wait_send()
        for d in locs:
            d.wait()


def run_comm(comm, *, name):
    n_cin, n_cout = len(comm.ins), len(comm.outs)

    def body(*refs):
        c_in, c_out, sems = refs[:n_cin], refs[n_cin:n_cin + n_cout], refs[n_cin + n_cout:]
        comm.start(c_in, c_out, *sems)
        comm.wait(c_in, c_out, *sems)

    outs = pl.pallas_call(
        body, name=name, in_specs=[ANY] * n_cin, out_specs=[ANY] * n_cout, out_shape=list(comm.outs),
        scratch_shapes=comm.sem_shapes(), compiler_params=pltpu.CompilerParams(has_side_effects=True))(*comm.ins)
    return list(outs)


def sum_slots(x, *, name):
    s, r, c = x.shape
    tb = 8
    while tb * 2 <= min(r, 512) and r % (tb * 2) == 0 and tb * 2 * c * 4 * (s + 1) <= ROW_BLOCK_BYTES:
        tb *= 2

    def body(x_ref, o_ref):
        acc = x_ref[0].astype(F32)
        for i in range(1, s):
            acc = acc + x_ref[i].astype(F32)
        o_ref[...] = acc

    return pl.pallas_call(
        body, name=name, grid=(r // tb,), in_specs=[pl.BlockSpec((s, tb, c), lambda i: (0, i, 0))],
        out_specs=pl.BlockSpec((tb, c), lambda i: (i, 0)),
        out_shape=jax.ShapeDtypeStruct((r, c), F32), compiler_params=_cparams(1))(x)


def _adam_math(w, g, m, v):
    m = ADAM_B1 * m + (1.0 - ADAM_B1) * g
    v = ADAM_B2 * v + (1.0 - ADAM_B2) * (g * g)
    m_hat = m / (1.0 - ADAM_B1 ** ADAM_STEP)
    v_hat = v / (1.0 - ADAM_B2 ** ADAM_STEP)
    delta = -ADAM_LR * (m_hat / (jnp.sqrt(v_hat) + ADAM_EPS) + ADAM_WD * w)
    return delta, m, v


def adamw_pair(w, g2, m, v, *, name):
    r, c = w.shape
    tb = 8
    while tb * 2 <= min(r, 512) and r % (tb * 2) == 0 and tb * 2 * c * 4 * 9 <= 2 * ROW_BLOCK_BYTES:
        tb *= 2

    def body(w_ref, g_ref, m_ref, v_ref, go_ref, d_ref, mo_ref, vo_ref):
        g = g_ref[0] + g_ref[1]
        d, mn, vn = _adam_math(w_ref[...], g, m_ref[...], v_ref[...])
        go_ref[...] = g
        d_ref[...] = d
        mo_ref[...] = mn
        vo_ref[...] = vn

    blk = pl.BlockSpec((tb, c), lambda i: (i, 0))
    return pl.pallas_call(
        body, name=name, grid=(r // tb,), in_specs=[blk, pl.BlockSpec((2, tb, c), lambda i: (0, i, 0)), blk, blk],
        out_specs=[blk] * 4, out_shape=[jax.ShapeDtypeStruct((r, c), F32)] * 4,
        compiler_params=_cparams(1))(w, g2, m, v)


def adamw_flat(w, g, m, v, *, name):
    def fn(w_, g_, m_, v_):
        return _adam_math(w_, g_, m_, v_)
    return rowk(fn, [w, g, m, v], [], [(w.shape[1], F32)] * 3, name=name)


def to_heads(x):
    t, d = x.shape
    return x.reshape(t, d // HEAD, HEAD).transpose(1, 0, 2)


def from_heads(x):
    h, t, n = x.shape
    return x.transpose(1, 0, 2).reshape(t, h * n)


def pack_flat(arrs, lanes=128, row_mult=8):
    flat = jnp.concatenate([a.reshape(-1).astype(F32) for a in arrs])
    n = flat.shape[0]
    rows = -(-n // lanes)
    rows = -(-rows // row_mult) * row_mult
    return jnp.pad(flat, (0, rows * lanes - n)).reshape(rows, lanes)


def unpack_flat(buf, shapes):
    flat = buf.reshape(-1)
    outs, off = [], 0
    for s in shapes:
        n = int(np.prod(s))
        outs.append(flat[off:off + n].reshape(s))
        off += n
    return outs


def block_diag_from(w_gab):
    g, a, b = w_gab.shape
    eye = jnp.eye(g, dtype=w_gab.dtype)
    return (w_gab[:, :, None, :] * eye[:, None, :, None]).reshape(g * a, g * b)


def block_diag_extract(m, g):
    a, b = m.shape[0] // g, m.shape[1] // g
    idx = jnp.arange(g)
    return m.reshape(g, a, g, b)[idx, :, idx, :]


def kernel(x, p, ffn1_norm, ffn1_w_gate, ffn1_w_up, ffn1_w_down, mix_norm, ffn2_norm, ffn2_w_gate, ffn2_w_up, ffn2_w_down, ple_norm, ple_w_gate, ple_w_proj, ab_w_in, att_q_gain, att_k_gain, att_rel_bias, rwkv_mu, rwkv_w0, rwkv_w_up, rwkv_a0, rwkv_a_up, rwkv_g_up, rwkv_k_k, rwkv_k_a, rwkv_r_k, rwkv_lnx_w, rwkv_lnx_b, ab_w_out, ssm_w_in, ssm_lambda_re, ssm_lambda_im, ssm_log_dt, ssm_b_re, ssm_b_im, ssm_c_re, ssm_c_im, ssm_d, ssm_w_out, loss_target, m_ffn1_norm, m_ffn1_w_gate, m_ffn1_w_up, m_ffn1_w_down, m_mix_norm, m_ffn2_norm, m_ffn2_w_gate, m_ffn2_w_up, m_ffn2_w_down, m_ple_norm, m_ple_w_gate, m_ple_w_proj, m_ab_w_in, m_att_q_gain, m_att_k_gain, m_att_rel_bias, m_rwkv_mu, m_rwkv_w0, m_rwkv_w_up, m_rwkv_a0, m_rwkv_a_up, m_rwkv_g_up, m_rwkv_k_k, m_rwkv_k_a, m_rwkv_r_k, m_rwkv_lnx_w, m_rwkv_lnx_b, m_ab_w_out, m_ssm_w_in, m_ssm_lambda_re, m_ssm_lambda_im, m_ssm_log_dt, m_ssm_b_re, m_ssm_b_im, m_ssm_c_re, m_ssm_c_im, m_ssm_d, m_ssm_w_out, v_ffn1_norm, v_ffn1_w_gate, v_ffn1_w_up, v_ffn1_w_down, v_mix_norm, v_ffn2_norm, v_ffn2_w_gate, v_ffn2_w_up, v_ffn2_w_down, v_ple_norm, v_ple_w_gate, v_ple_w_proj, v_ab_w_in, v_att_q_gain, v_att_k_gain, v_att_rel_bias, v_rwkv_mu, v_rwkv_w0, v_rwkv_w_up, v_rwkv_a0, v_rwkv_a_up, v_rwkv_g_up, v_rwkv_k_k, v_rwkv_k_a, v_rwkv_r_k, v_rwkv_lnx_w, v_rwkv_lnx_b, v_ab_w_out, v_ssm_w_in, v_ssm_lambda_re, v_ssm_lambda_im, v_ssm_log_dt, v_ssm_b_re, v_ssm_b_im, v_ssm_c_re, v_ssm_c_im, v_ssm_d, v_ssm_w_out):
    A = dict(locals())
    W = {n: A[n] for n in W_NAMES}
    return _step(A['x'], A['p'], A['loss_target'], W, {n: A['m_' + n] for n in W_NAMES},
                 {n: A['v_' + n] for n in W_NAMES})


def _step(x, p, target, W, M, V):
    assert x.shape[0] == 1
    t, d = x.shape[1], x.shape[2]
    depth = p.shape[0]
    h0 = x[0]
    tgt = target[0]
    qchip = 2 * lax.axis_index("x") + lax.axis_index("y")
    d_rw = W['rwkv_w0'].shape[1]
    d_att = W['ab_w_out'].shape[1] * 4 - d_rw
    n_h_att, n_h_rw = d_att // HEAD, d_rw // HEAD
    n_bin = 3 * d_rw + DECAY_LORA + AAA_LORA + GATE_LORA
    d_ssm = W['ssm_w_in'].shape[2]
    n_grp = d_ssm // SSM_GROUP
    gp = n_grp * SSM_STATE

    queue = []
    gathered = {}
    finished = {}
    grads = {}
    queued_grads = set()
    n_alone = [0]

    def ag_thunk(name, layer):
        def thunk():
            cm = Comm()
            cm.all_gather(W[name][layer].astype(BF16))

            def cont(outs):
                gathered[(name, layer)] = outs[0]
            return cm, cont
        return thunk

    def rs_thunk(name, layer, g4):
        shard_shape = W[name].shape[1:]
        rows, cols = int(np.prod(shard_shape[:-1])), shard_shape[-1]

        def thunk():
            cm = Comm()
            cm.scatter(g4.reshape((4,) + shard_shape))

            def cont(outs):
                s_c = sum_slots(outs[0].reshape(4, rows, cols), name=f"rs_sum_{name}")

                def thunk2():
                    cm2 = Comm()
                    cm2.swap(s_c)

                    def cont2(outs2):
                        finished[(name, layer)] = adamw_pair(
                            W[name][layer].reshape(rows, cols), outs2[0], M[name][layer].reshape(rows, cols),
                            V[name][layer].reshape(rows, cols), name=f"adamw_{name}")
                    return cm2, cont2
                queue.append(thunk2)
            return cm, cont
        return thunk

    def enqueue_ready():
        for n in BIG:
            for li_, g4 in enumerate(grads.get(n, [])):
                if g4 is not None and (n, li_) not in queued_grads:
                    queued_grads.add((n, li_))
                    queue.append(rs_thunk(n, li_, g4))

    def hmm(*args, **kw):
        enqueue_ready()
        if not queue:
            return mm(*args, **kw)
        cm, cont = queue.pop(0)()
        out, couts = mm(*args, comm=cm, **kw)
        cont(couts)
        return out

    def flush(until=None):
        enqueue_ready()
        while queue and not (until is not None and until()):
            cm, cont = queue.pop(0)()
            n_alone[0] += 1
            cont(run_comm(cm, name=f"comm_alone{n_alone[0]}"))
            enqueue_ready()

    def gather(name, layer):
        flush(until=lambda: (name, layer) in gathered)
        return gathered[(name, layer)]
    small = {}

    def add_small(name, val, layer=None, nl=1):
        if layer is None:
            small[name] = val
        else:
            small.setdefault(name, [None] * nl)[layer] = val

    def ffn_fwd(h, pre, i):
        g = W[pre + '_norm'][i][None]
        wg, wu, wd = gather(pre + '_w_gate', i), gather(pre + '_w_up', i), gather(pre + '_w_down', i)
        wd2 = wd.reshape(-1, d)
        n = rowk(lambda hh, gg: f_rms(hh, gg), [h], [g], [(d, BF16)], name=f"{pre}_rms")[0]
        a = hmm(n, wg, out_dtype=BF16, name=f"{pre}_gate")
        b = hmm(n, wu, out_dtype=BF16, name=f"{pre}_up")
        f = a.shape[1]
        u = rowk(lambda aa, bb: f_swiglu(*_f32(aa, bb)), [a, b], [], [(f, BF16)], name=f"{pre}_swiglu")[0]
        h_out = hmm(u, wd2, res=h, alpha=0.5, name=f"{pre}_down")
        return h_out, dict(h=h, g=g, n=n, a=a, b=b, u=u, wg=wg, wu=wu, wd2=wd2)

    def ffn_bwd(dh, sv, pre, i):
        f = sv['a'].shape[1]
        dwd = hmm(sv['u'], dh, ta=True, alpha=0.5, out_dtype=GRAD_XFER, name=f"{pre}_d_wdown")
        du = hmm(dh, sv['wd2'], tb=True, alpha=0.5, out_dtype=BF16, name=f"{pre}_d_u")
        da, db = rowk(vjp_rows(f_swiglu, 2, 1), [sv['a'], sv['b'], du], [], [(f, BF16), (f, BF16)],
                      name=f"{pre}_d_swiglu")
        dwg = hmm(sv['n'], da, ta=True, nshard=4, out3=True, out_dtype=GRAD_XFER, name=f"{pre}_d_wgate")
        dwu = hmm(sv['n'], db, ta=True, nshard=4, out3=True, out_dtype=GRAD_XFER, name=f"{pre}_d_wup")
        dn = hmm(da, sv['wg'], tb=True, name=f"{pre}_d_n1")
        dn = hmm(db, sv['wu'], tb=True, res=dn, name=f"{pre}_d_n2")
        dh_in, dg = rms_bwd(sv['h'], dn, dh, sv['g'], name=f"{pre}_d_rms")
        grads.setdefault(pre + '_w_gate', [None] * depth)[i] = dwg
        grads.setdefault(pre + '_w_up', [None] * depth)[i] = dwu
        grads.setdefault(pre + '_w_down', [None] * depth)[i] = dwd.reshape(4, -1, d)
        add_small(pre + '_norm', dg[0], i, depth)
        return dh_in

    def rms_bwd(h, dn, dh_res, g, *, name):
        def fn(hh, dnn, dres, gg):
            _, pull = jax.vjp(f_rms, hh, gg)
            dh_, dg_ = pull(dnn)
            return dh_ + dres, dg_
        return rowk(fn, [h, dn, dh_res], [g], [(d, F32)], [(1, d)], name=name)

    def head_consts(nh):
        e = np.kron(np.eye(nh, dtype=np.float32), np.ones((HEAD, 1), np.float32))
        return jnp.asarray(e), jnp.asarray(e.T)

    def mixer_ab_fwd(h):
        g = W['mix_norm'][0][None]
        win, wout = gather('ab_w_in', 0), gather('ab_w_out', 0).reshape(-1, d)
        hn = rowk(lambda hh, gg: f_rms(hh, gg), [h], [g], [(d, BF16)], name="mixab_rms")[0]
        proj4 = hmm(hn, win, out3=True, name="mixab_proj")
        proj = proj4.transpose(1, 0, 2).reshape(t, -1)
        q2, k2, v2 = [to_heads(proj[:, j * d_att:(j + 1) * d_att]).reshape(n_h_att * t, HEAD) for j in range(3)]
        qg, kg = W['att_q_gain'], W['att_k_gain']
        f_qn = lambda qq, gg: f_rms(qq, gg) * (HEAD ** -0.5)
        qn = rowk(f_qn, [q2], [qg], [(HEAD, BF16)], name="att_qnorm")[0].reshape(n_h_att, t, HEAD)
        kn = rowk(f_rms, [k2], [kg], [(HEAD, BF16)], name="att_knorm")[0].reshape(n_h_att, t, HEAD)
        knp = jnp.pad(kn, ((0, 0), (PAD, 0), (0, 0)))
        vp = jnp.pad(v2.astype(BF16).reshape(n_h_att, t, HEAD), ((0, 0), (PAD, 0), (0, 0)))
        bias = W['att_rel_bias'][0][:, _rel_index()]
        o = att_fwd(qn, knp, vp, bias, name="att_fwd")
        att = from_heads(o)
        mu = W['rwkv_mu']
        zs = ts_fwd(proj, 3 * d_att, n_bin, mu, name="rwkv_shift")
        e, et = head_consts(n_h_rw)
        zpad = jnp.zeros((AAA_LORA, d_rw), F32)
        wup_p = jnp.concatenate([W['rwkv_w_up_full'], zpad], 0)
        aup_p = jnp.concatenate([zpad, W['rwkv_a_up_full']], 0)
        pre_c = [W['rwkv_w0'], wup_p, W['rwkv_a0'], aup_p, W['rwkv_g_up_full'], W['rwkv_k_k'], W['rwkv_k_a'], e, et]
        pre = rowk(f_rwkv_pre, [zs], pre_c, [(d_rw, F32)] * 7, name="rwkv_pre")
        r_, lw_, kk_, vv_, ia_, ib_, gg_ = pre
        hm = [to_heads(u_) for u_ in (r_, lw_, kk_, vv_, ia_, ib_)]
        y_h, p0s = rwkv_fwd(*hm, name="rwkv_scan")
        y = from_heads(y_h)
        post_c = [W['rwkv_r_k'].reshape(1, d_rw), W['rwkv_lnx_w'], W['rwkv_lnx_b'], e, et]
        rw = rowk(f_rwkv_post, [y, r_, kk_, vv_, gg_], post_c, [(d_rw, BF16)], name="rwkv_post")[0]
        cat = jnp.concatenate([att, rw], axis=1)
        h_out = hmm(cat, wout, res=h, name="mixab_out")
        sv = dict(h=h, g=g, hn=hn, win=win, wout=wout, proj=proj, q2=q2, k2=k2, qn=qn, knp=knp, vp=vp, bias=bias,
                  zs=zs, pre_c=pre_c, pre=pre, hm=hm, p0s=p0s, y=y, post_c=post_c, cat=cat, qg=qg, kg=kg, mu=mu)
        return h_out, sv

    def mixer_ab_bwd(dh, sv):
        dwout = hmm(sv['cat'], dh, ta=True, out_dtype=GRAD_XFER, name="mixab_d_wout")
        grads['ab_w_out'] = [dwout.reshape(4, -1, d)]
        dcat = hmm(dh, sv['wout'], tb=True, name="mixab_d_cat")
        datt, drw = dcat[:, :d_att], dcat[:, d_att:]
        r_, lw_, kk_, vv_, ia_, ib_, gg_ = sv['pre']
        post = rowk(vjp_rows(f_rwkv_post, 5, 1), [sv['y'], r_, kk_, vv_, gg_, drw], sv['post_c'],
                    [(d_rw, F32)] * 5, [(1, d_rw)] * 3 + [sv['post_c'][3].shape, sv['post_c'][4].shape],
                    name="rwkv_d_post")
        dy, dr1, dk1, dv1, dg1 = post[:5]
        add_small('rwkv_r_k', post[5].reshape(W['rwkv_r_k'].shape))
        add_small('rwkv_lnx_w', post[6])
        add_small('rwkv_lnx_b', post[7])
        dscan = rwkv_bwd(*sv['hm'], sv['p0s'], to_heads(dy), name="rwkv_d_scan")
        dr2, dlw, dk2, dv2, dia, dib = [from_heads(u_) for u_ in dscan]

        def pre_bwd(zs, dra, drb, dlw_, dka, dkb, dva, dvb, dia_, dib_, dg_, *consts):
            _, pull = jax.vjp(f_rwkv_pre, zs, *consts)
            return pull((dra + drb, dlw_, dka + dkb, dva + dvb, dia_, dib_, dg_))

        pc = sv['pre_c']
        preb = rowk(pre_bwd, [sv['zs'], dr1, dr2, dlw, dk1, dk2, dv1, dv2, dia, dib, dg1], pc,
                    [(n_bin, F32)], [c.shape for c in pc], name="rwkv_d_pre")
        dzs = preb[0]
        add_small('rwkv_w0', preb[1])
        add_small('rwkv_w_up', preb[2][:DECAY_LORA])
        add_small('rwkv_a0', preb[3])
        add_small('rwkv_a_up', preb[4][DECAY_LORA:])
        add_small('rwkv_g_up', preb[5])
        add_small('rwkv_k_k', preb[6])
        add_small('rwkv_k_a', preb[7])
        dz, dmu = ts_bwd(sv['proj'], 3 * d_att, dzs, sv['mu'], name="rwkv_d_shift")
        add_small('rwkv_mu', dmu)
        do = to_heads(datt).astype(BF16)
        dqn, dknp, dvp, dbias = att_bwd(sv['qn'], sv['knp'], sv['vp'], sv['bias'], do, name="att_bwd")
        add_small('att_rel_bias', relbias_reduce(dbias, name="att_d_relbias")[None])
        f_qn = lambda qq, gg: f_rms(qq, gg) * (HEAD ** -0.5)
        dq2, dqg = rowk(vjp_rows(f_qn, 1, 1), [sv['q2'], dqn.reshape(-1, HEAD)], [sv['qg']], [(HEAD, F32)],
                        [(1, HEAD)], name="att_d_qnorm")
        dk2_, dkg = rowk(vjp_rows(f_rms, 1, 1), [sv['k2'], dknp[:, PAD:].reshape(-1, HEAD)], [sv['kg']],
                         [(HEAD, F32)], [(1, HEAD)], name="att_d_knorm")
        add_small('att_q_gain', dqg)
        add_small('att_k_gain', dkg)
        dproj = jnp.concatenate([from_heads(dq2.reshape(n_h_att, t, HEAD)), from_heads(dk2_.reshape(n_h_att, t, HEAD)),
                                 from_heads(dvp[:, PAD:]), dz], axis=1)
        dproj4 = dproj.reshape(t, 4, -1).transpose(1, 0, 2).astype(BF16)
        grads['ab_w_in'] = [hmm(sv['hn'], dproj4, ta=True, out3=True, out_dtype=GRAD_XFER, name="mixab_d_win")]
        dhn = hmm(dproj4, sv['win'], tb=True, name="mixab_d_hn")
        dh_in, dg = rms_bwd(sv['h'], dhn, dh, sv['g'], name="mixab_d_rms")
        add_small('mix_norm', dg[0], 0, depth)
        return dh_in

    def ssm_params():
        lr, li = W['ssm_lambda_re'][0], W['ssm_lambda_im'][0]
        ldt = W['ssm_log_dt'][0][:, None]
        ab = rowk(f_ssm_ab, [lr, li, ldt], [], [(SSM_STATE, F32)] * 4, name="ssm_ab", tb=n_grp)
        br = W['ssm_b_re'][0].reshape(gp, SSM_GROUP)
        bi = W['ssm_b_im'][0].reshape(gp, SSM_GROUP)
        z_re, z_im = ab[2].reshape(gp, 1), ab[3].reshape(gp, 1)
        bb = rowk(f_ssm_bb, [br, bi, z_re, z_im], [], [(SSM_GROUP, F32)] * 2, name="ssm_bb", tb=gp)
        return dict(lr=lr, li=li, ldt=ldt, ab=ab, br=br, bi=bi, z_re=z_re, z_im=z_im, bb=bb)

    def mixer_s5_fwd(h):
        g = W['mix_norm'][1][None]
        win, wout = gather('ssm_w_in', 0).reshape(d, d_ssm), gather('ssm_w_out', 0)
        hn = rowk(lambda hh, gg: f_rms(hh, gg), [h], [g], [(d, BF16)], name="s5_rms")[0]
        u = hmm(hn, win, name="s5_in")
        sp = ssm_params()
        bbd_re = block_diag_from(sp['bb'][0].reshape(n_grp, SSM_STATE, SSM_GROUP).transpose(0, 2, 1)).astype(BF16)
        bbd_im = block_diag_from(sp['bb'][1].reshape(n_grp, SSM_STATE, SSM_GROUP).transpose(0, 2, 1)).astype(BF16)
        cbd_re = block_diag_from(W['ssm_c_re'][0].transpose(0, 2, 1)).astype(BF16)
        cbd_im = block_diag_from(W['ssm_c_im'][0].transpose(0, 2, 1)).astype(BF16)
        ub = u.astype(BF16)
        bu_re = hmm(ub, bbd_re, name="s5_bu_re").reshape(t, gp // 128, 128)
        bu_im = hmm(ub, bbd_im, name="s5_bu_im").reshape(t, gp // 128, 128)
        a_re, a_im = sp['ab'][0].reshape(gp // 128, 128), sp['ab'][1].reshape(gp // 128, 128)
        h_re, h_im = ssm_scan_fwd(bu_re, bu_im, a_re, a_im, name="s5_scan")
        hb_re, hb_im = h_re.reshape(t, gp).astype(BF16), h_im.reshape(t, gp).astype(BF16)
        y = hmm(hb_re, cbd_re, name="s5_y_re")
        y = hmm(hb_im, cbd_im, res=y, alpha=-1.0, name="s5_y_im")
        dsk = W['ssm_d_full']
        f_act = lambda yy, uu, dd: f_gelu(yy + dd * uu)
        yg = rowk(f_act, [y, u], [dsk], [(d_ssm, BF16)], name="s5_gelu")[0]
        z = hmm(yg, wout, name="s5_out")
        f_glu = lambda zz, hh: hh + zz[:, :d] * f_sigmoid(zz[:, d:])
        h_out = rowk(f_glu, [z, h], [], [(d, F32)], name="s5_glu")[0]
        sv = dict(h=h, g=g, hn=hn, win=win, wout=wout, u=u, ub=ub, sp=sp, bbd_re=bbd_re, bbd_im=bbd_im,
                  cbd_re=cbd_re, cbd_im=cbd_im, a_re=a_re, a_im=a_im, h_re=h_re, h_im=h_im, hb_re=hb_re,
                  hb_im=hb_im, y=y, dsk=dsk, yg=yg, z=z)
        return h_out, sv

    def mixer_s5_bwd(dh, sv):
        f_glu = lambda zz: zz[:, :d] * f_sigmoid(zz[:, d:])
        dz = rowk(vjp_rows(f_glu, 1, 1), [sv['z'], dh], [], [(2 * d, BF16)], name="s5_d_glu")[0]
        grads['ssm_w_out'] = [hmm(sv['yg'], dz, ta=True, nshard=4, out3=True, out_dtype=GRAD_XFER, name="s5_d_wout")]
        dyg = hmm(dz, sv['wout'], tb=True, name="s5_d_yg")
        f_act = lambda yy, uu, dd: f_gelu(yy + dd * uu)
        dy, du1, ddsk = rowk(vjp_rows(f_act, 2, 1), [sv['y'], sv['u'], dyg], [sv['dsk']],
                             [(d_ssm, F32), (d_ssm, F32)], [(1, d_ssm)], name="s5_d_gelu")
        add_small('ssm_d', ddsk)
        dyb = dy.astype(BF16)
        dcbd_re = hmm(sv['hb_re'], dyb, ta=True, name="s5_d_c_re")
        dcbd_im = hmm(sv['hb_im'], dyb, ta=True, alpha=-1.0, name="s5_d_c_im")
        add_small('ssm_c_re', block_diag_extract(dcbd_re, n_grp).transpose(0, 2, 1)[None])
        add_small('ssm_c_im', block_diag_extract(dcbd_im, n_grp).transpose(0, 2, 1)[None])
        dh_re = hmm(dyb, sv['cbd_re'], tb=True, name="s5_d_h_re").reshape(t, gp // 128, 128)
        dh_im = hmm(dyb, sv['cbd_im'], tb=True, alpha=-1.0, name="s5_d_h_im").reshape(t, gp // 128, 128)
        hp_re = jnp.pad(sv['h_re'][:-1], ((1, 0), (0, 0), (0, 0)))
        hp_im = jnp.pad(sv['h_im'][:-1], ((1, 0), (0, 0), (0, 0)))
        g_re, g_im, da_re, da_im = ssm_scan_bwd(dh_re, dh_im, hp_re, hp_im, sv['a_re'], sv['a_im'], name="s5_d_scan")
        gb_re, gb_im = g_re.reshape(t, gp).astype(BF16), g_im.reshape(t, gp).astype(BF16)
        dbbd_re = hmm(sv['ub'], gb_re, ta=True, name="s5_d_bb_re")
        dbbd_im = hmm(sv['ub'], gb_im, ta=True, name="s5_d_bb_im")
        du = hmm(gb_re, sv['bbd_re'], tb=True, res=du1, name="s5_d_u_re")
        du = hmm(gb_im, sv['bbd_im'], tb=True, res=du, name="s5_d_u_im")
        sp = sv['sp']
        dbb_re = block_diag_extract(dbbd_re, n_grp).transpose(0, 2, 1).reshape(gp, SSM_GROUP)
        dbb_im = block_diag_extract(dbbd_im, n_grp).transpose(0, 2, 1).reshape(gp, SSM_GROUP)
        dbr, dbi, dz_re, dz_im = rowk(vjp_rows(f_ssm_bb, 4, 2),
                                      [sp['br'], sp['bi'], sp['z_re'], sp['z_im'], dbb_re, dbb_im], [],
                                      [(SSM_GROUP, F32)] * 2 + [(1, F32)] * 2, name="ssm_d_bb", tb=gp)
        add_small('ssm_b_re', dbr.reshape(W['ssm_b_re'].shape))
        add_small('ssm_b_im', dbi.reshape(W['ssm_b_im'].shape))
        dlr, dli, dldt = rowk(vjp_rows(f_ssm_ab, 3, 4),
                              [sp['lr'], sp['li'], sp['ldt'], da_re.reshape(n_grp, SSM_STATE),
                               da_im.reshape(n_grp, SSM_STATE), dz_re.reshape(n_grp, SSM_STATE),
                               dz_im.reshape(n_grp, SSM_STATE)], [],
                              [(SSM_STATE, F32)] * 2 + [(1, F32)], name="ssm_d_ab", tb=n_grp)
        add_small('ssm_lambda_re', dlr[None])
        add_small('ssm_lambda_im', dli[None])
        add_small('ssm_log_dt', dldt.reshape(1, n_grp))
        grads['ssm_w_in'] = [hmm(sv['hn'], du, ta=True, out_dtype=GRAD_XFER, name="s5_d_win").reshape(4, -1, d_ssm)]
        dhn = hmm(du, sv['win'], tb=True, name="s5_d_hn")
        dh_in, dg = rms_bwd(sv['h'], dhn, dh, sv['g'], name="s5_d_rms")
        add_small('mix_norm', dg[0], 1, depth)
        return dh_in

    def ple_fwd(h, i):
        g = W['ple_norm'][i][None]
        wpg = gather('ple_w_gate', i).reshape(d, d)
        wpp = gather('ple_w_proj', i)
        n = rowk(lambda hh, gg: f_rms(hh, gg), [h], [g], [(d, BF16)], name="ple_rms")[0]
        zg = hmm(n, wpg, name="ple_gate")
        pb = p[i, 0].astype(BF16)
        pp = hmm(pb, wpp, name="ple_proj")
        f_ple = lambda zz, pq, hh: hh + f_sigmoid(zz) * pq
        h_out = rowk(f_ple, [zg, pp, h], [], [(d, F32)], name="ple_mix")[0]
        return h_out, dict(h=h, g=g, n=n, zg=zg, pp=pp, pb=pb, wpg=wpg, wpp=wpp)

    def ple_bwd(dh, sv, i):
        f_ple = lambda zz, pq: f_sigmoid(zz) * pq
        dzg, dpp = rowk(vjp_rows(f_ple, 2, 1), [sv['zg'], sv['pp'], dh], [], [(d, BF16), (d, BF16)],
                        name="ple_d_mix")
        grads.setdefault('ple_w_proj', [None] * depth)[i] = hmm(sv['pb'], dpp, ta=True, nshard=4, out3=True,
                                                               out_dtype=GRAD_XFER, name="ple_d_wproj")
        grads.setdefault('ple_w_gate', [None] * depth)[i] = hmm(sv['n'], dzg, ta=True, out_dtype=GRAD_XFER,
                                                               name="ple_d_wgate").reshape(4, -1, d)
        dn = hmm(dzg, sv['wpg'], tb=True, name="ple_d_n")
        dh_in, dg = rms_bwd(sv['h'], dn, dh, sv['g'], name="ple_d_rms")
        add_small('ple_norm', dg[0], i, depth)
        return dh_in

    ag_order = []
    for i in range(depth):
        ag_order += [('ffn1_w_gate', i), ('ffn1_w_up', i), ('ffn1_w_down', i)]
        ag_order += [('ab_w_in', 0), ('ab_w_out', 0)] if i % 2 == 0 else [('ssm_w_in', 0), ('ssm_w_out', 0)]
        ag_order += [('ffn2_w_gate', i), ('ffn2_w_up', i), ('ffn2_w_down', i), ('ple_w_gate', i), ('ple_w_proj', i)]
    first = Comm()
    for n in SMALL_SHARDED:
        first.all_gather(W[n])
    for n, li_ in ag_order[:2]:
        first.all_gather(W[n][li_].astype(BF16))
    first_out = run_comm(first, name="ag_first")
    W = dict(W)
    for n, full in zip(SMALL_SHARDED, first_out):
        w = W[n]
        W[n + '_full'] = jnp.moveaxis(full, 0, -2).reshape(w.shape[1:-1] + (4 * w.shape[-1],))
    W['ssm_d_full'] = W['ssm_d_full'][None]
    for key, g_ in zip(ag_order[:2], first_out[len(SMALL_SHARDED):]):
        gathered[key] = g_
    queue.extend(ag_thunk(n, li_) for n, li_ in ag_order[2:])

    h = h0
    saved = []
    for i in range(depth):
        sv = {}
        h, sv['ffn1'] = ffn_fwd(h, 'ffn1', i)
        if i % 2 == 0:
            h, sv['mix'] = mixer_ab_fwd(h)
        else:
            h, sv['mix'] = mixer_s5_fwd(h)
        h, sv['ffn2'] = ffn_fwd(h, 'ffn2', i)
        h, sv['ple'] = ple_fwd(h, i)
        saved.append(sv)

    def f_loss(y, tg):
        e = y - tg
        part = 0.5 * jnp.sum(jnp.sum(e * e, axis=-1, keepdims=True) * (1.0 / d), axis=0, keepdims=True)
        return e * (1.0 / d), jnp.broadcast_to(part, (1, 128))
    dh, loss_part = rowk(f_loss, [h, tgt], [], [(d, F32)], [(1, 128)], name="loss")
    loss = lax.psum(loss_part[0, 0], ("x", "y", "c"))

    for i in reversed(range(depth)):
        sv = saved[i]
        dh = ple_bwd(dh, sv['ple'], i)
        dh = ffn_bwd(dh, sv['ffn2'], 'ffn2', i)
        if i % 2 == 0:
            dh = mixer_ab_bwd(dh, sv['mix'])
        else:
            dh = mixer_s5_bwd(dh, sv['mix'])
        dh = ffn_bwd(dh, sv['ffn1'], 'ffn1', i)
    grad_x = dh[None]

    small_names = [n for n in W_NAMES if n not in BIG]
    small_full = []
    for n in small_names:
        v_ = small[n]
        if isinstance(v_, list):
            v_ = jnp.stack(v_)
        full_shape = W[n].shape[:-1] + (4 * W[n].shape[-1],) if n in SMALL_SHARDED else W[n].shape
        small_full.append(v_.reshape(full_shape))
    packed = pack_flat(small_full)
    ar = Comm()
    ar.gather_all(packed)
    summed = sum_slots(run_comm(ar, name="ar_small")[0], name="ar_small_sum")
    small_tot = unpack_flat(summed, [a.shape for a in small_full])
    g_small = {}
    for n, a in zip(small_names, small_tot):
        if n in SMALL_SHARDED:
            ns = W[n].shape[-1]
            a = lax.dynamic_slice_in_dim(a, qchip * ns, ns, axis=a.ndim - 1)
        g_small[n] = a

    out = {}
    pk = lambda dct: pack_flat([dct[n] for n in small_names])
    res = adamw_flat(pk(W), pk(g_small), pk(M), pk(V), name="adamw_small")
    shapes = [W[n].shape for n in small_names]
    for kind, buf in zip(('delta', 'm', 'v'), res):
        for n, a in zip(small_names, unpack_flat(buf, shapes)):
            out[(kind, n)] = a
    for n in small_names:
        out[('grad', n)] = g_small[n]

    flush()
    for n in BIG:
        per_layer = [finished[(n, li_)] for li_ in range(W[n].shape[0])]
        for kind, idx in zip(('grad', 'delta', 'm', 'v'), range(4)):
            out[(kind, n)] = jnp.stack([pl_[idx].reshape(W[n].shape[1:]) for pl_ in per_layer])

    return (loss, grad_x, *[out[('grad', n)] for n in W_NAMES], *[out[('delta', n)] for n in W_NAMES],
            *[out[('m', n)] for n in W_NAMES], *[out[('v', n)] for n in W_NAMES])
```

```python
import functools
import math

import numpy as np
import jax
import jax.numpy as jnp
from jax import lax
from jax.experimental import pallas as pl
from jax.experimental.pallas import tpu as pltpu

F32 = jnp.float32
BF16 = jnp.bfloat16
HI = lax.Precision.HIGHEST
MESH = pl.DeviceIdType.MESH

CHUNK = 64
N_LEFT = 8
BAND = (N_LEFT + 1) * CHUNK
PAD = N_LEFT * CHUNK
HEAD = 64
REL_CLIP = 128
N_REL = (CHUNK - 1) + REL_CLIP + 1
DECAY_LORA = 64
AAA_LORA = 64
GATE_LORA = 128
SSM_GROUP = 16
SSM_STATE = 64
RMS_EPS = 1e-6
GN_EPS = 64e-5
NEG_BIG = -1e30

ADAM_LR = 0.001
ADAM_B1 = 0.9
ADAM_B2 = 0.999
ADAM_EPS = 1e-08
ADAM_WD = 0.01
ADAM_STEP = 10

RW_CHUNK = 64
RW_HEADS = 4
ATT_HEADS = 2
VMEM_LIMIT = 56 * 1024 * 1024
ROW_BLOCK_BYTES = 6 * 1024 * 1024
GRAD_XFER = BF16
LOCAL_CHUNKS = 4

W_NAMES = ['ffn1_norm', 'ffn1_w_gate', 'ffn1_w_up', 'ffn1_w_down', 'mix_norm', 'ffn2_norm', 'ffn2_w_gate',
           'ffn2_w_up', 'ffn2_w_down', 'ple_norm', 'ple_w_gate', 'ple_w_proj', 'ab_w_in', 'att_q_gain',
           'att_k_gain', 'att_rel_bias', 'rwkv_mu', 'rwkv_w0', 'rwkv_w_up', 'rwkv_a0', 'rwkv_a_up',
           'rwkv_g_up', 'rwkv_k_k', 'rwkv_k_a', 'rwkv_r_k', 'rwkv_lnx_w', 'rwkv_lnx_b', 'ab_w_out',
           'ssm_w_in', 'ssm_lambda_re', 'ssm_lambda_im', 'ssm_log_dt', 'ssm_b_re', 'ssm_b_im', 'ssm_c_re',
           'ssm_c_im', 'ssm_d', 'ssm_w_out']
BIG = ['ffn1_w_gate', 'ffn1_w_up', 'ffn1_w_down', 'ffn2_w_gate', 'ffn2_w_up', 'ffn2_w_down',
       'ple_w_gate', 'ple_w_proj', 'ab_w_in', 'ab_w_out', 'ssm_w_in', 'ssm_w_out']
SMALL_SHARDED = ['rwkv_w_up', 'rwkv_a_up', 'rwkv_g_up', 'ssm_d']


def _cparams(n_axes):
    return pltpu.CompilerParams(dimension_semantics=("arbitrary",) * n_axes, vmem_limit_bytes=VMEM_LIMIT)


def _pick(n, prefs):
    for p in prefs:
        if n % p == 0:
            return p
    return n


def mm(a, b, *, ta=False, tb=False, nshard=None, out3=False, out_dtype=F32, res=None, alpha=1.0, comm=None, name):
    a3, b3 = a.ndim == 3, b.ndim == 3
    if a3:
        assert not ta
        sk, m, ks = a.shape
        k = sk * ks
    elif ta:
        k, m = a.shape
    else:
        m, k = a.shape
    kshard = None
    if b3 and not tb:
        s, kb, ns = b.shape
        n = s * ns
        nshard = s
    elif b3 and tb:
        sk2, n, ks2 = b.shape
        kb = sk2 * ks2
        kshard = (sk2, ks2)
    elif tb:
        n, kb = b.shape
    else:
        kb, n = b.shape
    assert k == kb, (a.shape, b.shape, ta, tb)
    if a3:
        assert kshard is None or kshard == (sk, ks)
        kshard = (sk, ks)
    if nshard is not None:
        tn, nj = n // nshard, nshard
    else:
        assert not out3
        tn = _pick(n, (1024, 1408, 1280, 512, 640, 256, 128))
        nj = n // tn
    if kshard is not None:
        nk, tk = kshard
    else:
        tk = _pick(k, (2048, 1408, 1024, 512, 256, 128))
        nk = k // tk
    tm = _pick(m, (512, 256, 128))
    ni = m // tm

    if a3:
        a_spec = pl.BlockSpec((None, tm, tk), lambda i, j, kk: (kk, i, 0))
    elif ta:
        a_spec = pl.BlockSpec((tk, tm), lambda i, j, kk: (kk, i))
    else:
        a_spec = pl.BlockSpec((tm, tk), lambda i, j, kk: (i, kk))
    if b3 and not tb:
        b_spec = pl.BlockSpec((None, tk, tn), lambda i, j, kk: (j, kk, 0))
    elif b3 and tb:
        b_spec = pl.BlockSpec((None, tn, tk), lambda i, j, kk: (kk, j, 0))
    elif tb:
        b_spec = pl.BlockSpec((tn, tk), lambda i, j, kk: (j, kk))
    else:
        b_spec = pl.BlockSpec((tk, tn), lambda i, j, kk: (kk, j))
    if out3:
        o_spec = pl.BlockSpec((None, tm, tn), lambda i, j, kk: (j, i, 0))
        o_shape = (nj, m, tn)
    else:
        o_spec = pl.BlockSpec((tm, tn), lambda i, j, kk: (i, j))
        o_shape = (m, n)
    has_res = res is not None
    dn = (((0 if ta else 1,), (1 if tb else 0,)), ((), ()))

    n_cin = len(comm.ins) if comm else 0
    n_cout = len(comm.outs) if comm else 0
    n_in = 2 + has_res

    def body(*refs):
        a_ref, b_ref = refs[0], refs[1]
        r_ref = refs[2] if has_res else None
        c_in = refs[n_in:n_in + n_cin]
        o_ref = refs[n_in + n_cin]
        c_out = refs[n_in + n_cin + 1:n_in + n_cin + 1 + n_cout]
        scratch = refs[n_in + n_cin + 1 + n_cout:]
        acc_ref = scratch[0] if nk > 1 else None
        sems = scratch[1:] if nk > 1 else scratch
        i, j, kk = pl.program_id(0), pl.program_id(1), pl.program_id(2)

        if comm:
            @pl.when((i == 0) & (j == 0) & (kk == 0))
            def _():
                comm.start(c_in, c_out, *sems)

        def finish(acc):
            val = acc * alpha if alpha != 1.0 else acc
            if has_res:
                val = val + r_ref[...].astype(F32)
            o_ref[...] = val.astype(o_ref.dtype)

        part = lax.dot_general(a_ref[...].astype(BF16), b_ref[...].astype(BF16), dn, preferred_element_type=F32)
        if nk == 1:
            finish(part)
        else:
            @pl.when(kk == 0)
            def _():
                acc_ref[...] = part

            @pl.when(kk > 0)
            def _():
                acc_ref[...] += part

            @pl.when(kk == nk - 1)
            def _():
                finish(acc_ref[...])

        if comm:
            @pl.when((i == ni - 1) & (j == nj - 1) & (kk == nk - 1))
            def _():
                comm.wait(c_in, c_out, *sems)

    in_specs = [a_spec, b_spec] + ([o_spec] if has_res else []) + [ANY] * n_cin
    args = (a, b) + ((res,) if has_res else ()) + (tuple(comm.ins) if comm else ())
    out_specs = [o_spec] + [ANY] * n_cout
    out_shape = [jax.ShapeDtypeStruct(o_shape, out_dtype)] + (list(comm.outs) if comm else [])
    scratch_shapes = ([pltpu.VMEM((tm, tn), F32)] if nk > 1 else []) + (comm.sem_shapes() if comm else [])
    aliases = {n_in + ii: 1 + oi for ii, oi in comm.aliases.items()} if comm else {}
    outs = pl.pallas_call(
        body, name=name, grid=(ni, nj, nk), in_specs=in_specs, out_specs=out_specs, out_shape=out_shape,
        scratch_shapes=scratch_shapes, input_output_aliases=aliases, compiler_params=_cparams(3))(*args)
    if comm:
        return outs[0], list(outs[1:])
    return outs[0]


def rowk(fn, rows, consts, out_rows, out_accs=(), *, name, tb=None):
    t = rows[0].shape[0]
    nr, nc, no, na = len(rows), len(consts), len(out_rows), len(out_accs)
    if tb is None:
        per_row = sum(r.shape[1] * 4 for r in rows) + sum(n * 4 for n, _ in out_rows)
        tb = 8
        while tb * 2 <= min(t, 1024) and tb * 2 * per_row <= ROW_BLOCK_BYTES and t % (tb * 2) == 0:
            tb *= 2
    assert t % tb == 0
    nb = t // tb

    def body(*refs):
        r_in, c_in = refs[:nr], refs[nr:nr + nc]
        o_rows, o_accs = refs[nr + nc:nr + nc + no], refs[nr + nc + no:]
        outs = fn(*[r[...] for r in r_in], *[c[...] for c in c_in])
        if not isinstance(outs, (tuple, list)):
            outs = (outs,)
        assert len(outs) == no + na, (name, len(outs), no, na)
        for ref, v in zip(o_rows, outs[:no]):
            ref[...] = v.astype(ref.dtype)
        if na:
            @pl.when(pl.program_id(0) == 0)
            def _():
                for ref in o_accs:
                    ref[...] = jnp.zeros_like(ref)
            for ref, v in zip(o_accs, outs[no:]):
                ref[...] += v.astype(F32)

    in_specs = [pl.BlockSpec((tb, r.shape[1]), lambda i: (i, 0)) for r in rows]
    in_specs += [pl.BlockSpec(c.shape, lambda i, nd=c.ndim: (0,) * nd) for c in consts]
    out_specs = [pl.BlockSpec((tb, n), lambda i: (i, 0)) for n, _ in out_rows]
    out_specs += [pl.BlockSpec(s, lambda i, nd=len(s): (0,) * nd) for s in out_accs]
    out_shape = [jax.ShapeDtypeStruct((t, n), d) for n, d in out_rows]
    out_shape += [jax.ShapeDtypeStruct(s, F32) for s in out_accs]
    res = pl.pallas_call(body, name=name, grid=(nb,), in_specs=in_specs, out_specs=out_specs,
                         out_shape=out_shape, compiler_params=_cparams(1))(*rows, *consts)
    return res


def _f32(*xs):
    return [x.astype(F32) for x in xs]


def vjp_rows(f, n_rows, n_cots):
    def fn(*args):
        rows = _f32(*args[:n_rows])
        cots = _f32(*args[n_rows:n_rows + n_cots])
        consts = _f32(*args[n_rows + n_cots:])
        outs, pull = jax.vjp(f, *rows, *consts)
        if not isinstance(outs, (tuple, list)):
            cots = cots[0]
        else:
            cots = tuple(cots)
        return pull(cots)
    return fn


def hdot(x, y):
    return jnp.dot(x, y, precision=HI, preferred_element_type=F32)


def f_rms(h, g):
    return h * lax.rsqrt(jnp.mean(h * h, axis=-1, keepdims=True) + RMS_EPS) * g


def f_sigmoid(x):
    return 1.0 / (1.0 + jnp.exp(-x))


def f_swiglu(a, b):
    return a * f_sigmoid(a) * b


def f_softplus(x):
    return jnp.maximum(x, 0.0) + jnp.log(1.0 + jnp.exp(-jnp.abs(x)))


def f_gelu(x):
    return 0.5 * x * (1.0 + jnp.tanh(math.sqrt(2.0 / math.pi) * (x + 0.044715 * (x * x * x))))


def f_rwkv_pre(zs, w0, wup_p, a0, aup_p, g_up, k_k, k_a, e, et):
    d = w0.shape[1]
    r, k, v = zs[:, :d], zs[:, d:2 * d], zs[:, 2 * d:3 * d]
    xwa = zs[:, 3 * d:3 * d + DECAY_LORA + AAA_LORA]
    xg = zs[:, 3 * d + DECAY_LORA + AAA_LORA:]
    w_log = -f_softplus(-(w0 + hdot(jnp.tanh(xwa), wup_p))) - 0.5
    logw = -jnp.exp(w_log)
    a = f_sigmoid(a0 + hdot(xwa, aup_p))
    g = hdot(f_sigmoid(xg), g_up)
    kk = k * k_k
    nrm = jnp.maximum(jnp.sqrt(hdot(kk * kk, e)), 1e-12)
    kk = kk * hdot(1.0 / nrm, et)
    k2 = k * (1.0 + (a - 1.0) * k_a)
    return r, logw, k2, v, -kk, kk * a, g


def f_rwkv_post(y, r, k2, v, g, r_k, lnx_w, lnx_b, e, et):
    inv = 1.0 / HEAD
    mean = hdot(hdot(y, e) * inv, et)
    yc = y - mean
    var = hdot(yc * yc, e) * inv
    yn = yc * hdot(lax.rsqrt(var + GN_EPS), et) * lnx_w + lnx_b
    bonus = hdot(hdot(r * k2 * r_k, e), et) * v
    return (yn + bonus) * g


def f_ssm_ab(lr, li, log_dt):
    dt = jnp.exp(log_dt)
    mag = jnp.exp(lr * dt)
    ab_re, ab_im = mag * jnp.cos(li * dt), mag * jnp.sin(li * dt)
    denom = lr * lr + li * li
    z_re = ((ab_re - 1.0) * lr + ab_im * li) / denom
    z_im = (ab_im * lr - (ab_re - 1.0) * li) / denom
    return ab_re, ab_im, z_re, z_im


def f_ssm_bb(br, bi, z_re, z_im):
    return z_re * br - z_im * bi, z_re * bi + z_im * br


def _col_block(n):
    return _pick(n, (256, 128))


def ts_fwd(proj, col0, width, mu, *, name):
    t = proj.shape[0]
    cb = _col_block(width)
    assert col0 % cb == 0 and width % cb == 0
    off = col0 // cb

    def body(z_ref, mu_ref, o_ref):
        z = z_ref[...]
        row = lax.broadcasted_iota(jnp.int32, z.shape, 0)
        prev = jnp.where(row == 0, 0.0, pltpu.roll(z, 1, 0))
        o_ref[...] = z + (prev - z) * mu_ref[...]

    return pl.pallas_call(
        body, name=name, grid=(width // cb,),
        in_specs=[pl.BlockSpec((t, cb), lambda j: (0, j + off)), pl.BlockSpec((1, cb), lambda j: (0, j))],
        out_specs=pl.BlockSpec((t, cb), lambda j: (0, j)),
        out_shape=jax.ShapeDtypeStruct((t, width), F32), compiler_params=_cparams(1))(proj, mu)


def ts_bwd(proj, col0, dzs, mu, *, name):
    t, width = dzs.shape
    cb = _col_block(width)
    off = col0 // cb

    def body(z_ref, d_ref, mu_ref, dz_ref, dmu_ref):
        z, d, m = z_ref[...], d_ref[...], mu_ref[...]
        row = lax.broadcasted_iota(jnp.int32, z.shape, 0)
        prev = jnp.where(row == 0, 0.0, pltpu.roll(z, 1, 0))
        dm = d * m
        nxt = jnp.where(row == t - 1, 0.0, pltpu.roll(dm, t - 1, 0))
        dz_ref[...] = d - dm + nxt
        dmu_ref[...] = jnp.sum(d * (prev - z), axis=0, keepdims=True)

    return pl.pallas_call(
        body, name=name, grid=(width // cb,),
        in_specs=[pl.BlockSpec((t, cb), lambda j: (0, j + off)), pl.BlockSpec((t, cb), lambda j: (0, j)),
                  pl.BlockSpec((1, cb), lambda j: (0, j))],
        out_specs=[pl.BlockSpec((t, cb), lambda j: (0, j)), pl.BlockSpec((1, cb), lambda j: (0, j))],
        out_shape=[jax.ShapeDtypeStruct((t, width), F32), jax.ShapeDtypeStruct((1, width), F32)],
        compiler_params=_cparams(1))(proj, dzs, mu)


def _att_scores(qn, kb, bias, c):
    s = jnp.einsum('hqd,hkd->hqk', qn, kb, preferred_element_type=F32) + bias
    col = lax.broadcasted_iota(jnp.int32, s.shape, 2)
    s = jnp.where(col >= PAD - c * CHUNK, s, NEG_BIG)
    s = s - jnp.max(s, axis=-1, keepdims=True)
    e = jnp.exp(s)
    return e / jnp.sum(e, axis=-1, keepdims=True)


def att_fwd(qn, knp, vp, bias, *, name):
    h, t, _ = qn.shape
    hb = ATT_HEADS
    nc = t // CHUNK

    def body(q_ref, k_ref, v_ref, b_ref, o_ref):
        c = pl.program_id(1)
        start = pl.multiple_of(c * CHUNK, CHUNK)
        kb = k_ref[:, pl.ds(start, BAND), :]
        vb = v_ref[:, pl.ds(start, BAND), :]
        p = _att_scores(q_ref[...], kb, b_ref[...], c)
        o_ref[...] = jnp.einsum('hqk,hkd->hqd', p.astype(BF16), vb, preferred_element_type=F32).astype(o_ref.dtype)

    return pl.pallas_call(
        body, name=name, grid=(h // hb, nc),
        in_specs=[pl.BlockSpec((hb, CHUNK, HEAD), lambda g, c: (g, c, 0)),
                  pl.BlockSpec((hb, t + PAD, HEAD), lambda g, c: (g, 0, 0)),
                  pl.BlockSpec((hb, t + PAD, HEAD), lambda g, c: (g, 0, 0)),
                  pl.BlockSpec((hb, CHUNK, BAND), lambda g, c: (g, 0, 0))],
        out_specs=pl.BlockSpec((hb, CHUNK, HEAD), lambda g, c: (g, c, 0)),
        out_shape=jax.ShapeDtypeStruct((h, t, HEAD), BF16), compiler_params=_cparams(2))(qn, knp, vp, bias)


def att_bwd(qn, knp, vp, bias, do, *, name):
    h, t, _ = qn.shape
    hb = ATT_HEADS
    nc = t // CHUNK

    def body(q_ref, k_ref, v_ref, b_ref, do_ref, dq_ref, dk_ref, dv_ref, db_ref):
        c = pl.program_id(1)

        @pl.when(c == 0)
        def _():
            dk_ref[...] = jnp.zeros_like(dk_ref)
            dv_ref[...] = jnp.zeros_like(dv_ref)
            db_ref[...] = jnp.zeros_like(db_ref)

        start = pl.multiple_of(c * CHUNK, CHUNK)
        qv = q_ref[...]
        kb = k_ref[:, pl.ds(start, BAND), :]
        vb = v_ref[:, pl.ds(start, BAND), :]
        p = _att_scores(qv, kb, b_ref[...], c)
        dov = do_ref[...]
        dp = jnp.einsum('hqd,hkd->hqk', dov, vb, preferred_element_type=F32)
        ds = p * (dp - jnp.sum(p * dp, axis=-1, keepdims=True))
        db_ref[...] += ds
        dsb = ds.astype(BF16)
        dq_ref[...] = jnp.einsum('hqk,hkd->hqd', dsb, kb, preferred_element_type=F32)
        dst = jnp.swapaxes(dsb, 1, 2)
        pt = jnp.swapaxes(p.astype(BF16), 1, 2)
        dk_ref[:, pl.ds(start, BAND), :] += jnp.einsum('hkq,hqd->hkd', dst, qv, preferred_element_type=F32)
        dv_ref[:, pl.ds(start, BAND), :] += jnp.einsum('hkq,hqd->hkd', pt, dov, preferred_element_type=F32)

    blk_q = pl.BlockSpec((hb, CHUNK, HEAD), lambda g, c: (g, c, 0))
    blk_k = pl.BlockSpec((hb, t + PAD, HEAD), lambda g, c: (g, 0, 0))
    blk_b = pl.BlockSpec((hb, CHUNK, BAND), lambda g, c: (g, 0, 0))
    return pl.pallas_call(
        body, name=name, grid=(h // hb, nc),
        in_specs=[blk_q, blk_k, blk_k, blk_b, blk_q],
        out_specs=[blk_q, blk_k, blk_k, blk_b],
        out_shape=[jax.ShapeDtypeStruct((h, t, HEAD), F32), jax.ShapeDtypeStruct((h, t + PAD, HEAD), F32),
                   jax.ShapeDtypeStruct((h, t + PAD, HEAD), F32), jax.ShapeDtypeStruct((h, CHUNK, BAND), F32)],
        compiler_params=_cparams(2))(qn, knp, vp, bias, do)


def _rel_index():
    i = np.arange(CHUNK)[:, None]
    j = np.arange(BAND)[None, :]
    return np.clip(i + PAD - j, -(CHUNK - 1), REL_CLIP) + (CHUNK - 1)


def relbias_reduce(dbias, *, name):
    h = dbias.shape[0]
    onehot = jnp.asarray((_rel_index()[:, :, None] == np.arange(N_REL)[None, None, :]).astype(np.float32))
    dbt = jnp.swapaxes(dbias, 0, 1)

    def body(d_ref, oh_ref, o_ref):
        @pl.when(pl.program_id(0) == 0)
        def _():
            o_ref[...] = jnp.zeros_like(o_ref)
        o_ref[...] += hdot(d_ref[...], oh_ref[...])

    return pl.pallas_call(
        body, name=name, grid=(CHUNK,),
        in_specs=[pl.BlockSpec((None, h, BAND), lambda i: (i, 0, 0)),
                  pl.BlockSpec((None, BAND, N_REL), lambda i: (i, 0, 0))],
        out_specs=pl.BlockSpec((h, N_REL), lambda i: (0, 0)),
        out_shape=jax.ShapeDtypeStruct((h, N_REL), F32), compiler_params=_cparams(1))(dbt, onehot)


def _bt(x):
    return jnp.swapaxes(x, 1, 2)


def _bmm_raw(x, y):
    return lax.dot_general(x, y, (((2,), (1,)), ((0,), (0,))), precision=HI, preferred_element_type=F32)


@jax.custom_vjp
def bmm(x, y):
    return _bmm_raw(x, y)


def _bmm_fwd(x, y):
    return _bmm_raw(x, y), (x, y)


def _bmm_bwd(saved, dz):
    x, y = saved
    return _bmm_raw(dz, _bt(y)), _bmm_raw(_bt(x), dz)


bmm.defvjp(_bmm_fwd, _bmm_bwd)


def rwkv_chunk(p0, r, lw, k, v, a, b):
    g, c, n = r.shape
    row = lax.broadcasted_iota(jnp.int32, (g, c, c), 1)
    col = lax.broadcasted_iota(jnp.int32, (g, c, c), 2)
    incl, strict = row >= col, row > col
    cs = bmm(incl.astype(F32), lw)
    cs_end = cs[:, c - 1:c, :]
    e_cs = jnp.exp(cs)
    e_neg = jnp.exp(-cs)
    at = a * jnp.exp(cs - lw)
    rt = r * e_cs
    bt_, kt = b * e_neg, k * e_neg
    e_tail = jnp.exp(cs_end - cs)
    bh, kh = b * e_tail, k * e_tail
    btt, ktt = _bt(bt_), _bt(kt)
    a_ab = jnp.where(strict, bmm(at, btt), 0.0)
    a_ak = jnp.where(strict, bmm(at, ktt), 0.0)
    a_rb = jnp.where(incl, bmm(rt, btt), 0.0)
    a_rk = jnp.where(incl, bmm(rt, ktt), 0.0)
    tinv = jnp.where(row == col, 1.0, 0.0) + a_ab
    npow = a_ab
    for _ in range(int(math.log2(c)) - 1):
        npow = bmm(npow, npow)
        tinv = tinv + bmm(tinv, npow)
    u = bmm(tinv, bmm(at, p0) + bmm(a_ak, v))
    y = bmm(rt, p0) + bmm(a_rb, u) + bmm(a_rk, v)
    rown = lax.broadcasted_iota(jnp.int32, (g, n, n), 1)
    coln = lax.broadcasted_iota(jnp.int32, (g, n, n), 2)
    dg = jnp.where(rown == coln, jnp.exp(cs_end), 0.0)
    p1 = bmm(dg, p0) + bmm(_bt(bh), u) + bmm(_bt(kh), v)
    return y, p1


def rwkv_fwd(r, lw, k, v, a, b, *, name):
    h, t, n = r.shape
    g, c = min(RW_HEADS, h), RW_CHUNK
    nch = t // c

    def body(r_ref, lw_ref, k_ref, v_ref, a_ref, b_ref, y_ref, p_ref, st_ref):
        @pl.when(pl.program_id(1) == 0)
        def _():
            st_ref[...] = jnp.zeros_like(st_ref)
        p0 = st_ref[...]
        p_ref[...] = p0
        y, p1 = rwkv_chunk(p0, r_ref[...], lw_ref[...], k_ref[...], v_ref[...], a_ref[...], b_ref[...])
        y_ref[...] = y
        st_ref[...] = p1

    blk = pl.BlockSpec((g, c, n), lambda i, j: (i, j, 0))
    pblk = pl.BlockSpec((g, None, n, n), lambda i, j: (i, j, 0, 0))
    return pl.pallas_call(
        body, name=name, grid=(h // g, nch), in_specs=[blk] * 6, out_specs=[blk, pblk],
        out_shape=[jax.ShapeDtypeStruct((h, t, n), F32), jax.ShapeDtypeStruct((h, nch, n, n), F32)],
        scratch_shapes=[pltpu.VMEM((g, n, n), F32)], compiler_params=_cparams(2))(r, lw, k, v, a, b)


def rwkv_bwd(r, lw, k, v, a, b, p0s, dy, *, name):
    h, t, n = r.shape
    g, c = min(RW_HEADS, h), RW_CHUNK
    nch = t // c

    def body(r_ref, lw_ref, k_ref, v_ref, a_ref, b_ref, p_ref, dy_ref,
             dr_ref, dlw_ref, dk_ref, dv_ref, da_ref, db_ref, dp_ref):
        @pl.when(pl.program_id(1) == 0)
        def _():
            dp_ref[...] = jnp.zeros_like(dp_ref)
        _, pull = jax.vjp(rwkv_chunk, p_ref[...], r_ref[...], lw_ref[...], k_ref[...], v_ref[...],
                          a_ref[...], b_ref[...])
        dp0, dr, dlw, dk, dv, da, db = pull((dy_ref[...], dp_ref[...]))
        dr_ref[...] = dr
        dlw_ref[...] = dlw
        dk_ref[...] = dk
        dv_ref[...] = dv
        da_ref[...] = da
        db_ref[...] = db
        dp_ref[...] = dp0

    blk = pl.BlockSpec((g, c, n), lambda i, j: (i, nch - 1 - j, 0))
    pblk = pl.BlockSpec((g, None, n, n), lambda i, j: (i, nch - 1 - j, 0, 0))
    return pl.pallas_call(
        body, name=name, grid=(h // g, nch), in_specs=[blk] * 6 + [pblk, blk], out_specs=[blk] * 6,
        out_shape=[jax.ShapeDtypeStruct((h, t, n), F32)] * 6,
        scratch_shapes=[pltpu.VMEM((g, n, n), F32)], compiler_params=_cparams(2))(r, lw, k, v, a, b, p0s, dy)


def _time_block(t):
    return _pick(t, (256, 128, 64))


def ssm_scan_fwd(bu_re, bu_im, a_re, a_im, *, name):
    t, rr, ln = bu_re.shape
    tb = _time_block(t)

    def body(br_ref, bi_ref, ar_ref, ai_ref, hr_ref, hi_ref, sr_ref, si_ref):
        @pl.when(pl.program_id(0) == 0)
        def _():
            sr_ref[...] = jnp.zeros_like(sr_ref)
            si_ref[...] = jnp.zeros_like(si_ref)
        ar, ai = ar_ref[...], ai_ref[...]

        def step(i, carry):
            hr, hi = carry
            nr = ar * hr - ai * hi + br_ref[i]
            ni = ar * hi + ai * hr + bi_ref[i]
            hr_ref[i] = nr
            hi_ref[i] = ni
            return nr, ni

        hr, hi = lax.fori_loop(0, tb, step, (sr_ref[...], si_ref[...]))
        sr_ref[...] = hr
        si_ref[...] = hi

    blk = pl.BlockSpec((tb, rr, ln), lambda i: (i, 0, 0))
    cblk = pl.BlockSpec((rr, ln), lambda i: (0, 0))
    return pl.pallas_call(
        body, name=name, grid=(t // tb,), in_specs=[blk, blk, cblk, cblk], out_specs=[blk, blk],
        out_shape=[jax.ShapeDtypeStruct((t, rr, ln), F32)] * 2,
        scratch_shapes=[pltpu.VMEM((rr, ln), F32)] * 2, compiler_params=_cparams(1))(bu_re, bu_im, a_re, a_im)


def ssm_scan_bwd(dh_re, dh_im, hp_re, hp_im, a_re, a_im, *, name):
    t, rr, ln = dh_re.shape
    tb = _time_block(t)
    nb = t // tb

    def body(dr_ref, di_ref, pr_ref, pi_ref, ar_ref, ai_ref, gr_ref, gi_ref, dar_ref, dai_ref, sr_ref, si_ref):
        @pl.when(pl.program_id(0) == 0)
        def _():
            sr_ref[...] = jnp.zeros_like(sr_ref)
            si_ref[...] = jnp.zeros_like(si_ref)
            dar_ref[...] = jnp.zeros_like(dar_ref)
            dai_ref[...] = jnp.zeros_like(dai_ref)
        ar, ai = ar_ref[...], ai_ref[...]

        def step(ii, carry):
            gr, gi, dar, dai = carry
            i = tb - 1 - ii
            nr = dr_ref[i] + ar * gr + ai * gi
            ni = di_ref[i] - ai * gr + ar * gi
            gr_ref[i] = nr
            gi_ref[i] = ni
            pr, pi = pr_ref[i], pi_ref[i]
            dar = dar + nr * pr + ni * pi
            dai = dai - nr * pi + ni * pr
            return nr, ni, dar, dai

        gr, gi, dar, dai = lax.fori_loop(0, tb, step, (sr_ref[...], si_ref[...], dar_ref[...], dai_ref[...]))
        sr_ref[...] = gr
        si_ref[...] = gi
        dar_ref[...] = dar
        dai_ref[...] = dai

    blk = pl.BlockSpec((tb, rr, ln), lambda i: (nb - 1 - i, 0, 0))
    cblk = pl.BlockSpec((rr, ln), lambda i: (0, 0))
    return pl.pallas_call(
        body, name=name, grid=(nb,), in_specs=[blk] * 4 + [cblk, cblk], out_specs=[blk, blk, cblk, cblk],
        out_shape=[jax.ShapeDtypeStruct((t, rr, ln), F32)] * 2 + [jax.ShapeDtypeStruct((rr, ln), F32)] * 2,
        scratch_shapes=[pltpu.VMEM((rr, ln), F32)] * 2,
        compiler_params=_cparams(1))(dh_re, dh_im, hp_re, hp_im, a_re, a_im)


ANY = pl.BlockSpec(memory_space=pl.ANY)


def _rows(ref, lead, ch, nchunk):
    base = ref if lead is None else ref.at[lead]
    if nchunk == 1:
        return base
    n = base.shape[0] // nchunk
    return base.at[pl.ds(ch * n, n)]


class Comm:
    def __init__(self):
        self.ins, self.outs, self.ops, self.aliases = [], [], [], {}
        self.n_remote, self.n_local = 0, 0
        self.ici = False

    def _add(self, kind, src, out_shape, n_peers, nchunk, n_local=1, alias=False):
        lead_len = src.shape[1] if kind in ("scatter", "rs1") else src.shape[0]
        while lead_len % nchunk:
            nchunk //= 2
        self.ops.append((kind, len(self.ins), len(self.outs), self.n_remote, self.n_local, nchunk))
        if alias:
            self.aliases[len(self.ins)] = len(self.outs)
        self.ins.append(src)
        self.outs.append(jax.ShapeDtypeStruct(out_shape, src.dtype))
        self.n_remote += n_peers * nchunk
        self.n_local += n_local
        self.ici = self.ici or kind in ("gather", "scatter", "gather_all", "ag1", "rs2")
        return len(self.outs) - 1

    def ag1(self, w):
        return self._add("ag1", w, (4,) + w.shape, 3, 1, n_local=LOCAL_CHUNKS)

    def ag2(self, g):
        return self._add("ag2", g, g.shape, 3, 1, n_local=0, alias=True)

    def rs1(self, g4):
        s, r, c = g4.shape
        return self._add("rs1", g4, (s, r // 2, c), 4, 1, n_local=0)

    def rs2(self, h4):
        return self._add("rs2", h4, (3,) + h4.shape[1:], 3, 1, n_local=0)

    def rs3(self, s, nchunk=4):
        return self._add("rs3", s, s.shape, 1, nchunk, n_local=0)

    def all_gather(self, w, nchunk=2):
        return self._add("gather", w, (4,) + w.shape, 3, nchunk)

    def scatter(self, g4, nchunk=2):
        return self._add("scatter", g4, g4.shape, 3, nchunk)

    def swap(self, s, nchunk=8):
        return self._add("swap", s, (2,) + s.shape, 1, nchunk)

    def gather_all(self, v, nchunk=1):
        return self._add("gather_all", v, (8,) + v.shape, 7, nchunk)

    def sem_shapes(self):
        return [pltpu.SemaphoreType.DMA((self.n_remote,)), pltpu.SemaphoreType.DMA((self.n_remote,)),
                pltpu.SemaphoreType.DMA((max(self.n_local, 1),))]

    def _two_level(self, kind, src, dst, r0, l0, nchunk, x, y, c, send, recv, lsem, sends, recvs, locs):
        chips = [(1 - x, y), (x, 1 - y), (1 - x, 1 - y)]
        me, sib = 2 * x + y, (x, y, 1 - c)

        def half(ref, hc):
            n = ref.shape[0] // 2
            return ref.at[pl.ds(hc * n, n)]

        def both(k, dev, s_ref, d_send, d_recv):
            mk = functools.partial(pltpu.make_async_remote_copy, src_ref=s_ref, send_sem=send.at[k],
                                   recv_sem=recv.at[k], device_id=dev, device_id_type=MESH)
            sends.append(mk(dst_ref=d_send))
            recvs.append(mk(dst_ref=d_recv))

        if kind == "ag1":
            n = src.shape[0] // LOCAL_CHUNKS
            for j in range(LOCAL_CHUNKS):
                rows = pl.ds(j * n, n)
                locs.append(pltpu.make_async_copy(src.at[rows], dst.at[me].at[rows], lsem.at[l0 + j]))
            for pj, (px, py) in enumerate(chips):
                both(r0 + pj, (px, py, c), half(src, c), half(dst.at[me], c), half(dst.at[2 * px + py], c))
        elif kind == "ag2":
            for pj, (px, py) in enumerate(chips):
                got = dst.at[2 * px + py]
                both(r0 + pj, sib, half(got, c), half(got, c), half(got, 1 - c))
        elif kind == "rs1":
            for q in range(4):
                both(r0 + q, sib, half(src.at[q], 1 - c), dst.at[q], dst.at[q])
        elif kind == "rs2":
            for pj, (px, py) in enumerate(chips):
                both(r0 + pj, (px, py, c), src.at[2 * px + py], dst.at[pj], dst.at[pj])
        else:
            for ch in range(nchunk):
                both(r0 + ch, sib, _rows(src, None, ch, nchunk), _rows(dst, None, ch, nchunk),
                     _rows(dst, None, ch, nchunk))

    def _descs(self, c_in, c_out, send, recv, lsem):
        x, y, c = lax.axis_index("x"), lax.axis_index("y"), lax.axis_index("c")
        sends, recvs, locs = [], [], []
        for kind, ii, oi, r0, l0, nchunk in self.ops:
            src, dst = c_in[ii], c_out[oi]
            if kind in ("ag1", "ag2", "rs1", "rs2", "rs3"):
                self._two_level(kind, src, dst, r0, l0, nchunk, x, y, c, send, recv, lsem, sends, recvs, locs)
                continue
            if kind == "swap":
                me = c
                peers = [((x, y, 1 - c), 1 - c)]
            elif kind == "gather_all":
                me = 4 * x + 2 * y + c
                flips = [(dx, dy, dc) for dx in (0, 1) for dy in (0, 1) for dc in (0, 1) if dx + dy + dc]
                peers = []
                for dx, dy, dc in flips:
                    px, py, pc = (x + dx) % 2, (y + dy) % 2, (c + dc) % 2
                    peers.append(((px, py, pc), 4 * px + 2 * py + pc))
            else:
                me = 2 * x + y
                peers = [((px, py, c), 2 * px + py) for px, py in ((1 - x, y), (x, 1 - y), (1 - x, 1 - y))]
            if kind == "scatter":
                locs.append(pltpu.make_async_copy(src.at[me], dst.at[me], lsem.at[l0]))
            else:
                locs.append(pltpu.make_async_copy(src, dst.at[me], lsem.at[l0]))
            for pj, (dev, peer_slot) in enumerate(peers):
                for ch in range(nchunk):
                    k = r0 + pj * nchunk + ch
                    s_src = _rows(src, peer_slot if kind == "scatter" else None, ch, nchunk)
                    mk = functools.partial(pltpu.make_async_remote_copy, send_sem=send.at[k], recv_sem=recv.at[k],
                                           device_id=dev, device_id_type=MESH)
                    sends.append(mk(src_ref=s_src, dst_ref=_rows(dst, me, ch, nchunk)))
                    recvs.append(mk(src_ref=s_src, dst_ref=_rows(dst, peer_slot, ch, nchunk)))
        return sends, recvs, locs

    def start(self, c_in, c_out, send, recv, lsem):
        sends, _, locs = self._descs(c_in, c_out, send, recv, lsem)
        for d in locs + sends:
            d.start()

    def wait(self, c_in, c_out, send, recv, lsem):
        sends, recvs, locs = self._descs(c_in, c_out, send, recv, lsem)
        for d in recvs:
            d.wait_recv()
        for d in sends:
            d.wait_send()
        for d in locs:
            d.wait()


def run_comm(comm, *, name):
    n_cin, n_cout = len(comm.ins), len(comm.outs)

    def body(*refs):
        c_in, c_out, sems = refs[:n_cin], refs[n_cin:n_cin + n_cout], refs[n_cin + n_cout:]
        comm.start(c_in, c_out, *sems)
        comm.wait(c_in, c_out, *sems)

    outs = pl.pallas_call(
        body, name=name, in_specs=[ANY] * n_cin, out_specs=[ANY] * n_cout, out_shape=list(comm.outs),
        scratch_shapes=comm.sem_shapes(), input_output_aliases=dict(comm.aliases),
        compiler_params=pltpu.CompilerParams(has_side_effects=True))(*comm.ins)
    return list(outs)


def _core_index():
    return jnp.reshape(lax.axis_index("c"), (1,)).astype(jnp.int32)


def _chip_index():
    return jnp.reshape(2 * lax.axis_index("x") + lax.axis_index("y"), (1,)).astype(jnp.int32)


def _row_block(rows, bytes_per_row, cap=512):
    tb = 16
    while tb * 2 <= min(rows, cap) and rows % (tb * 2) == 0 and tb * 2 * bytes_per_row <= ROW_BLOCK_BYTES:
        tb *= 2
    return tb


def add_half(g4, got, *, name):
    s, r, c = g4.shape
    r2 = r // 2
    tb = _row_block(r2, c * 8)
    nb = r2 // tb

    def body(c_ref, g_ref, x_ref, o_ref):
        o_ref[...] = (g_ref[...].astype(F32) + x_ref[...].astype(F32)).astype(o_ref.dtype)

    grid_spec = pltpu.PrefetchScalarGridSpec(
        num_scalar_prefetch=1, grid=(s, nb),
        in_specs=[pl.BlockSpec((None, tb, c), lambda q, i, cr: (q, cr[0] * nb + i, 0)),
                  pl.BlockSpec((None, tb, c), lambda q, i, cr: (q, i, 0))],
        out_specs=pl.BlockSpec((None, tb, c), lambda q, i, cr: (q, i, 0)))
    return pl.pallas_call(body, name=name, grid_spec=grid_spec, out_shape=jax.ShapeDtypeStruct((s, r2, c), g4.dtype),
                          compiler_params=_cparams(2))(_core_index(), g4, got)


def sum_chips(h4, got3, *, name):
    _, r2, c = h4.shape
    tb = _row_block(r2, c * 12)
    nb = r2 // tb

    def body(q_ref, h_ref, y_ref, o_ref):
        o_ref[...] = ((h_ref[...].astype(F32) + y_ref[0].astype(F32)) + y_ref[1].astype(F32)) + y_ref[2].astype(F32)

    grid_spec = pltpu.PrefetchScalarGridSpec(
        num_scalar_prefetch=1, grid=(nb,),
        in_specs=[pl.BlockSpec((None, tb, c), lambda i, qr: (qr[0], i, 0)),
                  pl.BlockSpec((3, tb, c), lambda i, qr: (0, i, 0))],
        out_specs=pl.BlockSpec((tb, c), lambda i, qr: (i, 0)))
    return pl.pallas_call(body, name=name, grid_spec=grid_spec, out_shape=jax.ShapeDtypeStruct((r2, c), F32),
                          compiler_params=_cparams(1))(_chip_index(), h4, got3)


def adamw_big(w, mine, theirs, m, v, *, name):
    nl, r, c = w.shape
    r2 = r // 2
    tb = _row_block(r2, c * 4 * 10, cap=256)
    nb2 = r2 // tb
    nb = 2 * nb2

    def body(c_ref, w_ref, m_ref, v_ref, *rest):
        g_refs, (go_ref, d_ref, mo_ref, vo_ref) = rest[:2 * nl], rest[2 * nl:]
        layer, i = pl.program_id(0), pl.program_id(1)
        own = (i // nb2) == c_ref[0]
        for l0 in range(nl):
            @pl.when(layer == l0)
            def _(l0=l0):
                g = jnp.where(own, g_refs[2 * l0][...], g_refs[2 * l0 + 1][...])
                d, mn, vn = _adam_math(w_ref[...], g, m_ref[...], v_ref[...])
                go_ref[...] = g
                d_ref[...] = d
                mo_ref[...] = mn
                vo_ref[...] = vn

    blk = pl.BlockSpec((None, tb, c), lambda l, i, cr: (l, i, 0))

    def half_spec(l0):
        return pl.BlockSpec((tb, c), lambda l, i, cr: (jnp.where(l == l0, i % nb2, jnp.where(l < l0, 0, nb2 - 1)), 0))

    in_specs = [blk, blk, blk]
    args = [w, m, v]
    for l0 in range(nl):
        in_specs += [half_spec(l0), half_spec(l0)]
        args += [mine[l0], theirs[l0]]
    grid_spec = pltpu.PrefetchScalarGridSpec(num_scalar_prefetch=1, grid=(nl, nb), in_specs=in_specs,
                                             out_specs=[blk] * 4)
    return pl.pallas_call(body, name=name, grid_spec=grid_spec, out_shape=[jax.ShapeDtypeStruct(w.shape, F32)] * 4,
                          compiler_params=_cparams(2))(_core_index(), *args)


def sum_slots(x, *, name):
    s, r, c = x.shape
    tb = 8
    while tb * 2 <= min(r, 512) and r % (tb * 2) == 0 and tb * 2 * c * 4 * (s + 1) <= ROW_BLOCK_BYTES:
        tb *= 2

    def body(x_ref, o_ref):
        acc = x_ref[0].astype(F32)
        for i in range(1, s):
            acc = acc + x_ref[i].astype(F32)
        o_ref[...] = acc

    return pl.pallas_call(
        body, name=name, grid=(r // tb,), in_specs=[pl.BlockSpec((s, tb, c), lambda i: (0, i, 0))],
        out_specs=pl.BlockSpec((tb, c), lambda i: (i, 0)),
        out_shape=jax.ShapeDtypeStruct((r, c), F32), compiler_params=_cparams(1))(x)


def _adam_math(w, g, m, v):
    m = ADAM_B1 * m + (1.0 - ADAM_B1) * g
    v = ADAM_B2 * v + (1.0 - ADAM_B2) * (g * g)
    m_hat = m / (1.0 - ADAM_B1 ** ADAM_STEP)
    v_hat = v / (1.0 - ADAM_B2 ** ADAM_STEP)
    delta = -ADAM_LR * (m_hat / (jnp.sqrt(v_hat) + ADAM_EPS) + ADAM_WD * w)
    return delta, m, v


def adamw_pair(w, g2, m, v, *, name):
    r, c = w.shape
    tb = 8
    while tb * 2 <= min(r, 512) and r % (tb * 2) == 0 and tb * 2 * c * 4 * 9 <= 2 * ROW_BLOCK_BYTES:
        tb *= 2

    def body(w_ref, g_ref, m_ref, v_ref, go_ref, d_ref, mo_ref, vo_ref):
        g = g_ref[0] + g_ref[1]
        d, mn, vn = _adam_math(w_ref[...], g, m_ref[...], v_ref[...])
        go_ref[...] = g
        d_ref[...] = d
        mo_ref[...] = mn
        vo_ref[...] = vn

    blk = pl.BlockSpec((tb, c), lambda i: (i, 0))
    return pl.pallas_call(
        body, name=name, grid=(r // tb,), in_specs=[blk, pl.BlockSpec((2, tb, c), lambda i: (0, i, 0)), blk, blk],
        out_specs=[blk] * 4, out_shape=[jax.ShapeDtypeStruct((r, c), F32)] * 4,
        compiler_params=_cparams(1))(w, g2, m, v)


def adamw_flat(w, g, m, v, *, name):
    def fn(w_, g_, m_, v_):
        return _adam_math(w_, g_, m_, v_)
    return rowk(fn, [w, g, m, v], [], [(w.shape[1], F32)] * 3, name=name)


def to_heads(x):
    t, d = x.shape
    return x.reshape(t, d // HEAD, HEAD).transpose(1, 0, 2)


def from_heads(x):
    h, t, n = x.shape
    return x.transpose(1, 0, 2).reshape(t, h * n)


def pack_flat(arrs, lanes=128, row_mult=8):
    flat = jnp.concatenate([a.reshape(-1).astype(F32) for a in arrs])
    n = flat.shape[0]
    rows = -(-n // lanes)
    rows = -(-rows // row_mult) * row_mult
    return jnp.pad(flat, (0, rows * lanes - n)).reshape(rows, lanes)


def unpack_flat(buf, shapes):
    flat = buf.reshape(-1)
    outs, off = [], 0
    for s in shapes:
        n = int(np.prod(s))
        outs.append(flat[off:off + n].reshape(s))
        off += n
    return outs


def block_diag_from(w_gab):
    g, a, b = w_gab.shape
    eye = jnp.eye(g, dtype=w_gab.dtype)
    return (w_gab[:, :, None, :] * eye[:, None, :, None]).reshape(g * a, g * b)


def block_diag_extract(m, g):
    a, b = m.shape[0] // g, m.shape[1] // g
    idx = jnp.arange(g)
    return m.reshape(g, a, g, b)[idx, :, idx, :]


def kernel(x, p, ffn1_norm, ffn1_w_gate, ffn1_w_up, ffn1_w_down, mix_norm, ffn2_norm, ffn2_w_gate, ffn2_w_up, ffn2_w_down, ple_norm, ple_w_gate, ple_w_proj, ab_w_in, att_q_gain, att_k_gain, att_rel_bias, rwkv_mu, rwkv_w0, rwkv_w_up, rwkv_a0, rwkv_a_up, rwkv_g_up, rwkv_k_k, rwkv_k_a, rwkv_r_k, rwkv_lnx_w, rwkv_lnx_b, ab_w_out, ssm_w_in, ssm_lambda_re, ssm_lambda_im, ssm_log_dt, ssm_b_re, ssm_b_im, ssm_c_re, ssm_c_im, ssm_d, ssm_w_out, loss_target, m_ffn1_norm, m_ffn1_w_gate, m_ffn1_w_up, m_ffn1_w_down, m_mix_norm, m_ffn2_norm, m_ffn2_w_gate, m_ffn2_w_up, m_ffn2_w_down, m_ple_norm, m_ple_w_gate, m_ple_w_proj, m_ab_w_in, m_att_q_gain, m_att_k_gain, m_att_rel_bias, m_rwkv_mu, m_rwkv_w0, m_rwkv_w_up, m_rwkv_a0, m_rwkv_a_up, m_rwkv_g_up, m_rwkv_k_k, m_rwkv_k_a, m_rwkv_r_k, m_rwkv_lnx_w, m_rwkv_lnx_b, m_ab_w_out, m_ssm_w_in, m_ssm_lambda_re, m_ssm_lambda_im, m_ssm_log_dt, m_ssm_b_re, m_ssm_b_im, m_ssm_c_re, m_ssm_c_im, m_ssm_d, m_ssm_w_out, v_ffn1_norm, v_ffn1_w_gate, v_ffn1_w_up, v_ffn1_w_down, v_mix_norm, v_ffn2_norm, v_ffn2_w_gate, v_ffn2_w_up, v_ffn2_w_down, v_ple_norm, v_ple_w_gate, v_ple_w_proj, v_ab_w_in, v_att_q_gain, v_att_k_gain, v_att_rel_bias, v_rwkv_mu, v_rwkv_w0, v_rwkv_w_up, v_rwkv_a0, v_rwkv_a_up, v_rwkv_g_up, v_rwkv_k_k, v_rwkv_k_a, v_rwkv_r_k, v_rwkv_lnx_w, v_rwkv_lnx_b, v_ab_w_out, v_ssm_w_in, v_ssm_lambda_re, v_ssm_lambda_im, v_ssm_log_dt, v_ssm_b_re, v_ssm_b_im, v_ssm_c_re, v_ssm_c_im, v_ssm_d, v_ssm_w_out):
    A = dict(locals())
    W = {n: A[n] for n in W_NAMES}
    return _step(A['x'], A['p'], A['loss_target'], W, {n: A['m_' + n] for n in W_NAMES},
                 {n: A['v_' + n] for n in W_NAMES})


def _step(x, p, target, W, M, V):
    assert x.shape[0] == 1
    t, d = x.shape[1], x.shape[2]
    depth = p.shape[0]
    h0 = x[0]
    tgt = target[0]
    qchip = 2 * lax.axis_index("x") + lax.axis_index("y")
    d_rw = W['rwkv_w0'].shape[1]
    d_att = W['ab_w_out'].shape[1] * 4 - d_rw
    n_h_att, n_h_rw = d_att // HEAD, d_rw // HEAD
    n_bin = 3 * d_rw + DECAY_LORA + AAA_LORA + GATE_LORA
    d_ssm = W['ssm_w_in'].shape[2]
    n_grp = d_ssm // SSM_GROUP
    gp = n_grp * SSM_STATE

    queue = []
    gathered = {}
    halves = {}
    grads = {}
    queued_grads = set()
    n_alone = [0]

    def ag_entry(name, layer):
        def add1(cm):
            return cm.ag1(W[name][layer].astype(BF16))

        def cont1(outs, hd):
            got = outs[hd]

            def cont2(outs2, hd2):
                gathered[(name, layer)] = outs2[hd2]
            queue.insert(0, (lambda cm: cm.ag2(got), cont2))
        return add1, cont1

    def rs_entry(name, layer, g4):
        shard_shape = W[name].shape[1:]
        rows, cols = int(np.prod(shard_shape[:-1])), shard_shape[-1]
        g4 = g4.reshape(4, rows, cols)

        def cont1(outs, hd):
            h4 = add_half(g4, outs[hd], name=f"rs_add_{name}")

            def cont2(outs2, hd2):
                mine = sum_chips(h4, outs2[hd2], name=f"rs_sum_{name}")

                def cont3(outs3, hd3):
                    halves[(name, layer)] = (mine, outs3[hd3])
                queue.insert(0, (lambda cm: cm.rs3(mine), cont3))
            queue.insert(0, (lambda cm: cm.rs2(h4), cont2))
        return (lambda cm: cm.rs1(g4)), cont1

    def enqueue_ready():
        for n in BIG:
            for li_, g4 in enumerate(grads.get(n, [])):
                if g4 is not None and (n, li_) not in queued_grads:
                    queued_grads.add((n, li_))
                    queue.append(rs_entry(n, li_, g4))

    def take():
        cm, conts = Comm(), []
        while queue and not cm.ici:
            add, cont = queue.pop(0)
            conts.append((cont, add(cm)))
        return cm, conts

    def hmm(*args, **kw):
        enqueue_ready()
        if not queue:
            return mm(*args, **kw)
        cm, conts = take()
        out, couts = mm(*args, comm=cm, **kw)
        for cont, hd in reversed(conts):
            cont(couts, hd)
        return out

    def flush(until=None):
        enqueue_ready()
        while queue and not (until is not None and until()):
            cm, conts = take()
            n_alone[0] += 1
            couts = run_comm(cm, name=f"comm_alone{n_alone[0]}")
            for cont, hd in reversed(conts):
                cont(couts, hd)
            enqueue_ready()

    def gather(name, layer):
        flush(until=lambda: (name, layer) in gathered)
        return gathered[(name, layer)]
    small = {}

    def add_small(name, val, layer=None, nl=1):
        if layer is None:
            small[name] = val
        else:
            small.setdefault(name, [None] * nl)[layer] = val

    def ffn_fwd(h, pre, i):
        g = W[pre + '_norm'][i][None]
        wg, wu, wd = gather(pre + '_w_gate', i), gather(pre + '_w_up', i), gather(pre + '_w_down', i)
        wd2 = wd.reshape(-1, d)
        n = rowk(lambda hh, gg: f_rms(hh, gg), [h], [g], [(d, BF16)], name=f"{pre}_rms")[0]
        a = hmm(n, wg, out_dtype=BF16, name=f"{pre}_gate")
        b = hmm(n, wu, out_dtype=BF16, name=f"{pre}_up")
        f = a.shape[1]
        u = rowk(lambda aa, bb: f_swiglu(*_f32(aa, bb)), [a, b], [], [(f, BF16)], name=f"{pre}_swiglu")[0]
        h_out = hmm(u, wd2, res=h, alpha=0.5, name=f"{pre}_down")
        return h_out, dict(h=h, g=g, n=n, a=a, b=b, u=u, wg=wg, wu=wu, wd2=wd2)

    def ffn_bwd(dh, sv, pre, i):
        f = sv['a'].shape[1]
        dwd = hmm(sv['u'], dh, ta=True, alpha=0.5, out_dtype=GRAD_XFER, name=f"{pre}_d_wdown")
        du = hmm(dh, sv['wd2'], tb=True, alpha=0.5, out_dtype=BF16, name=f"{pre}_d_u")
        da, db = rowk(vjp_rows(f_swiglu, 2, 1), [sv['a'], sv['b'], du], [], [(f, BF16), (f, BF16)],
                      name=f"{pre}_d_swiglu")
        dwg = hmm(sv['n'], da, ta=True, nshard=4, out3=True, out_dtype=GRAD_XFER, name=f"{pre}_d_wgate")
        dwu = hmm(sv['n'], db, ta=True, nshard=4, out3=True, out_dtype=GRAD_XFER, name=f"{pre}_d_wup")
        dn = hmm(da, sv['wg'], tb=True, name=f"{pre}_d_n1")
        dn = hmm(db, sv['wu'], tb=True, res=dn, name=f"{pre}_d_n2")
        dh_in, dg = rms_bwd(sv['h'], dn, dh, sv['g'], name=f"{pre}_d_rms")
        grads.setdefault(pre + '_w_gate', [None] * depth)[i] = dwg
        grads.setdefault(pre + '_w_up', [None] * depth)[i] = dwu
        grads.setdefault(pre + '_w_down', [None] * depth)[i] = dwd.reshape(4, -1, d)
        add_small(pre + '_norm', dg[0], i, depth)
        return dh_in

    def rms_bwd(h, dn, dh_res, g, *, name):
        def fn(hh, dnn, dres, gg):
            _, pull = jax.vjp(f_rms, hh, gg)
            dh_, dg_ = pull(dnn)
            return dh_ + dres, dg_
        return rowk(fn, [h, dn, dh_res], [g], [(d, F32)], [(1, d)], name=name)

    def head_consts(nh):
        e = np.kron(np.eye(nh, dtype=np.float32), np.ones((HEAD, 1), np.float32))
        return jnp.asarray(e), jnp.asarray(e.T)

    def mixer_ab_fwd(h):
        g = W['mix_norm'][0][None]
        win, wout = gather('ab_w_in', 0), gather('ab_w_out', 0).reshape(-1, d)
        hn = rowk(lambda hh, gg: f_rms(hh, gg), [h], [g], [(d, BF16)], name="mixab_rms")[0]
        proj4 = hmm(hn, win, out3=True, name="mixab_proj")
        proj = proj4.transpose(1, 0, 2).reshape(t, -1)
        q2, k2, v2 = [to_heads(proj[:, j * d_att:(j + 1) * d_att]).reshape(n_h_att * t, HEAD) for j in range(3)]
        qg, kg = W['att_q_gain'], W['att_k_gain']
        f_qn = lambda qq, gg: f_rms(qq, gg) * (HEAD ** -0.5)
        qn = rowk(f_qn, [q2], [qg], [(HEAD, BF16)], name="att_qnorm")[0].reshape(n_h_att, t, HEAD)
        kn = rowk(f_rms, [k2], [kg], [(HEAD, BF16)], name="att_knorm")[0].reshape(n_h_att, t, HEAD)
        knp = jnp.pad(kn, ((0, 0), (PAD, 0), (0, 0)))
        vp = jnp.pad(v2.astype(BF16).reshape(n_h_att, t, HEAD), ((0, 0), (PAD, 0), (0, 0)))
        bias = W['att_rel_bias'][0][:, _rel_index()]
        o = att_fwd(qn, knp, vp, bias, name="att_fwd")
        att = from_heads(o)
        mu = W['rwkv_mu']
        zs = ts_fwd(proj, 3 * d_att, n_bin, mu, name="rwkv_shift")
        e, et = head_consts(n_h_rw)
        zpad = jnp.zeros((AAA_LORA, d_rw), F32)
        wup_p = jnp.concatenate([W['rwkv_w_up_full'], zpad], 0)
        aup_p = jnp.concatenate([zpad, W['rwkv_a_up_full']], 0)
        pre_c = [W['rwkv_w0'], wup_p, W['rwkv_a0'], aup_p, W['rwkv_g_up_full'], W['rwkv_k_k'], W['rwkv_k_a'], e, et]
        pre = rowk(f_rwkv_pre, [zs], pre_c, [(d_rw, F32)] * 7, name="rwkv_pre")
        r_, lw_, kk_, vv_, ia_, ib_, gg_ = pre
        hm = [to_heads(u_) for u_ in (r_, lw_, kk_, vv_, ia_, ib_)]
        y_h, p0s = rwkv_fwd(*hm, name="rwkv_scan")
        y = from_heads(y_h)
        post_c = [W['rwkv_r_k'].reshape(1, d_rw), W['rwkv_lnx_w'], W['rwkv_lnx_b'], e, et]
        rw = rowk(f_rwkv_post, [y, r_, kk_, vv_, gg_], post_c, [(d_rw, BF16)], name="rwkv_post")[0]
        cat = jnp.concatenate([att, rw], axis=1)
        h_out = hmm(cat, wout, res=h, name="mixab_out")
        sv = dict(h=h, g=g, hn=hn, win=win, wout=wout, proj=proj, q2=q2, k2=k2, qn=qn, knp=knp, vp=vp, bias=bias,
                  zs=zs, pre_c=pre_c, pre=pre, hm=hm, p0s=p0s, y=y, post_c=post_c, cat=cat, qg=qg, kg=kg, mu=mu)
        return h_out, sv

    def mixer_ab_bwd(dh, sv):
        dwout = hmm(sv['cat'], dh, ta=True, out_dtype=GRAD_XFER, name="mixab_d_wout")
        grads['ab_w_out'] = [dwout.reshape(4, -1, d)]
        dcat = hmm(dh, sv['wout'], tb=True, name="mixab_d_cat")
        datt, drw = dcat[:, :d_att], dcat[:, d_att:]
        r_, lw_, kk_, vv_, ia_, ib_, gg_ = sv['pre']
        post = rowk(vjp_rows(f_rwkv_post, 5, 1), [sv['y'], r_, kk_, vv_, gg_, drw], sv['post_c'],
                    [(d_rw, F32)] * 5, [(1, d_rw)] * 3 + [sv['post_c'][3].shape, sv['post_c'][4].shape],
                    name="rwkv_d_post")
        dy, dr1, dk1, dv1, dg1 = post[:5]
        add_small('rwkv_r_k', post[5].reshape(W['rwkv_r_k'].shape))
        add_small('rwkv_lnx_w', post[6])
        add_small('rwkv_lnx_b', post[7])
        dscan = rwkv_bwd(*sv['hm'], sv['p0s'], to_heads(dy), name="rwkv_d_scan")
        dr2, dlw, dk2, dv2, dia, dib = [from_heads(u_) for u_ in dscan]

        def pre_bwd(zs, dra, drb, dlw_, dka, dkb, dva, dvb, dia_, dib_, dg_, *consts):
            _, pull = jax.vjp(f_rwkv_pre, zs, *consts)
            return pull((dra + drb, dlw_, dka + dkb, dva + dvb, dia_, dib_, dg_))

        pc = sv['pre_c']
        preb = rowk(pre_bwd, [sv['zs'], dr1, dr2, dlw, dk1, dk2, dv1, dv2, dia, dib, dg1], pc,
                    [(n_bin, F32)], [c.shape for c in pc], name="rwkv_d_pre")
        dzs = preb[0]
        add_small('rwkv_w0', preb[1])
        add_small('rwkv_w_up', preb[2][:DECAY_LORA])
        add_small('rwkv_a0', preb[3])
        add_small('rwkv_a_up', preb[4][DECAY_LORA:])
        add_small('rwkv_g_up', preb[5])
        add_small('rwkv_k_k', preb[6])
        add_small('rwkv_k_a', preb[7])
        dz, dmu = ts_bwd(sv['proj'], 3 * d_att, dzs, sv['mu'], name="rwkv_d_shift")
        add_small('rwkv_mu', dmu)
        do = to_heads(datt).astype(BF16)
        dqn, dknp, dvp, dbias = att_bwd(sv['qn'], sv['knp'], sv['vp'], sv['bias'], do, name="att_bwd")
        add_small('att_rel_bias', relbias_reduce(dbias, name="att_d_relbias")[None])
        f_qn = lambda qq, gg: f_rms(qq, gg) * (HEAD ** -0.5)
        dq2, dqg = rowk(vjp_rows(f_qn, 1, 1), [sv['q2'], dqn.reshape(-1, HEAD)], [sv['qg']], [(HEAD, F32)],
                        [(1, HEAD)], name="att_d_qnorm")
        dk2_, dkg = rowk(vjp_rows(f_rms, 1, 1), [sv['k2'], dknp[:, PAD:].reshape(-1, HEAD)], [sv['kg']],
                         [(HEAD, F32)], [(1, HEAD)], name="att_d_knorm")
        add_small('att_q_gain', dqg)
        add_small('att_k_gain', dkg)
        dproj = jnp.concatenate([from_heads(dq2.reshape(n_h_att, t, HEAD)), from_heads(dk2_.reshape(n_h_att, t, HEAD)),
                                 from_heads(dvp[:, PAD:]), dz], axis=1)
        dproj4 = dproj.reshape(t, 4, -1).transpose(1, 0, 2).astype(BF16)
        grads['ab_w_in'] = [hmm(sv['hn'], dproj4, ta=True, out3=True, out_dtype=GRAD_XFER, name="mixab_d_win")]
        dhn = hmm(dproj4, sv['win'], tb=True, name="mixab_d_hn")
        dh_in, dg = rms_bwd(sv['h'], dhn, dh, sv['g'], name="mixab_d_rms")
        add_small('mix_norm', dg[0], 0, depth)
        return dh_in

    def ssm_params():
        lr, li = W['ssm_lambda_re'][0], W['ssm_lambda_im'][0]
        ldt = W['ssm_log_dt'][0][:, None]
        ab = rowk(f_ssm_ab, [lr, li, ldt], [], [(SSM_STATE, F32)] * 4, name="ssm_ab", tb=n_grp)
        br = W['ssm_b_re'][0].reshape(gp, SSM_GROUP)
        bi = W['ssm_b_im'][0].reshape(gp, SSM_GROUP)
        z_re, z_im = ab[2].reshape(gp, 1), ab[3].reshape(gp, 1)
        bb = rowk(f_ssm_bb, [br, bi, z_re, z_im], [], [(SSM_GROUP, F32)] * 2, name="ssm_bb", tb=gp)
        return dict(lr=lr, li=li, ldt=ldt, ab=ab, br=br, bi=bi, z_re=z_re, z_im=z_im, bb=bb)

    def mixer_s5_fwd(h):
        g = W['mix_norm'][1][None]
        win, wout = gather('ssm_w_in', 0).reshape(d, d_ssm), gather('ssm_w_out', 0)
        hn = rowk(lambda hh, gg: f_rms(hh, gg), [h], [g], [(d, BF16)], name="s5_rms")[0]
        u = hmm(hn, win, name="s5_in")
        sp = ssm_params()
        bbd_re = block_diag_from(sp['bb'][0].reshape(n_grp, SSM_STATE, SSM_GROUP).transpose(0, 2, 1)).astype(BF16)
        bbd_im = block_diag_from(sp['bb'][1].reshape(n_grp, SSM_STATE, SSM_GROUP).transpose(0, 2, 1)).astype(BF16)
        cbd_re = block_diag_from(W['ssm_c_re'][0].transpose(0, 2, 1)).astype(BF16)
        cbd_im = block_diag_from(W['ssm_c_im'][0].transpose(0, 2, 1)).astype(BF16)
        ub = u.astype(BF16)
        bu_re = hmm(ub, bbd_re, name="s5_bu_re").reshape(t, gp // 128, 128)
        bu_im = hmm(ub, bbd_im, name="s5_bu_im").reshape(t, gp // 128, 128)
        a_re, a_im = sp['ab'][0].reshape(gp // 128, 128), sp['ab'][1].reshape(gp // 128, 128)
        h_re, h_im = ssm_scan_fwd(bu_re, bu_im, a_re, a_im, name="s5_scan")
        hb_re, hb_im = h_re.reshape(t, gp).astype(BF16), h_im.reshape(t, gp).astype(BF16)
        y = hmm(hb_re, cbd_re, name="s5_y_re")
        y = hmm(hb_im, cbd_im, res=y, alpha=-1.0, name="s5_y_im")
        dsk = W['ssm_d_full']
        f_act = lambda yy, uu, dd: f_gelu(yy + dd * uu)
        yg = rowk(f_act, [y, u], [dsk], [(d_ssm, BF16)], name="s5_gelu")[0]
        z = hmm(yg, wout, name="s5_out")
        f_glu = lambda zz, hh: hh + zz[:, :d] * f_sigmoid(zz[:, d:])
        h_out = rowk(f_glu, [z, h], [], [(d, F32)], name="s5_glu")[0]
        sv = dict(h=h, g=g, hn=hn, win=win, wout=wout, u=u, ub=ub, sp=sp, bbd_re=bbd_re, bbd_im=bbd_im,
                  cbd_re=cbd_re, cbd_im=cbd_im, a_re=a_re, a_im=a_im, h_re=h_re, h_im=h_im, hb_re=hb_re,
                  hb_im=hb_im, y=y, dsk=dsk, yg=yg, z=z)
        return h_out, sv

    def mixer_s5_bwd(dh, sv):
        f_glu = lambda zz: zz[:, :d] * f_sigmoid(zz[:, d:])
        dz = rowk(vjp_rows(f_glu, 1, 1), [sv['z'], dh], [], [(2 * d, BF16)], name="s5_d_glu")[0]
        grads['ssm_w_out'] = [hmm(sv['yg'], dz, ta=True, nshard=4, out3=True, out_dtype=GRAD_XFER, name="s5_d_wout")]
        dyg = hmm(dz, sv['wout'], tb=True, name="s5_d_yg")
        f_act = lambda yy, uu, dd: f_gelu(yy + dd * uu)
        dy, du1, ddsk = rowk(vjp_rows(f_act, 2, 1), [sv['y'], sv['u'], dyg], [sv['dsk']],
                             [(d_ssm, F32), (d_ssm, F32)], [(1, d_ssm)], name="s5_d_gelu")
        add_small('ssm_d', ddsk)
        dyb = dy.astype(BF16)
        dcbd_re = hmm(sv['hb_re'], dyb, ta=True, name="s5_d_c_re")
        dcbd_im = hmm(sv['hb_im'], dyb, ta=True, alpha=-1.0, name="s5_d_c_im")
        add_small('ssm_c_re', block_diag_extract(dcbd_re, n_grp).transpose(0, 2, 1)[None])
        add_small('ssm_c_im', block_diag_extract(dcbd_im, n_grp).transpose(0, 2, 1)[None])
        dh_re = hmm(dyb, sv['cbd_re'], tb=True, name="s5_d_h_re").reshape(t, gp // 128, 128)
        dh_im = hmm(dyb, sv['cbd_im'], tb=True, alpha=-1.0, name="s5_d_h_im").reshape(t, gp // 128, 128)
        hp_re = jnp.pad(sv['h_re'][:-1], ((1, 0), (0, 0), (0, 0)))
        hp_im = jnp.pad(sv['h_im'][:-1], ((1, 0), (0, 0), (0, 0)))
        g_re, g_im, da_re, da_im = ssm_scan_bwd(dh_re, dh_im, hp_re, hp_im, sv['a_re'], sv['a_im'], name="s5_d_scan")
        gb_re, gb_im = g_re.reshape(t, gp).astype(BF16), g_im.reshape(t, gp).astype(BF16)
        dbbd_re = hmm(sv['ub'], gb_re, ta=True, name="s5_d_bb_re")
        dbbd_im = hmm(sv['ub'], gb_im, ta=True, name="s5_d_bb_im")
        du = hmm(gb_re, sv['bbd_re'], tb=True, res=du1, name="s5_d_u_re")
        du = hmm(gb_im, sv['bbd_im'], tb=True, res=du, name="s5_d_u_im")
        sp = sv['sp']
        dbb_re = block_diag_extract(dbbd_re, n_grp).transpose(0, 2, 1).reshape(gp, SSM_GROUP)
        dbb_im = block_diag_extract(dbbd_im, n_grp).transpose(0, 2, 1).reshape(gp, SSM_GROUP)
        dbr, dbi, dz_re, dz_im = rowk(vjp_rows(f_ssm_bb, 4, 2),
                                      [sp['br'], sp['bi'], sp['z_re'], sp['z_im'], dbb_re, dbb_im], [],
                                      [(SSM_GROUP, F32)] * 2 + [(1, F32)] * 2, name="ssm_d_bb", tb=gp)
        add_small('ssm_b_re', dbr.reshape(W['ssm_b_re'].shape))
        add_small('ssm_b_im', dbi.reshape(W['ssm_b_im'].shape))
        dlr, dli, dldt = rowk(vjp_rows(f_ssm_ab, 3, 4),
                              [sp['lr'], sp['li'], sp['ldt'], da_re.reshape(n_grp, SSM_STATE),
                               da_im.reshape(n_grp, SSM_STATE), dz_re.reshape(n_grp, SSM_STATE),
                               dz_im.reshape(n_grp, SSM_STATE)], [],
                              [(SSM_STATE, F32)] * 2 + [(1, F32)], name="ssm_d_ab", tb=n_grp)
        add_small('ssm_lambda_re', dlr[None])
        add_small('ssm_lambda_im', dli[None])
        add_small('ssm_log_dt', dldt.reshape(1, n_grp))
        grads['ssm_w_in'] = [hmm(sv['hn'], du, ta=True, out_dtype=GRAD_XFER, name="s5_d_win").reshape(4, -1, d_ssm)]
        dhn = hmm(du, sv['win'], tb=True, name="s5_d_hn")
        dh_in, dg = rms_bwd(sv['h'], dhn, dh, sv['g'], name="s5_d_rms")
        add_small('mix_norm', dg[0], 1, depth)
        return dh_in

    def ple_fwd(h, i):
        g = W['ple_norm'][i][None]
        wpg = gather('ple_w_gate', i).reshape(d, d)
        wpp = gather('ple_w_proj', i)
        n = rowk(lambda hh, gg: f_rms(hh, gg), [h], [g], [(d, BF16)], name="ple_rms")[0]
        zg = hmm(n, wpg, name="ple_gate")
        pb = p[i, 0].astype(BF16)
        pp = hmm(pb, wpp, name="ple_proj")
        f_ple = lambda zz, pq, hh: hh + f_sigmoid(zz) * pq
        h_out = rowk(f_ple, [zg, pp, h], [], [(d, F32)], name="ple_mix")[0]
        return h_out, dict(h=h, g=g, n=n, zg=zg, pp=pp, pb=pb, wpg=wpg, wpp=wpp)

    def ple_bwd(dh, sv, i):
        f_ple = lambda zz, pq: f_sigmoid(zz) * pq
        dzg, dpp = rowk(vjp_rows(f_ple, 2, 1), [sv['zg'], sv['pp'], dh], [], [(d, BF16), (d, BF16)],
                        name="ple_d_mix")
        grads.setdefault('ple_w_proj', [None] * depth)[i] = hmm(sv['pb'], dpp, ta=True, nshard=4, out3=True,
                                                               out_dtype=GRAD_XFER, name="ple_d_wproj")
        grads.setdefault('ple_w_gate', [None] * depth)[i] = hmm(sv['n'], dzg, ta=True, out_dtype=GRAD_XFER,
                                                               name="ple_d_wgate").reshape(4, -1, d)
        dn = hmm(dzg, sv['wpg'], tb=True, name="ple_d_n")
        dh_in, dg = rms_bwd(sv['h'], dn, dh, sv['g'], name="ple_d_rms")
        add_small('ple_norm', dg[0], i, depth)
        return dh_in

    ag_order = []
    for i in range(depth):
        ag_order += [('ffn1_w_gate', i), ('ffn1_w_up', i), ('ffn1_w_down', i)]
        ag_order += [('ab_w_in', 0), ('ab_w_out', 0)] if i % 2 == 0 else [('ssm_w_in', 0), ('ssm_w_out', 0)]
        ag_order += [('ffn2_w_gate', i), ('ffn2_w_up', i), ('ffn2_w_down', i), ('ple_w_gate', i), ('ple_w_proj', i)]
    first = Comm()
    for n in SMALL_SHARDED:
        first.all_gather(W[n], nchunk=1)
    first_out = run_comm(first, name="ag_small")
    W = dict(W)
    for n, full in zip(SMALL_SHARDED, first_out):
        w = W[n]
        W[n + '_full'] = jnp.moveaxis(full, 0, -2).reshape(w.shape[1:-1] + (4 * w.shape[-1],))
    W['ssm_d_full'] = W['ssm_d_full'][None]
    queue.extend(ag_entry(n, li_) for n, li_ in ag_order)
    flush(until=lambda: ag_order[1] in gathered)

    h = h0
    saved = []
    for i in range(depth):
        sv = {}
        h, sv['ffn1'] = ffn_fwd(h, 'ffn1', i)
        if i % 2 == 0:
            h, sv['mix'] = mixer_ab_fwd(h)
        else:
            h, sv['mix'] = mixer_s5_fwd(h)
        h, sv['ffn2'] = ffn_fwd(h, 'ffn2', i)
        h, sv['ple'] = ple_fwd(h, i)
        saved.append(sv)

    def f_loss(y, tg):
        e = y - tg
        part = 0.5 * jnp.sum(jnp.sum(e * e, axis=-1, keepdims=True) * (1.0 / d), axis=0, keepdims=True)
        return e * (1.0 / d), jnp.broadcast_to(part, (1, 128))
    dh, loss_part = rowk(f_loss, [h, tgt], [], [(d, F32)], [(1, 128)], name="loss")
    loss = lax.psum(loss_part[0, 0], ("x", "y", "c"))

    for i in reversed(range(depth)):
        sv = saved[i]
        dh = ple_bwd(dh, sv['ple'], i)
        dh = ffn_bwd(dh, sv['ffn2'], 'ffn2', i)
        if i % 2 == 0:
            dh = mixer_ab_bwd(dh, sv['mix'])
        else:
            dh = mixer_s5_bwd(dh, sv['mix'])
        dh = ffn_bwd(dh, sv['ffn1'], 'ffn1', i)
    grad_x = dh[None]

    small_names = [n for n in W_NAMES if n not in BIG]
    small_full = []
    for n in small_names:
        v_ = small[n]
        if isinstance(v_, list):
            v_ = jnp.stack(v_)
        full_shape = W[n].shape[:-1] + (4 * W[n].shape[-1],) if n in SMALL_SHARDED else W[n].shape
        small_full.append(v_.reshape(full_shape))
    packed = pack_flat(small_full)
    ar = Comm()
    ar.gather_all(packed)
    summed = sum_slots(run_comm(ar, name="ar_small")[0], name="ar_small_sum")
    small_tot = unpack_flat(summed, [a.shape for a in small_full])
    g_small = {}
    for n, a in zip(small_names, small_tot):
        if n in SMALL_SHARDED:
            ns = W[n].shape[-1]
            a = lax.dynamic_slice_in_dim(a, qchip * ns, ns, axis=a.ndim - 1)
        g_small[n] = a

    out = {}
    pk = lambda dct: pack_flat([dct[n] for n in small_names])
    res = adamw_flat(pk(W), pk(g_small), pk(M), pk(V), name="adamw_small")
    shapes = [W[n].shape for n in small_names]
    for kind, buf in zip(('delta', 'm', 'v'), res):
        for n, a in zip(small_names, unpack_flat(buf, shapes)):
            out[(kind, n)] = a
    for n in small_names:
        out[('grad', n)] = g_small[n]

    flush()
    for n in BIG:
        nl = W[n].shape[0]
        rows, cols = int(np.prod(W[n].shape[1:-1])), W[n].shape[-1]
        res = adamw_big(W[n].reshape(nl, rows, cols), [halves[(n, li_)][0] for li_ in range(nl)],
                        [halves[(n, li_)][1] for li_ in range(nl)], M[n].reshape(nl, rows, cols),
                        V[n].reshape(nl, rows, cols), name=f"adamw_{n}")
        for kind, a in zip(('grad', 'delta', 'm', 'v'), res):
            out[(kind, n)] = a.reshape(W[n].shape)

    return (loss, grad_x, *[out[('grad', n)] for n in W_NAMES], *[out[('delta', n)] for n in W_NAMES],
            *[out[('m', n)] for n in W_NAMES], *[out[('v', n)] for n in W_NAMES])
```

```python
import functools
import math

import numpy as np
import jax
import jax.numpy as jnp
from jax import lax
from jax.experimental import pallas as pl
from jax.experimental.pallas import tpu as pltpu

F32 = jnp.float32
BF16 = jnp.bfloat16
HI = lax.Precision.HIGHEST
MESH = pl.DeviceIdType.MESH

CHUNK = 64
N_LEFT = 8
BAND = (N_LEFT + 1) * CHUNK
PAD = N_LEFT * CHUNK
HEAD = 64
REL_CLIP = 128
N_REL = (CHUNK - 1) + REL_CLIP + 1
DECAY_LORA = 64
AAA_LORA = 64
GATE_LORA = 128
SSM_GROUP = 16
SSM_STATE = 64
RMS_EPS = 1e-6
GN_EPS = 64e-5
NEG_BIG = -1e30

ADAM_LR = 0.001
ADAM_B1 = 0.9
ADAM_B2 = 0.999
ADAM_EPS = 1e-08
ADAM_WD = 0.01
ADAM_STEP = 10

RW_CHUNK = 64
RW_HEADS = 4
ATT_HEADS = 2
VMEM_LIMIT = 56 * 1024 * 1024
ROW_BLOCK_BYTES = 6 * 1024 * 1024
GRAD_XFER = BF16
LOCAL_CHUNKS = 4

W_NAMES = ['ffn1_norm', 'ffn1_w_gate', 'ffn1_w_up', 'ffn1_w_down', 'mix_norm', 'ffn2_norm', 'ffn2_w_gate',
           'ffn2_w_up', 'ffn2_w_down', 'ple_norm', 'ple_w_gate', 'ple_w_proj', 'ab_w_in', 'att_q_gain',
           'att_k_gain', 'att_rel_bias', 'rwkv_mu', 'rwkv_w0', 'rwkv_w_up', 'rwkv_a0', 'rwkv_a_up',
           'rwkv_g_up', 'rwkv_k_k', 'rwkv_k_a', 'rwkv_r_k', 'rwkv_lnx_w', 'rwkv_lnx_b', 'ab_w_out',
           'ssm_w_in', 'ssm_lambda_re', 'ssm_lambda_im', 'ssm_log_dt', 'ssm_b_re', 'ssm_b_im', 'ssm_c_re',
           'ssm_c_im', 'ssm_d', 'ssm_w_out']
BIG = ['ffn1_w_gate', 'ffn1_w_up', 'ffn1_w_down', 'ffn2_w_gate', 'ffn2_w_up', 'ffn2_w_down',
       'ple_w_gate', 'ple_w_proj', 'ab_w_in', 'ab_w_out', 'ssm_w_in', 'ssm_w_out']
SMALL_SHARDED = ['rwkv_w_up', 'rwkv_a_up', 'rwkv_g_up', 'ssm_d']


def _cparams(n_axes):
    return pltpu.CompilerParams(dimension_semantics=("arbitrary",) * n_axes, vmem_limit_bytes=VMEM_LIMIT)


def _pick(n, prefs):
    for p in prefs:
        if n % p == 0:
            return p
    return n


def mm(a, b, *, ta=False, tb=False, nshard=None, out3=False, out_dtype=F32, res=None, alpha=1.0, comm=None, name):
    a3, b3 = a.ndim == 3, b.ndim == 3
    if a3:
        assert not ta
        sk, m, ks = a.shape
        k = sk * ks
    elif ta:
        k, m = a.shape
    else:
        m, k = a.shape
    kshard = None
    if b3 and not tb:
        s, kb, ns = b.shape
        n = s * ns
        nshard = s
    elif b3 and tb:
        sk2, n, ks2 = b.shape
        kb = sk2 * ks2
        kshard = (sk2, ks2)
    elif tb:
        n, kb = b.shape
    else:
        kb, n = b.shape
    assert k == kb, (a.shape, b.shape, ta, tb)
    if a3:
        assert kshard is None or kshard == (sk, ks)
        kshard = (sk, ks)
    if nshard is not None:
        tn, nj = n // nshard, nshard
    else:
        assert not out3
        tn = _pick(n, (1024, 1408, 1280, 512, 640, 256, 128))
        nj = n // tn
    if kshard is not None:
        nk, tk = kshard
    else:
        tk = _pick(k, (2048, 1408, 1280, 1024, 640, 512, 256, 128))
        nk = k // tk
    tm = _pick(m, (512, 256, 128))
    ni = m // tm

    if a3:
        a_spec = pl.BlockSpec((None, tm, tk), lambda i, j, kk: (kk, i, 0))
    elif ta:
        a_spec = pl.BlockSpec((tk, tm), lambda i, j, kk: (kk, i))
    else:
        a_spec = pl.BlockSpec((tm, tk), lambda i, j, kk: (i, kk))
    if b3 and not tb:
        b_spec = pl.BlockSpec((None, tk, tn), lambda i, j, kk: (j, kk, 0))
    elif b3 and tb:
        b_spec = pl.BlockSpec((None, tn, tk), lambda i, j, kk: (kk, j, 0))
    elif tb:
        b_spec = pl.BlockSpec((tn, tk), lambda i, j, kk: (j, kk))
    else:
        b_spec = pl.BlockSpec((tk, tn), lambda i, j, kk: (kk, j))
    if out3:
        o_spec = pl.BlockSpec((None, tm, tn), lambda i, j, kk: (j, i, 0))
        o_shape = (nj, m, tn)
    else:
        o_spec = pl.BlockSpec((tm, tn), lambda i, j, kk: (i, j))
        o_shape = (m, n)
    has_res = res is not None
    dn = (((0 if ta else 1,), (1 if tb else 0,)), ((), ()))

    n_cin = len(comm.ins) if comm else 0
    n_cout = len(comm.outs) if comm else 0
    n_in = 2 + has_res

    def body(*refs):
        a_ref, b_ref = refs[0], refs[1]
        r_ref = refs[2] if has_res else None
        c_in = refs[n_in:n_in + n_cin]
        o_ref = refs[n_in + n_cin]
        c_out = refs[n_in + n_cin + 1:n_in + n_cin + 1 + n_cout]
        scratch = refs[n_in + n_cin + 1 + n_cout:]
        acc_ref = scratch[0] if nk > 1 else None
        sems = scratch[1:] if nk > 1 else scratch
        i, j, kk = pl.program_id(0), pl.program_id(1), pl.program_id(2)

        if comm:
            @pl.when((i == 0) & (j == 0) & (kk == 0))
            def _():
                comm.start(c_in, c_out, *sems)

        def finish(acc):
            val = acc * alpha if alpha != 1.0 else acc
            if has_res:
                val = val + r_ref[...].astype(F32)
            o_ref[...] = val.astype(o_ref.dtype)

        part = lax.dot_general(a_ref[...].astype(BF16), b_ref[...].astype(BF16), dn, preferred_element_type=F32)
        if nk == 1:
            finish(part)
        else:
            @pl.when(kk == 0)
            def _():
                acc_ref[...] = part

            @pl.when(kk > 0)
            def _():
                acc_ref[...] += part

            @pl.when(kk == nk - 1)
            def _():
                finish(acc_ref[...])

        if comm:
            @pl.when((i == ni - 1) & (j == nj - 1) & (kk == nk - 1))
            def _():
                comm.wait(c_in, c_out, *sems)

    in_specs = [a_spec, b_spec] + ([o_spec] if has_res else []) + [ANY] * n_cin
    args = (a, b) + ((res,) if has_res else ()) + (tuple(comm.ins) if comm else ())
    out_specs = [o_spec] + [ANY] * n_cout
    out_shape = [jax.ShapeDtypeStruct(o_shape, out_dtype)] + (list(comm.outs) if comm else [])
    scratch_shapes = ([pltpu.VMEM((tm, tn), F32)] if nk > 1 else []) + (comm.sem_shapes() if comm else [])
    aliases = {n_in + ii: 1 + oi for ii, oi in comm.aliases.items()} if comm else {}
    outs = pl.pallas_call(
        body, name=name, grid=(ni, nj, nk), in_specs=in_specs, out_specs=out_specs, out_shape=out_shape,
        scratch_shapes=scratch_shapes, input_output_aliases=aliases, compiler_params=_cparams(3))(*args)
    if comm:
        return outs[0], list(outs[1:])
    return outs[0]


def rowk(fn, rows, consts, out_rows, out_accs=(), *, name, tb=None):
    t = rows[0].shape[0]
    nr, nc, no, na = len(rows), len(consts), len(out_rows), len(out_accs)
    if tb is None:
        per_row = sum(r.shape[1] * 4 for r in rows) + sum(n * 4 for n, _ in out_rows)
        tb = 8
        while tb * 2 <= min(t, 1024) and tb * 2 * per_row <= ROW_BLOCK_BYTES and t % (tb * 2) == 0:
            tb *= 2
    assert t % tb == 0
    nb = t // tb

    def body(*refs):
        r_in, c_in = refs[:nr], refs[nr:nr + nc]
        o_rows, o_accs = refs[nr + nc:nr + nc + no], refs[nr + nc + no:]
        outs = fn(*[r[...] for r in r_in], *[c[...] for c in c_in])
        if not isinstance(outs, (tuple, list)):
            outs = (outs,)
        assert len(outs) == no + na, (name, len(outs), no, na)
        for ref, v in zip(o_rows, outs[:no]):
            ref[...] = v.astype(ref.dtype)
        if na:
            @pl.when(pl.program_id(0) == 0)
            def _():
                for ref in o_accs:
                    ref[...] = jnp.zeros_like(ref)
            for ref, v in zip(o_accs, outs[no:]):
                ref[...] += v.astype(F32)

    in_specs = [pl.BlockSpec((tb, r.shape[1]), lambda i: (i, 0)) for r in rows]
    in_specs += [pl.BlockSpec(c.shape, lambda i, nd=c.ndim: (0,) * nd) for c in consts]
    out_specs = [pl.BlockSpec((tb, n), lambda i: (i, 0)) for n, _ in out_rows]
    out_specs += [pl.BlockSpec(s, lambda i, nd=len(s): (0,) * nd) for s in out_accs]
    out_shape = [jax.ShapeDtypeStruct((t, n), d) for n, d in out_rows]
    out_shape += [jax.ShapeDtypeStruct(s, F32) for s in out_accs]
    res = pl.pallas_call(body, name=name, grid=(nb,), in_specs=in_specs, out_specs=out_specs,
                         out_shape=out_shape, compiler_params=_cparams(1))(*rows, *consts)
    return res


def _f32(*xs):
    return [x.astype(F32) for x in xs]


def vjp_rows(f, n_rows, n_cots):
    def fn(*args):
        rows = _f32(*args[:n_rows])
        cots = _f32(*args[n_rows:n_rows + n_cots])
        consts = _f32(*args[n_rows + n_cots:])
        outs, pull = jax.vjp(f, *rows, *consts)
        if not isinstance(outs, (tuple, list)):
            cots = cots[0]
        else:
            cots = tuple(cots)
        return pull(cots)
    return fn


def hdot(x, y):
    return jnp.dot(x, y, precision=HI, preferred_element_type=F32)


def f_rms(h, g):
    return h * lax.rsqrt(jnp.mean(h * h, axis=-1, keepdims=True) + RMS_EPS) * g


def f_sigmoid(x):
    return 1.0 / (1.0 + jnp.exp(-x))


def f_swiglu(a, b):
    return a * f_sigmoid(a) * b


def f_softplus(x):
    return jnp.maximum(x, 0.0) + jnp.log(1.0 + jnp.exp(-jnp.abs(x)))


def f_gelu(x):
    return 0.5 * x * (1.0 + jnp.tanh(math.sqrt(2.0 / math.pi) * (x + 0.044715 * (x * x * x))))


def f_rwkv_pre(zs, w0, wup_p, a0, aup_p, g_up, k_k, k_a, e, et):
    d = w0.shape[1]
    r, k, v = zs[:, :d], zs[:, d:2 * d], zs[:, 2 * d:3 * d]
    xwa = zs[:, 3 * d:3 * d + DECAY_LORA + AAA_LORA]
    xg = zs[:, 3 * d + DECAY_LORA + AAA_LORA:]
    w_log = -f_softplus(-(w0 + hdot(jnp.tanh(xwa), wup_p))) - 0.5
    logw = -jnp.exp(w_log)
    a = f_sigmoid(a0 + hdot(xwa, aup_p))
    g = hdot(f_sigmoid(xg), g_up)
    kk = k * k_k
    nrm = jnp.maximum(jnp.sqrt(hdot(kk * kk, e)), 1e-12)
    kk = kk * hdot(1.0 / nrm, et)
    k2 = k * (1.0 + (a - 1.0) * k_a)
    return r, logw, k2, v, -kk, kk * a, g


def f_rwkv_post(y, r, k2, v, g, r_k, lnx_w, lnx_b, e, et):
    inv = 1.0 / HEAD
    mean = hdot(hdot(y, e) * inv, et)
    yc = y - mean
    var = hdot(yc * yc, e) * inv
    yn = yc * hdot(lax.rsqrt(var + GN_EPS), et) * lnx_w + lnx_b
    bonus = hdot(hdot(r * k2 * r_k, e), et) * v
    return (yn + bonus) * g


def f_ssm_ab(lr, li, log_dt):
    dt = jnp.exp(log_dt)
    mag = jnp.exp(lr * dt)
    ab_re, ab_im = mag * jnp.cos(li * dt), mag * jnp.sin(li * dt)
    denom = lr * lr + li * li
    z_re = ((ab_re - 1.0) * lr + ab_im * li) / denom
    z_im = (ab_im * lr - (ab_re - 1.0) * li) / denom
    return ab_re, ab_im, z_re, z_im


def f_ssm_bb(br, bi, z_re, z_im):
    return z_re * br - z_im * bi, z_re * bi + z_im * br


def _col_block(n):
    return _pick(n, (256, 128))


def ts_fwd(proj, col0, width, mu, *, name):
    t = proj.shape[0]
    cb = _col_block(width)
    assert col0 % cb == 0 and width % cb == 0
    off = col0 // cb

    def body(z_ref, mu_ref, o_ref):
        z = z_ref[...]
        row = lax.broadcasted_iota(jnp.int32, z.shape, 0)
        prev = jnp.where(row == 0, 0.0, pltpu.roll(z, 1, 0))
        o_ref[...] = z + (prev - z) * mu_ref[...]

    return pl.pallas_call(
        body, name=name, grid=(width // cb,),
        in_specs=[pl.BlockSpec((t, cb), lambda j: (0, j + off)), pl.BlockSpec((1, cb), lambda j: (0, j))],
        out_specs=pl.BlockSpec((t, cb), lambda j: (0, j)),
        out_shape=jax.ShapeDtypeStruct((t, width), F32), compiler_params=_cparams(1))(proj, mu)


def ts_bwd(proj, col0, dzs, mu, *, name):
    t, width = dzs.shape
    cb = _col_block(width)
    off = col0 // cb

    def body(z_ref, d_ref, mu_ref, dz_ref, dmu_ref):
        z, d, m = z_ref[...], d_ref[...], mu_ref[...]
        row = lax.broadcasted_iota(jnp.int32, z.shape, 0)
        prev = jnp.where(row == 0, 0.0, pltpu.roll(z, 1, 0))
        dm = d * m
        nxt = jnp.where(row == t - 1, 0.0, pltpu.roll(dm, t - 1, 0))
        dz_ref[...] = d - dm + nxt
        dmu_ref[...] = jnp.sum(d * (prev - z), axis=0, keepdims=True)

    return pl.pallas_call(
        body, name=name, grid=(width // cb,),
        in_specs=[pl.BlockSpec((t, cb), lambda j: (0, j + off)), pl.BlockSpec((t, cb), lambda j: (0, j)),
                  pl.BlockSpec((1, cb), lambda j: (0, j))],
        out_specs=[pl.BlockSpec((t, cb), lambda j: (0, j)), pl.BlockSpec((1, cb), lambda j: (0, j))],
        out_shape=[jax.ShapeDtypeStruct((t, width), F32), jax.ShapeDtypeStruct((1, width), F32)],
        compiler_params=_cparams(1))(proj, dzs, mu)


def _att_scores(qn, kb, bias, c):
    s = jnp.einsum('hqd,hkd->hqk', qn, kb, preferred_element_type=F32) + bias
    col = lax.broadcasted_iota(jnp.int32, s.shape, 2)
    s = jnp.where(col >= PAD - c * CHUNK, s, NEG_BIG)
    s = s - jnp.max(s, axis=-1, keepdims=True)
    e = jnp.exp(s)
    return e / jnp.sum(e, axis=-1, keepdims=True)


def att_fwd(qn, knp, vp, bias, *, name):
    h, t, _ = qn.shape
    hb = ATT_HEADS
    nc = t // CHUNK

    def body(q_ref, k_ref, v_ref, b_ref, o_ref):
        c = pl.program_id(1)
        start = pl.multiple_of(c * CHUNK, CHUNK)
        kb = k_ref[:, pl.ds(start, BAND), :]
        vb = v_ref[:, pl.ds(start, BAND), :]
        p = _att_scores(q_ref[...], kb, b_ref[...], c)
        o_ref[...] = jnp.einsum('hqk,hkd->hqd', p.astype(BF16), vb, preferred_element_type=F32).astype(o_ref.dtype)

    return pl.pallas_call(
        body, name=name, grid=(h // hb, nc),
        in_specs=[pl.BlockSpec((hb, CHUNK, HEAD), lambda g, c: (g, c, 0)),
                  pl.BlockSpec((hb, t + PAD, HEAD), lambda g, c: (g, 0, 0)),
                  pl.BlockSpec((hb, t + PAD, HEAD), lambda g, c: (g, 0, 0)),
                  pl.BlockSpec((hb, CHUNK, BAND), lambda g, c: (g, 0, 0))],
        out_specs=pl.BlockSpec((hb, CHUNK, HEAD), lambda g, c: (g, c, 0)),
        out_shape=jax.ShapeDtypeStruct((h, t, HEAD), BF16), compiler_params=_cparams(2))(qn, knp, vp, bias)


def att_bwd(qn, knp, vp, bias, do, *, name):
    h, t, _ = qn.shape
    hb = ATT_HEADS
    nc = t // CHUNK

    def body(q_ref, k_ref, v_ref, b_ref, do_ref, dq_ref, dk_ref, dv_ref, db_ref):
        c = pl.program_id(1)

        @pl.when(c == 0)
        def _():
            dk_ref[...] = jnp.zeros_like(dk_ref)
            dv_ref[...] = jnp.zeros_like(dv_ref)
            db_ref[...] = jnp.zeros_like(db_ref)

        start = pl.multiple_of(c * CHUNK, CHUNK)
        qv = q_ref[...]
        kb = k_ref[:, pl.ds(start, BAND), :]
        vb = v_ref[:, pl.ds(start, BAND), :]
        p = _att_scores(qv, kb, b_ref[...], c)
        dov = do_ref[...]
        dp = jnp.einsum('hqd,hkd->hqk', dov, vb, preferred_element_type=F32)
        ds = p * (dp - jnp.sum(p * dp, axis=-1, keepdims=True))
        db_ref[...] += ds
        dsb = ds.astype(BF16)
        dq_ref[...] = jnp.einsum('hqk,hkd->hqd', dsb, kb, preferred_element_type=F32)
        dst = jnp.swapaxes(dsb, 1, 2)
        pt = jnp.swapaxes(p.astype(BF16), 1, 2)
        dk_ref[:, pl.ds(start, BAND), :] += jnp.einsum('hkq,hqd->hkd', dst, qv, preferred_element_type=F32)
        dv_ref[:, pl.ds(start, BAND), :] += jnp.einsum('hkq,hqd->hkd', pt, dov, preferred_element_type=F32)

    blk_q = pl.BlockSpec((hb, CHUNK, HEAD), lambda g, c: (g, c, 0))
    blk_k = pl.BlockSpec((hb, t + PAD, HEAD), lambda g, c: (g, 0, 0))
    blk_b = pl.BlockSpec((hb, CHUNK, BAND), lambda g, c: (g, 0, 0))
    return pl.pallas_call(
        body, name=name, grid=(h // hb, nc),
        in_specs=[blk_q, blk_k, blk_k, blk_b, blk_q],
        out_specs=[blk_q, blk_k, blk_k, blk_b],
        out_shape=[jax.ShapeDtypeStruct((h, t, HEAD), F32), jax.ShapeDtypeStruct((h, t + PAD, HEAD), F32),
                   jax.ShapeDtypeStruct((h, t + PAD, HEAD), F32), jax.ShapeDtypeStruct((h, CHUNK, BAND), F32)],
        compiler_params=_cparams(2))(qn, knp, vp, bias, do)


def _rel_index():
    i = np.arange(CHUNK)[:, None]
    j = np.arange(BAND)[None, :]
    return np.clip(i + PAD - j, -(CHUNK - 1), REL_CLIP) + (CHUNK - 1)


def _rel_onehot():
    return (jnp.asarray(_rel_index())[:, :, None] == jnp.arange(N_REL)[None, None, :]).astype(F32)


def relbias_expand(rel, *, name):
    h = rel.shape[0]
    onehot_t = jnp.swapaxes(_rel_onehot(), 1, 2)

    def body(r_ref, oh_ref, o_ref):
        o_ref[...] = hdot(r_ref[...], oh_ref[...])

    out = pl.pallas_call(
        body, name=name, grid=(CHUNK,),
        in_specs=[pl.BlockSpec((h, N_REL), lambda i: (0, 0)), pl.BlockSpec((None, N_REL, BAND), lambda i: (i, 0, 0))],
        out_specs=pl.BlockSpec((None, h, BAND), lambda i: (i, 0, 0)),
        out_shape=jax.ShapeDtypeStruct((CHUNK, h, BAND), F32), compiler_params=_cparams(1))(rel, onehot_t)
    return jnp.swapaxes(out, 0, 1)


def relbias_reduce(dbias, *, name):
    h = dbias.shape[0]
    onehot = _rel_onehot()
    dbt = jnp.swapaxes(dbias, 0, 1)

    def body(d_ref, oh_ref, o_ref):
        @pl.when(pl.program_id(0) == 0)
        def _():
            o_ref[...] = jnp.zeros_like(o_ref)
        o_ref[...] += hdot(d_ref[...], oh_ref[...])

    return pl.pallas_call(
        body, name=name, grid=(CHUNK,),
        in_specs=[pl.BlockSpec((None, h, BAND), lambda i: (i, 0, 0)),
                  pl.BlockSpec((None, BAND, N_REL), lambda i: (i, 0, 0))],
        out_specs=pl.BlockSpec((h, N_REL), lambda i: (0, 0)),
        out_shape=jax.ShapeDtypeStruct((h, N_REL), F32), compiler_params=_cparams(1))(dbt, onehot)


def _bt(x):
    return jnp.swapaxes(x, 1, 2)


_BDN = (((2,), (1,)), ((0,), (0,)))


def _bmm_exact(x, y):
    return lax.dot_general(x, y, _BDN, precision=HI, preferred_element_type=F32)


def _split_bf16(x):
    hi = x.astype(BF16)
    return hi, (x - hi.astype(F32)).astype(BF16)


def _bmm_3pass(x, y):
    xh, xl = _split_bf16(x)
    yh, yl = _split_bf16(y)
    dot = lambda p, q: lax.dot_general(p, q, _BDN, preferred_element_type=F32)
    return dot(xh, yh) + (dot(xh, yl) + dot(xl, yh))


def _make_bmm(raw):
    @jax.custom_vjp
    def f(x, y):
        return raw(x, y)

    def fwd(x, y):
        return raw(x, y), (x, y)

    def bwd(saved, dz):
        x, y = saved
        return raw(dz, _bt(y)), raw(_bt(x), dz)

    f.defvjp(fwd, bwd)
    return f


bmm = _make_bmm(_bmm_3pass)
bmm_exact = _make_bmm(_bmm_exact)


def rwkv_chunk(p0, r, lw, k, v, a, b):
    g, c, n = r.shape
    row = lax.broadcasted_iota(jnp.int32, (g, c, c), 1)
    col = lax.broadcasted_iota(jnp.int32, (g, c, c), 2)
    incl, strict = row >= col, row > col
    cs = bmm_exact(incl.astype(F32), lw)
    cs_end = cs[:, c - 1:c, :]
    e_cs = jnp.exp(cs)
    e_neg = jnp.exp(-cs)
    at = a * jnp.exp(cs - lw)
    rt = r * e_cs
    bt_, kt = b * e_neg, k * e_neg
    e_tail = jnp.exp(cs_end - cs)
    bh, kh = b * e_tail, k * e_tail
    btt, ktt = _bt(bt_), _bt(kt)
    a_ab = jnp.where(strict, bmm(at, btt), 0.0)
    a_ak = jnp.where(strict, bmm(at, ktt), 0.0)
    a_rb = jnp.where(incl, bmm(rt, btt), 0.0)
    a_rk = jnp.where(incl, bmm(rt, ktt), 0.0)
    tinv = jnp.where(row == col, 1.0, 0.0) + a_ab
    npow = a_ab
    for _ in range(int(math.log2(c)) - 1):
        npow = bmm(npow, npow)
        tinv = tinv + bmm(tinv, npow)
    u = bmm(tinv, bmm(at, p0) + bmm(a_ak, v))
    y = bmm(rt, p0) + bmm(a_rb, u) + bmm(a_rk, v)
    rown = lax.broadcasted_iota(jnp.int32, (g, n, n), 1)
    coln = lax.broadcasted_iota(jnp.int32, (g, n, n), 2)
    dg = jnp.where(rown == coln, jnp.exp(cs_end), 0.0)
    p1 = bmm(dg, p0) + bmm(_bt(bh), u) + bmm(_bt(kh), v)
    return y, p1


def rwkv_fwd(r, lw, k, v, a, b, *, name):
    h, t, n = r.shape
    g, c = min(RW_HEADS, h), RW_CHUNK
    nch = t // c

    def body(r_ref, lw_ref, k_ref, v_ref, a_ref, b_ref, y_ref, p_ref, st_ref):
        @pl.when(pl.program_id(1) == 0)
        def _():
            st_ref[...] = jnp.zeros_like(st_ref)
        p0 = st_ref[...]
        p_ref[...] = p0
        y, p1 = rwkv_chunk(p0, r_ref[...], lw_ref[...], k_ref[...], v_ref[...], a_ref[...], b_ref[...])
        y_ref[...] = y
        st_ref[...] = p1

    blk = pl.BlockSpec((g, c, n), lambda i, j: (i, j, 0))
    pblk = pl.BlockSpec((g, None, n, n), lambda i, j: (i, j, 0, 0))
    return pl.pallas_call(
        body, name=name, grid=(h // g, nch), in_specs=[blk] * 6, out_specs=[blk, pblk],
        out_shape=[jax.ShapeDtypeStruct((h, t, n), F32), jax.ShapeDtypeStruct((h, nch, n, n), F32)],
        scratch_shapes=[pltpu.VMEM((g, n, n), F32)], compiler_params=_cparams(2))(r, lw, k, v, a, b)


def rwkv_bwd(r, lw, k, v, a, b, p0s, dy, *, name):
    h, t, n = r.shape
    g, c = min(RW_HEADS, h), RW_CHUNK
    nch = t // c

    def body(r_ref, lw_ref, k_ref, v_ref, a_ref, b_ref, p_ref, dy_ref,
             dr_ref, dlw_ref, dk_ref, dv_ref, da_ref, db_ref, dp_ref):
        @pl.when(pl.program_id(1) == 0)
        def _():
            dp_ref[...] = jnp.zeros_like(dp_ref)
        _, pull = jax.vjp(rwkv_chunk, p_ref[...], r_ref[...], lw_ref[...], k_ref[...], v_ref[...],
                          a_ref[...], b_ref[...])
        dp0, dr, dlw, dk, dv, da, db = pull((dy_ref[...], dp_ref[...]))
        dr_ref[...] = dr
        dlw_ref[...] = dlw
        dk_ref[...] = dk
        dv_ref[...] = dv
        da_ref[...] = da
        db_ref[...] = db
        dp_ref[...] = dp0

    blk = pl.BlockSpec((g, c, n), lambda i, j: (i, nch - 1 - j, 0))
    pblk = pl.BlockSpec((g, None, n, n), lambda i, j: (i, nch - 1 - j, 0, 0))
    return pl.pallas_call(
        body, name=name, grid=(h // g, nch), in_specs=[blk] * 6 + [pblk, blk], out_specs=[blk] * 6,
        out_shape=[jax.ShapeDtypeStruct((h, t, n), F32)] * 6,
        scratch_shapes=[pltpu.VMEM((g, n, n), F32)], compiler_params=_cparams(2))(r, lw, k, v, a, b, p0s, dy)


def _time_block(t):
    return _pick(t, (256, 128, 64))


def ssm_scan_fwd(bu_re, bu_im, a_re, a_im, *, name):
    t, rr, ln = bu_re.shape
    tb = _time_block(t)

    def body(br_ref, bi_ref, ar_ref, ai_ref, hr_ref, hi_ref, sr_ref, si_ref):
        @pl.when(pl.program_id(0) == 0)
        def _():
            sr_ref[...] = jnp.zeros_like(sr_ref)
            si_ref[...] = jnp.zeros_like(si_ref)
        ar, ai = ar_ref[...], ai_ref[...]

        def step(i, carry):
            hr, hi = carry
            nr = ar * hr - ai * hi + br_ref[i]
            ni = ar * hi + ai * hr + bi_ref[i]
            hr_ref[i] = nr
            hi_ref[i] = ni
            return nr, ni

        hr, hi = lax.fori_loop(0, tb, step, (sr_ref[...], si_ref[...]))
        sr_ref[...] = hr
        si_ref[...] = hi

    blk = pl.BlockSpec((tb, rr, ln), lambda i: (i, 0, 0))
    cblk = pl.BlockSpec((rr, ln), lambda i: (0, 0))
    return pl.pallas_call(
        body, name=name, grid=(t // tb,), in_specs=[blk, blk, cblk, cblk], out_specs=[blk, blk],
        out_shape=[jax.ShapeDtypeStruct((t, rr, ln), F32)] * 2,
        scratch_shapes=[pltpu.VMEM((rr, ln), F32)] * 2, compiler_params=_cparams(1))(bu_re, bu_im, a_re, a_im)


def ssm_scan_bwd(dh_re, dh_im, hp_re, hp_im, a_re, a_im, *, name):
    t, rr, ln = dh_re.shape
    tb = _time_block(t)
    nb = t // tb

    def body(dr_ref, di_ref, pr_ref, pi_ref, ar_ref, ai_ref, gr_ref, gi_ref, dar_ref, dai_ref, sr_ref, si_ref):
        @pl.when(pl.program_id(0) == 0)
        def _():
            sr_ref[...] = jnp.zeros_like(sr_ref)
            si_ref[...] = jnp.zeros_like(si_ref)
            dar_ref[...] = jnp.zeros_like(dar_ref)
            dai_ref[...] = jnp.zeros_like(dai_ref)
        ar, ai = ar_ref[...], ai_ref[...]

        def step(ii, carry):
            gr, gi, dar, dai = carry
            i = tb - 1 - ii
            nr = dr_ref[i] + ar * gr + ai * gi
            ni = di_ref[i] - ai * gr + ar * gi
            gr_ref[i] = nr
            gi_ref[i] = ni
            pr, pi = pr_ref[i], pi_ref[i]
            dar = dar + nr * pr + ni * pi
            dai = dai - nr * pi + ni * pr
            return nr, ni, dar, dai

        gr, gi, dar, dai = lax.fori_loop(0, tb, step, (sr_ref[...], si_ref[...], dar_ref[...], dai_ref[...]))
        sr_ref[...] = gr
        si_ref[...] = gi
        dar_ref[...] = dar
        dai_ref[...] = dai

    blk = pl.BlockSpec((tb, rr, ln), lambda i: (nb - 1 - i, 0, 0))
    cblk = pl.BlockSpec((rr, ln), lambda i: (0, 0))
    return pl.pallas_call(
        body, name=name, grid=(nb,), in_specs=[blk] * 4 + [cblk, cblk], out_specs=[blk, blk, cblk, cblk],
        out_shape=[jax.ShapeDtypeStruct((t, rr, ln), F32)] * 2 + [jax.ShapeDtypeStruct((rr, ln), F32)] * 2,
        scratch_shapes=[pltpu.VMEM((rr, ln), F32)] * 2,
        compiler_params=_cparams(1))(dh_re, dh_im, hp_re, hp_im, a_re, a_im)


ANY = pl.BlockSpec(memory_space=pl.ANY)


def _rows(ref, lead, ch, nchunk):
    base = ref if lead is None else ref.at[lead]
    if nchunk == 1:
        return base
    n = base.shape[0] // nchunk
    return base.at[pl.ds(ch * n, n)]


class Comm:
    def __init__(self):
        self.ins, self.outs, self.ops, self.aliases = [], [], [], {}
        self.n_remote, self.n_local = 0, 0
        self.ici = False

    def _add(self, kind, src, out_shape, n_peers, nchunk, n_local=1, alias=False):
        lead_len = src.shape[1] if kind in ("scatter", "rs1") else src.shape[0]
        while lead_len % nchunk:
            nchunk //= 2
        self.ops.append((kind, len(self.ins), len(self.outs), self.n_remote, self.n_local, nchunk))
        if alias:
            self.aliases[len(self.ins)] = len(self.outs)
        self.ins.append(src)
        self.outs.append(jax.ShapeDtypeStruct(out_shape, src.dtype))
        self.n_remote += n_peers * nchunk
        self.n_local += n_local
        self.ici = self.ici or kind in ("gather", "scatter", "gather_all", "ag1", "rs2")
        return len(self.outs) - 1

    def ag1(self, w):
        return self._add("ag1", w, (4,) + w.shape, 3, 1, n_local=LOCAL_CHUNKS)

    def ag2(self, g):
        return self._add("ag2", g, g.shape, 3, 1, n_local=0, alias=True)

    def rs1(self, g4):
        s, r, c = g4.shape
        return self._add("rs1", g4, (s, r // 2, c), 4, 1, n_local=0)

    def rs2(self, h4):
        return self._add("rs2", h4, (3,) + h4.shape[1:], 3, 1, n_local=0)

    def rs3(self, s, nchunk=4):
        return self._add("rs3", s, s.shape, 1, nchunk, n_local=0)

    def all_gather(self, w, nchunk=2):
        return self._add("gather", w, (4,) + w.shape, 3, nchunk)

    def scatter(self, g4, nchunk=2):
        return self._add("scatter", g4, g4.shape, 3, nchunk)

    def swap(self, s, nchunk=8):
        return self._add("swap", s, (2,) + s.shape, 1, nchunk)

    def gather_all(self, v, nchunk=1):
        return self._add("gather_all", v, (8,) + v.shape, 7, nchunk)

    def sem_shapes(self):
        return [pltpu.SemaphoreType.DMA((self.n_remote,)), pltpu.SemaphoreType.DMA((self.n_remote,)),
                pltpu.SemaphoreType.DMA((max(self.n_local, 1),))]

    def _two_level(self, kind, src, dst, r0, l0, nchunk, x, y, c, send, recv, lsem, sends, recvs, locs):
        chips = [(1 - x, y), (x, 1 - y), (1 - x, 1 - y)]
        me, sib = 2 * x + y, (x, y, 1 - c)

        def half(ref, hc):
            n = ref.shape[0] // 2
            return ref.at[pl.ds(hc * n, n)]

        def both(k, dev, s_ref, d_send, d_recv):
            mk = functools.partial(pltpu.make_async_remote_copy, src_ref=s_ref, send_sem=send.at[k],
                                   recv_sem=recv.at[k], device_id=dev, device_id_type=MESH)
            sends.append(mk(dst_ref=d_send))
            recvs.append(mk(dst_ref=d_recv))

        if kind == "ag1":
            n = src.shape[0] // LOCAL_CHUNKS
            for j in range(LOCAL_CHUNKS):
                rows = pl.ds(j * n, n)
                locs.append(pltpu.make_async_copy(src.at[rows], dst.at[me].at[rows], lsem.at[l0 + j]))
            for pj, (px, py) in enumerate(chips):
                both(r0 + pj, (px, py, c), half(src, c), half(dst.at[me], c), half(dst.at[2 * px + py], c))
        elif kind == "ag2":
            for pj, (px, py) in enumerate(chips):
                got = dst.at[2 * px + py]
                both(r0 + pj, sib, half(got, c), half(got, c), half(got, 1 - c))
        elif kind == "rs1":
            for q in range(4):
                both(r0 + q, sib, half(src.at[q], 1 - c), dst.at[q], dst.at[q])
        elif kind == "rs2":
            for pj, (px, py) in enumerate(chips):
                both(r0 + pj, (px, py, c), src.at[2 * px + py], dst.at[pj], dst.at[pj])
        else:
            for ch in range(nchunk):
                both(r0 + ch, sib, _rows(src, None, ch, nchunk), _rows(dst, None, ch, nchunk),
                     _rows(dst, None, ch, nchunk))

    def _descs(self, c_in, c_out, send, recv, lsem):
        x, y, c = lax.axis_index("x"), lax.axis_index("y"), lax.axis_index("c")
        sends, recvs, locs = [], [], []
        for kind, ii, oi, r0, l0, nchunk in self.ops:
            src, dst = c_in[ii], c_out[oi]
            if kind in ("ag1", "ag2", "rs1", "rs2", "rs3"):
                self._two_level(kind, src, dst, r0, l0, nchunk, x, y, c, send, recv, lsem, sends, recvs, locs)
                continue
            if kind == "swap":
                me = c
                peers = [((x, y, 1 - c), 1 - c)]
            elif kind == "gather_all":
                me = 4 * x + 2 * y + c
                flips = [(dx, dy, dc) for dx in (0, 1) for dy in (0, 1) for dc in (0, 1) if dx + dy + dc]
                peers = []
                for dx, dy, dc in flips:
                    px, py, pc = (x + dx) % 2, (y + dy) % 2, (c + dc) % 2
                    peers.append(((px, py, pc), 4 * px + 2 * py + pc))
            else:
                me = 2 * x + y
                peers = [((px, py, c), 2 * px + py) for px, py in ((1 - x, y), (x, 1 - y), (1 - x, 1 - y))]
            if kind == "scatter":
                locs.append(pltpu.make_async_copy(src.at[me], dst.at[me], lsem.at[l0]))
            else:
                locs.append(pltpu.make_async_copy(src, dst.at[me], lsem.at[l0]))
            for pj, (dev, peer_slot) in enumerate(peers):
                for ch in range(nchunk):
                    k = r0 + pj * nchunk + ch
                    s_src = _rows(src, peer_slot if kind == "scatter" else None, ch, nchunk)
                    mk = functools.partial(pltpu.make_async_remote_copy, send_sem=send.at[k], recv_sem=recv.at[k],
                                           device_id=dev, device_id_type=MESH)
                    sends.append(mk(src_ref=s_src, dst_ref=_rows(dst, me, ch, nchunk)))
                    recvs.append(mk(src_ref=s_src, dst_ref=_rows(dst, peer_slot, ch, nchunk)))
        return sends, recvs, locs

    def start(self, c_in, c_out, send, recv, lsem):
        sends, _, locs = self._descs(c_in, c_out, send, recv, lsem)
        for d in locs + sends:
            d.start()

    def wait(self, c_in, c_out, send, recv, lsem):
        sends, recvs, locs = self._descs(c_in, c_out, send, recv, lsem)
        for d in recvs:
            d.wait_recv()
        for d in sends:
            d.wait_send()
        for d in locs:
            d.wait()


def run_comm(comm, *, name):
    n_cin, n_cout = len(comm.ins), len(comm.outs)

    def body(*refs):
        c_in, c_out, sems = refs[:n_cin], refs[n_cin:n_cin + n_cout], refs[n_cin + n_cout:]
        comm.start(c_in, c_out, *sems)
        comm.wait(c_in, c_out, *sems)

    outs = pl.pallas_call(
        body, name=name, in_specs=[ANY] * n_cin, out_specs=[ANY] * n_cout, out_shape=list(comm.outs),
        scratch_shapes=comm.sem_shapes(), input_output_aliases=dict(comm.aliases),
        compiler_params=pltpu.CompilerParams(has_side_effects=True))(*comm.ins)
    return list(outs)


def _core_index():
    return jnp.reshape(lax.axis_index("c"), (1,)).astype(jnp.int32)


def _chip_index():
    return jnp.reshape(2 * lax.axis_index("x") + lax.axis_index("y"), (1,)).astype(jnp.int32)


def _row_block(rows, bytes_per_row, cap=512):
    tb = 16
    while tb * 2 <= min(rows, cap) and rows % (tb * 2) == 0 and tb * 2 * bytes_per_row <= ROW_BLOCK_BYTES:
        tb *= 2
    return tb


def add_half(g4, got, *, name):
    s, r, c = g4.shape
    r2 = r // 2
    tb = _row_block(r2, c * 8)
    nb = r2 // tb

    def body(c_ref, g_ref, x_ref, o_ref):
        o_ref[...] = (g_ref[...].astype(F32) + x_ref[...].astype(F32)).astype(o_ref.dtype)

    grid_spec = pltpu.PrefetchScalarGridSpec(
        num_scalar_prefetch=1, grid=(s, nb),
        in_specs=[pl.BlockSpec((None, tb, c), lambda q, i, cr: (q, cr[0] * nb + i, 0)),
                  pl.BlockSpec((None, tb, c), lambda q, i, cr: (q, i, 0))],
        out_specs=pl.BlockSpec((None, tb, c), lambda q, i, cr: (q, i, 0)))
    return pl.pallas_call(body, name=name, grid_spec=grid_spec, out_shape=jax.ShapeDtypeStruct((s, r2, c), g4.dtype),
                          compiler_params=_cparams(2))(_core_index(), g4, got)


def sum_chips(h4, got3, *, name):
    _, r2, c = h4.shape
    tb = _row_block(r2, c * 12)
    nb = r2 // tb

    def body(q_ref, h_ref, y_ref, o_ref):
        o_ref[...] = ((h_ref[...].astype(F32) + y_ref[0].astype(F32)) + y_ref[1].astype(F32)) + y_ref[2].astype(F32)

    grid_spec = pltpu.PrefetchScalarGridSpec(
        num_scalar_prefetch=1, grid=(nb,),
        in_specs=[pl.BlockSpec((None, tb, c), lambda i, qr: (qr[0], i, 0)),
                  pl.BlockSpec((3, tb, c), lambda i, qr: (0, i, 0))],
        out_specs=pl.BlockSpec((tb, c), lambda i, qr: (i, 0)))
    return pl.pallas_call(body, name=name, grid_spec=grid_spec, out_shape=jax.ShapeDtypeStruct((r2, c), F32),
                          compiler_params=_cparams(1))(_chip_index(), h4, got3)


def adamw_big(w, mine, theirs, m, v, *, comm=None, name):
    nl, r, c = w.shape
    r2 = r // 2
    tb = _row_block(r2, c * 4 * 10, cap=256)
    nb2 = r2 // tb
    nb = 2 * nb2
    n_cin = len(comm.ins) if comm else 0
    n_cout = len(comm.outs) if comm else 0

    def body(c_ref, w_ref, m_ref, v_ref, *rest):
        g_refs, rest = rest[:2 * nl], rest[2 * nl:]
        c_in, (go_ref, d_ref, mo_ref, vo_ref), rest = rest[:n_cin], rest[n_cin:n_cin + 4], rest[n_cin + 4:]
        c_out, sems = rest[:n_cout], rest[n_cout:]
        layer, i = pl.program_id(0), pl.program_id(1)
        if comm:
            @pl.when((layer == 0) & (i == 0))
            def _():
                comm.start(c_in, c_out, *sems)

            @pl.when((layer == nl - 1) & (i == nb - 1))
            def _():
                comm.wait(c_in, c_out, *sems)
        own = (i // nb2) == c_ref[0]
        for l0 in range(nl):
            @pl.when(layer == l0)
            def _(l0=l0):
                g = jnp.where(own, g_refs[2 * l0][...], g_refs[2 * l0 + 1][...])
                d, mn, vn = _adam_math(w_ref[...], g, m_ref[...], v_ref[...])
                go_ref[...] = g
                d_ref[...] = d
                mo_ref[...] = mn
                vo_ref[...] = vn

    blk = pl.BlockSpec((None, tb, c), lambda l, i, cr: (l, i, 0))

    def half_spec(l0):
        return pl.BlockSpec((tb, c), lambda l, i, cr: (jnp.where(l == l0, i % nb2, jnp.where(l < l0, 0, nb2 - 1)), 0))

    in_specs = [blk, blk, blk]
    args = [w, m, v]
    for l0 in range(nl):
        in_specs += [half_spec(l0), half_spec(l0)]
        args += [mine[l0], theirs[l0]]
    if comm:
        in_specs += [ANY] * n_cin
        args += list(comm.ins)
    grid_spec = pltpu.PrefetchScalarGridSpec(
        num_scalar_prefetch=1, grid=(nl, nb), in_specs=in_specs, out_specs=[blk] * 4 + [ANY] * n_cout,
        scratch_shapes=comm.sem_shapes() if comm else [])
    aliases = {1 + 3 + 2 * nl + ii: 4 + oi for ii, oi in comm.aliases.items()} if comm else {}
    outs = pl.pallas_call(
        body, name=name, grid_spec=grid_spec,
        out_shape=[jax.ShapeDtypeStruct(w.shape, F32)] * 4 + (list(comm.outs) if comm else []),
        input_output_aliases=aliases, compiler_params=_cparams(2))(_core_index(), *args)
    if comm:
        return list(outs[:4]), list(outs[4:])
    return list(outs)


def sum_slots(x, *, name):
    s, r, c = x.shape
    tb = 8
    while tb * 2 <= min(r, 512) and r % (tb * 2) == 0 and tb * 2 * c * 4 * (s + 1) <= ROW_BLOCK_BYTES:
        tb *= 2

    def body(x_ref, o_ref):
        acc = x_ref[0].astype(F32)
        for i in range(1, s):
            acc = acc + x_ref[i].astype(F32)
        o_ref[...] = acc

    return pl.pallas_call(
        body, name=name, grid=(r // tb,), in_specs=[pl.BlockSpec((s, tb, c), lambda i: (0, i, 0))],
        out_specs=pl.BlockSpec((tb, c), lambda i: (i, 0)),
        out_shape=jax.ShapeDtypeStruct((r, c), F32), compiler_params=_cparams(1))(x)


def _adam_math(w, g, m, v):
    m = ADAM_B1 * m + (1.0 - ADAM_B1) * g
    v = ADAM_B2 * v + (1.0 - ADAM_B2) * (g * g)
    m_hat = m / (1.0 - ADAM_B1 ** ADAM_STEP)
    v_hat = v / (1.0 - ADAM_B2 ** ADAM_STEP)
    delta = -ADAM_LR * (m_hat / (jnp.sqrt(v_hat) + ADAM_EPS) + ADAM_WD * w)
    return delta, m, v


def adamw_pair(w, g2, m, v, *, name):
    r, c = w.shape
    tb = 8
    while tb * 2 <= min(r, 512) and r % (tb * 2) == 0 and tb * 2 * c * 4 * 9 <= 2 * ROW_BLOCK_BYTES:
        tb *= 2

    def body(w_ref, g_ref, m_ref, v_ref, go_ref, d_ref, mo_ref, vo_ref):
        g = g_ref[0] + g_ref[1]
        d, mn, vn = _adam_math(w_ref[...], g, m_ref[...], v_ref[...])
        go_ref[...] = g
        d_ref[...] = d
        mo_ref[...] = mn
        vo_ref[...] = vn

    blk = pl.BlockSpec((tb, c), lambda i: (i, 0))
    return pl.pallas_call(
        body, name=name, grid=(r // tb,), in_specs=[blk, pl.BlockSpec((2, tb, c), lambda i: (0, i, 0)), blk, blk],
        out_specs=[blk] * 4, out_shape=[jax.ShapeDtypeStruct((r, c), F32)] * 4,
        compiler_params=_cparams(1))(w, g2, m, v)


def adamw_flat(w, g, m, v, *, name):
    def fn(w_, g_, m_, v_):
        return _adam_math(w_, g_, m_, v_)
    return rowk(fn, [w, g, m, v], [], [(w.shape[1], F32)] * 3, name=name)


def to_heads(x):
    t, d = x.shape
    return x.reshape(t, d // HEAD, HEAD).transpose(1, 0, 2)


def from_heads(x):
    h, t, n = x.shape
    return x.transpose(1, 0, 2).reshape(t, h * n)


def pack_flat(arrs, lanes=128, row_mult=8):
    flat = jnp.concatenate([a.reshape(-1).astype(F32) for a in arrs])
    n = flat.shape[0]
    rows = -(-n // lanes)
    rows = -(-rows // row_mult) * row_mult
    return jnp.pad(flat, (0, rows * lanes - n)).reshape(rows, lanes)


def unpack_flat(buf, shapes):
    flat = buf.reshape(-1)
    outs, off = [], 0
    for s in shapes:
        n = int(np.prod(s))
        outs.append(flat[off:off + n].reshape(s))
        off += n
    return outs


def block_diag_from(w_gab):
    g, a, b = w_gab.shape
    eye = jnp.eye(g, dtype=w_gab.dtype)
    return (w_gab[:, :, None, :] * eye[:, None, :, None]).reshape(g * a, g * b)


def block_diag_extract(m, g):
    a, b = m.shape[0] // g, m.shape[1] // g
    eye = jnp.eye(g, dtype=m.dtype)
    return jnp.sum(m.reshape(g, a, g, b) * eye[:, None, :, None], axis=2)


def kernel(x, p, ffn1_norm, ffn1_w_gate, ffn1_w_up, ffn1_w_down, mix_norm, ffn2_norm, ffn2_w_gate, ffn2_w_up, ffn2_w_down, ple_norm, ple_w_gate, ple_w_proj, ab_w_in, att_q_gain, att_k_gain, att_rel_bias, rwkv_mu, rwkv_w0, rwkv_w_up, rwkv_a0, rwkv_a_up, rwkv_g_up, rwkv_k_k, rwkv_k_a, rwkv_r_k, rwkv_lnx_w, rwkv_lnx_b, ab_w_out, ssm_w_in, ssm_lambda_re, ssm_lambda_im, ssm_log_dt, ssm_b_re, ssm_b_im, ssm_c_re, ssm_c_im, ssm_d, ssm_w_out, loss_target, m_ffn1_norm, m_ffn1_w_gate, m_ffn1_w_up, m_ffn1_w_down, m_mix_norm, m_ffn2_norm, m_ffn2_w_gate, m_ffn2_w_up, m_ffn2_w_down, m_ple_norm, m_ple_w_gate, m_ple_w_proj, m_ab_w_in, m_att_q_gain, m_att_k_gain, m_att_rel_bias, m_rwkv_mu, m_rwkv_w0, m_rwkv_w_up, m_rwkv_a0, m_rwkv_a_up, m_rwkv_g_up, m_rwkv_k_k, m_rwkv_k_a, m_rwkv_r_k, m_rwkv_lnx_w, m_rwkv_lnx_b, m_ab_w_out, m_ssm_w_in, m_ssm_lambda_re, m_ssm_lambda_im, m_ssm_log_dt, m_ssm_b_re, m_ssm_b_im, m_ssm_c_re, m_ssm_c_im, m_ssm_d, m_ssm_w_out, v_ffn1_norm, v_ffn1_w_gate, v_ffn1_w_up, v_ffn1_w_down, v_mix_norm, v_ffn2_norm, v_ffn2_w_gate, v_ffn2_w_up, v_ffn2_w_down, v_ple_norm, v_ple_w_gate, v_ple_w_proj, v_ab_w_in, v_att_q_gain, v_att_k_gain, v_att_rel_bias, v_rwkv_mu, v_rwkv_w0, v_rwkv_w_up, v_rwkv_a0, v_rwkv_a_up, v_rwkv_g_up, v_rwkv_k_k, v_rwkv_k_a, v_rwkv_r_k, v_rwkv_lnx_w, v_rwkv_lnx_b, v_ab_w_out, v_ssm_w_in, v_ssm_lambda_re, v_ssm_lambda_im, v_ssm_log_dt, v_ssm_b_re, v_ssm_b_im, v_ssm_c_re, v_ssm_c_im, v_ssm_d, v_ssm_w_out):
    A = dict(locals())
    W = {n: A[n] for n in W_NAMES}
    return _step(A['x'], A['p'], A['loss_target'], W, {n: A['m_' + n] for n in W_NAMES},
                 {n: A['v_' + n] for n in W_NAMES})


def _step(x, p, target, W, M, V):
    assert x.shape[0] == 1
    t, d = x.shape[1], x.shape[2]
    depth = p.shape[0]
    h0 = x[0]
    tgt = target[0]
    qchip = 2 * lax.axis_index("x") + lax.axis_index("y")
    d_rw = W['rwkv_w0'].shape[1]
    d_att = W['ab_w_out'].shape[1] * 4 - d_rw
    n_h_att, n_h_rw = d_att // HEAD, d_rw // HEAD
    n_bin = 3 * d_rw + DECAY_LORA + AAA_LORA + GATE_LORA
    d_ssm = W['ssm_w_in'].shape[2]
    n_grp = d_ssm // SSM_GROUP
    gp = n_grp * SSM_STATE

    queue = []
    gathered = {}
    halves = {}
    grads = {}
    queued_grads = set()
    n_alone = [0]

    def ag_entry(name, layer):
        def add1(cm):
            return cm.ag1(W[name][layer].astype(BF16))

        def cont1(outs, hd):
            got = outs[hd]

            def cont2(outs2, hd2):
                gathered[(name, layer)] = outs2[hd2]
            queue.insert(0, (lambda cm: cm.ag2(got), cont2))
        return add1, cont1

    def rs_entry(name, layer, g4):
        shard_shape = W[name].shape[1:]
        rows, cols = int(np.prod(shard_shape[:-1])), shard_shape[-1]
        g4 = g4.reshape(4, rows, cols)

        def cont1(outs, hd):
            h4 = add_half(g4, outs[hd], name=f"rs_add_{name}")

            def cont2(outs2, hd2):
                mine = sum_chips(h4, outs2[hd2], name=f"rs_sum_{name}")

                def cont3(outs3, hd3):
                    halves[(name, layer)] = (mine, outs3[hd3])
                queue.insert(0, (lambda cm: cm.rs3(mine), cont3))
            queue.insert(0, (lambda cm: cm.rs2(h4), cont2))
        return (lambda cm: cm.rs1(g4)), cont1

    def enqueue_ready():
        for n in BIG:
            for li_, g4 in enumerate(grads.get(n, [])):
                if g4 is not None and (n, li_) not in queued_grads:
                    queued_grads.add((n, li_))
                    queue.append(rs_entry(n, li_, g4))

    def take():
        cm, conts = Comm(), []
        while queue and not cm.ici:
            add, cont = queue.pop(0)
            conts.append((cont, add(cm)))
        return cm, conts

    def hmm(*args, **kw):
        enqueue_ready()
        if not queue:
            return mm(*args, **kw)
        cm, conts = take()
        out, couts = mm(*args, comm=cm, **kw)
        for cont, hd in reversed(conts):
            cont(couts, hd)
        return out

    def flush(until=None, at_least_one=False):
        enqueue_ready()
        while queue and (at_least_one or not (until is not None and until())):
            at_least_one = False
            cm, conts = take()
            n_alone[0] += 1
            couts = run_comm(cm, name=f"comm_alone{n_alone[0]}")
            for cont, hd in reversed(conts):
                cont(couts, hd)
            enqueue_ready()

    def gather(name, layer):
        flush(until=lambda: (name, layer) in gathered)
        return gathered[(name, layer)]
    small = {}

    def add_small(name, val, layer=None, nl=1):
        if layer is None:
            small[name] = val
        else:
            small.setdefault(name, [None] * nl)[layer] = val

    def ffn_fwd(h, pre, i):
        g = W[pre + '_norm'][i][None]
        wg, wu, wd = gather(pre + '_w_gate', i), gather(pre + '_w_up', i), gather(pre + '_w_down', i)
        wd2 = wd.reshape(-1, d)
        n = rowk(lambda hh, gg: f_rms(hh, gg), [h], [g], [(d, BF16)], name=f"{pre}_rms")[0]
        a = hmm(n, wg, out_dtype=BF16, name=f"{pre}_gate")
        b = hmm(n, wu, out_dtype=BF16, name=f"{pre}_up")
        f = a.shape[1]
        u = rowk(lambda aa, bb: f_swiglu(*_f32(aa, bb)), [a, b], [], [(f, BF16)], name=f"{pre}_swiglu")[0]
        h_out = hmm(u, wd2, res=h, alpha=0.5, name=f"{pre}_down")
        return h_out, dict(h=h, g=g, n=n, a=a, b=b, u=u, wg=wg, wu=wu, wd2=wd2)

    def ffn_bwd(dh, sv, pre, i):
        f = sv['a'].shape[1]
        dwd = hmm(sv['u'], dh, ta=True, alpha=0.5, out_dtype=GRAD_XFER, name=f"{pre}_d_wdown")
        du = hmm(dh, sv['wd2'], tb=True, alpha=0.5, out_dtype=BF16, name=f"{pre}_d_u")
        da, db = rowk(vjp_rows(f_swiglu, 2, 1), [sv['a'], sv['b'], du], [], [(f, BF16), (f, BF16)],
                      name=f"{pre}_d_swiglu")
        dwg = hmm(sv['n'], da, ta=True, nshard=4, out3=True, out_dtype=GRAD_XFER, name=f"{pre}_d_wgate")
        dwu = hmm(sv['n'], db, ta=True, nshard=4, out3=True, out_dtype=GRAD_XFER, name=f"{pre}_d_wup")
        dn = hmm(da, sv['wg'], tb=True, name=f"{pre}_d_n1")
        dn = hmm(db, sv['wu'], tb=True, res=dn, name=f"{pre}_d_n2")
        dh_in, dg = rms_bwd(sv['h'], dn, dh, sv['g'], name=f"{pre}_d_rms")
        grads.setdefault(pre + '_w_gate', [None] * depth)[i] = dwg
        grads.setdefault(pre + '_w_up', [None] * depth)[i] = dwu
        grads.setdefault(pre + '_w_down', [None] * depth)[i] = dwd.reshape(4, -1, d)
        add_small(pre + '_norm', dg[0], i, depth)
        return dh_in

    def rms_bwd(h, dn, dh_res, g, *, name):
        def fn(hh, dnn, dres, gg):
            _, pull = jax.vjp(f_rms, hh, gg)
            dh_, dg_ = pull(dnn)
            return dh_ + dres, dg_
        return rowk(fn, [h, dn, dh_res], [g], [(d, F32)], [(1, d)], name=name)

    def head_consts(nh):
        e = np.kron(np.eye(nh, dtype=np.float32), np.ones((HEAD, 1), np.float32))
        return jnp.asarray(e), jnp.asarray(e.T)

    def mixer_ab_fwd(h):
        g = W['mix_norm'][0][None]
        win = gather('ab_w_in', 0).transpose(1, 0, 2).reshape(d, -1)
        wout = gather('ab_w_out', 0).reshape(-1, d)
        hn = rowk(lambda hh, gg: f_rms(hh, gg), [h], [g], [(d, BF16)], name="mixab_rms")[0]
        proj = hmm(hn, win, name="mixab_proj")
        q2, k2, v2 = [to_heads(proj[:, j * d_att:(j + 1) * d_att]).reshape(n_h_att * t, HEAD) for j in range(3)]
        qg, kg = W['att_q_gain'], W['att_k_gain']
        f_qn = lambda qq, gg: f_rms(qq, gg) * (HEAD ** -0.5)
        qn = rowk(f_qn, [q2], [qg], [(HEAD, BF16)], name="att_qnorm")[0].reshape(n_h_att, t, HEAD)
        kn = rowk(f_rms, [k2], [kg], [(HEAD, BF16)], name="att_knorm")[0].reshape(n_h_att, t, HEAD)
        knp = jnp.pad(kn, ((0, 0), (PAD, 0), (0, 0)))
        vp = jnp.pad(v2.astype(BF16).reshape(n_h_att, t, HEAD), ((0, 0), (PAD, 0), (0, 0)))
        bias = relbias_expand(W['att_rel_bias'][0], name="att_relbias")
        o = att_fwd(qn, knp, vp, bias, name="att_fwd")
        att = from_heads(o)
        mu = W['rwkv_mu']
        zs = ts_fwd(proj, 3 * d_att, n_bin, mu, name="rwkv_shift")
        e, et = head_consts(n_h_rw)
        zpad = jnp.zeros((AAA_LORA, d_rw), F32)
        wup_p = jnp.concatenate([W['rwkv_w_up_full'], zpad], 0)
        aup_p = jnp.concatenate([zpad, W['rwkv_a_up_full']], 0)
        pre_c = [W['rwkv_w0'], wup_p, W['rwkv_a0'], aup_p, W['rwkv_g_up_full'], W['rwkv_k_k'], W['rwkv_k_a'], e, et]
        pre = rowk(f_rwkv_pre, [zs], pre_c, [(d_rw, F32)] * 7, name="rwkv_pre")
        r_, lw_, kk_, vv_, ia_, ib_, gg_ = pre
        hm = [to_heads(u_) for u_ in (r_, lw_, kk_, vv_, ia_, ib_)]
        y_h, p0s = rwkv_fwd(*hm, name="rwkv_scan")
        y = from_heads(y_h)
        post_c = [W['rwkv_r_k'].reshape(1, d_rw), W['rwkv_lnx_w'], W['rwkv_lnx_b'], e, et]
        rw = rowk(f_rwkv_post, [y, r_, kk_, vv_, gg_], post_c, [(d_rw, BF16)], name="rwkv_post")[0]
        cat = jnp.concatenate([att, rw], axis=1)
        h_out = hmm(cat, wout, res=h, name="mixab_out")
        sv = dict(h=h, g=g, hn=hn, win=win, wout=wout, proj=proj, q2=q2, k2=k2, qn=qn, knp=knp, vp=vp, bias=bias,
                  zs=zs, pre_c=pre_c, pre=pre, hm=hm, p0s=p0s, y=y, post_c=post_c, cat=cat, qg=qg, kg=kg, mu=mu)
        return h_out, sv

    def mixer_ab_bwd(dh, sv):
        dwout = hmm(sv['cat'], dh, ta=True, out_dtype=GRAD_XFER, name="mixab_d_wout")
        grads['ab_w_out'] = [dwout.reshape(4, -1, d)]
        dcat = hmm(dh, sv['wout'], tb=True, name="mixab_d_cat")
        datt, drw = dcat[:, :d_att], dcat[:, d_att:]
        r_, lw_, kk_, vv_, ia_, ib_, gg_ = sv['pre']
        post = rowk(vjp_rows(f_rwkv_post, 5, 1), [sv['y'], r_, kk_, vv_, gg_, drw], sv['post_c'],
                    [(d_rw, F32)] * 5, [(1, d_rw)] * 3 + [sv['post_c'][3].shape, sv['post_c'][4].shape],
                    name="rwkv_d_post")
        dy, dr1, dk1, dv1, dg1 = post[:5]
        add_small('rwkv_r_k', post[5].reshape(W['rwkv_r_k'].shape))
        add_small('rwkv_lnx_w', post[6])
        add_small('rwkv_lnx_b', post[7])
        dscan = rwkv_bwd(*sv['hm'], sv['p0s'], to_heads(dy), name="rwkv_d_scan")
        dr2, dlw, dk2, dv2, dia, dib = [from_heads(u_) for u_ in dscan]

        def pre_bwd(zs, dra, drb, dlw_, dka, dkb, dva, dvb, dia_, dib_, dg_, *consts):
            _, pull = jax.vjp(f_rwkv_pre, zs, *consts)
            return pull((dra + drb, dlw_, dka + dkb, dva + dvb, dia_, dib_, dg_))

        pc = sv['pre_c']
        preb = rowk(pre_bwd, [sv['zs'], dr1, dr2, dlw, dk1, dk2, dv1, dv2, dia, dib, dg1], pc,
                    [(n_bin, F32)], [c.shape for c in pc], name="rwkv_d_pre")
        dzs = preb[0]
        add_small('rwkv_w0', preb[1])
        add_small('rwkv_w_up', preb[2][:DECAY_LORA])
        add_small('rwkv_a0', preb[3])
        add_small('rwkv_a_up', preb[4][DECAY_LORA:])
        add_small('rwkv_g_up', preb[5])
        add_small('rwkv_k_k', preb[6])
        add_small('rwkv_k_a', preb[7])
        dz, dmu = ts_bwd(sv['proj'], 3 * d_att, dzs, sv['mu'], name="rwkv_d_shift")
        add_small('rwkv_mu', dmu)
        do = to_heads(datt).astype(BF16)
        dqn, dknp, dvp, dbias = att_bwd(sv['qn'], sv['knp'], sv['vp'], sv['bias'], do, name="att_bwd")
        add_small('att_rel_bias', relbias_reduce(dbias, name="att_d_relbias")[None])
        f_qn = lambda qq, gg: f_rms(qq, gg) * (HEAD ** -0.5)
        dq2, dqg = rowk(vjp_rows(f_qn, 1, 1), [sv['q2'], dqn.reshape(-1, HEAD)], [sv['qg']], [(HEAD, F32)],
                        [(1, HEAD)], name="att_d_qnorm")
        dk2_, dkg = rowk(vjp_rows(f_rms, 1, 1), [sv['k2'], dknp[:, PAD:].reshape(-1, HEAD)], [sv['kg']],
                         [(HEAD, F32)], [(1, HEAD)], name="att_d_knorm")
        add_small('att_q_gain', dqg)
        add_small('att_k_gain', dkg)
        dproj = jnp.concatenate([from_heads(dq2.reshape(n_h_att, t, HEAD)), from_heads(dk2_.reshape(n_h_att, t, HEAD)),
                                 from_heads(dvp[:, PAD:]), dz], axis=1)
        dproj = dproj.astype(BF16)
        dwin = hmm(sv['hn'], dproj, ta=True, out_dtype=GRAD_XFER, name="mixab_d_win")
        grads['ab_w_in'] = [dwin.reshape(d, 4, -1).transpose(1, 0, 2)]
        dhn = hmm(dproj, sv['win'], tb=True, name="mixab_d_hn")
        dh_in, dg = rms_bwd(sv['h'], dhn, dh, sv['g'], name="mixab_d_rms")
        add_small('mix_norm', dg[0], 0, depth)
        return dh_in

    def ssm_params():
        lr, li = W['ssm_lambda_re'][0], W['ssm_lambda_im'][0]
        ldt = W['ssm_log_dt'][0][:, None]
        ab = rowk(f_ssm_ab, [lr, li, ldt], [], [(SSM_STATE, F32)] * 4, name="ssm_ab", tb=n_grp)
        br = W['ssm_b_re'][0].reshape(gp, SSM_GROUP)
        bi = W['ssm_b_im'][0].reshape(gp, SSM_GROUP)
        z_re, z_im = ab[2].reshape(gp, 1), ab[3].reshape(gp, 1)
        bb = rowk(f_ssm_bb, [br, bi, z_re, z_im], [], [(SSM_GROUP, F32)] * 2, name="ssm_bb", tb=gp)
        return dict(lr=lr, li=li, ldt=ldt, ab=ab, br=br, bi=bi, z_re=z_re, z_im=z_im, bb=bb)

    def mixer_s5_fwd(h):
        g = W['mix_norm'][1][None]
        win, wout = gather('ssm_w_in', 0).reshape(d, d_ssm), gather('ssm_w_out', 0)
        hn = rowk(lambda hh, gg: f_rms(hh, gg), [h], [g], [(d, BF16)], name="s5_rms")[0]
        u = hmm(hn, win, name="s5_in")
        sp = ssm_params()
        bbd_re = block_diag_from(sp['bb'][0].reshape(n_grp, SSM_STATE, SSM_GROUP).transpose(0, 2, 1)).astype(BF16)
        bbd_im = block_diag_from(sp['bb'][1].reshape(n_grp, SSM_STATE, SSM_GROUP).transpose(0, 2, 1)).astype(BF16)
        cbd_re = block_diag_from(W['ssm_c_re'][0].transpose(0, 2, 1)).astype(BF16)
        cbd_im = block_diag_from(W['ssm_c_im'][0].transpose(0, 2, 1)).astype(BF16)
        ub = u.astype(BF16)
        bu_re = hmm(ub, bbd_re, name="s5_bu_re").reshape(t, gp // 128, 128)
        bu_im = hmm(ub, bbd_im, name="s5_bu_im").reshape(t, gp // 128, 128)
        a_re, a_im = sp['ab'][0].reshape(gp // 128, 128), sp['ab'][1].reshape(gp // 128, 128)
        h_re, h_im = ssm_scan_fwd(bu_re, bu_im, a_re, a_im, name="s5_scan")
        hb_re, hb_im = h_re.reshape(t, gp).astype(BF16), h_im.reshape(t, gp).astype(BF16)
        y = hmm(hb_re, cbd_re, name="s5_y_re")
        y = hmm(hb_im, cbd_im, res=y, alpha=-1.0, name="s5_y_im")
        dsk = W['ssm_d_full']
        f_act = lambda yy, uu, dd: f_gelu(yy + dd * uu)
        yg = rowk(f_act, [y, u], [dsk], [(d_ssm, BF16)], name="s5_gelu")[0]
        z = hmm(yg, wout, name="s5_out")
        f_glu = lambda zz, hh: hh + zz[:, :d] * f_sigmoid(zz[:, d:])
        h_out = rowk(f_glu, [z, h], [], [(d, F32)], name="s5_glu")[0]
        sv = dict(h=h, g=g, hn=hn, win=win, wout=wout, u=u, ub=ub, sp=sp, bbd_re=bbd_re, bbd_im=bbd_im,
                  cbd_re=cbd_re, cbd_im=cbd_im, a_re=a_re, a_im=a_im, h_re=h_re, h_im=h_im, hb_re=hb_re,
                  hb_im=hb_im, y=y, dsk=dsk, yg=yg, z=z)
        return h_out, sv

    def mixer_s5_bwd(dh, sv):
        f_glu = lambda zz: zz[:, :d] * f_sigmoid(zz[:, d:])
        dz = rowk(vjp_rows(f_glu, 1, 1), [sv['z'], dh], [], [(2 * d, BF16)], name="s5_d_glu")[0]
        grads['ssm_w_out'] = [hmm(sv['yg'], dz, ta=True, nshard=4, out3=True, out_dtype=GRAD_XFER, name="s5_d_wout")]
        dyg = hmm(dz, sv['wout'], tb=True, name="s5_d_yg")
        f_act = lambda yy, uu, dd: f_gelu(yy + dd * uu)
        dy, du1, ddsk = rowk(vjp_rows(f_act, 2, 1), [sv['y'], sv['u'], dyg], [sv['dsk']],
                             [(d_ssm, F32), (d_ssm, F32)], [(1, d_ssm)], name="s5_d_gelu")
        add_small('ssm_d', ddsk)
        dyb = dy.astype(BF16)
        dcbd_re = hmm(sv['hb_re'], dyb, ta=True, name="s5_d_c_re")
        dcbd_im = hmm(sv['hb_im'], dyb, ta=True, alpha=-1.0, name="s5_d_c_im")
        add_small('ssm_c_re', block_diag_extract(dcbd_re, n_grp).transpose(0, 2, 1)[None])
        add_small('ssm_c_im', block_diag_extract(dcbd_im, n_grp).transpose(0, 2, 1)[None])
        dh_re = hmm(dyb, sv['cbd_re'], tb=True, name="s5_d_h_re").reshape(t, gp // 128, 128)
        dh_im = hmm(dyb, sv['cbd_im'], tb=True, alpha=-1.0, name="s5_d_h_im").reshape(t, gp // 128, 128)
        hp_re = jnp.pad(sv['h_re'][:-1], ((1, 0), (0, 0), (0, 0)))
        hp_im = jnp.pad(sv['h_im'][:-1], ((1, 0), (0, 0), (0, 0)))
        g_re, g_im, da_re, da_im = ssm_scan_bwd(dh_re, dh_im, hp_re, hp_im, sv['a_re'], sv['a_im'], name="s5_d_scan")
        gb_re, gb_im = g_re.reshape(t, gp).astype(BF16), g_im.reshape(t, gp).astype(BF16)
        dbbd_re = hmm(sv['ub'], gb_re, ta=True, name="s5_d_bb_re")
        dbbd_im = hmm(sv['ub'], gb_im, ta=True, name="s5_d_bb_im")
        du = hmm(gb_re, sv['bbd_re'], tb=True, res=du1, name="s5_d_u_re")
        du = hmm(gb_im, sv['bbd_im'], tb=True, res=du, name="s5_d_u_im")
        sp = sv['sp']
        dbb_re = block_diag_extract(dbbd_re, n_grp).transpose(0, 2, 1).reshape(gp, SSM_GROUP)
        dbb_im = block_diag_extract(dbbd_im, n_grp).transpose(0, 2, 1).reshape(gp, SSM_GROUP)
        dbr, dbi, dz_re, dz_im = rowk(vjp_rows(f_ssm_bb, 4, 2),
                                      [sp['br'], sp['bi'], sp['z_re'], sp['z_im'], dbb_re, dbb_im], [],
                                      [(SSM_GROUP, F32)] * 2 + [(1, F32)] * 2, name="ssm_d_bb", tb=gp)
        add_small('ssm_b_re', dbr.reshape(W['ssm_b_re'].shape))
        add_small('ssm_b_im', dbi.reshape(W['ssm_b_im'].shape))
        dlr, dli, dldt = rowk(vjp_rows(f_ssm_ab, 3, 4),
                              [sp['lr'], sp['li'], sp['ldt'], da_re.reshape(n_grp, SSM_STATE),
                               da_im.reshape(n_grp, SSM_STATE), dz_re.reshape(n_grp, SSM_STATE),
                               dz_im.reshape(n_grp, SSM_STATE)], [],
                              [(SSM_STATE, F32)] * 2 + [(1, F32)], name="ssm_d_ab", tb=n_grp)
        add_small('ssm_lambda_re', dlr[None])
        add_small('ssm_lambda_im', dli[None])
        add_small('ssm_log_dt', dldt.reshape(1, n_grp))
        grads['ssm_w_in'] = [hmm(sv['hn'], du, ta=True, out_dtype=GRAD_XFER, name="s5_d_win").reshape(4, -1, d_ssm)]
        dhn = hmm(du, sv['win'], tb=True, name="s5_d_hn")
        dh_in, dg = rms_bwd(sv['h'], dhn, dh, sv['g'], name="s5_d_rms")
        add_small('mix_norm', dg[0], 1, depth)
        return dh_in

    def ple_fwd(h, i):
        g = W['ple_norm'][i][None]
        wpg = gather('ple_w_gate', i).reshape(d, d)
        wpp = gather('ple_w_proj', i)
        n = rowk(lambda hh, gg: f_rms(hh, gg), [h], [g], [(d, BF16)], name="ple_rms")[0]
        zg = hmm(n, wpg, name="ple_gate")
        pb = p[i, 0].astype(BF16)
        pp = hmm(pb, wpp, name="ple_proj")
        f_ple = lambda zz, pq, hh: hh + f_sigmoid(zz) * pq
        h_out = rowk(f_ple, [zg, pp, h], [], [(d, F32)], name="ple_mix")[0]
        return h_out, dict(h=h, g=g, n=n, zg=zg, pp=pp, pb=pb, wpg=wpg, wpp=wpp)

    def ple_bwd(dh, sv, i):
        f_ple = lambda zz, pq: f_sigmoid(zz) * pq
        dzg, dpp = rowk(vjp_rows(f_ple, 2, 1), [sv['zg'], sv['pp'], dh], [], [(d, BF16), (d, BF16)],
                        name="ple_d_mix")
        grads.setdefault('ple_w_proj', [None] * depth)[i] = hmm(sv['pb'], dpp, ta=True, nshard=4, out3=True,
                                                               out_dtype=GRAD_XFER, name="ple_d_wproj")
        grads.setdefault('ple_w_gate', [None] * depth)[i] = hmm(sv['n'], dzg, ta=True, out_dtype=GRAD_XFER,
                                                               name="ple_d_wgate").reshape(4, -1, d)
        dn = hmm(dzg, sv['wpg'], tb=True, name="ple_d_n")
        dh_in, dg = rms_bwd(sv['h'], dn, dh, sv['g'], name="ple_d_rms")
        add_small('ple_norm', dg[0], i, depth)
        return dh_in

    ag_order = []
    for i in range(depth):
        ag_order += [('ffn1_w_gate', i), ('ffn1_w_up', i), ('ffn1_w_down', i)]
        ag_order += [('ab_w_in', 0), ('ab_w_out', 0)] if i % 2 == 0 else [('ssm_w_in', 0), ('ssm_w_out', 0)]
        ag_order += [('ffn2_w_gate', i), ('ffn2_w_up', i), ('ffn2_w_down', i), ('ple_w_gate', i), ('ple_w_proj', i)]
    first = Comm()
    for n in SMALL_SHARDED:
        first.all_gather(W[n], nchunk=1)
    first_out = run_comm(first, name="ag_small")
    W = dict(W)
    for n, full in zip(SMALL_SHARDED, first_out):
        w = W[n]
        W[n + '_full'] = jnp.moveaxis(full, 0, -2).reshape(w.shape[1:-1] + (4 * w.shape[-1],))
    W['ssm_d_full'] = W['ssm_d_full'][None]
    queue.extend(ag_entry(n, li_) for n, li_ in ag_order)
    flush(until=lambda: ag_order[1] in gathered)

    h = h0
    saved = []
    for i in range(depth):
        sv = {}
        h, sv['ffn1'] = ffn_fwd(h, 'ffn1', i)
        if i % 2 == 0:
            h, sv['mix'] = mixer_ab_fwd(h)
        else:
            h, sv['mix'] = mixer_s5_fwd(h)
        h, sv['ffn2'] = ffn_fwd(h, 'ffn2', i)
        h, sv['ple'] = ple_fwd(h, i)
        saved.append(sv)

    def f_loss(y, tg):
        e = y - tg
        part = 0.5 * jnp.sum(jnp.sum(e * e, axis=-1, keepdims=True) * (1.0 / d), axis=0, keepdims=True)
        return e * (1.0 / d), jnp.broadcast_to(part, (1, 128))
    dh, loss_part = rowk(f_loss, [h, tgt], [], [(d, F32)], [(1, 128)], name="loss")
    loss = lax.psum(loss_part[0, 0], ("x", "y", "c"))

    for i in reversed(range(depth)):
        sv = saved[i]
        dh = ple_bwd(dh, sv['ple'], i)
        dh = ffn_bwd(dh, sv['ffn2'], 'ffn2', i)
        if i % 2 == 0:
            dh = mixer_ab_bwd(dh, sv['mix'])
        else:
            dh = mixer_s5_bwd(dh, sv['mix'])
        dh = ffn_bwd(dh, sv['ffn1'], 'ffn1', i)
    grad_x = dh[None]

    small_names = [n for n in W_NAMES if n not in BIG]
    small_full = []
    for n in small_names:
        v_ = small[n]
        if isinstance(v_, list):
            v_ = jnp.stack(v_)
        full_shape = W[n].shape[:-1] + (4 * W[n].shape[-1],) if n in SMALL_SHARDED else W[n].shape
        small_full.append(v_.reshape(full_shape))
    packed = pack_flat(small_full)
    ar_got = []
    queue.append((lambda cm: cm.gather_all(packed), lambda outs, hd: ar_got.append(outs[hd])))

    out = {}
    todo = list(BIG)
    while todo:
        enqueue_ready()
        ready = [n for n in todo if all((n, li_) in halves for li_ in range(W[n].shape[0]))]
        if not ready:
            assert queue, todo
            flush(until=lambda: True, at_least_one=True)
            continue
        n = ready[0]
        todo.remove(n)
        nl = W[n].shape[0]
        rows, cols = int(np.prod(W[n].shape[1:-1])), W[n].shape[-1]
        a_args = (W[n].reshape(nl, rows, cols), [halves[(n, li_)][0] for li_ in range(nl)],
                  [halves[(n, li_)][1] for li_ in range(nl)], M[n].reshape(nl, rows, cols),
                  V[n].reshape(nl, rows, cols))
        if queue:
            cm, conts = take()
            res, couts = adamw_big(*a_args, comm=cm, name=f"adamw_{n}")
            for cont, hd in reversed(conts):
                cont(couts, hd)
        else:
            res = adamw_big(*a_args, name=f"adamw_{n}")
        for kind, a in zip(('grad', 'delta', 'm', 'v'), res):
            out[(kind, n)] = a.reshape(W[n].shape)
    flush()

    summed = sum_slots(ar_got[0], name="ar_small_sum")
    small_tot = unpack_flat(summed, [a.shape for a in small_full])
    g_small = {}
    for n, a in zip(small_names, small_tot):
        if n in SMALL_SHARDED:
            ns = W[n].shape[-1]
            a = lax.dynamic_slice_in_dim(a, qchip * ns, ns, axis=a.ndim - 1)
        g_small[n] = a

    pk = lambda dct: pack_flat([dct[n] for n in small_names])
    res = adamw_flat(pk(W), pk(g_small), pk(M), pk(V), name="adamw_small")
    shapes = [W[n].shape for n in small_names]
    for kind, buf in zip(('delta', 'm', 'v'), res):
        for n, a in zip(small_names, unpack_flat(buf, shapes)):
            out[(kind, n)] = a
    for n in small_names:
        out[('grad', n)] = g_small[n]

    return (loss, grad_x, *[out[('grad', n)] for n in W_NAMES], *[out[('delta', n)] for n in W_NAMES],
            *[out[('m', n)] for n in W_NAMES], *[out[('v', n)] for n in W_NAMES])
```

```python
import functools
import math

import numpy as np
import jax
import jax.numpy as jnp
from jax import lax
from jax.experimental import pallas as pl
from jax.experimental.pallas import tpu as pltpu

F32 = jnp.float32
BF16 = jnp.bfloat16
HI = lax.Precision.HIGHEST
MESH = pl.DeviceIdType.MESH

CHUNK = 64
N_LEFT = 8
BAND = (N_LEFT + 1) * CHUNK
PAD = N_LEFT * CHUNK
HEAD = 64
REL_CLIP = 128
N_REL = (CHUNK - 1) + REL_CLIP + 1
DECAY_LORA = 64
AAA_LORA = 64
GATE_LORA = 128
SSM_GROUP = 16
SSM_STATE = 64
RMS_EPS = 1e-6
GN_EPS = 64e-5
NEG_BIG = -1e30

ADAM_LR = 0.001
ADAM_B1 = 0.9
ADAM_B2 = 0.999
ADAM_EPS = 1e-08
ADAM_WD = 0.01
ADAM_STEP = 10

RW_CHUNK = 64
RW_HEADS = 4
ATT_HEADS = 2
VMEM_LIMIT = 56 * 1024 * 1024
ROW_BLOCK_BYTES = 6 * 1024 * 1024
GRAD_XFER = BF16
LOCAL_CHUNKS = 4

W_NAMES = ['ffn1_norm', 'ffn1_w_gate', 'ffn1_w_up', 'ffn1_w_down', 'mix_norm', 'ffn2_norm', 'ffn2_w_gate',
           'ffn2_w_up', 'ffn2_w_down', 'ple_norm', 'ple_w_gate', 'ple_w_proj', 'ab_w_in', 'att_q_gain',
           'att_k_gain', 'att_rel_bias', 'rwkv_mu', 'rwkv_w0', 'rwkv_w_up', 'rwkv_a0', 'rwkv_a_up',
           'rwkv_g_up', 'rwkv_k_k', 'rwkv_k_a', 'rwkv_r_k', 'rwkv_lnx_w', 'rwkv_lnx_b', 'ab_w_out',
           'ssm_w_in', 'ssm_lambda_re', 'ssm_lambda_im', 'ssm_log_dt', 'ssm_b_re', 'ssm_b_im', 'ssm_c_re',
           'ssm_c_im', 'ssm_d', 'ssm_w_out']
BIG = ['ffn1_w_gate', 'ffn1_w_up', 'ffn1_w_down', 'ffn2_w_gate', 'ffn2_w_up', 'ffn2_w_down',
       'ple_w_gate', 'ple_w_proj', 'ab_w_in', 'ab_w_out', 'ssm_w_in', 'ssm_w_out']
SMALL_SHARDED = ['rwkv_w_up', 'rwkv_a_up', 'rwkv_g_up', 'ssm_d']


def _cparams(n_axes):
    return pltpu.CompilerParams(dimension_semantics=("arbitrary",) * n_axes, vmem_limit_bytes=VMEM_LIMIT)


def _pick(n, prefs):
    for p in prefs:
        if n % p == 0:
            return p
    return n


def mm(a, b, *, ta=False, tb=False, nshard=None, out3=False, out_dtype=F32, res=None, alpha=1.0, comm=None, name):
    a3, b3 = a.ndim == 3, b.ndim == 3
    if a3:
        assert not ta
        sk, m, ks = a.shape
        k = sk * ks
    elif ta:
        k, m = a.shape
    else:
        m, k = a.shape
    kshard = None
    if b3 and not tb:
        s, kb, ns = b.shape
        n = s * ns
        nshard = s
    elif b3 and tb:
        sk2, n, ks2 = b.shape
        kb = sk2 * ks2
        kshard = (sk2, ks2)
    elif tb:
        n, kb = b.shape
    else:
        kb, n = b.shape
    assert k == kb, (a.shape, b.shape, ta, tb)
    if a3:
        assert kshard is None or kshard == (sk, ks)
        kshard = (sk, ks)
    if nshard is not None:
        tn, nj = n // nshard, nshard
    else:
        assert not out3
        tn = _pick(n, (1024, 1408, 1280, 512, 640, 256, 128))
        nj = n // tn
    if kshard is not None:
        nk, tk = kshard
    else:
        tk = _pick(k, (2048, 1408, 1280, 1024, 640, 512, 256, 128))
        nk = k // tk
    tm = _pick(m, (512, 256, 128))
    ni = m // tm

    if a3:
        a_spec = pl.BlockSpec((None, tm, tk), lambda i, j, kk: (kk, i, 0))
    elif ta:
        a_spec = pl.BlockSpec((tk, tm), lambda i, j, kk: (kk, i))
    else:
        a_spec = pl.BlockSpec((tm, tk), lambda i, j, kk: (i, kk))
    if b3 and not tb:
        b_spec = pl.BlockSpec((None, tk, tn), lambda i, j, kk: (j, kk, 0))
    elif b3 and tb:
        b_spec = pl.BlockSpec((None, tn, tk), lambda i, j, kk: (kk, j, 0))
    elif tb:
        b_spec = pl.BlockSpec((tn, tk), lambda i, j, kk: (j, kk))
    else:
        b_spec = pl.BlockSpec((tk, tn), lambda i, j, kk: (kk, j))
    if out3:
        o_spec = pl.BlockSpec((None, tm, tn), lambda i, j, kk: (j, i, 0))
        o_shape = (nj, m, tn)
    else:
        o_spec = pl.BlockSpec((tm, tn), lambda i, j, kk: (i, j))
        o_shape = (m, n)
    has_res = res is not None
    dn = (((0 if ta else 1,), (1 if tb else 0,)), ((), ()))

    n_cin = len(comm.ins) if comm else 0
    n_cout = len(comm.outs) if comm else 0
    n_in = 2 + has_res

    def body(*refs):
        a_ref, b_ref = refs[0], refs[1]
        r_ref = refs[2] if has_res else None
        c_in = refs[n_in:n_in + n_cin]
        o_ref = refs[n_in + n_cin]
        c_out = refs[n_in + n_cin + 1:n_in + n_cin + 1 + n_cout]
        scratch = refs[n_in + n_cin + 1 + n_cout:]
        acc_ref = scratch[0] if nk > 1 else None
        sems = scratch[1:] if nk > 1 else scratch
        i, j, kk = pl.program_id(0), pl.program_id(1), pl.program_id(2)

        if comm:
            @pl.when((i == 0) & (j == 0) & (kk == 0))
            def _():
                comm.start(c_in, c_out, *sems)

        def finish(acc):
            val = acc * alpha if alpha != 1.0 else acc
            if has_res:
                val = val + r_ref[...].astype(F32)
            o_ref[...] = val.astype(o_ref.dtype)

        part = lax.dot_general(a_ref[...].astype(BF16), b_ref[...].astype(BF16), dn, preferred_element_type=F32)
        if nk == 1:
            finish(part)
        else:
            @pl.when(kk == 0)
            def _():
                acc_ref[...] = part

            @pl.when(kk > 0)
            def _():
                acc_ref[...] += part

            @pl.when(kk == nk - 1)
            def _():
                finish(acc_ref[...])

        if comm:
            @pl.when((i == ni - 1) & (j == nj - 1) & (kk == nk - 1))
            def _():
                comm.wait(c_in, c_out, *sems)

    in_specs = [a_spec, b_spec] + ([o_spec] if has_res else []) + [ANY] * n_cin
    args = (a, b) + ((res,) if has_res else ()) + (tuple(comm.ins) if comm else ())
    out_specs = [o_spec] + [ANY] * n_cout
    out_shape = [jax.ShapeDtypeStruct(o_shape, out_dtype)] + (list(comm.outs) if comm else [])
    scratch_shapes = ([pltpu.VMEM((tm, tn), F32)] if nk > 1 else []) + (comm.sem_shapes() if comm else [])
    aliases = {n_in + ii: 1 + oi for ii, oi in comm.aliases.items()} if comm else {}
    outs = pl.pallas_call(
        body, name=name, grid=(ni, nj, nk), in_specs=in_specs, out_specs=out_specs, out_shape=out_shape,
        scratch_shapes=scratch_shapes, input_output_aliases=aliases, compiler_params=_cparams(3))(*args)
    if comm:
        return outs[0], list(outs[1:])
    return outs[0]


def rowk(fn, rows, consts, out_rows, out_accs=(), *, name, tb=None):
    t = rows[0].shape[0]
    nr, nc, no, na = len(rows), len(consts), len(out_rows), len(out_accs)
    if tb is None:
        per_row = sum(r.shape[1] * 4 for r in rows) + sum(n * 4 for n, _ in out_rows)
        tb = 8
        while tb * 2 <= min(t, 1024) and tb * 2 * per_row <= ROW_BLOCK_BYTES and t % (tb * 2) == 0:
            tb *= 2
    assert t % tb == 0
    nb = t // tb

    def body(*refs):
        r_in, c_in = refs[:nr], refs[nr:nr + nc]
        o_rows, o_accs = refs[nr + nc:nr + nc + no], refs[nr + nc + no:]
        outs = fn(*[r[...] for r in r_in], *[c[...] for c in c_in])
        if not isinstance(outs, (tuple, list)):
            outs = (outs,)
        assert len(outs) == no + na, (name, len(outs), no, na)
        for ref, v in zip(o_rows, outs[:no]):
            ref[...] = v.astype(ref.dtype)
        if na:
            @pl.when(pl.program_id(0) == 0)
            def _():
                for ref in o_accs:
                    ref[...] = jnp.zeros_like(ref)
            for ref, v in zip(o_accs, outs[no:]):
                ref[...] += v.astype(F32)

    in_specs = [pl.BlockSpec((tb, r.shape[1]), lambda i: (i, 0)) for r in rows]
    in_specs += [pl.BlockSpec(c.shape, lambda i, nd=c.ndim: (0,) * nd) for c in consts]
    out_specs = [pl.BlockSpec((tb, n), lambda i: (i, 0)) for n, _ in out_rows]
    out_specs += [pl.BlockSpec(s, lambda i, nd=len(s): (0,) * nd) for s in out_accs]
    out_shape = [jax.ShapeDtypeStruct((t, n), d) for n, d in out_rows]
    out_shape += [jax.ShapeDtypeStruct(s, F32) for s in out_accs]
    res = pl.pallas_call(body, name=name, grid=(nb,), in_specs=in_specs, out_specs=out_specs,
                         out_shape=out_shape, compiler_params=_cparams(1))(*rows, *consts)
    return res


def _f32(*xs):
    return [x.astype(F32) for x in xs]


def vjp_rows(f, n_rows, n_cots):
    def fn(*args):
        rows = _f32(*args[:n_rows])
        cots = _f32(*args[n_rows:n_rows + n_cots])
        consts = _f32(*args[n_rows + n_cots:])
        outs, pull = jax.vjp(f, *rows, *consts)
        if not isinstance(outs, (tuple, list)):
            cots = cots[0]
        else:
            cots = tuple(cots)
        return pull(cots)
    return fn


def hdot(x, y):
    return jnp.dot(x, y, precision=HI, preferred_element_type=F32)


def f_rms(h, g):
    return h * lax.rsqrt(jnp.mean(h * h, axis=-1, keepdims=True) + RMS_EPS) * g


def f_sigmoid(x):
    return 1.0 / (1.0 + jnp.exp(-x))


def f_swiglu(a, b):
    return a * f_sigmoid(a) * b


def f_softplus(x):
    return jnp.maximum(x, 0.0) + jnp.log(1.0 + jnp.exp(-jnp.abs(x)))


def f_gelu(x):
    return 0.5 * x * (1.0 + jnp.tanh(math.sqrt(2.0 / math.pi) * (x + 0.044715 * (x * x * x))))


def f_rwkv_pre(zs, w0, wup_p, a0, aup_p, g_up, k_k, k_a, e, et):
    d = w0.shape[1]
    r, k, v = zs[:, :d], zs[:, d:2 * d], zs[:, 2 * d:3 * d]
    xwa = zs[:, 3 * d:3 * d + DECAY_LORA + AAA_LORA]
    xg = zs[:, 3 * d + DECAY_LORA + AAA_LORA:]
    w_log = -f_softplus(-(w0 + hdot(jnp.tanh(xwa), wup_p))) - 0.5
    logw = -jnp.exp(w_log)
    a = f_sigmoid(a0 + hdot(xwa, aup_p))
    g = hdot(f_sigmoid(xg), g_up)
    kk = k * k_k
    nrm = jnp.maximum(jnp.sqrt(hdot(kk * kk, e)), 1e-12)
    kk = kk * hdot(1.0 / nrm, et)
    k2 = k * (1.0 + (a - 1.0) * k_a)
    return r, logw, k2, v, -kk, kk * a, g


def f_rwkv_post(y, r, k2, v, g, r_k, lnx_w, lnx_b, e, et):
    inv = 1.0 / HEAD
    mean = hdot(hdot(y, e) * inv, et)
    yc = y - mean
    var = hdot(yc * yc, e) * inv
    yn = yc * hdot(lax.rsqrt(var + GN_EPS), et) * lnx_w + lnx_b
    bonus = hdot(hdot(r * k2 * r_k, e), et) * v
    return (yn + bonus) * g


def f_ssm_ab(lr, li, log_dt):
    dt = jnp.exp(log_dt)
    mag = jnp.exp(lr * dt)
    ab_re, ab_im = mag * jnp.cos(li * dt), mag * jnp.sin(li * dt)
    denom = lr * lr + li * li
    z_re = ((ab_re - 1.0) * lr + ab_im * li) / denom
    z_im = (ab_im * lr - (ab_re - 1.0) * li) / denom
    return ab_re, ab_im, z_re, z_im


def f_ssm_bb(br, bi, z_re, z_im):
    return z_re * br - z_im * bi, z_re * bi + z_im * br


def _col_block(n):
    return _pick(n, (256, 128))


def ts_fwd(proj, col0, width, mu, *, name):
    t = proj.shape[0]
    cb = _col_block(width)
    assert col0 % cb == 0 and width % cb == 0
    off = col0 // cb

    def body(z_ref, mu_ref, o_ref):
        z = z_ref[...]
        row = lax.broadcasted_iota(jnp.int32, z.shape, 0)
        prev = jnp.where(row == 0, 0.0, pltpu.roll(z, 1, 0))
        o_ref[...] = z + (prev - z) * mu_ref[...]

    return pl.pallas_call(
        body, name=name, grid=(width // cb,),
        in_specs=[pl.BlockSpec((t, cb), lambda j: (0, j + off)), pl.BlockSpec((1, cb), lambda j: (0, j))],
        out_specs=pl.BlockSpec((t, cb), lambda j: (0, j)),
        out_shape=jax.ShapeDtypeStruct((t, width), F32), compiler_params=_cparams(1))(proj, mu)


def ts_bwd(proj, col0, dzs, mu, *, name):
    t, width = dzs.shape
    cb = _col_block(width)
    off = col0 // cb

    def body(z_ref, d_ref, mu_ref, dz_ref, dmu_ref):
        z, d, m = z_ref[...], d_ref[...], mu_ref[...]
        row = lax.broadcasted_iota(jnp.int32, z.shape, 0)
        prev = jnp.where(row == 0, 0.0, pltpu.roll(z, 1, 0))
        dm = d * m
        nxt = jnp.where(row == t - 1, 0.0, pltpu.roll(dm, t - 1, 0))
        dz_ref[...] = d - dm + nxt
        dmu_ref[...] = jnp.sum(d * (prev - z), axis=0, keepdims=True)

    return pl.pallas_call(
        body, name=name, grid=(width // cb,),
        in_specs=[pl.BlockSpec((t, cb), lambda j: (0, j + off)), pl.BlockSpec((t, cb), lambda j: (0, j)),
                  pl.BlockSpec((1, cb), lambda j: (0, j))],
        out_specs=[pl.BlockSpec((t, cb), lambda j: (0, j)), pl.BlockSpec((1, cb), lambda j: (0, j))],
        out_shape=[jax.ShapeDtypeStruct((t, width), F32), jax.ShapeDtypeStruct((1, width), F32)],
        compiler_params=_cparams(1))(proj, dzs, mu)


def _att_scores(qn, kb, bias, c):
    s = jnp.einsum('hqd,hkd->hqk', qn, kb, preferred_element_type=F32) + bias
    col = lax.broadcasted_iota(jnp.int32, s.shape, 2)
    s = jnp.where(col >= PAD - c * CHUNK, s, NEG_BIG)
    s = s - jnp.max(s, axis=-1, keepdims=True)
    e = jnp.exp(s)
    return e / jnp.sum(e, axis=-1, keepdims=True)


def att_fwd(qn, knp, vp, bias, *, name):
    h, t, _ = qn.shape
    hb = ATT_HEADS
    nc = t // CHUNK

    def body(q_ref, k_ref, v_ref, b_ref, o_ref):
        c = pl.program_id(1)
        start = pl.multiple_of(c * CHUNK, CHUNK)
        kb = k_ref[:, pl.ds(start, BAND), :]
        vb = v_ref[:, pl.ds(start, BAND), :]
        p = _att_scores(q_ref[...], kb, b_ref[...], c)
        o_ref[...] = jnp.einsum('hqk,hkd->hqd', p.astype(BF16), vb, preferred_element_type=F32).astype(o_ref.dtype)

    return pl.pallas_call(
        body, name=name, grid=(h // hb, nc),
        in_specs=[pl.BlockSpec((hb, CHUNK, HEAD), lambda g, c: (g, c, 0)),
                  pl.BlockSpec((hb, t + PAD, HEAD), lambda g, c: (g, 0, 0)),
                  pl.BlockSpec((hb, t + PAD, HEAD), lambda g, c: (g, 0, 0)),
                  pl.BlockSpec((hb, CHUNK, BAND), lambda g, c: (g, 0, 0))],
        out_specs=pl.BlockSpec((hb, CHUNK, HEAD), lambda g, c: (g, c, 0)),
        out_shape=jax.ShapeDtypeStruct((h, t, HEAD), BF16), compiler_params=_cparams(2))(qn, knp, vp, bias)


def att_bwd(qn, knp, vp, bias, do, *, name):
    h, t, _ = qn.shape
    hb = ATT_HEADS
    nc = t // CHUNK

    def body(q_ref, k_ref, v_ref, b_ref, do_ref, dq_ref, dk_ref, dv_ref, db_ref):
        c = pl.program_id(1)

        @pl.when(c == 0)
        def _():
            dk_ref[...] = jnp.zeros_like(dk_ref)
            dv_ref[...] = jnp.zeros_like(dv_ref)
            db_ref[...] = jnp.zeros_like(db_ref)

        start = pl.multiple_of(c * CHUNK, CHUNK)
        qv = q_ref[...]
        kb = k_ref[:, pl.ds(start, BAND), :]
        vb = v_ref[:, pl.ds(start, BAND), :]
        p = _att_scores(qv, kb, b_ref[...], c)
        dov = do_ref[...]
        dp = jnp.einsum('hqd,hkd->hqk', dov, vb, preferred_element_type=F32)
        ds = p * (dp - jnp.sum(p * dp, axis=-1, keepdims=True))
        db_ref[...] += ds
        dsb = ds.astype(BF16)
        dq_ref[...] = jnp.einsum('hqk,hkd->hqd', dsb, kb, preferred_element_type=F32)
        dst = jnp.swapaxes(dsb, 1, 2)
        pt = jnp.swapaxes(p.astype(BF16), 1, 2)
        dk_ref[:, pl.ds(start, BAND), :] += jnp.einsum('hkq,hqd->hkd', dst, qv, preferred_element_type=F32)
        dv_ref[:, pl.ds(start, BAND), :] += jnp.einsum('hkq,hqd->hkd', pt, dov, preferred_element_type=F32)

    blk_q = pl.BlockSpec((hb, CHUNK, HEAD), lambda g, c: (g, c, 0))
    blk_k = pl.BlockSpec((hb, t + PAD, HEAD), lambda g, c: (g, 0, 0))
    blk_b = pl.BlockSpec((hb, CHUNK, BAND), lambda g, c: (g, 0, 0))
    return pl.pallas_call(
        body, name=name, grid=(h // hb, nc),
        in_specs=[blk_q, blk_k, blk_k, blk_b, blk_q],
        out_specs=[blk_q, blk_k, blk_k, blk_b],
        out_shape=[jax.ShapeDtypeStruct((h, t, HEAD), F32), jax.ShapeDtypeStruct((h, t + PAD, HEAD), F32),
                   jax.ShapeDtypeStruct((h, t + PAD, HEAD), F32), jax.ShapeDtypeStruct((h, CHUNK, BAND), F32)],
        compiler_params=_cparams(2))(qn, knp, vp, bias, do)


def _rel_index():
    i = np.arange(CHUNK)[:, None]
    j = np.arange(BAND)[None, :]
    return np.clip(i + PAD - j, -(CHUNK - 1), REL_CLIP) + (CHUNK - 1)


def _rel_onehot():
    return (jnp.asarray(_rel_index())[:, :, None] == jnp.arange(N_REL)[None, None, :]).astype(F32)


def relbias_expand(rel, *, name):
    h = rel.shape[0]
    onehot_t = jnp.swapaxes(_rel_onehot(), 1, 2)

    def body(r_ref, oh_ref, o_ref):
        o_ref[...] = hdot(r_ref[...], oh_ref[...])

    out = pl.pallas_call(
        body, name=name, grid=(CHUNK,),
        in_specs=[pl.BlockSpec((h, N_REL), lambda i: (0, 0)), pl.BlockSpec((None, N_REL, BAND), lambda i: (i, 0, 0))],
        out_specs=pl.BlockSpec((None, h, BAND), lambda i: (i, 0, 0)),
        out_shape=jax.ShapeDtypeStruct((CHUNK, h, BAND), F32), compiler_params=_cparams(1))(rel, onehot_t)
    return jnp.swapaxes(out, 0, 1)


def relbias_reduce(dbias, *, name):
    h = dbias.shape[0]
    onehot = _rel_onehot()
    dbt = jnp.swapaxes(dbias, 0, 1)

    def body(d_ref, oh_ref, o_ref):
        @pl.when(pl.program_id(0) == 0)
        def _():
            o_ref[...] = jnp.zeros_like(o_ref)
        o_ref[...] += hdot(d_ref[...], oh_ref[...])

    return pl.pallas_call(
        body, name=name, grid=(CHUNK,),
        in_specs=[pl.BlockSpec((None, h, BAND), lambda i: (i, 0, 0)),
                  pl.BlockSpec((None, BAND, N_REL), lambda i: (i, 0, 0))],
        out_specs=pl.BlockSpec((h, N_REL), lambda i: (0, 0)),
        out_shape=jax.ShapeDtypeStruct((h, N_REL), F32), compiler_params=_cparams(1))(dbt, onehot)


def _bt(x):
    return jnp.swapaxes(x, 1, 2)


_BDN = (((2,), (1,)), ((0,), (0,)))


def _bmm_exact(x, y):
    return lax.dot_general(x, y, _BDN, precision=HI, preferred_element_type=F32)


def _split_bf16(x):
    hi = x.astype(BF16)
    return hi, (x - hi.astype(F32)).astype(BF16)


def _bmm_3pass(x, y):
    xh, xl = _split_bf16(x)
    yh, yl = _split_bf16(y)
    dot = lambda p, q: lax.dot_general(p, q, _BDN, preferred_element_type=F32)
    return dot(xh, yh) + (dot(xh, yl) + dot(xl, yh))


def _make_bmm(raw):
    @jax.custom_vjp
    def f(x, y):
        return raw(x, y)

    def fwd(x, y):
        return raw(x, y), (x, y)

    def bwd(saved, dz):
        x, y = saved
        return raw(dz, _bt(y)), raw(_bt(x), dz)

    f.defvjp(fwd, bwd)
    return f


bmm = _make_bmm(_bmm_3pass)
bmm_exact = _make_bmm(_bmm_exact)


def rwkv_chunk(p0, r, lw, k, v, a, b):
    g, c, n = r.shape
    row = lax.broadcasted_iota(jnp.int32, (g, c, c), 1)
    col = lax.broadcasted_iota(jnp.int32, (g, c, c), 2)
    incl, strict = row >= col, row > col
    cs = bmm_exact(incl.astype(F32), lw)
    cs_end = cs[:, c - 1:c, :]
    e_cs = jnp.exp(cs)
    e_neg = jnp.exp(-cs)
    at = a * jnp.exp(cs - lw)
    rt = r * e_cs
    bt_, kt = b * e_neg, k * e_neg
    e_tail = jnp.exp(cs_end - cs)
    bh, kh = b * e_tail, k * e_tail
    ar = jnp.concatenate([at, rt], axis=1)
    x_b, x_k = bmm(ar, _bt(bt_)), bmm(ar, _bt(kt))
    a_ab = jnp.where(strict, x_b[:, :c], 0.0)
    a_rb = jnp.where(incl, x_b[:, c:], 0.0)
    a_ak = jnp.where(strict, x_k[:, :c], 0.0)
    a_rk = jnp.where(incl, x_k[:, c:], 0.0)
    tinv = jnp.where(row == col, 1.0, 0.0) + a_ab
    npow = bmm(a_ab, a_ab)
    for _ in range(int(math.log2(c)) - 1):
        both = bmm(jnp.concatenate([tinv, npow], axis=1), npow)
        tinv = tinv + both[:, :c]
        npow = both[:, c:]
    xp = bmm(ar, p0)
    xv = bmm(jnp.concatenate([a_ak, a_rk], axis=1), v)
    u = bmm(tinv, xp[:, :c] + xv[:, :c])
    y = xp[:, c:] + bmm(a_rb, u) + xv[:, c:]
    rown = lax.broadcasted_iota(jnp.int32, (g, n, n), 1)
    coln = lax.broadcasted_iota(jnp.int32, (g, n, n), 2)
    dg = jnp.where(rown == coln, jnp.exp(cs_end), 0.0)
    p1 = bmm(dg, p0) + bmm(_bt(jnp.concatenate([bh, kh], axis=1)), jnp.concatenate([u, v], axis=1))
    return y, p1


def rwkv_fwd(r, lw, k, v, a, b, *, name):
    h, t, n = r.shape
    g, c = min(RW_HEADS, h), RW_CHUNK
    nch = t // c

    def body(r_ref, lw_ref, k_ref, v_ref, a_ref, b_ref, y_ref, p_ref, st_ref):
        @pl.when(pl.program_id(1) == 0)
        def _():
            st_ref[...] = jnp.zeros_like(st_ref)
        p0 = st_ref[...]
        p_ref[...] = p0
        y, p1 = rwkv_chunk(p0, r_ref[...], lw_ref[...], k_ref[...], v_ref[...], a_ref[...], b_ref[...])
        y_ref[...] = y
        st_ref[...] = p1

    blk = pl.BlockSpec((g, c, n), lambda i, j: (i, j, 0))
    pblk = pl.BlockSpec((g, None, n, n), lambda i, j: (i, j, 0, 0))
    return pl.pallas_call(
        body, name=name, grid=(h // g, nch), in_specs=[blk] * 6, out_specs=[blk, pblk],
        out_shape=[jax.ShapeDtypeStruct((h, t, n), F32), jax.ShapeDtypeStruct((h, nch, n, n), F32)],
        scratch_shapes=[pltpu.VMEM((g, n, n), F32)], compiler_params=_cparams(2))(r, lw, k, v, a, b)


def rwkv_bwd(r, lw, k, v, a, b, p0s, dy, *, name):
    h, t, n = r.shape
    g, c = min(RW_HEADS, h), RW_CHUNK
    nch = t // c

    def body(r_ref, lw_ref, k_ref, v_ref, a_ref, b_ref, p_ref, dy_ref,
             dr_ref, dlw_ref, dk_ref, dv_ref, da_ref, db_ref, dp_ref):
        @pl.when(pl.program_id(1) == 0)
        def _():
            dp_ref[...] = jnp.zeros_like(dp_ref)
        _, pull = jax.vjp(rwkv_chunk, p_ref[...], r_ref[...], lw_ref[...], k_ref[...], v_ref[...],
                          a_ref[...], b_ref[...])
        dp0, dr, dlw, dk, dv, da, db = pull((dy_ref[...], dp_ref[...]))
        dr_ref[...] = dr
        dlw_ref[...] = dlw
        dk_ref[...] = dk
        dv_ref[...] = dv
        da_ref[...] = da
        db_ref[...] = db
        dp_ref[...] = dp0

    blk = pl.BlockSpec((g, c, n), lambda i, j: (i, nch - 1 - j, 0))
    pblk = pl.BlockSpec((g, None, n, n), lambda i, j: (i, nch - 1 - j, 0, 0))
    return pl.pallas_call(
        body, name=name, grid=(h // g, nch), in_specs=[blk] * 6 + [pblk, blk], out_specs=[blk] * 6,
        out_shape=[jax.ShapeDtypeStruct((h, t, n), F32)] * 6,
        scratch_shapes=[pltpu.VMEM((g, n, n), F32)], compiler_params=_cparams(2))(r, lw, k, v, a, b, p0s, dy)


def _time_block(t):
    return _pick(t, (256, 128, 64))


def ssm_scan_fwd(bu_re, bu_im, a_re, a_im, *, name):
    t, rr, ln = bu_re.shape
    tb = _time_block(t)

    def body(br_ref, bi_ref, ar_ref, ai_ref, hr_ref, hi_ref, sr_ref, si_ref):
        @pl.when(pl.program_id(0) == 0)
        def _():
            sr_ref[...] = jnp.zeros_like(sr_ref)
            si_ref[...] = jnp.zeros_like(si_ref)
        ar, ai = ar_ref[...], ai_ref[...]

        def step(i, carry):
            hr, hi = carry
            nr = ar * hr - ai * hi + br_ref[i]
            ni = ar * hi + ai * hr + bi_ref[i]
            hr_ref[i] = nr
            hi_ref[i] = ni
            return nr, ni

        hr, hi = lax.fori_loop(0, tb, step, (sr_ref[...], si_ref[...]))
        sr_ref[...] = hr
        si_ref[...] = hi

    blk = pl.BlockSpec((tb, rr, ln), lambda i: (i, 0, 0))
    cblk = pl.BlockSpec((rr, ln), lambda i: (0, 0))
    return pl.pallas_call(
        body, name=name, grid=(t // tb,), in_specs=[blk, blk, cblk, cblk], out_specs=[blk, blk],
        out_shape=[jax.ShapeDtypeStruct((t, rr, ln), F32)] * 2,
        scratch_shapes=[pltpu.VMEM((rr, ln), F32)] * 2, compiler_params=_cparams(1))(bu_re, bu_im, a_re, a_im)


def ssm_scan_bwd(dh_re, dh_im, hp_re, hp_im, a_re, a_im, *, name):
    t, rr, ln = dh_re.shape
    tb = _time_block(t)
    nb = t // tb

    def body(dr_ref, di_ref, pr_ref, pi_ref, ar_ref, ai_ref, gr_ref, gi_ref, dar_ref, dai_ref, sr_ref, si_ref):
        @pl.when(pl.program_id(0) == 0)
        def _():
            sr_ref[...] = jnp.zeros_like(sr_ref)
            si_ref[...] = jnp.zeros_like(si_ref)
            dar_ref[...] = jnp.zeros_like(dar_ref)
            dai_ref[...] = jnp.zeros_like(dai_ref)
        ar, ai = ar_ref[...], ai_ref[...]

        def step(ii, carry):
            gr, gi, dar, dai = carry
            i = tb - 1 - ii
            nr = dr_ref[i] + ar * gr + ai * gi
            ni = di_ref[i] - ai * gr + ar * gi
            gr_ref[i] = nr
            gi_ref[i] = ni
            pr, pi = pr_ref[i], pi_ref[i]
            dar = dar + nr * pr + ni * pi
            dai = dai - nr * pi + ni * pr
            return nr, ni, dar, dai

        gr, gi, dar, dai = lax.fori_loop(0, tb, step, (sr_ref[...], si_ref[...], dar_ref[...], dai_ref[...]))
        sr_ref[...] = gr
        si_ref[...] = gi
        dar_ref[...] = dar
        dai_ref[...] = dai

    blk = pl.BlockSpec((tb, rr, ln), lambda i: (nb - 1 - i, 0, 0))
    cblk = pl.BlockSpec((rr, ln), lambda i: (0, 0))
    return pl.pallas_call(
        body, name=name, grid=(nb,), in_specs=[blk] * 4 + [cblk, cblk], out_specs=[blk, blk, cblk, cblk],
        out_shape=[jax.ShapeDtypeStruct((t, rr, ln), F32)] * 2 + [jax.ShapeDtypeStruct((rr, ln), F32)] * 2,
        scratch_shapes=[pltpu.VMEM((rr, ln), F32)] * 2,
        compiler_params=_cparams(1))(dh_re, dh_im, hp_re, hp_im, a_re, a_im)


ANY = pl.BlockSpec(memory_space=pl.ANY)


def _rows(ref, lead, ch, nchunk):
    base = ref if lead is None else ref.at[lead]
    if nchunk == 1:
        return base
    n = base.shape[0] // nchunk
    return base.at[pl.ds(ch * n, n)]


class Comm:
    def __init__(self):
        self.ins, self.outs, self.ops, self.aliases = [], [], [], {}
        self.n_remote, self.n_local = 0, 0
        self.ici = False

    def _add(self, kind, src, out_shape, n_peers, nchunk, n_local=1, alias=False):
        lead_len = src.shape[1] if kind in ("scatter", "rs1") else src.shape[0]
        while lead_len % nchunk:
            nchunk //= 2
        self.ops.append((kind, len(self.ins), len(self.outs), self.n_remote, self.n_local, nchunk))
        if alias:
            self.aliases[len(self.ins)] = len(self.outs)
        self.ins.append(src)
        self.outs.append(jax.ShapeDtypeStruct(out_shape, src.dtype))
        self.n_remote += n_peers * nchunk
        self.n_local += n_local
        self.ici = self.ici or kind in ("gather", "scatter", "gather_all", "ag1", "rs2")
        return len(self.outs) - 1

    def ag1(self, w):
        return self._add("ag1", w, (4,) + w.shape, 3, 1, n_local=LOCAL_CHUNKS)

    def ag2(self, g):
        return self._add("ag2", g, g.shape, 3, 1, n_local=0, alias=True)

    def rs1(self, g4):
        s, r, c = g4.shape
        return self._add("rs1", g4, (s, r // 2, c), 4, 1, n_local=0)

    def rs2(self, h4):
        return self._add("rs2", h4, (3,) + h4.shape[1:], 3, 1, n_local=0)

    def rs3(self, s, nchunk=4):
        return self._add("rs3", s, s.shape, 1, nchunk, n_local=0)

    def all_gather(self, w, nchunk=2):
        return self._add("gather", w, (4,) + w.shape, 3, nchunk)

    def scatter(self, g4, nchunk=2):
        return self._add("scatter", g4, g4.shape, 3, nchunk)

    def swap(self, s, nchunk=8):
        return self._add("swap", s, (2,) + s.shape, 1, nchunk)

    def gather_all(self, v, nchunk=1):
        return self._add("gather_all", v, (8,) + v.shape, 7, nchunk)

    def sem_shapes(self):
        return [pltpu.SemaphoreType.DMA((self.n_remote,)), pltpu.SemaphoreType.DMA((self.n_remote,)),
                pltpu.SemaphoreType.DMA((max(self.n_local, 1),))]

    def _two_level(self, kind, src, dst, r0, l0, nchunk, x, y, c, send, recv, lsem, sends, recvs, locs):
        chips = [(1 - x, y), (x, 1 - y), (1 - x, 1 - y)]
        me, sib = 2 * x + y, (x, y, 1 - c)

        def half(ref, hc):
            n = ref.shape[0] // 2
            return ref.at[pl.ds(hc * n, n)]

        def both(k, dev, s_ref, d_send, d_recv):
            mk = functools.partial(pltpu.make_async_remote_copy, src_ref=s_ref, send_sem=send.at[k],
                                   recv_sem=recv.at[k], device_id=dev, device_id_type=MESH)
            sends.append(mk(dst_ref=d_send))
            recvs.append(mk(dst_ref=d_recv))

        if kind == "ag1":
            n = src.shape[0] // LOCAL_CHUNKS
            for j in range(LOCAL_CHUNKS):
                rows = pl.ds(j * n, n)
                locs.append(pltpu.make_async_copy(src.at[rows], dst.at[me].at[rows], lsem.at[l0 + j]))
            for pj, (px, py) in enumerate(chips):
                both(r0 + pj, (px, py, c), half(src, c), half(dst.at[me], c), half(dst.at[2 * px + py], c))
        elif kind == "ag2":
            for pj, (px, py) in enumerate(chips):
                got = dst.at[2 * px + py]
                both(r0 + pj, sib, half(got, c), half(got, c), half(got, 1 - c))
        elif kind == "rs1":
            for q in range(4):
                both(r0 + q, sib, half(src.at[q], 1 - c), dst.at[q], dst.at[q])
        elif kind == "rs2":
            for pj, (px, py) in enumerate(chips):
                both(r0 + pj, (px, py, c), src.at[2 * px + py], dst.at[pj], dst.at[pj])
        else:
            for ch in range(nchunk):
                both(r0 + ch, sib, _rows(src, None, ch, nchunk), _rows(dst, None, ch, nchunk),
                     _rows(dst, None, ch, nchunk))

    def _descs(self, c_in, c_out, send, recv, lsem):
        x, y, c = lax.axis_index("x"), lax.axis_index("y"), lax.axis_index("c")
        sends, recvs, locs = [], [], []
        for kind, ii, oi, r0, l0, nchunk in self.ops:
            src, dst = c_in[ii], c_out[oi]
            if kind in ("ag1", "ag2", "rs1", "rs2", "rs3"):
                self._two_level(kind, src, dst, r0, l0, nchunk, x, y, c, send, recv, lsem, sends, recvs, locs)
                continue
            if kind == "swap":
                me = c
                peers = [((x, y, 1 - c), 1 - c)]
            elif kind == "gather_all":
                me = 4 * x + 2 * y + c
                flips = [(dx, dy, dc) for dx in (0, 1) for dy in (0, 1) for dc in (0, 1) if dx + dy + dc]
                peers = []
                for dx, dy, dc in flips:
                    px, py, pc = (x + dx) % 2, (y + dy) % 2, (c + dc) % 2
                    peers.append(((px, py, pc), 4 * px + 2 * py + pc))
            else:
                me = 2 * x + y
                peers = [((px, py, c), 2 * px + py) for px, py in ((1 - x, y), (x, 1 - y), (1 - x, 1 - y))]
            if kind == "scatter":
                locs.append(pltpu.make_async_copy(src.at[me], dst.at[me], lsem.at[l0]))
            else:
                locs.append(pltpu.make_async_copy(src, dst.at[me], lsem.at[l0]))
            for pj, (dev, peer_slot) in enumerate(peers):
                for ch in range(nchunk):
                    k = r0 + pj * nchunk + ch
                    s_src = _rows(src, peer_slot if kind == "scatter" else None, ch, nchunk)
                    mk = functools.partial(pltpu.make_async_remote_copy, send_sem=send.at[k], recv_sem=recv.at[k],
                                           device_id=dev, device_id_type=MESH)
                    sends.append(mk(src_ref=s_src, dst_ref=_rows(dst, me, ch, nchunk)))
                    recvs.append(mk(src_ref=s_src, dst_ref=_rows(dst, peer_slot, ch, nchunk)))
        return sends, recvs, locs

    def start(self, c_in, c_out, send, recv, lsem):
        sends, _, locs = self._descs(c_in, c_out, send, recv, lsem)
        for d in locs + sends:
            d.start()

    def wait(self, c_in, c_out, send, recv, lsem):
        sends, recvs, locs = self._descs(c_in, c_out, send, recv, lsem)
        for d in recvs:
            d.wait_recv()
        for d in sends:
            d.wait_send()
        for d in locs:
            d.wait()


def run_comm(comm, *, name):
    n_cin, n_cout = len(comm.ins), len(comm.outs)

    def body(*refs):
        c_in, c_out, sems = refs[:n_cin], refs[n_cin:n_cin + n_cout], refs[n_cin + n_cout:]
        comm.start(c_in, c_out, *sems)
        comm.wait(c_in, c_out, *sems)

    outs = pl.pallas_call(
        body, name=name, in_specs=[ANY] * n_cin, out_specs=[ANY] * n_cout, out_shape=list(comm.outs),
        scratch_shapes=comm.sem_shapes(), input_output_aliases=dict(comm.aliases),
        compiler_params=pltpu.CompilerParams(has_side_effects=True))(*comm.ins)
    return list(outs)


def _core_index():
    return jnp.reshape(lax.axis_index("c"), (1,)).astype(jnp.int32)


def _chip_index():
    return jnp.reshape(2 * lax.axis_index("x") + lax.axis_index("y"), (1,)).astype(jnp.int32)


def _row_block(rows, bytes_per_row, cap=512):
    tb = 16
    while tb * 2 <= min(rows, cap) and rows % (tb * 2) == 0 and tb * 2 * bytes_per_row <= ROW_BLOCK_BYTES:
        tb *= 2
    return tb


def add_half(g4, got, *, name):
    s, r, c = g4.shape
    r2 = r // 2
    tb = _row_block(r2, c * 8)
    nb = r2 // tb

    def body(c_ref, g_ref, x_ref, o_ref):
        o_ref[...] = (g_ref[...].astype(F32) + x_ref[...].astype(F32)).astype(o_ref.dtype)

    grid_spec = pltpu.PrefetchScalarGridSpec(
        num_scalar_prefetch=1, grid=(s, nb),
        in_specs=[pl.BlockSpec((None, tb, c), lambda q, i, cr: (q, cr[0] * nb + i, 0)),
                  pl.BlockSpec((None, tb, c), lambda q, i, cr: (q, i, 0))],
        out_specs=pl.BlockSpec((None, tb, c), lambda q, i, cr: (q, i, 0)))
    return pl.pallas_call(body, name=name, grid_spec=grid_spec, out_shape=jax.ShapeDtypeStruct((s, r2, c), g4.dtype),
                          compiler_params=_cparams(2))(_core_index(), g4, got)


def sum_chips(h4, got3, *, name):
    _, r2, c = h4.shape
    tb = _row_block(r2, c * 12)
    nb = r2 // tb

    def body(q_ref, h_ref, y_ref, o_ref):
        o_ref[...] = ((h_ref[...].astype(F32) + y_ref[0].astype(F32)) + y_ref[1].astype(F32)) + y_ref[2].astype(F32)

    grid_spec = pltpu.PrefetchScalarGridSpec(
        num_scalar_prefetch=1, grid=(nb,),
        in_specs=[pl.BlockSpec((None, tb, c), lambda i, qr: (qr[0], i, 0)),
                  pl.BlockSpec((3, tb, c), lambda i, qr: (0, i, 0))],
        out_specs=pl.BlockSpec((tb, c), lambda i, qr: (i, 0)))
    return pl.pallas_call(body, name=name, grid_spec=grid_spec, out_shape=jax.ShapeDtypeStruct((r2, c), F32),
                          compiler_params=_cparams(1))(_chip_index(), h4, got3)


def adamw_big(w, mine, theirs, m, v, *, comm=None, name):
    nl, r, c = w.shape
    r2 = r // 2
    tb = _row_block(r2, c * 4 * 10, cap=256)
    nb2 = r2 // tb
    nb = 2 * nb2
    n_cin = len(comm.ins) if comm else 0
    n_cout = len(comm.outs) if comm else 0

    def body(c_ref, w_ref, m_ref, v_ref, *rest):
        g_refs, rest = rest[:2 * nl], rest[2 * nl:]
        c_in, (go_ref, d_ref, mo_ref, vo_ref), rest = rest[:n_cin], rest[n_cin:n_cin + 4], rest[n_cin + 4:]
        c_out, sems = rest[:n_cout], rest[n_cout:]
        layer, i = pl.program_id(0), pl.program_id(1)
        if comm:
            @pl.when((layer == 0) & (i == 0))
            def _():
                comm.start(c_in, c_out, *sems)

            @pl.when((layer == nl - 1) & (i == nb - 1))
            def _():
                comm.wait(c_in, c_out, *sems)
        own = (i // nb2) == c_ref[0]
        for l0 in range(nl):
            @pl.when(layer == l0)
            def _(l0=l0):
                g = jnp.where(own, g_refs[2 * l0][...], g_refs[2 * l0 + 1][...])
                d, mn, vn = _adam_math(w_ref[...], g, m_ref[...], v_ref[...])
                go_ref[...] = g
                d_ref[...] = d
                mo_ref[...] = mn
                vo_ref[...] = vn

    blk = pl.BlockSpec((None, tb, c), lambda l, i, cr: (l, i, 0))

    def half_spec(l0):
        return pl.BlockSpec((tb, c), lambda l, i, cr: (jnp.where(l == l0, i % nb2, jnp.where(l < l0, 0, nb2 - 1)), 0))

    in_specs = [blk, blk, blk]
    args = [w, m, v]
    for l0 in range(nl):
        in_specs += [half_spec(l0), half_spec(l0)]
        args += [mine[l0], theirs[l0]]
    if comm:
        in_specs += [ANY] * n_cin
        args += list(comm.ins)
    grid_spec = pltpu.PrefetchScalarGridSpec(
        num_scalar_prefetch=1, grid=(nl, nb), in_specs=in_specs, out_specs=[blk] * 4 + [ANY] * n_cout,
        scratch_shapes=comm.sem_shapes() if comm else [])
    aliases = {1 + 3 + 2 * nl + ii: 4 + oi for ii, oi in comm.aliases.items()} if comm else {}
    outs = pl.pallas_call(
        body, name=name, grid_spec=grid_spec,
        out_shape=[jax.ShapeDtypeStruct(w.shape, F32)] * 4 + (list(comm.outs) if comm else []),
        input_output_aliases=aliases, compiler_params=_cparams(2))(_core_index(), *args)
    if comm:
        return list(outs[:4]), list(outs[4:])
    return list(outs)


def sum_slots(x, *, name):
    s, r, c = x.shape
    tb = 8
    while tb * 2 <= min(r, 512) and r % (tb * 2) == 0 and tb * 2 * c * 4 * (s + 1) <= ROW_BLOCK_BYTES:
        tb *= 2

    def body(x_ref, o_ref):
        acc = x_ref[0].astype(F32)
        for i in range(1, s):
            acc = acc + x_ref[i].astype(F32)
        o_ref[...] = acc

    return pl.pallas_call(
        body, name=name, grid=(r // tb,), in_specs=[pl.BlockSpec((s, tb, c), lambda i: (0, i, 0))],
        out_specs=pl.BlockSpec((tb, c), lambda i: (i, 0)),
        out_shape=jax.ShapeDtypeStruct((r, c), F32), compiler_params=_cparams(1))(x)


def _adam_math(w, g, m, v):
    m = ADAM_B1 * m + (1.0 - ADAM_B1) * g
    v = ADAM_B2 * v + (1.0 - ADAM_B2) * (g * g)
    m_hat = m / (1.0 - ADAM_B1 ** ADAM_STEP)
    v_hat = v / (1.0 - ADAM_B2 ** ADAM_STEP)
    delta = -ADAM_LR * (m_hat / (jnp.sqrt(v_hat) + ADAM_EPS) + ADAM_WD * w)
    return delta, m, v


def adamw_pair(w, g2, m, v, *, name):
    r, c = w.shape
    tb = 8
    while tb * 2 <= min(r, 512) and r % (tb * 2) == 0 and tb * 2 * c * 4 * 9 <= 2 * ROW_BLOCK_BYTES:
        tb *= 2

    def body(w_ref, g_ref, m_ref, v_ref, go_ref, d_ref, mo_ref, vo_ref):
        g = g_ref[0] + g_ref[1]
        d, mn, vn = _adam_math(w_ref[...], g, m_ref[...], v_ref[...])
        go_ref[...] = g
        d_ref[...] = d
        mo_ref[...] = mn
        vo_ref[...] = vn

    blk = pl.BlockSpec((tb, c), lambda i: (i, 0))
    return pl.pallas_call(
        body, name=name, grid=(r // tb,), in_specs=[blk, pl.BlockSpec((2, tb, c), lambda i: (0, i, 0)), blk, blk],
        out_specs=[blk] * 4, out_shape=[jax.ShapeDtypeStruct((r, c), F32)] * 4,
        compiler_params=_cparams(1))(w, g2, m, v)


def adamw_flat(w, g, m, v, *, name):
    def fn(w_, g_, m_, v_):
        return _adam_math(w_, g_, m_, v_)
    return rowk(fn, [w, g, m, v], [], [(w.shape[1], F32)] * 3, name=name)


def to_heads(x):
    t, d = x.shape
    return x.reshape(t, d // HEAD, HEAD).transpose(1, 0, 2)


def from_heads(x):
    h, t, n = x.shape
    return x.transpose(1, 0, 2).reshape(t, h * n)


def pack_flat(arrs, lanes=128, row_mult=512):
    flat = jnp.concatenate([a.reshape(-1).astype(F32) for a in arrs])
    n = flat.shape[0]
    rows = -(-n // lanes)
    rows = -(-rows // row_mult) * row_mult
    return jnp.pad(flat, (0, rows * lanes - n)).reshape(rows, lanes)


def unpack_flat(buf, shapes):
    flat = buf.reshape(-1)
    outs, off = [], 0
    for s in shapes:
        n = int(np.prod(s))
        outs.append(flat[off:off + n].reshape(s))
        off += n
    return outs


def block_diag_from(w_gab):
    g, a, b = w_gab.shape
    eye = jnp.eye(g, dtype=w_gab.dtype)
    return (w_gab[:, :, None, :] * eye[:, None, :, None]).reshape(g * a, g * b)


def block_diag_extract(m, g):
    a, b = m.shape[0] // g, m.shape[1] // g
    eye = jnp.eye(g, dtype=m.dtype)
    return jnp.sum(m.reshape(g, a, g, b) * eye[:, None, :, None], axis=2)


def kernel(x, p, ffn1_norm, ffn1_w_gate, ffn1_w_up, ffn1_w_down, mix_norm, ffn2_norm, ffn2_w_gate, ffn2_w_up, ffn2_w_down, ple_norm, ple_w_gate, ple_w_proj, ab_w_in, att_q_gain, att_k_gain, att_rel_bias, rwkv_mu, rwkv_w0, rwkv_w_up, rwkv_a0, rwkv_a_up, rwkv_g_up, rwkv_k_k, rwkv_k_a, rwkv_r_k, rwkv_lnx_w, rwkv_lnx_b, ab_w_out, ssm_w_in, ssm_lambda_re, ssm_lambda_im, ssm_log_dt, ssm_b_re, ssm_b_im, ssm_c_re, ssm_c_im, ssm_d, ssm_w_out, loss_target, m_ffn1_norm, m_ffn1_w_gate, m_ffn1_w_up, m_ffn1_w_down, m_mix_norm, m_ffn2_norm, m_ffn2_w_gate, m_ffn2_w_up, m_ffn2_w_down, m_ple_norm, m_ple_w_gate, m_ple_w_proj, m_ab_w_in, m_att_q_gain, m_att_k_gain, m_att_rel_bias, m_rwkv_mu, m_rwkv_w0, m_rwkv_w_up, m_rwkv_a0, m_rwkv_a_up, m_rwkv_g_up, m_rwkv_k_k, m_rwkv_k_a, m_rwkv_r_k, m_rwkv_lnx_w, m_rwkv_lnx_b, m_ab_w_out, m_ssm_w_in, m_ssm_lambda_re, m_ssm_lambda_im, m_ssm_log_dt, m_ssm_b_re, m_ssm_b_im, m_ssm_c_re, m_ssm_c_im, m_ssm_d, m_ssm_w_out, v_ffn1_norm, v_ffn1_w_gate, v_ffn1_w_up, v_ffn1_w_down, v_mix_norm, v_ffn2_norm, v_ffn2_w_gate, v_ffn2_w_up, v_ffn2_w_down, v_ple_norm, v_ple_w_gate, v_ple_w_proj, v_ab_w_in, v_att_q_gain, v_att_k_gain, v_att_rel_bias, v_rwkv_mu, v_rwkv_w0, v_rwkv_w_up, v_rwkv_a0, v_rwkv_a_up, v_rwkv_g_up, v_rwkv_k_k, v_rwkv_k_a, v_rwkv_r_k, v_rwkv_lnx_w, v_rwkv_lnx_b, v_ab_w_out, v_ssm_w_in, v_ssm_lambda_re, v_ssm_lambda_im, v_ssm_log_dt, v_ssm_b_re, v_ssm_b_im, v_ssm_c_re, v_ssm_c_im, v_ssm_d, v_ssm_w_out):
    A = dict(locals())
    W = {n: A[n] for n in W_NAMES}
    return _step(A['x'], A['p'], A['loss_target'], W, {n: A['m_' + n] for n in W_NAMES},
                 {n: A['v_' + n] for n in W_NAMES})


def _step(x, p, target, W, M, V):
    assert x.shape[0] == 1
    t, d = x.shape[1], x.shape[2]
    depth = p.shape[0]
    h0 = x[0]
    tgt = target[0]
    qchip = 2 * lax.axis_index("x") + lax.axis_index("y")
    d_rw = W['rwkv_w0'].shape[1]
    d_att = W['ab_w_out'].shape[1] * 4 - d_rw
    n_h_att, n_h_rw = d_att // HEAD, d_rw // HEAD
    n_bin = 3 * d_rw + DECAY_LORA + AAA_LORA + GATE_LORA
    d_ssm = W['ssm_w_in'].shape[2]
    n_grp = d_ssm // SSM_GROUP
    gp = n_grp * SSM_STATE

    queue = []
    gathered = {}
    halves = {}
    grads = {}
    queued_grads = set()
    n_alone = [0]

    def ag_entry(name, layer):
        def add1(cm):
            return cm.ag1(W[name][layer].astype(BF16))

        def cont1(outs, hd):
            got = outs[hd]

            def cont2(outs2, hd2):
                gathered[(name, layer)] = outs2[hd2]
            queue.insert(0, (False, lambda cm: cm.ag2(got), cont2))
        return True, add1, cont1

    def rs_entry(name, layer, g4):
        shard_shape = W[name].shape[1:]
        rows, cols = int(np.prod(shard_shape[:-1])), shard_shape[-1]
        g4 = g4.reshape(4, rows, cols)

        def cont1(outs, hd):
            h4 = add_half(g4, outs[hd], name=f"rs_add_{name}")

            def cont2(outs2, hd2):
                mine = sum_chips(h4, outs2[hd2], name=f"rs_sum_{name}")

                def cont3(outs3, hd3):
                    halves[(name, layer)] = (mine, outs3[hd3])
                queue.insert(0, (False, lambda cm: cm.rs3(mine), cont3))
            queue.insert(0, (True, lambda cm: cm.rs2(h4), cont2))
        return False, (lambda cm: cm.rs1(g4)), cont1

    def enqueue_ready():
        for n in BIG:
            for li_, g4 in enumerate(grads.get(n, [])):
                if g4 is not None and (n, li_) not in queued_grads:
                    queued_grads.add((n, li_))
                    queue.append(rs_entry(n, li_, g4))

    def take(max_ici=1):
        cm, conts, n_ici = Comm(), [], 0
        while queue and (n_ici < max_ici or not queue[0][0]):
            is_ici, add, cont = queue.pop(0)
            conts.append((cont, add(cm)))
            n_ici += int(is_ici)
        return cm, conts

    def hmm(*args, **kw):
        enqueue_ready()
        if not queue:
            return mm(*args, **kw)
        cm, conts = take()
        out, couts = mm(*args, comm=cm, **kw)
        for cont, hd in reversed(conts):
            cont(couts, hd)
        return out

    def alone(max_ici=1):
        cm, conts = take(max_ici)
        assert conts
        n_alone[0] += 1
        couts = run_comm(cm, name=f"comm_alone{n_alone[0]}")
        for cont, hd in reversed(conts):
            cont(couts, hd)

    def flush(until=None, at_least_one=False):
        enqueue_ready()
        while queue and (at_least_one or not (until is not None and until())):
            at_least_one = False
            alone()
            enqueue_ready()

    def gather(name, layer):
        flush(until=lambda: (name, layer) in gathered)
        return gathered[(name, layer)]
    small = {}

    def add_small(name, val, layer=None, nl=1):
        if layer is None:
            small[name] = val
        else:
            small.setdefault(name, [None] * nl)[layer] = val

    def ffn_fwd(h, pre, i):
        g = W[pre + '_norm'][i][None]
        wg, wu, wd = gather(pre + '_w_gate', i), gather(pre + '_w_up', i), gather(pre + '_w_down', i)
        wd2 = wd.reshape(-1, d)
        n = rowk(lambda hh, gg: f_rms(hh, gg), [h], [g], [(d, BF16)], name=f"{pre}_rms")[0]
        a = hmm(n, wg, out_dtype=BF16, name=f"{pre}_gate")
        b = hmm(n, wu, out_dtype=BF16, name=f"{pre}_up")
        f = a.shape[1]
        u = rowk(lambda aa, bb: f_swiglu(*_f32(aa, bb)), [a, b], [], [(f, BF16)], name=f"{pre}_swiglu")[0]
        h_out = hmm(u, wd2, res=h, alpha=0.5, name=f"{pre}_down")
        return h_out, dict(h=h, g=g, n=n, a=a, b=b, u=u, wg=wg, wu=wu, wd2=wd2)

    def ffn_bwd(dh, sv, pre, i):
        f = sv['a'].shape[1]
        dwd = hmm(sv['u'], dh, ta=True, alpha=0.5, out_dtype=GRAD_XFER, name=f"{pre}_d_wdown")
        du = hmm(dh, sv['wd2'], tb=True, alpha=0.5, out_dtype=BF16, name=f"{pre}_d_u")
        da, db = rowk(vjp_rows(f_swiglu, 2, 1), [sv['a'], sv['b'], du], [], [(f, BF16), (f, BF16)],
                      name=f"{pre}_d_swiglu")
        dwg = hmm(sv['n'], da, ta=True, nshard=4, out3=True, out_dtype=GRAD_XFER, name=f"{pre}_d_wgate")
        dwu = hmm(sv['n'], db, ta=True, nshard=4, out3=True, out_dtype=GRAD_XFER, name=f"{pre}_d_wup")
        dn = hmm(da, sv['wg'], tb=True, name=f"{pre}_d_n1")
        dn = hmm(db, sv['wu'], tb=True, res=dn, name=f"{pre}_d_n2")
        dh_in, dg = rms_bwd(sv['h'], dn, dh, sv['g'], name=f"{pre}_d_rms")
        grads.setdefault(pre + '_w_gate', [None] * depth)[i] = dwg
        grads.setdefault(pre + '_w_up', [None] * depth)[i] = dwu
        grads.setdefault(pre + '_w_down', [None] * depth)[i] = dwd.reshape(4, -1, d)
        add_small(pre + '_norm', dg[0], i, depth)
        return dh_in

    def rms_bwd(h, dn, dh_res, g, *, name):
        def fn(hh, dnn, dres, gg):
            _, pull = jax.vjp(f_rms, hh, gg)
            dh_, dg_ = pull(dnn)
            return dh_ + dres, dg_
        return rowk(fn, [h, dn, dh_res], [g], [(d, F32)], [(1, d)], name=name)

    def head_consts(nh):
        e = np.kron(np.eye(nh, dtype=np.float32), np.ones((HEAD, 1), np.float32))
        return jnp.asarray(e), jnp.asarray(e.T)

    def mixer_ab_fwd(h):
        g = W['mix_norm'][0][None]
        win = gather('ab_w_in', 0).transpose(1, 0, 2).reshape(d, -1)
        wout = gather('ab_w_out', 0).reshape(-1, d)
        hn = rowk(lambda hh, gg: f_rms(hh, gg), [h], [g], [(d, BF16)], name="mixab_rms")[0]
        proj = hmm(hn, win, name="mixab_proj")
        q2, k2, v2 = [to_heads(proj[:, j * d_att:(j + 1) * d_att]).reshape(n_h_att * t, HEAD) for j in range(3)]
        qg, kg = W['att_q_gain'], W['att_k_gain']
        f_qn = lambda qq, gg: f_rms(qq, gg) * (HEAD ** -0.5)
        qn = rowk(f_qn, [q2], [qg], [(HEAD, BF16)], name="att_qnorm")[0].reshape(n_h_att, t, HEAD)
        kn = rowk(f_rms, [k2], [kg], [(HEAD, BF16)], name="att_knorm")[0].reshape(n_h_att, t, HEAD)
        knp = jnp.pad(kn, ((0, 0), (PAD, 0), (0, 0)))
        vp = jnp.pad(v2.astype(BF16).reshape(n_h_att, t, HEAD), ((0, 0), (PAD, 0), (0, 0)))
        bias = relbias_expand(W['att_rel_bias'][0], name="att_relbias")
        o = att_fwd(qn, knp, vp, bias, name="att_fwd")
        att = from_heads(o)
        mu = W['rwkv_mu']
        zs = ts_fwd(proj, 3 * d_att, n_bin, mu, name="rwkv_shift")
        e, et = head_consts(n_h_rw)
        zpad = jnp.zeros((AAA_LORA, d_rw), F32)
        wup_p = jnp.concatenate([W['rwkv_w_up_full'], zpad], 0)
        aup_p = jnp.concatenate([zpad, W['rwkv_a_up_full']], 0)
        pre_c = [W['rwkv_w0'], wup_p, W['rwkv_a0'], aup_p, W['rwkv_g_up_full'], W['rwkv_k_k'], W['rwkv_k_a'], e, et]
        pre = rowk(f_rwkv_pre, [zs], pre_c, [(d_rw, F32)] * 7, name="rwkv_pre")
        r_, lw_, kk_, vv_, ia_, ib_, gg_ = pre
        hm = [to_heads(u_) for u_ in (r_, lw_, kk_, vv_, ia_, ib_)]
        y_h, p0s = rwkv_fwd(*hm, name="rwkv_scan")
        y = from_heads(y_h)
        post_c = [W['rwkv_r_k'].reshape(1, d_rw), W['rwkv_lnx_w'], W['rwkv_lnx_b'], e, et]
        rw = rowk(f_rwkv_post, [y, r_, kk_, vv_, gg_], post_c, [(d_rw, BF16)], name="rwkv_post")[0]
        cat = jnp.concatenate([att, rw], axis=1)
        h_out = hmm(cat, wout, res=h, name="mixab_out")
        sv = dict(h=h, g=g, hn=hn, win=win, wout=wout, proj=proj, q2=q2, k2=k2, qn=qn, knp=knp, vp=vp, bias=bias,
                  zs=zs, pre_c=pre_c, pre=pre, hm=hm, p0s=p0s, y=y, post_c=post_c, cat=cat, qg=qg, kg=kg, mu=mu)
        return h_out, sv

    def mixer_ab_bwd(dh, sv):
        dwout = hmm(sv['cat'], dh, ta=True, out_dtype=GRAD_XFER, name="mixab_d_wout")
        grads['ab_w_out'] = [dwout.reshape(4, -1, d)]
        dcat = hmm(dh, sv['wout'], tb=True, name="mixab_d_cat")
        datt, drw = dcat[:, :d_att], dcat[:, d_att:]
        r_, lw_, kk_, vv_, ia_, ib_, gg_ = sv['pre']
        post = rowk(vjp_rows(f_rwkv_post, 5, 1), [sv['y'], r_, kk_, vv_, gg_, drw], sv['post_c'],
                    [(d_rw, F32)] * 5, [(1, d_rw)] * 3 + [sv['post_c'][3].shape, sv['post_c'][4].shape],
                    name="rwkv_d_post")
        dy, dr1, dk1, dv1, dg1 = post[:5]
        add_small('rwkv_r_k', post[5].reshape(W['rwkv_r_k'].shape))
        add_small('rwkv_lnx_w', post[6])
        add_small('rwkv_lnx_b', post[7])
        dscan = rwkv_bwd(*sv['hm'], sv['p0s'], to_heads(dy), name="rwkv_d_scan")
        dr2, dlw, dk2, dv2, dia, dib = [from_heads(u_) for u_ in dscan]

        def pre_bwd(zs, dra, drb, dlw_, dka, dkb, dva, dvb, dia_, dib_, dg_, *consts):
            _, pull = jax.vjp(f_rwkv_pre, zs, *consts)
            return pull((dra + drb, dlw_, dka + dkb, dva + dvb, dia_, dib_, dg_))

        pc = sv['pre_c']
        preb = rowk(pre_bwd, [sv['zs'], dr1, dr2, dlw, dk1, dk2, dv1, dv2, dia, dib, dg1], pc,
                    [(n_bin, F32)], [c.shape for c in pc], name="rwkv_d_pre")
        dzs = preb[0]
        add_small('rwkv_w0', preb[1])
        add_small('rwkv_w_up', preb[2][:DECAY_LORA])
        add_small('rwkv_a0', preb[3])
        add_small('rwkv_a_up', preb[4][DECAY_LORA:])
        add_small('rwkv_g_up', preb[5])
        add_small('rwkv_k_k', preb[6])
        add_small('rwkv_k_a', preb[7])
        dz, dmu = ts_bwd(sv['proj'], 3 * d_att, dzs, sv['mu'], name="rwkv_d_shift")
        add_small('rwkv_mu', dmu)
        do = to_heads(datt).astype(BF16)
        dqn, dknp, dvp, dbias = att_bwd(sv['qn'], sv['knp'], sv['vp'], sv['bias'], do, name="att_bwd")
        add_small('att_rel_bias', relbias_reduce(dbias, name="att_d_relbias")[None])
        f_qn = lambda qq, gg: f_rms(qq, gg) * (HEAD ** -0.5)
        dq2, dqg = rowk(vjp_rows(f_qn, 1, 1), [sv['q2'], dqn.reshape(-1, HEAD)], [sv['qg']], [(HEAD, F32)],
                        [(1, HEAD)], name="att_d_qnorm")
        dk2_, dkg = rowk(vjp_rows(f_rms, 1, 1), [sv['k2'], dknp[:, PAD:].reshape(-1, HEAD)], [sv['kg']],
                         [(HEAD, F32)], [(1, HEAD)], name="att_d_knorm")
        add_small('att_q_gain', dqg)
        add_small('att_k_gain', dkg)
        dproj = jnp.concatenate([from_heads(dq2.reshape(n_h_att, t, HEAD)), from_heads(dk2_.reshape(n_h_att, t, HEAD)),
                                 from_heads(dvp[:, PAD:]), dz], axis=1)
        dproj = dproj.astype(BF16)
        dwin = hmm(sv['hn'], dproj, ta=True, out_dtype=GRAD_XFER, name="mixab_d_win")
        grads['ab_w_in'] = [dwin.reshape(d, 4, -1).transpose(1, 0, 2)]
        dhn = hmm(dproj, sv['win'], tb=True, name="mixab_d_hn")
        dh_in, dg = rms_bwd(sv['h'], dhn, dh, sv['g'], name="mixab_d_rms")
        add_small('mix_norm', dg[0], 0, depth)
        return dh_in

    def ssm_params():
        lr, li = W['ssm_lambda_re'][0], W['ssm_lambda_im'][0]
        ldt = W['ssm_log_dt'][0][:, None]
        ab = rowk(f_ssm_ab, [lr, li, ldt], [], [(SSM_STATE, F32)] * 4, name="ssm_ab", tb=n_grp)
        br = W['ssm_b_re'][0].reshape(gp, SSM_GROUP)
        bi = W['ssm_b_im'][0].reshape(gp, SSM_GROUP)
        z_re, z_im = ab[2].reshape(gp, 1), ab[3].reshape(gp, 1)
        bb = rowk(f_ssm_bb, [br, bi, z_re, z_im], [], [(SSM_GROUP, F32)] * 2, name="ssm_bb", tb=gp)
        return dict(lr=lr, li=li, ldt=ldt, ab=ab, br=br, bi=bi, z_re=z_re, z_im=z_im, bb=bb)

    def mixer_s5_fwd(h):
        g = W['mix_norm'][1][None]
        win, wout = gather('ssm_w_in', 0).reshape(d, d_ssm), gather('ssm_w_out', 0)
        hn = rowk(lambda hh, gg: f_rms(hh, gg), [h], [g], [(d, BF16)], name="s5_rms")[0]
        u = hmm(hn, win, name="s5_in")
        sp = ssm_params()
        bbd_re = block_diag_from(sp['bb'][0].reshape(n_grp, SSM_STATE, SSM_GROUP).transpose(0, 2, 1)).astype(BF16)
        bbd_im = block_diag_from(sp['bb'][1].reshape(n_grp, SSM_STATE, SSM_GROUP).transpose(0, 2, 1)).astype(BF16)
        cbd_re = block_diag_from(W['ssm_c_re'][0].transpose(0, 2, 1)).astype(BF16)
        cbd_im = block_diag_from(W['ssm_c_im'][0].transpose(0, 2, 1)).astype(BF16)
        ub = u.astype(BF16)
        bu_re = hmm(ub, bbd_re, name="s5_bu_re").reshape(t, gp // 128, 128)
        bu_im = hmm(ub, bbd_im, name="s5_bu_im").reshape(t, gp // 128, 128)
        a_re, a_im = sp['ab'][0].reshape(gp // 128, 128), sp['ab'][1].reshape(gp // 128, 128)
        h_re, h_im = ssm_scan_fwd(bu_re, bu_im, a_re, a_im, name="s5_scan")
        hb_re, hb_im = h_re.reshape(t, gp).astype(BF16), h_im.reshape(t, gp).astype(BF16)
        y = hmm(hb_re, cbd_re, name="s5_y_re")
        y = hmm(hb_im, cbd_im, res=y, alpha=-1.0, name="s5_y_im")
        dsk = W['ssm_d_full']
        f_act = lambda yy, uu, dd: f_gelu(yy + dd * uu)
        yg = rowk(f_act, [y, u], [dsk], [(d_ssm, BF16)], name="s5_gelu")[0]
        z = hmm(yg, wout, name="s5_out")
        f_glu = lambda zz, hh: hh + zz[:, :d] * f_sigmoid(zz[:, d:])
        h_out = rowk(f_glu, [z, h], [], [(d, F32)], name="s5_glu")[0]
        sv = dict(h=h, g=g, hn=hn, win=win, wout=wout, u=u, ub=ub, sp=sp, bbd_re=bbd_re, bbd_im=bbd_im,
                  cbd_re=cbd_re, cbd_im=cbd_im, a_re=a_re, a_im=a_im, h_re=h_re, h_im=h_im, hb_re=hb_re,
                  hb_im=hb_im, y=y, dsk=dsk, yg=yg, z=z)
        return h_out, sv

    def mixer_s5_bwd(dh, sv):
        f_glu = lambda zz: zz[:, :d] * f_sigmoid(zz[:, d:])
        dz = rowk(vjp_rows(f_glu, 1, 1), [sv['z'], dh], [], [(2 * d, BF16)], name="s5_d_glu")[0]
        grads['ssm_w_out'] = [hmm(sv['yg'], dz, ta=True, nshard=4, out3=True, out_dtype=GRAD_XFER, name="s5_d_wout")]
        dyg = hmm(dz, sv['wout'], tb=True, name="s5_d_yg")
        f_act = lambda yy, uu, dd: f_gelu(yy + dd * uu)
        dy, du1, ddsk = rowk(vjp_rows(f_act, 2, 1), [sv['y'], sv['u'], dyg], [sv['dsk']],
                             [(d_ssm, F32), (d_ssm, F32)], [(1, d_ssm)], name="s5_d_gelu")
        add_small('ssm_d', ddsk)
        dyb = dy.astype(BF16)
        dcbd_re = hmm(sv['hb_re'], dyb, ta=True, name="s5_d_c_re")
        dcbd_im = hmm(sv['hb_im'], dyb, ta=True, alpha=-1.0, name="s5_d_c_im")
        add_small('ssm_c_re', block_diag_extract(dcbd_re, n_grp).transpose(0, 2, 1)[None])
        add_small('ssm_c_im', block_diag_extract(dcbd_im, n_grp).transpose(0, 2, 1)[None])
        dh_re = hmm(dyb, sv['cbd_re'], tb=True, name="s5_d_h_re").reshape(t, gp // 128, 128)
        dh_im = hmm(dyb, sv['cbd_im'], tb=True, alpha=-1.0, name="s5_d_h_im").reshape(t, gp // 128, 128)
        hp_re = jnp.pad(sv['h_re'][:-1], ((1, 0), (0, 0), (0, 0)))
        hp_im = jnp.pad(sv['h_im'][:-1], ((1, 0), (0, 0), (0, 0)))
        g_re, g_im, da_re, da_im = ssm_scan_bwd(dh_re, dh_im, hp_re, hp_im, sv['a_re'], sv['a_im'], name="s5_d_scan")
        gb_re, gb_im = g_re.reshape(t, gp).astype(BF16), g_im.reshape(t, gp).astype(BF16)
        dbbd_re = hmm(sv['ub'], gb_re, ta=True, name="s5_d_bb_re")
        dbbd_im = hmm(sv['ub'], gb_im, ta=True, name="s5_d_bb_im")
        du = hmm(gb_re, sv['bbd_re'], tb=True, res=du1, name="s5_d_u_re")
        du = hmm(gb_im, sv['bbd_im'], tb=True, res=du, name="s5_d_u_im")
        sp = sv['sp']
        dbb_re = block_diag_extract(dbbd_re, n_grp).transpose(0, 2, 1).reshape(gp, SSM_GROUP)
        dbb_im = block_diag_extract(dbbd_im, n_grp).transpose(0, 2, 1).reshape(gp, SSM_GROUP)
        dbr, dbi, dz_re, dz_im = rowk(vjp_rows(f_ssm_bb, 4, 2),
                                      [sp['br'], sp['bi'], sp['z_re'], sp['z_im'], dbb_re, dbb_im], [],
                                      [(SSM_GROUP, F32)] * 2 + [(1, F32)] * 2, name="ssm_d_bb", tb=gp)
        add_small('ssm_b_re', dbr.reshape(W['ssm_b_re'].shape))
        add_small('ssm_b_im', dbi.reshape(W['ssm_b_im'].shape))
        dlr, dli, dldt = rowk(vjp_rows(f_ssm_ab, 3, 4),
                              [sp['lr'], sp['li'], sp['ldt'], da_re.reshape(n_grp, SSM_STATE),
                               da_im.reshape(n_grp, SSM_STATE), dz_re.reshape(n_grp, SSM_STATE),
                               dz_im.reshape(n_grp, SSM_STATE)], [],
                              [(SSM_STATE, F32)] * 2 + [(1, F32)], name="ssm_d_ab", tb=n_grp)
        add_small('ssm_lambda_re', dlr[None])
        add_small('ssm_lambda_im', dli[None])
        add_small('ssm_log_dt', dldt.reshape(1, n_grp))
        grads['ssm_w_in'] = [hmm(sv['hn'], du, ta=True, out_dtype=GRAD_XFER, name="s5_d_win").reshape(4, -1, d_ssm)]
        dhn = hmm(du, sv['win'], tb=True, name="s5_d_hn")
        dh_in, dg = rms_bwd(sv['h'], dhn, dh, sv['g'], name="s5_d_rms")
        add_small('mix_norm', dg[0], 1, depth)
        return dh_in

    def ple_fwd(h, i):
        g = W['ple_norm'][i][None]
        wpg = gather('ple_w_gate', i).reshape(d, d)
        wpp = gather('ple_w_proj', i)
        n = rowk(lambda hh, gg: f_rms(hh, gg), [h], [g], [(d, BF16)], name="ple_rms")[0]
        zg = hmm(n, wpg, name="ple_gate")
        pb = p[i, 0].astype(BF16)
        pp = hmm(pb, wpp, name="ple_proj")
        f_ple = lambda zz, pq, hh: hh + f_sigmoid(zz) * pq
        h_out = rowk(f_ple, [zg, pp, h], [], [(d, F32)], name="ple_mix")[0]
        return h_out, dict(h=h, g=g, n=n, zg=zg, pp=pp, pb=pb, wpg=wpg, wpp=wpp)

    def ple_bwd(dh, sv, i):
        f_ple = lambda zz, pq: f_sigmoid(zz) * pq
        dzg, dpp = rowk(vjp_rows(f_ple, 2, 1), [sv['zg'], sv['pp'], dh], [], [(d, BF16), (d, BF16)],
                        name="ple_d_mix")
        grads.setdefault('ple_w_proj', [None] * depth)[i] = hmm(sv['pb'], dpp, ta=True, nshard=4, out3=True,
                                                               out_dtype=GRAD_XFER, name="ple_d_wproj")
        grads.setdefault('ple_w_gate', [None] * depth)[i] = hmm(sv['n'], dzg, ta=True, out_dtype=GRAD_XFER,
                                                               name="ple_d_wgate").reshape(4, -1, d)
        dn = hmm(dzg, sv['wpg'], tb=True, name="ple_d_n")
        dh_in, dg = rms_bwd(sv['h'], dn, dh, sv['g'], name="ple_d_rms")
        add_small('ple_norm', dg[0], i, depth)
        return dh_in

    ag_order = []
    for i in range(depth):
        ag_order += [('ffn1_w_gate', i), ('ffn1_w_up', i), ('ffn1_w_down', i)]
        ag_order += [('ab_w_in', 0), ('ab_w_out', 0)] if i % 2 == 0 else [('ssm_w_in', 0), ('ssm_w_out', 0)]
        ag_order += [('ffn2_w_gate', i), ('ffn2_w_up', i), ('ffn2_w_down', i), ('ple_w_gate', i), ('ple_w_proj', i)]
    first = Comm()
    for n in SMALL_SHARDED:
        first.all_gather(W[n], nchunk=1)
    first_out = run_comm(first, name="ag_small")
    W = dict(W)
    for n, full in zip(SMALL_SHARDED, first_out):
        w = W[n]
        W[n + '_full'] = jnp.moveaxis(full, 0, -2).reshape(w.shape[1:-1] + (4 * w.shape[-1],))
    W['ssm_d_full'] = W['ssm_d_full'][None]
    queue.extend(ag_entry(n, li_) for n, li_ in ag_order)
    alone(3)
    alone(0)

    h = h0
    saved = []
    for i in range(depth):
        sv = {}
        h, sv['ffn1'] = ffn_fwd(h, 'ffn1', i)
        if i % 2 == 0:
            h, sv['mix'] = mixer_ab_fwd(h)
        else:
            h, sv['mix'] = mixer_s5_fwd(h)
        h, sv['ffn2'] = ffn_fwd(h, 'ffn2', i)
        h, sv['ple'] = ple_fwd(h, i)
        saved.append(sv)

    def f_loss(y, tg):
        e = y - tg
        part = 0.5 * jnp.sum(jnp.sum(e * e, axis=-1, keepdims=True) * (1.0 / d), axis=0, keepdims=True)
        return e * (1.0 / d), jnp.broadcast_to(part, (1, 128))
    dh, loss_part = rowk(f_loss, [h, tgt], [], [(d, F32)], [(1, 128)], name="loss")
    loss = lax.psum(loss_part[0, 0], ("x", "y", "c"))

    for i in reversed(range(depth)):
        sv = saved[i]
        dh = ple_bwd(dh, sv['ple'], i)
        dh = ffn_bwd(dh, sv['ffn2'], 'ffn2', i)
        if i % 2 == 0:
            dh = mixer_ab_bwd(dh, sv['mix'])
        else:
            dh = mixer_s5_bwd(dh, sv['mix'])
        dh = ffn_bwd(dh, sv['ffn1'], 'ffn1', i)
    grad_x = dh[None]

    small_names = [n for n in W_NAMES if n not in BIG]
    small_full = []
    for n in small_names:
        v_ = small[n]
        if isinstance(v_, list):
            v_ = jnp.stack(v_)
        full_shape = W[n].shape[:-1] + (4 * W[n].shape[-1],) if n in SMALL_SHARDED else W[n].shape
        small_full.append(v_.reshape(full_shape))
    packed = pack_flat(small_full)
    ar_got = []
    queue.append((True, lambda cm: cm.gather_all(packed), lambda outs, hd: ar_got.append(outs[hd])))

    out = {}
    todo = list(BIG)
    while todo:
        enqueue_ready()
        ready = [n for n in todo if all((n, li_) in halves for li_ in range(W[n].shape[0]))]
        if not ready:
            assert queue, todo
            flush(until=lambda: True, at_least_one=True)
            continue
        n = ready[0]
        todo.remove(n)
        nl = W[n].shape[0]
        rows, cols = int(np.prod(W[n].shape[1:-1])), W[n].shape[-1]
        a_args = (W[n].reshape(nl, rows, cols), [halves[(n, li_)][0] for li_ in range(nl)],
                  [halves[(n, li_)][1] for li_ in range(nl)], M[n].reshape(nl, rows, cols),
                  V[n].reshape(nl, rows, cols))
        if queue:
            cm, conts = take()
            res, couts = adamw_big(*a_args, comm=cm, name=f"adamw_{n}")
            for cont, hd in reversed(conts):
                cont(couts, hd)
        else:
            res = adamw_big(*a_args, name=f"adamw_{n}")
        for kind, a in zip(('grad', 'delta', 'm', 'v'), res):
            out[(kind, n)] = a.reshape(W[n].shape)
    flush()

    summed = sum_slots(ar_got[0], name="ar_small_sum")
    small_tot = unpack_flat(summed, [a.shape for a in small_full])
    g_small = {}
    for n, a in zip(small_names, small_tot):
        if n in SMALL_SHARDED:
            ns = W[n].shape[-1]
            a = lax.dynamic_slice_in_dim(a, qchip * ns, ns, axis=a.ndim - 1)
        g_small[n] = a

    pk = lambda dct: pack_flat([dct[n] for n in small_names])
    res = adamw_flat(pk(W), pk(g_small), pk(M), pk(V), name="adamw_small")
    shapes = [W[n].shape for n in small_names]
    for kind, buf in zip(('delta', 'm', 'v'), res):
        for n, a in zip(small_names, unpack_flat(buf, shapes)):
            out[(kind, n)] = a
    for n in small_names:
        out[('grad', n)] = g_small[n]

    return (loss, grad_x, *[out[('grad', n)] for n in W_NAMES], *[out[('delta', n)] for n in W_NAMES],
            *[out[('m', n)] for n in W_NAMES], *[out[('v', n)] for n in W_NAMES])
```

```python
import functools
import math

import numpy as np
import jax
import jax.numpy as jnp
from jax import lax
from jax.experimental import pallas as pl
from jax.experimental.pallas import tpu as pltpu

F32 = jnp.float32
BF16 = jnp.bfloat16
HI = lax.Precision.HIGHEST
MESH = pl.DeviceIdType.MESH

CHUNK = 64
N_LEFT = 8
BAND = (N_LEFT + 1) * CHUNK
PAD = N_LEFT * CHUNK
HEAD = 64
REL_CLIP = 128
N_REL = (CHUNK - 1) + REL_CLIP + 1
DECAY_LORA = 64
AAA_LORA = 64
GATE_LORA = 128
SSM_GROUP = 16
SSM_STATE = 64
RMS_EPS = 1e-6
GN_EPS = 64e-5
NEG_BIG = -1e30

ADAM_LR = 0.001
ADAM_B1 = 0.9
ADAM_B2 = 0.999
ADAM_EPS = 1e-08
ADAM_WD = 0.01
ADAM_STEP = 10

RW_CHUNK = 64
RW_HEADS = 8
ATT_HEADS = 2
VMEM_LIMIT = 56 * 1024 * 1024
ROW_BLOCK_BYTES = 6 * 1024 * 1024
GRAD_XFER = BF16
LOCAL_CHUNKS = 4

W_NAMES = ['ffn1_norm', 'ffn1_w_gate', 'ffn1_w_up', 'ffn1_w_down', 'mix_norm', 'ffn2_norm', 'ffn2_w_gate',
           'ffn2_w_up', 'ffn2_w_down', 'ple_norm', 'ple_w_gate', 'ple_w_proj', 'ab_w_in', 'att_q_gain',
           'att_k_gain', 'att_rel_bias', 'rwkv_mu', 'rwkv_w0', 'rwkv_w_up', 'rwkv_a0', 'rwkv_a_up',
           'rwkv_g_up', 'rwkv_k_k', 'rwkv_k_a', 'rwkv_r_k', 'rwkv_lnx_w', 'rwkv_lnx_b', 'ab_w_out',
           'ssm_w_in', 'ssm_lambda_re', 'ssm_lambda_im', 'ssm_log_dt', 'ssm_b_re', 'ssm_b_im', 'ssm_c_re',
           'ssm_c_im', 'ssm_d', 'ssm_w_out']
BIG = ['ffn1_w_gate', 'ffn1_w_up', 'ffn1_w_down', 'ffn2_w_gate', 'ffn2_w_up', 'ffn2_w_down',
       'ple_w_gate', 'ple_w_proj', 'ab_w_in', 'ab_w_out', 'ssm_w_in', 'ssm_w_out']
SMALL_SHARDED = ['rwkv_w_up', 'rwkv_a_up', 'rwkv_g_up', 'ssm_d']


def _cparams(n_axes):
    return pltpu.CompilerParams(dimension_semantics=("arbitrary",) * n_axes, vmem_limit_bytes=VMEM_LIMIT)


def _pick(n, prefs):
    for p in prefs:
        if n % p == 0:
            return p
    return n


def mm(a, b, *, ta=False, tb=False, nshard=None, out3=False, out_dtype=F32, res=None, alpha=1.0, comm=None,
       epi=None, name):
    a3, b3 = a.ndim == 3, b.ndim == 3
    if a3:
        assert not ta
        sk, m, ks = a.shape
        k = sk * ks
    elif ta:
        k, m = a.shape
    else:
        m, k = a.shape
    kshard = None
    if b3 and not tb:
        s, kb, ns = b.shape
        n = s * ns
        nshard = s
    elif b3 and tb:
        sk2, n, ks2 = b.shape
        kb = sk2 * ks2
        kshard = (sk2, ks2)
    elif tb:
        n, kb = b.shape
    else:
        kb, n = b.shape
    assert k == kb, (a.shape, b.shape, ta, tb)
    if a3:
        assert kshard is None or kshard == (sk, ks)
        kshard = (sk, ks)
    if nshard is not None:
        tn, nj = n // nshard, nshard
    else:
        assert not out3
        tn = _pick(n, (1024, 1408, 1280, 512, 640, 256, 128))
        nj = n // tn
    if kshard is not None:
        nk, tk = kshard
    else:
        tk = _pick(k, (2048, 1408, 1280, 1024, 640, 512, 256, 128))
        nk = k // tk
    tm = _pick(m, (256, 128) if epi else (512, 256, 128))
    ni = m // tm

    if a3:
        a_spec = pl.BlockSpec((None, tm, tk), lambda i, j, kk: (kk, i, 0))
    elif ta:
        a_spec = pl.BlockSpec((tk, tm), lambda i, j, kk: (kk, i))
    else:
        a_spec = pl.BlockSpec((tm, tk), lambda i, j, kk: (i, kk))
    if b3 and not tb:
        b_spec = pl.BlockSpec((None, tk, tn), lambda i, j, kk: (j, kk, 0))
    elif b3 and tb:
        b_spec = pl.BlockSpec((None, tn, tk), lambda i, j, kk: (kk, j, 0))
    elif tb:
        b_spec = pl.BlockSpec((tn, tk), lambda i, j, kk: (j, kk))
    else:
        b_spec = pl.BlockSpec((tk, tn), lambda i, j, kk: (kk, j))
    if out3:
        o_spec = pl.BlockSpec((None, tm, tn), lambda i, j, kk: (j, i, 0))
        o_shape = (nj, m, tn)
    else:
        o_spec = pl.BlockSpec((tm, tn), lambda i, j, kk: (i, j))
        o_shape = (m, n)
    has_res = res is not None
    dn = (((0 if ta else 1,), (1 if tb else 0,)), ((), ()))

    n_cin = len(comm.ins) if comm else 0
    n_cout = len(comm.outs) if comm else 0
    epi_fn, epi_extra, epi_dtypes = epi if epi else (None, [], [out_dtype])
    assert not (epi and out3)
    n_extra, n_out = len(epi_extra), len(epi_dtypes)
    n_in = 2 + has_res + n_extra

    def body(*refs):
        a_ref, b_ref = refs[0], refs[1]
        r_ref = refs[2] if has_res else None
        e_refs = refs[2 + has_res:n_in]
        c_in = refs[n_in:n_in + n_cin]
        o_refs = refs[n_in + n_cin:n_in + n_cin + n_out]
        c_out = refs[n_in + n_cin + n_out:n_in + n_cin + n_out + n_cout]
        scratch = refs[n_in + n_cin + n_out + n_cout:]
        acc_ref = scratch[0] if nk > 1 else None
        sems = scratch[1:] if nk > 1 else scratch
        i, j, kk = pl.program_id(0), pl.program_id(1), pl.program_id(2)

        if comm:
            @pl.when((i == 0) & (j == 0) & (kk == 0))
            def _():
                comm.start(c_in, c_out, *sems)

        def finish(acc):
            val = acc * alpha if alpha != 1.0 else acc
            if has_res:
                val = val + r_ref[...].astype(F32)
            vals = epi_fn(val, *[e[...] for e in e_refs]) if epi else (val,)
            for o_ref, v in zip(o_refs, vals):
                o_ref[...] = v.astype(o_ref.dtype)

        part = lax.dot_general(a_ref[...].astype(BF16), b_ref[...].astype(BF16), dn, preferred_element_type=F32)
        if nk == 1:
            finish(part)
        else:
            @pl.when(kk == 0)
            def _():
                acc_ref[...] = part

            @pl.when(kk > 0)
            def _():
                acc_ref[...] += part

            @pl.when(kk == nk - 1)
            def _():
                finish(acc_ref[...])

        if comm:
            @pl.when((i == ni - 1) & (j == nj - 1) & (kk == nk - 1))
            def _():
                comm.wait(c_in, c_out, *sems)

    in_specs = [a_spec, b_spec] + [o_spec] * (has_res + n_extra) + [ANY] * n_cin
    args = (a, b) + ((res,) if has_res else ()) + tuple(epi_extra) + (tuple(comm.ins) if comm else ())
    out_specs = [o_spec] * n_out + [ANY] * n_cout
    out_shape = [jax.ShapeDtypeStruct(o_shape, dt) for dt in epi_dtypes] + (list(comm.outs) if comm else [])
    scratch_shapes = ([pltpu.VMEM((tm, tn), F32)] if nk > 1 else []) + (comm.sem_shapes() if comm else [])
    aliases = {n_in + ii: n_out + oi for ii, oi in comm.aliases.items()} if comm else {}
    outs = pl.pallas_call(
        body, name=name, grid=(ni, nj, nk), in_specs=in_specs, out_specs=out_specs, out_shape=out_shape,
        scratch_shapes=scratch_shapes, input_output_aliases=aliases, compiler_params=_cparams(3))(*args)
    main = list(outs[:n_out]) if epi else outs[0]
    if comm:
        return main, list(outs[n_out:])
    return main


def rowk(fn, rows, consts, out_rows, out_accs=(), *, name, tb=None):
    t = rows[0].shape[0]
    nr, nc, no, na = len(rows), len(consts), len(out_rows), len(out_accs)
    if tb is None:
        per_row = sum(r.shape[1] * 4 for r in rows) + sum(n * 4 for n, _ in out_rows)
        tb = 8
        while tb * 2 <= min(t, 1024) and tb * 2 * per_row <= ROW_BLOCK_BYTES and t % (tb * 2) == 0:
            tb *= 2
    assert t % tb == 0
    nb = t // tb

    def body(*refs):
        r_in, c_in = refs[:nr], refs[nr:nr + nc]
        o_rows, o_accs = refs[nr + nc:nr + nc + no], refs[nr + nc + no:]
        outs = fn(*[r[...] for r in r_in], *[c[...] for c in c_in])
        if not isinstance(outs, (tuple, list)):
            outs = (outs,)
        assert len(outs) == no + na, (name, len(outs), no, na)
        for ref, v in zip(o_rows, outs[:no]):
            ref[...] = v.astype(ref.dtype)
        if na:
            @pl.when(pl.program_id(0) == 0)
            def _():
                for ref in o_accs:
                    ref[...] = jnp.zeros_like(ref)
            for ref, v in zip(o_accs, outs[no:]):
                ref[...] += v.astype(F32)

    in_specs = [pl.BlockSpec((tb, r.shape[1]), lambda i: (i, 0)) for r in rows]
    in_specs += [pl.BlockSpec(c.shape, lambda i, nd=c.ndim: (0,) * nd) for c in consts]
    out_specs = [pl.BlockSpec((tb, n), lambda i: (i, 0)) for n, _ in out_rows]
    out_specs += [pl.BlockSpec(s, lambda i, nd=len(s): (0,) * nd) for s in out_accs]
    out_shape = [jax.ShapeDtypeStruct((t, n), d) for n, d in out_rows]
    out_shape += [jax.ShapeDtypeStruct(s, F32) for s in out_accs]
    res = pl.pallas_call(body, name=name, grid=(nb,), in_specs=in_specs, out_specs=out_specs,
                         out_shape=out_shape, compiler_params=_cparams(1))(*rows, *consts)
    return res


def _f32(*xs):
    return [x.astype(F32) for x in xs]


def vjp_rows(f, n_rows, n_cots):
    def fn(*args):
        rows = _f32(*args[:n_rows])
        cots = _f32(*args[n_rows:n_rows + n_cots])
        consts = _f32(*args[n_rows + n_cots:])
        outs, pull = jax.vjp(f, *rows, *consts)
        if not isinstance(outs, (tuple, list)):
            cots = cots[0]
        else:
            cots = tuple(cots)
        return pull(cots)
    return fn


def hdot(x, y):
    return jnp.dot(x, y, precision=HI, preferred_element_type=F32)


def f_rms(h, g):
    return h * lax.rsqrt(jnp.mean(h * h, axis=-1, keepdims=True) + RMS_EPS) * g


def f_sigmoid(x):
    return 1.0 / (1.0 + jnp.exp(-x))


def f_swiglu(a, b):
    return a * f_sigmoid(a) * b


def f_softplus(x):
    return jnp.maximum(x, 0.0) + jnp.log(1.0 + jnp.exp(-jnp.abs(x)))


def f_gelu(x):
    return 0.5 * x * (1.0 + jnp.tanh(math.sqrt(2.0 / math.pi) * (x + 0.044715 * (x * x * x))))


def f_rwkv_pre(zs, w0, wup_p, a0, aup_p, g_up, k_k, k_a, e, et):
    d = w0.shape[1]
    r, k, v = zs[:, :d], zs[:, d:2 * d], zs[:, 2 * d:3 * d]
    xwa = zs[:, 3 * d:3 * d + DECAY_LORA + AAA_LORA]
    xg = zs[:, 3 * d + DECAY_LORA + AAA_LORA:]
    w_log = -f_softplus(-(w0 + hdot(jnp.tanh(xwa), wup_p))) - 0.5
    logw = -jnp.exp(w_log)
    a = f_sigmoid(a0 + hdot(xwa, aup_p))
    g = hdot(f_sigmoid(xg), g_up)
    kk = k * k_k
    nrm = jnp.maximum(jnp.sqrt(hdot(kk * kk, e)), 1e-12)
    kk = kk * hdot(1.0 / nrm, et)
    k2 = k * (1.0 + (a - 1.0) * k_a)
    return r, logw, k2, v, -kk, kk * a, g


def f_rwkv_post(y, r, k2, v, g, r_k, lnx_w, lnx_b, e, et):
    inv = 1.0 / HEAD
    mean = hdot(hdot(y, e) * inv, et)
    yc = y - mean
    var = hdot(yc * yc, e) * inv
    yn = yc * hdot(lax.rsqrt(var + GN_EPS), et) * lnx_w + lnx_b
    bonus = hdot(hdot(r * k2 * r_k, e), et) * v
    return (yn + bonus) * g


def f_ssm_ab(lr, li, log_dt):
    dt = jnp.exp(log_dt)
    mag = jnp.exp(lr * dt)
    ab_re, ab_im = mag * jnp.cos(li * dt), mag * jnp.sin(li * dt)
    denom = lr * lr + li * li
    z_re = ((ab_re - 1.0) * lr + ab_im * li) / denom
    z_im = (ab_im * lr - (ab_re - 1.0) * li) / denom
    return ab_re, ab_im, z_re, z_im


def f_ssm_bb(br, bi, z_re, z_im):
    return z_re * br - z_im * bi, z_re * bi + z_im * br


def _col_block(n):
    return _pick(n, (256, 128))


def ts_fwd(proj, col0, width, mu, *, name):
    t = proj.shape[0]
    cb = _col_block(width)
    assert col0 % cb == 0 and width % cb == 0
    off = col0 // cb

    def body(z_ref, mu_ref, o_ref):
        z = z_ref[...]
        row = lax.broadcasted_iota(jnp.int32, z.shape, 0)
        prev = jnp.where(row == 0, 0.0, pltpu.roll(z, 1, 0))
        o_ref[...] = z + (prev - z) * mu_ref[...]

    return pl.pallas_call(
        body, name=name, grid=(width // cb,),
        in_specs=[pl.BlockSpec((t, cb), lambda j: (0, j + off)), pl.BlockSpec((1, cb), lambda j: (0, j))],
        out_specs=pl.BlockSpec((t, cb), lambda j: (0, j)),
        out_shape=jax.ShapeDtypeStruct((t, width), F32), compiler_params=_cparams(1))(proj, mu)


def ts_bwd(proj, col0, dzs, mu, *, name):
    t, width = dzs.shape
    cb = _col_block(width)
    off = col0 // cb

    def body(z_ref, d_ref, mu_ref, dz_ref, dmu_ref):
        z, d, m = z_ref[...], d_ref[...], mu_ref[...]
        row = lax.broadcasted_iota(jnp.int32, z.shape, 0)
        prev = jnp.where(row == 0, 0.0, pltpu.roll(z, 1, 0))
        dm = d * m
        nxt = jnp.where(row == t - 1, 0.0, pltpu.roll(dm, t - 1, 0))
        dz_ref[...] = d - dm + nxt
        dmu_ref[...] = jnp.sum(d * (prev - z), axis=0, keepdims=True)

    return pl.pallas_call(
        body, name=name, grid=(width // cb,),
        in_specs=[pl.BlockSpec((t, cb), lambda j: (0, j + off)), pl.BlockSpec((t, cb), lambda j: (0, j)),
                  pl.BlockSpec((1, cb), lambda j: (0, j))],
        out_specs=[pl.BlockSpec((t, cb), lambda j: (0, j)), pl.BlockSpec((1, cb), lambda j: (0, j))],
        out_shape=[jax.ShapeDtypeStruct((t, width), F32), jax.ShapeDtypeStruct((1, width), F32)],
        compiler_params=_cparams(1))(proj, dzs, mu)


def _att_scores(qn, kb, bias, c):
    s = jnp.einsum('hqd,hkd->hqk', qn, kb, preferred_element_type=F32) + bias
    col = lax.broadcasted_iota(jnp.int32, s.shape, 2)
    s = jnp.where(col >= PAD - c * CHUNK, s, NEG_BIG)
    s = s - jnp.max(s, axis=-1, keepdims=True)
    e = jnp.exp(s)
    return e / jnp.sum(e, axis=-1, keepdims=True)


def att_fwd(qn, knp, vp, bias, *, name):
    h, t, _ = qn.shape
    hb = ATT_HEADS
    nc = t // CHUNK

    def body(q_ref, k_ref, v_ref, b_ref, o_ref):
        c = pl.program_id(1)
        start = pl.multiple_of(c * CHUNK, CHUNK)
        kb = k_ref[:, pl.ds(start, BAND), :]
        vb = v_ref[:, pl.ds(start, BAND), :]
        p = _att_scores(q_ref[...], kb, b_ref[...], c)
        o_ref[...] = jnp.einsum('hqk,hkd->hqd', p.astype(BF16), vb, preferred_element_type=F32).astype(o_ref.dtype)

    return pl.pallas_call(
        body, name=name, grid=(h // hb, nc),
        in_specs=[pl.BlockSpec((hb, CHUNK, HEAD), lambda g, c: (g, c, 0)),
                  pl.BlockSpec((hb, t + PAD, HEAD), lambda g, c: (g, 0, 0)),
                  pl.BlockSpec((hb, t + PAD, HEAD), lambda g, c: (g, 0, 0)),
                  pl.BlockSpec((hb, CHUNK, BAND), lambda g, c: (g, 0, 0))],
        out_specs=pl.BlockSpec((hb, CHUNK, HEAD), lambda g, c: (g, c, 0)),
        out_shape=jax.ShapeDtypeStruct((h, t, HEAD), BF16), compiler_params=_cparams(2))(qn, knp, vp, bias)


def att_bwd(qn, knp, vp, bias, do, *, name):
    h, t, _ = qn.shape
    hb = ATT_HEADS
    nc = t // CHUNK

    def body(q_ref, k_ref, v_ref, b_ref, do_ref, dq_ref, dk_ref, dv_ref, db_ref):
        c = pl.program_id(1)

        @pl.when(c == 0)
        def _():
            dk_ref[...] = jnp.zeros_like(dk_ref)
            dv_ref[...] = jnp.zeros_like(dv_ref)
            db_ref[...] = jnp.zeros_like(db_ref)

        start = pl.multiple_of(c * CHUNK, CHUNK)
        qv = q_ref[...]
        kb = k_ref[:, pl.ds(start, BAND), :]
        vb = v_ref[:, pl.ds(start, BAND), :]
        p = _att_scores(qv, kb, b_ref[...], c)
        dov = do_ref[...]
        dp = jnp.einsum('hqd,hkd->hqk', dov, vb, preferred_element_type=F32)
        ds = p * (dp - jnp.sum(p * dp, axis=-1, keepdims=True))
        db_ref[...] += ds
        dsb = ds.astype(BF16)
        dq_ref[...] = jnp.einsum('hqk,hkd->hqd', dsb, kb, preferred_element_type=F32)
        dst = jnp.swapaxes(dsb, 1, 2)
        pt = jnp.swapaxes(p.astype(BF16), 1, 2)
        dk_ref[:, pl.ds(start, BAND), :] += jnp.einsum('hkq,hqd->hkd', dst, qv, preferred_element_type=F32)
        dv_ref[:, pl.ds(start, BAND), :] += jnp.einsum('hkq,hqd->hkd', pt, dov, preferred_element_type=F32)

    blk_q = pl.BlockSpec((hb, CHUNK, HEAD), lambda g, c: (g, c, 0))
    blk_k = pl.BlockSpec((hb, t + PAD, HEAD), lambda g, c: (g, 0, 0))
    blk_b = pl.BlockSpec((hb, CHUNK, BAND), lambda g, c: (g, 0, 0))
    return pl.pallas_call(
        body, name=name, grid=(h // hb, nc),
        in_specs=[blk_q, blk_k, blk_k, blk_b, blk_q],
        out_specs=[blk_q, blk_k, blk_k, blk_b],
        out_shape=[jax.ShapeDtypeStruct((h, t, HEAD), F32), jax.ShapeDtypeStruct((h, t + PAD, HEAD), F32),
                   jax.ShapeDtypeStruct((h, t + PAD, HEAD), F32), jax.ShapeDtypeStruct((h, CHUNK, BAND), F32)],
        compiler_params=_cparams(2))(qn, knp, vp, bias, do)


def _rel_index():
    i = np.arange(CHUNK)[:, None]
    j = np.arange(BAND)[None, :]
    return np.clip(i + PAD - j, -(CHUNK - 1), REL_CLIP) + (CHUNK - 1)


def _rel_onehot():
    return (jnp.asarray(_rel_index())[:, :, None] == jnp.arange(N_REL)[None, None, :]).astype(F32)


def relbias_expand(rel, *, name):
    h = rel.shape[0]
    onehot_t = jnp.swapaxes(_rel_onehot(), 1, 2)

    def body(r_ref, oh_ref, o_ref):
        o_ref[...] = hdot(r_ref[...], oh_ref[...])

    out = pl.pallas_call(
        body, name=name, grid=(CHUNK,),
        in_specs=[pl.BlockSpec((h, N_REL), lambda i: (0, 0)), pl.BlockSpec((None, N_REL, BAND), lambda i: (i, 0, 0))],
        out_specs=pl.BlockSpec((None, h, BAND), lambda i: (i, 0, 0)),
        out_shape=jax.ShapeDtypeStruct((CHUNK, h, BAND), F32), compiler_params=_cparams(1))(rel, onehot_t)
    return jnp.swapaxes(out, 0, 1)


def relbias_reduce(dbias, *, name):
    h = dbias.shape[0]
    onehot = _rel_onehot()
    dbt = jnp.swapaxes(dbias, 0, 1)

    def body(d_ref, oh_ref, o_ref):
        @pl.when(pl.program_id(0) == 0)
        def _():
            o_ref[...] = jnp.zeros_like(o_ref)
        o_ref[...] += hdot(d_ref[...], oh_ref[...])

    return pl.pallas_call(
        body, name=name, grid=(CHUNK,),
        in_specs=[pl.BlockSpec((None, h, BAND), lambda i: (i, 0, 0)),
                  pl.BlockSpec((None, BAND, N_REL), lambda i: (i, 0, 0))],
        out_specs=pl.BlockSpec((h, N_REL), lambda i: (0, 0)),
        out_shape=jax.ShapeDtypeStruct((h, N_REL), F32), compiler_params=_cparams(1))(dbt, onehot)


def _bt(x):
    return jnp.swapaxes(x, 1, 2)


_BDN = (((2,), (1,)), ((0,), (0,)))


def _bmm_exact(x, y):
    return lax.dot_general(x, y, _BDN, precision=HI, preferred_element_type=F32)


def _split_bf16(x):
    hi = x.astype(BF16)
    return hi, (x - hi.astype(F32)).astype(BF16)


def _bmm_3pass(x, y):
    xh, xl = _split_bf16(x)
    yh, yl = _split_bf16(y)
    dot = lambda p, q: lax.dot_general(p, q, _BDN, preferred_element_type=F32)
    return dot(xh, yh) + (dot(xh, yl) + dot(xl, yh))


def _make_bmm(raw):
    @jax.custom_vjp
    def f(x, y):
        return raw(x, y)

    def fwd(x, y):
        return raw(x, y), (x, y)

    def bwd(saved, dz):
        x, y = saved
        return raw(dz, _bt(y)), raw(_bt(x), dz)

    f.defvjp(fwd, bwd)
    return f


bmm = _make_bmm(_bmm_3pass)
bmm_exact = _make_bmm(_bmm_exact)


def rwkv_chunk(p0, r, lw, k, v, a, b):
    g, c, n = r.shape
    row = lax.broadcasted_iota(jnp.int32, (g, c, c), 1)
    col = lax.broadcasted_iota(jnp.int32, (g, c, c), 2)
    incl, strict = row >= col, row > col
    cs = bmm_exact(incl.astype(F32), lw)
    cs_end = cs[:, c - 1:c, :]
    e_cs = jnp.exp(cs)
    e_neg = jnp.exp(-cs)
    at = a * jnp.exp(cs - lw)
    rt = r * e_cs
    bt_, kt = b * e_neg, k * e_neg
    e_tail = jnp.exp(cs_end - cs)
    bh, kh = b * e_tail, k * e_tail
    ar = jnp.concatenate([at, rt], axis=1)
    x_b, x_k = bmm(ar, _bt(bt_)), bmm(ar, _bt(kt))
    a_ab = jnp.where(strict, x_b[:, :c], 0.0)
    a_rb = jnp.where(incl, x_b[:, c:], 0.0)
    a_ak = jnp.where(strict, x_k[:, :c], 0.0)
    a_rk = jnp.where(incl, x_k[:, c:], 0.0)
    tinv = jnp.where(row == col, 1.0, 0.0) + a_ab
    npow = bmm(a_ab, a_ab)
    for _ in range(int(math.log2(c)) - 1):
        both = bmm(jnp.concatenate([tinv, npow], axis=1), npow)
        tinv = tinv + both[:, :c]
        npow = both[:, c:]
    xp = bmm(ar, p0)
    xv = bmm(jnp.concatenate([a_ak, a_rk], axis=1), v)
    u = bmm(tinv, xp[:, :c] + xv[:, :c])
    y = xp[:, c:] + bmm(a_rb, u) + xv[:, c:]
    rown = lax.broadcasted_iota(jnp.int32, (g, n, n), 1)
    coln = lax.broadcasted_iota(jnp.int32, (g, n, n), 2)
    dg = jnp.where(rown == coln, jnp.exp(cs_end), 0.0)
    p1 = bmm(dg, p0) + bmm(_bt(jnp.concatenate([bh, kh], axis=1)), jnp.concatenate([u, v], axis=1))
    return y, p1


def rwkv_fwd(r, lw, k, v, a, b, *, name):
    h, t, n = r.shape
    g, c = min(RW_HEADS, h), RW_CHUNK
    nch = t // c

    def body(r_ref, lw_ref, k_ref, v_ref, a_ref, b_ref, y_ref, p_ref, st_ref):
        @pl.when(pl.program_id(1) == 0)
        def _():
            st_ref[...] = jnp.zeros_like(st_ref)
        p0 = st_ref[...]
        p_ref[...] = p0
        y, p1 = rwkv_chunk(p0, r_ref[...], lw_ref[...], k_ref[...], v_ref[...], a_ref[...], b_ref[...])
        y_ref[...] = y
        st_ref[...] = p1

    blk = pl.BlockSpec((g, c, n), lambda i, j: (i, j, 0))
    pblk = pl.BlockSpec((g, None, n, n), lambda i, j: (i, j, 0, 0))
    return pl.pallas_call(
        body, name=name, grid=(h // g, nch), in_specs=[blk] * 6, out_specs=[blk, pblk],
        out_shape=[jax.ShapeDtypeStruct((h, t, n), F32), jax.ShapeDtypeStruct((h, nch, n, n), F32)],
        scratch_shapes=[pltpu.VMEM((g, n, n), F32)], compiler_params=_cparams(2))(r, lw, k, v, a, b)


def rwkv_bwd(r, lw, k, v, a, b, p0s, dy, *, name):
    h, t, n = r.shape
    g, c = min(RW_HEADS, h), RW_CHUNK
    nch = t // c

    def body(r_ref, lw_ref, k_ref, v_ref, a_ref, b_ref, p_ref, dy_ref,
             dr_ref, dlw_ref, dk_ref, dv_ref, da_ref, db_ref, dp_ref):
        @pl.when(pl.program_id(1) == 0)
        def _():
            dp_ref[...] = jnp.zeros_like(dp_ref)
        _, pull = jax.vjp(rwkv_chunk, p_ref[...], r_ref[...], lw_ref[...], k_ref[...], v_ref[...],
                          a_ref[...], b_ref[...])
        dp0, dr, dlw, dk, dv, da, db = pull((dy_ref[...], dp_ref[...]))
        dr_ref[...] = dr
        dlw_ref[...] = dlw
        dk_ref[...] = dk
        dv_ref[...] = dv
        da_ref[...] = da
        db_ref[...] = db
        dp_ref[...] = dp0

    blk = pl.BlockSpec((g, c, n), lambda i, j: (i, nch - 1 - j, 0))
    pblk = pl.BlockSpec((g, None, n, n), lambda i, j: (i, nch - 1 - j, 0, 0))
    return pl.pallas_call(
        body, name=name, grid=(h // g, nch), in_specs=[blk] * 6 + [pblk, blk], out_specs=[blk] * 6,
        out_shape=[jax.ShapeDtypeStruct((h, t, n), F32)] * 6,
        scratch_shapes=[pltpu.VMEM((g, n, n), F32)], compiler_params=_cparams(2))(r, lw, k, v, a, b, p0s, dy)


def _time_block(t):
    return _pick(t, (256, 128, 64))


def ssm_scan_fwd(bu_re, bu_im, a_re, a_im, *, name):
    t, rr, ln = bu_re.shape
    tb = _time_block(t)

    def body(br_ref, bi_ref, ar_ref, ai_ref, hr_ref, hi_ref, sr_ref, si_ref):
        @pl.when(pl.program_id(0) == 0)
        def _():
            sr_ref[...] = jnp.zeros_like(sr_ref)
            si_ref[...] = jnp.zeros_like(si_ref)
        ar, ai = ar_ref[...], ai_ref[...]

        def step(i, carry):
            hr, hi = carry
            nr = ar * hr - ai * hi + br_ref[i]
            ni = ar * hi + ai * hr + bi_ref[i]
            hr_ref[i] = nr
            hi_ref[i] = ni
            return nr, ni

        hr, hi = lax.fori_loop(0, tb, step, (sr_ref[...], si_ref[...]))
        sr_ref[...] = hr
        si_ref[...] = hi

    blk = pl.BlockSpec((tb, rr, ln), lambda i: (i, 0, 0))
    cblk = pl.BlockSpec((rr, ln), lambda i: (0, 0))
    return pl.pallas_call(
        body, name=name, grid=(t // tb,), in_specs=[blk, blk, cblk, cblk], out_specs=[blk, blk],
        out_shape=[jax.ShapeDtypeStruct((t, rr, ln), F32)] * 2,
        scratch_shapes=[pltpu.VMEM((rr, ln), F32)] * 2, compiler_params=_cparams(1))(bu_re, bu_im, a_re, a_im)


def ssm_scan_bwd(dh_re, dh_im, hp_re, hp_im, a_re, a_im, *, name):
    t, rr, ln = dh_re.shape
    tb = _time_block(t)
    nb = t // tb

    def body(dr_ref, di_ref, pr_ref, pi_ref, ar_ref, ai_ref, gr_ref, gi_ref, dar_ref, dai_ref, sr_ref, si_ref):
        @pl.when(pl.program_id(0) == 0)
        def _():
            sr_ref[...] = jnp.zeros_like(sr_ref)
            si_ref[...] = jnp.zeros_like(si_ref)
            dar_ref[...] = jnp.zeros_like(dar_ref)
            dai_ref[...] = jnp.zeros_like(dai_ref)
        ar, ai = ar_ref[...], ai_ref[...]

        def step(ii, carry):
            gr, gi, dar, dai = carry
            i = tb - 1 - ii
            nr = dr_ref[i] + ar * gr + ai * gi
            ni = di_ref[i] - ai * gr + ar * gi
            gr_ref[i] = nr
            gi_ref[i] = ni
            pr, pi = pr_ref[i], pi_ref[i]
            dar = dar + nr * pr + ni * pi
            dai = dai - nr * pi + ni * pr
            return nr, ni, dar, dai

        gr, gi, dar, dai = lax.fori_loop(0, tb, step, (sr_ref[...], si_ref[...], dar_ref[...], dai_ref[...]))
        sr_ref[...] = gr
        si_ref[...] = gi
        dar_ref[...] = dar
        dai_ref[...] = dai

    blk = pl.BlockSpec((tb, rr, ln), lambda i: (nb - 1 - i, 0, 0))
    cblk = pl.BlockSpec((rr, ln), lambda i: (0, 0))
    return pl.pallas_call(
        body, name=name, grid=(nb,), in_specs=[blk] * 4 + [cblk, cblk], out_specs=[blk, blk, cblk, cblk],
        out_shape=[jax.ShapeDtypeStruct((t, rr, ln), F32)] * 2 + [jax.ShapeDtypeStruct((rr, ln), F32)] * 2,
        scratch_shapes=[pltpu.VMEM((rr, ln), F32)] * 2,
        compiler_params=_cparams(1))(dh_re, dh_im, hp_re, hp_im, a_re, a_im)


ANY = pl.BlockSpec(memory_space=pl.ANY)


def _rows(ref, lead, ch, nchunk):
    base = ref if lead is None else ref.at[lead]
    if nchunk == 1:
        return base
    n = base.shape[0] // nchunk
    return base.at[pl.ds(ch * n, n)]


class Comm:
    def __init__(self):
        self.ins, self.outs, self.ops, self.aliases = [], [], [], {}
        self.n_remote, self.n_local = 0, 0
        self.ici = False

    def _add(self, kind, src, out_shape, n_peers, nchunk, n_local=1, alias=False):
        lead_len = src.shape[1] if kind in ("scatter", "rs1") else src.shape[0]
        while lead_len % nchunk:
            nchunk //= 2
        self.ops.append((kind, len(self.ins), len(self.outs), self.n_remote, self.n_local, nchunk))
        if alias:
            self.aliases[len(self.ins)] = len(self.outs)
        self.ins.append(src)
        self.outs.append(jax.ShapeDtypeStruct(out_shape, src.dtype))
        self.n_remote += n_peers * nchunk
        self.n_local += n_local
        self.ici = self.ici or kind in ("gather", "scatter", "gather_all", "ag1", "rs2")
        return len(self.outs) - 1

    def ag1(self, w):
        return self._add("ag1", w, (4,) + w.shape, 3, 1, n_local=LOCAL_CHUNKS)

    def ag2(self, g):
        return self._add("ag2", g, g.shape, 3, 1, n_local=0, alias=True)

    def rs1(self, g4):
        s, r, c = g4.shape
        return self._add("rs1", g4, (s, r // 2, c), 4, 1, n_local=0)

    def rs2(self, h4):
        return self._add("rs2", h4, (3,) + h4.shape[1:], 3, 1, n_local=0)

    def rs3(self, s, nchunk=4):
        return self._add("rs3", s, s.shape, 1, nchunk, n_local=0)

    def all_gather(self, w, nchunk=2):
        return self._add("gather", w, (4,) + w.shape, 3, nchunk)

    def scatter(self, g4, nchunk=2):
        return self._add("scatter", g4, g4.shape, 3, nchunk)

    def swap(self, s, nchunk=8):
        return self._add("swap", s, (2,) + s.shape, 1, nchunk)

    def gather_all(self, v, nchunk=1):
        return self._add("gather_all", v, (8,) + v.shape, 7, nchunk)

    def sem_shapes(self):
        return [pltpu.SemaphoreType.DMA((self.n_remote,)), pltpu.SemaphoreType.DMA((self.n_remote,)),
                pltpu.SemaphoreType.DMA((max(self.n_local, 1),))]

    def _two_level(self, kind, src, dst, r0, l0, nchunk, x, y, c, send, recv, lsem, sends, recvs, locs):
        chips = [(1 - x, y), (x, 1 - y), (1 - x, 1 - y)]
        me, sib = 2 * x + y, (x, y, 1 - c)

        def half(ref, hc):
            n = ref.shape[0] // 2
            return ref.at[pl.ds(hc * n, n)]

        def both(k, dev, s_ref, d_send, d_recv):
            mk = functools.partial(pltpu.make_async_remote_copy, src_ref=s_ref, send_sem=send.at[k],
                                   recv_sem=recv.at[k], device_id=dev, device_id_type=MESH)
            sends.append(mk(dst_ref=d_send))
            recvs.append(mk(dst_ref=d_recv))

        if kind == "ag1":
            n = src.shape[0] // LOCAL_CHUNKS
            for j in range(LOCAL_CHUNKS):
                rows = pl.ds(j * n, n)
                locs.append(pltpu.make_async_copy(src.at[rows], dst.at[me].at[rows], lsem.at[l0 + j]))
            for pj, (px, py) in enumerate(chips):
                both(r0 + pj, (px, py, c), half(src, c), half(dst.at[me], c), half(dst.at[2 * px + py], c))
        elif kind == "ag2":
            for pj, (px, py) in enumerate(chips):
                got = dst.at[2 * px + py]
                both(r0 + pj, sib, half(got, c), half(got, c), half(got, 1 - c))
        elif kind == "rs1":
            for q in range(4):
                both(r0 + q, sib, half(src.at[q], 1 - c), dst.at[q], dst.at[q])
        elif kind == "rs2":
            for pj, (px, py) in enumerate(chips):
                both(r0 + pj, (px, py, c), src.at[2 * px + py], dst.at[pj], dst.at[pj])
        else:
            for ch in range(nchunk):
                both(r0 + ch, sib, _rows(src, None, ch, nchunk), _rows(dst, None, ch, nchunk),
                     _rows(dst, None, ch, nchunk))

    def _descs(self, c_in, c_out, send, recv, lsem):
        x, y, c = lax.axis_index("x"), lax.axis_index("y"), lax.axis_index("c")
        sends, recvs, locs = [], [], []
        for kind, ii, oi, r0, l0, nchunk in self.ops:
            src, dst = c_in[ii], c_out[oi]
            if kind in ("ag1", "ag2", "rs1", "rs2", "rs3"):
                self._two_level(kind, src, dst, r0, l0, nchunk, x, y, c, send, recv, lsem, sends, recvs, locs)
                continue
            if kind == "swap":
                me = c
                peers = [((x, y, 1 - c), 1 - c)]
            elif kind == "gather_all":
                me = 4 * x + 2 * y + c
                flips = [(dx, dy, dc) for dx in (0, 1) for dy in (0, 1) for dc in (0, 1) if dx + dy + dc]
                peers = []
                for dx, dy, dc in flips:
                    px, py, pc = (x + dx) % 2, (y + dy) % 2, (c + dc) % 2
                    peers.append(((px, py, pc), 4 * px + 2 * py + pc))
            else:
                me = 2 * x + y
                peers = [((px, py, c), 2 * px + py) for px, py in ((1 - x, y), (x, 1 - y), (1 - x, 1 - y))]
            if kind == "scatter":
                locs.append(pltpu.make_async_copy(src.at[me], dst.at[me], lsem.at[l0]))
            else:
                locs.append(pltpu.make_async_copy(src, dst.at[me], lsem.at[l0]))
            for pj, (dev, peer_slot) in enumerate(peers):
                for ch in range(nchunk):
                    k = r0 + pj * nchunk + ch
                    s_src = _rows(src, peer_slot if kind == "scatter" else None, ch, nchunk)
                    mk = functools.partial(pltpu.make_async_remote_copy, send_sem=send.at[k], recv_sem=recv.at[k],
                                           device_id=dev, device_id_type=MESH)
                    sends.append(mk(src_ref=s_src, dst_ref=_rows(dst, me, ch, nchunk)))
                    recvs.append(mk(src_ref=s_src, dst_ref=_rows(dst, peer_slot, ch, nchunk)))
        return sends, recvs, locs

    def start(self, c_in, c_out, send, recv, lsem):
        sends, _, locs = self._descs(c_in, c_out, send, recv, lsem)
        for d in locs + sends:
            d.start()

    def wait(self, c_in, c_out, send, recv, lsem):
        sends, recvs, locs = self._descs(c_in, c_out, send, recv, lsem)
        for d in recvs:
            d.wait_recv()
        for d in sends:
            d.wait_send()
        for d in locs:
            d.wait()


def run_comm(comm, *, name):
    n_cin, n_cout = len(comm.ins), len(comm.outs)

    def body(*refs):
        c_in, c_out, sems = refs[:n_cin], refs[n_cin:n_cin + n_cout], refs[n_cin + n_cout:]
        comm.start(c_in, c_out, *sems)
        comm.wait(c_in, c_out, *sems)

    outs = pl.pallas_call(
        body, name=name, in_specs=[ANY] * n_cin, out_specs=[ANY] * n_cout, out_shape=list(comm.outs),
        scratch_shapes=comm.sem_shapes(), input_output_aliases=dict(comm.aliases),
        compiler_params=pltpu.CompilerParams(has_side_effects=True))(*comm.ins)
    return list(outs)


def _core_index():
    return jnp.reshape(lax.axis_index("c"), (1,)).astype(jnp.int32)


def _chip_index():
    return jnp.reshape(2 * lax.axis_index("x") + lax.axis_index("y"), (1,)).astype(jnp.int32)


def _row_block(rows, bytes_per_row, cap=512):
    tb = 16
    while tb * 2 <= min(rows, cap) and rows % (tb * 2) == 0 and tb * 2 * bytes_per_row <= ROW_BLOCK_BYTES:
        tb *= 2
    return tb


def add_half(g4, got, *, name):
    s, r, c = g4.shape
    r2 = r // 2
    tb = _row_block(r2, c * 8)
    nb = r2 // tb

    def body(c_ref, g_ref, x_ref, o_ref):
        o_ref[...] = (g_ref[...].astype(F32) + x_ref[...].astype(F32)).astype(o_ref.dtype)

    grid_spec = pltpu.PrefetchScalarGridSpec(
        num_scalar_prefetch=1, grid=(s, nb),
        in_specs=[pl.BlockSpec((None, tb, c), lambda q, i, cr: (q, cr[0] * nb + i, 0)),
                  pl.BlockSpec((None, tb, c), lambda q, i, cr: (q, i, 0))],
        out_specs=pl.BlockSpec((None, tb, c), lambda q, i, cr: (q, i, 0)))
    return pl.pallas_call(body, name=name, grid_spec=grid_spec, out_shape=jax.ShapeDtypeStruct((s, r2, c), g4.dtype),
                          compiler_params=_cparams(2))(_core_index(), g4, got)


def sum_chips(h4, got3, *, name):
    _, r2, c = h4.shape
    tb = _row_block(r2, c * 12)
    nb = r2 // tb

    def body(q_ref, h_ref, y_ref, o_ref):
        o_ref[...] = ((h_ref[...].astype(F32) + y_ref[0].astype(F32)) + y_ref[1].astype(F32)) + y_ref[2].astype(F32)

    grid_spec = pltpu.PrefetchScalarGridSpec(
        num_scalar_prefetch=1, grid=(nb,),
        in_specs=[pl.BlockSpec((None, tb, c), lambda i, qr: (qr[0], i, 0)),
                  pl.BlockSpec((3, tb, c), lambda i, qr: (0, i, 0))],
        out_specs=pl.BlockSpec((tb, c), lambda i, qr: (i, 0)))
    return pl.pallas_call(body, name=name, grid_spec=grid_spec, out_shape=jax.ShapeDtypeStruct((r2, c), F32),
                          compiler_params=_cparams(1))(_chip_index(), h4, got3)


def adamw_big(w, mine, theirs, m, v, *, comm=None, name):
    nl, r, c = w.shape
    r2 = r // 2
    tb = _row_block(r2, c * 4 * 10, cap=256)
    nb2 = r2 // tb
    nb = 2 * nb2
    n_cin = len(comm.ins) if comm else 0
    n_cout = len(comm.outs) if comm else 0

    def body(c_ref, w_ref, m_ref, v_ref, *rest):
        g_refs, rest = rest[:2 * nl], rest[2 * nl:]
        c_in, (go_ref, d_ref, mo_ref, vo_ref), rest = rest[:n_cin], rest[n_cin:n_cin + 4], rest[n_cin + 4:]
        c_out, sems = rest[:n_cout], rest[n_cout:]
        layer, i = pl.program_id(0), pl.program_id(1)
        if comm:
            @pl.when((layer == 0) & (i == 0))
            def _():
                comm.start(c_in, c_out, *sems)

            @pl.when((layer == nl - 1) & (i == nb - 1))
            def _():
                comm.wait(c_in, c_out, *sems)
        own = (i // nb2) == c_ref[0]
        for l0 in range(nl):
            @pl.when(layer == l0)
            def _(l0=l0):
                g = jnp.where(own, g_refs[2 * l0][...], g_refs[2 * l0 + 1][...])
                d, mn, vn = _adam_math(w_ref[...], g, m_ref[...], v_ref[...])
                go_ref[...] = g
                d_ref[...] = d
                mo_ref[...] = mn
                vo_ref[...] = vn

    blk = pl.BlockSpec((None, tb, c), lambda l, i, cr: (l, i, 0))

    def half_spec(l0):
        return pl.BlockSpec((tb, c), lambda l, i, cr: (jnp.where(l == l0, i % nb2, jnp.where(l < l0, 0, nb2 - 1)), 0))

    in_specs = [blk, blk, blk]
    args = [w, m, v]
    for l0 in range(nl):
        in_specs += [half_spec(l0), half_spec(l0)]
        args += [mine[l0], theirs[l0]]
    if comm:
        in_specs += [ANY] * n_cin
        args += list(comm.ins)
    grid_spec = pltpu.PrefetchScalarGridSpec(
        num_scalar_prefetch=1, grid=(nl, nb), in_specs=in_specs, out_specs=[blk] * 4 + [ANY] * n_cout,
        scratch_shapes=comm.sem_shapes() if comm else [])
    aliases = {1 + 3 + 2 * nl + ii: 4 + oi for ii, oi in comm.aliases.items()} if comm else {}
    outs = pl.pallas_call(
        body, name=name, grid_spec=grid_spec,
        out_shape=[jax.ShapeDtypeStruct(w.shape, F32)] * 4 + (list(comm.outs) if comm else []),
        input_output_aliases=aliases, compiler_params=_cparams(2))(_core_index(), *args)
    if comm:
        return list(outs[:4]), list(outs[4:])
    return list(outs)


def sum_slots(x, *, name):
    s, r, c = x.shape
    tb = 8
    while tb * 2 <= min(r, 512) and r % (tb * 2) == 0 and tb * 2 * c * 4 * (s + 1) <= ROW_BLOCK_BYTES:
        tb *= 2

    def body(x_ref, o_ref):
        acc = x_ref[0].astype(F32)
        for i in range(1, s):
            acc = acc + x_ref[i].astype(F32)
        o_ref[...] = acc

    return pl.pallas_call(
        body, name=name, grid=(r // tb,), in_specs=[pl.BlockSpec((s, tb, c), lambda i: (0, i, 0))],
        out_specs=pl.BlockSpec((tb, c), lambda i: (i, 0)),
        out_shape=jax.ShapeDtypeStruct((r, c), F32), compiler_params=_cparams(1))(x)


def _adam_math(w, g, m, v):
    m = ADAM_B1 * m + (1.0 - ADAM_B1) * g
    v = ADAM_B2 * v + (1.0 - ADAM_B2) * (g * g)
    m_hat = m / (1.0 - ADAM_B1 ** ADAM_STEP)
    v_hat = v / (1.0 - ADAM_B2 ** ADAM_STEP)
    delta = -ADAM_LR * (m_hat / (jnp.sqrt(v_hat) + ADAM_EPS) + ADAM_WD * w)
    return delta, m, v


def adamw_pair(w, g2, m, v, *, name):
    r, c = w.shape
    tb = 8
    while tb * 2 <= min(r, 512) and r % (tb * 2) == 0 and tb * 2 * c * 4 * 9 <= 2 * ROW_BLOCK_BYTES:
        tb *= 2

    def body(w_ref, g_ref, m_ref, v_ref, go_ref, d_ref, mo_ref, vo_ref):
        g = g_ref[0] + g_ref[1]
        d, mn, vn = _adam_math(w_ref[...], g, m_ref[...], v_ref[...])
        go_ref[...] = g
        d_ref[...] = d
        mo_ref[...] = mn
        vo_ref[...] = vn

    blk = pl.BlockSpec((tb, c), lambda i: (i, 0))
    return pl.pallas_call(
        body, name=name, grid=(r // tb,), in_specs=[blk, pl.BlockSpec((2, tb, c), lambda i: (0, i, 0)), blk, blk],
        out_specs=[blk] * 4, out_shape=[jax.ShapeDtypeStruct((r, c), F32)] * 4,
        compiler_params=_cparams(1))(w, g2, m, v)


def adamw_flat(w, g, m, v, *, name):
    def fn(w_, g_, m_, v_):
        return _adam_math(w_, g_, m_, v_)
    return rowk(fn, [w, g, m, v], [], [(w.shape[1], F32)] * 3, name=name)


def to_heads(x):
    t, d = x.shape
    return x.reshape(t, d // HEAD, HEAD).transpose(1, 0, 2)


def from_heads(x):
    h, t, n = x.shape
    return x.transpose(1, 0, 2).reshape(t, h * n)


def pack_flat(arrs, lanes=128, row_mult=512):
    flat = jnp.concatenate([a.reshape(-1).astype(F32) for a in arrs])
    n = flat.shape[0]
    rows = -(-n // lanes)
    rows = -(-rows // row_mult) * row_mult
    return jnp.pad(flat, (0, rows * lanes - n)).reshape(rows, lanes)


def unpack_flat(buf, shapes):
    flat = buf.reshape(-1)
    outs, off = [], 0
    for s in shapes:
        n = int(np.prod(s))
        outs.append(flat[off:off + n].reshape(s))
        off += n
    return outs


def block_diag_from(w_gab):
    g, a, b = w_gab.shape
    eye = jnp.eye(g, dtype=w_gab.dtype)
    return (w_gab[:, :, None, :] * eye[:, None, :, None]).reshape(g * a, g * b)


def block_diag_extract(m, g):
    a, b = m.shape[0] // g, m.shape[1] // g
    eye = jnp.eye(g, dtype=m.dtype)
    return jnp.sum(m.reshape(g, a, g, b) * eye[:, None, :, None], axis=2)


def kernel(x, p, ffn1_norm, ffn1_w_gate, ffn1_w_up, ffn1_w_down, mix_norm, ffn2_norm, ffn2_w_gate, ffn2_w_up, ffn2_w_down, ple_norm, ple_w_gate, ple_w_proj, ab_w_in, att_q_gain, att_k_gain, att_rel_bias, rwkv_mu, rwkv_w0, rwkv_w_up, rwkv_a0, rwkv_a_up, rwkv_g_up, rwkv_k_k, rwkv_k_a, rwkv_r_k, rwkv_lnx_w, rwkv_lnx_b, ab_w_out, ssm_w_in, ssm_lambda_re, ssm_lambda_im, ssm_log_dt, ssm_b_re, ssm_b_im, ssm_c_re, ssm_c_im, ssm_d, ssm_w_out, loss_target, m_ffn1_norm, m_ffn1_w_gate, m_ffn1_w_up, m_ffn1_w_down, m_mix_norm, m_ffn2_norm, m_ffn2_w_gate, m_ffn2_w_up, m_ffn2_w_down, m_ple_norm, m_ple_w_gate, m_ple_w_proj, m_ab_w_in, m_att_q_gain, m_att_k_gain, m_att_rel_bias, m_rwkv_mu, m_rwkv_w0, m_rwkv_w_up, m_rwkv_a0, m_rwkv_a_up, m_rwkv_g_up, m_rwkv_k_k, m_rwkv_k_a, m_rwkv_r_k, m_rwkv_lnx_w, m_rwkv_lnx_b, m_ab_w_out, m_ssm_w_in, m_ssm_lambda_re, m_ssm_lambda_im, m_ssm_log_dt, m_ssm_b_re, m_ssm_b_im, m_ssm_c_re, m_ssm_c_im, m_ssm_d, m_ssm_w_out, v_ffn1_norm, v_ffn1_w_gate, v_ffn1_w_up, v_ffn1_w_down, v_mix_norm, v_ffn2_norm, v_ffn2_w_gate, v_ffn2_w_up, v_ffn2_w_down, v_ple_norm, v_ple_w_gate, v_ple_w_proj, v_ab_w_in, v_att_q_gain, v_att_k_gain, v_att_rel_bias, v_rwkv_mu, v_rwkv_w0, v_rwkv_w_up, v_rwkv_a0, v_rwkv_a_up, v_rwkv_g_up, v_rwkv_k_k, v_rwkv_k_a, v_rwkv_r_k, v_rwkv_lnx_w, v_rwkv_lnx_b, v_ab_w_out, v_ssm_w_in, v_ssm_lambda_re, v_ssm_lambda_im, v_ssm_log_dt, v_ssm_b_re, v_ssm_b_im, v_ssm_c_re, v_ssm_c_im, v_ssm_d, v_ssm_w_out):
    A = dict(locals())
    W = {n: A[n] for n in W_NAMES}
    return _step(A['x'], A['p'], A['loss_target'], W, {n: A['m_' + n] for n in W_NAMES},
                 {n: A['v_' + n] for n in W_NAMES})


def _step(x, p, target, W, M, V):
    assert x.shape[0] == 1
    t, d = x.shape[1], x.shape[2]
    depth = p.shape[0]
    h0 = x[0]
    tgt = target[0]
    qchip = 2 * lax.axis_index("x") + lax.axis_index("y")
    d_rw = W['rwkv_w0'].shape[1]
    d_att = W['ab_w_out'].shape[1] * 4 - d_rw
    n_h_att, n_h_rw = d_att // HEAD, d_rw // HEAD
    n_bin = 3 * d_rw + DECAY_LORA + AAA_LORA + GATE_LORA
    d_ssm = W['ssm_w_in'].shape[2]
    n_grp = d_ssm // SSM_GROUP
    gp = n_grp * SSM_STATE

    queue = []
    gathered = {}
    halves = {}
    grads = {}
    queued_grads = set()
    n_alone = [0]

    def ag_entry(name, layer):
        def add1(cm):
            return cm.ag1(W[name][layer].astype(BF16))

        def cont1(outs, hd):
            got = outs[hd]

            def cont2(outs2, hd2):
                gathered[(name, layer)] = outs2[hd2]
            queue.insert(0, (False, lambda cm: cm.ag2(got), cont2))
        return True, add1, cont1

    def rs_entry(name, layer, g4):
        shard_shape = W[name].shape[1:]
        rows, cols = int(np.prod(shard_shape[:-1])), shard_shape[-1]
        g4 = g4.reshape(4, rows, cols)

        def cont1(outs, hd):
            h4 = add_half(g4, outs[hd], name=f"rs_add_{name}")

            def cont2(outs2, hd2):
                mine = sum_chips(h4, outs2[hd2], name=f"rs_sum_{name}")

                def cont3(outs3, hd3):
                    halves[(name, layer)] = (mine, outs3[hd3])
                queue.insert(0, (False, lambda cm: cm.rs3(mine), cont3))
            queue.insert(0, (True, lambda cm: cm.rs2(h4), cont2))
        return False, (lambda cm: cm.rs1(g4)), cont1

    def enqueue_ready():
        for n in BIG:
            for li_, g4 in enumerate(grads.get(n, [])):
                if g4 is not None and (n, li_) not in queued_grads:
                    queued_grads.add((n, li_))
                    queue.append(rs_entry(n, li_, g4))

    def take(max_ici=1):
        cm, conts, n_ici = Comm(), [], 0
        while queue and (n_ici < max_ici or not queue[0][0]):
            is_ici, add, cont = queue.pop(0)
            conts.append((cont, add(cm)))
            n_ici += int(is_ici)
        return cm, conts

    def hmm(*args, **kw):
        enqueue_ready()
        if not queue:
            return mm(*args, **kw)
        cm, conts = take()
        out, couts = mm(*args, comm=cm, **kw)
        for cont, hd in reversed(conts):
            cont(couts, hd)
        return out

    def alone(max_ici=1):
        cm, conts = take(max_ici)
        assert conts
        n_alone[0] += 1
        couts = run_comm(cm, name=f"comm_alone{n_alone[0]}")
        for cont, hd in reversed(conts):
            cont(couts, hd)

    def flush(until=None, at_least_one=False):
        enqueue_ready()
        while queue and (at_least_one or not (until is not None and until())):
            at_least_one = False
            alone()
            enqueue_ready()

    def gather(name, layer):
        flush(until=lambda: (name, layer) in gathered)
        return gathered[(name, layer)]
    small = {}

    def add_small(name, val, layer=None, nl=1):
        if layer is None:
            small[name] = val
        else:
            small.setdefault(name, [None] * nl)[layer] = val

    def ffn_fwd(h, pre, i):
        g = W[pre + '_norm'][i][None]
        wg, wu, wd = gather(pre + '_w_gate', i), gather(pre + '_w_up', i), gather(pre + '_w_down', i)
        wd2 = wd.reshape(-1, d)
        n = rowk(lambda hh, gg: f_rms(hh, gg), [h], [g], [(d, BF16)], name=f"{pre}_rms")[0]
        a = hmm(n, wg, out_dtype=BF16, name=f"{pre}_gate")
        b, u = hmm(n, wu, epi=(lambda bb, aa: (bb, f_swiglu(aa.astype(F32), bb)), [a], [BF16, BF16]),
                   name=f"{pre}_up")
        h_out = hmm(u, wd2, res=h, alpha=0.5, name=f"{pre}_down")
        return h_out, dict(h=h, g=g, n=n, a=a, b=b, u=u, wg=wg, wu=wu, wd2=wd2)

    def ffn_bwd(dh, sv, pre, i):
        f = sv['a'].shape[1]
        dwd = hmm(sv['u'], dh, ta=True, alpha=0.5, out_dtype=GRAD_XFER, name=f"{pre}_d_wdown")
        def f_du(du, aa, bb):
            _, pull = jax.vjp(f_swiglu, aa.astype(F32), bb.astype(F32))
            return pull(du)
        da, db = hmm(dh, sv['wd2'], tb=True, alpha=0.5, epi=(f_du, [sv['a'], sv['b']], [BF16, BF16]),
                     name=f"{pre}_d_u")
        dwg = hmm(sv['n'], da, ta=True, nshard=4, out3=True, out_dtype=GRAD_XFER, name=f"{pre}_d_wgate")
        dwu = hmm(sv['n'], db, ta=True, nshard=4, out3=True, out_dtype=GRAD_XFER, name=f"{pre}_d_wup")
        dn = hmm(da, sv['wg'], tb=True, name=f"{pre}_d_n1")
        dn = hmm(db, sv['wu'], tb=True, res=dn, name=f"{pre}_d_n2")
        dh_in, dg = rms_bwd(sv['h'], dn, dh, sv['g'], name=f"{pre}_d_rms")
        grads.setdefault(pre + '_w_gate', [None] * depth)[i] = dwg
        grads.setdefault(pre + '_w_up', [None] * depth)[i] = dwu
        grads.setdefault(pre + '_w_down', [None] * depth)[i] = dwd.reshape(4, -1, d)
        add_small(pre + '_norm', dg[0], i, depth)
        return dh_in

    def rms_bwd(h, dn, dh_res, g, *, name):
        def fn(hh, dnn, dres, gg):
            _, pull = jax.vjp(f_rms, hh, gg)
            dh_, dg_ = pull(dnn)
            return dh_ + dres, dg_
        return rowk(fn, [h, dn, dh_res], [g], [(d, F32)], [(1, d)], name=name)

    def head_consts(nh):
        e = np.kron(np.eye(nh, dtype=np.float32), np.ones((HEAD, 1), np.float32))
        return jnp.asarray(e), jnp.asarray(e.T)

    def mixer_ab_fwd(h):
        g = W['mix_norm'][0][None]
        win = gather('ab_w_in', 0).transpose(1, 0, 2).reshape(d, -1)
        wout = gather('ab_w_out', 0).reshape(-1, d)
        hn = rowk(lambda hh, gg: f_rms(hh, gg), [h], [g], [(d, BF16)], name="mixab_rms")[0]
        proj = hmm(hn, win, name="mixab_proj")
        q2, k2, v2 = [to_heads(proj[:, j * d_att:(j + 1) * d_att]).reshape(n_h_att * t, HEAD) for j in range(3)]
        qg, kg = W['att_q_gain'], W['att_k_gain']
        f_qn = lambda qq, gg: f_rms(qq, gg) * (HEAD ** -0.5)
        qn = rowk(f_qn, [q2], [qg], [(HEAD, BF16)], name="att_qnorm")[0].reshape(n_h_att, t, HEAD)
        kn = rowk(f_rms, [k2], [kg], [(HEAD, BF16)], name="att_knorm")[0].reshape(n_h_att, t, HEAD)
        knp = jnp.pad(kn, ((0, 0), (PAD, 0), (0, 0)))
        vp = jnp.pad(v2.astype(BF16).reshape(n_h_att, t, HEAD), ((0, 0), (PAD, 0), (0, 0)))
        bias = relbias_expand(W['att_rel_bias'][0], name="att_relbias")
        o = att_fwd(qn, knp, vp, bias, name="att_fwd")
        att = from_heads(o)
        mu = W['rwkv_mu']
        zs = ts_fwd(proj, 3 * d_att, n_bin, mu, name="rwkv_shift")
        e, et = head_consts(n_h_rw)
        zpad = jnp.zeros((AAA_LORA, d_rw), F32)
        wup_p = jnp.concatenate([W['rwkv_w_up_full'], zpad], 0)
        aup_p = jnp.concatenate([zpad, W['rwkv_a_up_full']], 0)
        pre_c = [W['rwkv_w0'], wup_p, W['rwkv_a0'], aup_p, W['rwkv_g_up_full'], W['rwkv_k_k'], W['rwkv_k_a'], e, et]
        pre = rowk(f_rwkv_pre, [zs], pre_c, [(d_rw, F32)] * 7, name="rwkv_pre")
        r_, lw_, kk_, vv_, ia_, ib_, gg_ = pre
        hm = [to_heads(u_) for u_ in (r_, lw_, kk_, vv_, ia_, ib_)]
        y_h, p0s = rwkv_fwd(*hm, name="rwkv_scan")
        y = from_heads(y_h)
        post_c = [W['rwkv_r_k'].reshape(1, d_rw), W['rwkv_lnx_w'], W['rwkv_lnx_b'], e, et]
        rw = rowk(f_rwkv_post, [y, r_, kk_, vv_, gg_], post_c, [(d_rw, BF16)], name="rwkv_post")[0]
        cat = jnp.concatenate([att, rw], axis=1)
        h_out = hmm(cat, wout, res=h, name="mixab_out")
        sv = dict(h=h, g=g, hn=hn, win=win, wout=wout, proj=proj, q2=q2, k2=k2, qn=qn, knp=knp, vp=vp, bias=bias,
                  zs=zs, pre_c=pre_c, pre=pre, hm=hm, p0s=p0s, y=y, post_c=post_c, cat=cat, qg=qg, kg=kg, mu=mu)
        return h_out, sv

    def mixer_ab_bwd(dh, sv):
        dwout = hmm(sv['cat'], dh, ta=True, out_dtype=GRAD_XFER, name="mixab_d_wout")
        grads['ab_w_out'] = [dwout.reshape(4, -1, d)]
        dcat = hmm(dh, sv['wout'], tb=True, name="mixab_d_cat")
        datt, drw = dcat[:, :d_att], dcat[:, d_att:]
        r_, lw_, kk_, vv_, ia_, ib_, gg_ = sv['pre']
        post = rowk(vjp_rows(f_rwkv_post, 5, 1), [sv['y'], r_, kk_, vv_, gg_, drw], sv['post_c'],
                    [(d_rw, F32)] * 5, [(1, d_rw)] * 3 + [sv['post_c'][3].shape, sv['post_c'][4].shape],
                    name="rwkv_d_post")
        dy, dr1, dk1, dv1, dg1 = post[:5]
        add_small('rwkv_r_k', post[5].reshape(W['rwkv_r_k'].shape))
        add_small('rwkv_lnx_w', post[6])
        add_small('rwkv_lnx_b', post[7])
        dscan = rwkv_bwd(*sv['hm'], sv['p0s'], to_heads(dy), name="rwkv_d_scan")
        dr2, dlw, dk2, dv2, dia, dib = [from_heads(u_) for u_ in dscan]

        def pre_bwd(zs, dra, drb, dlw_, dka, dkb, dva, dvb, dia_, dib_, dg_, *consts):
            _, pull = jax.vjp(f_rwkv_pre, zs, *consts)
            return pull((dra + drb, dlw_, dka + dkb, dva + dvb, dia_, dib_, dg_))

        pc = sv['pre_c']
        preb = rowk(pre_bwd, [sv['zs'], dr1, dr2, dlw, dk1, dk2, dv1, dv2, dia, dib, dg1], pc,
                    [(n_bin, F32)], [c.shape for c in pc], name="rwkv_d_pre")
        dzs = preb[0]
        add_small('rwkv_w0', preb[1])
        add_small('rwkv_w_up', preb[2][:DECAY_LORA])
        add_small('rwkv_a0', preb[3])
        add_small('rwkv_a_up', preb[4][DECAY_LORA:])
        add_small('rwkv_g_up', preb[5])
        add_small('rwkv_k_k', preb[6])
        add_small('rwkv_k_a', preb[7])
        dz, dmu = ts_bwd(sv['proj'], 3 * d_att, dzs, sv['mu'], name="rwkv_d_shift")
        add_small('rwkv_mu', dmu)
        do = to_heads(datt).astype(BF16)
        dqn, dknp, dvp, dbias = att_bwd(sv['qn'], sv['knp'], sv['vp'], sv['bias'], do, name="att_bwd")
        add_small('att_rel_bias', relbias_reduce(dbias, name="att_d_relbias")[None])
        f_qn = lambda qq, gg: f_rms(qq, gg) * (HEAD ** -0.5)
        dq2, dqg = rowk(vjp_rows(f_qn, 1, 1), [sv['q2'], dqn.reshape(-1, HEAD)], [sv['qg']], [(HEAD, F32)],
                        [(1, HEAD)], name="att_d_qnorm")
        dk2_, dkg = rowk(vjp_rows(f_rms, 1, 1), [sv['k2'], dknp[:, PAD:].reshape(-1, HEAD)], [sv['kg']],
                         [(HEAD, F32)], [(1, HEAD)], name="att_d_knorm")
        add_small('att_q_gain', dqg)
        add_small('att_k_gain', dkg)
        dproj = jnp.concatenate([from_heads(dq2.reshape(n_h_att, t, HEAD)), from_heads(dk2_.reshape(n_h_att, t, HEAD)),
                                 from_heads(dvp[:, PAD:]), dz], axis=1)
        dproj = dproj.astype(BF16)
        dwin = hmm(sv['hn'], dproj, ta=True, out_dtype=GRAD_XFER, name="mixab_d_win")
        grads['ab_w_in'] = [dwin.reshape(d, 4, -1).transpose(1, 0, 2)]
        dhn = hmm(dproj, sv['win'], tb=True, name="mixab_d_hn")
        dh_in, dg = rms_bwd(sv['h'], dhn, dh, sv['g'], name="mixab_d_rms")
        add_small('mix_norm', dg[0], 0, depth)
        return dh_in

    def ssm_params():
        lr, li = W['ssm_lambda_re'][0], W['ssm_lambda_im'][0]
        ldt = W['ssm_log_dt'][0][:, None]
        ab = rowk(f_ssm_ab, [lr, li, ldt], [], [(SSM_STATE, F32)] * 4, name="ssm_ab", tb=n_grp)
        br = W['ssm_b_re'][0].reshape(gp, SSM_GROUP)
        bi = W['ssm_b_im'][0].reshape(gp, SSM_GROUP)
        z_re, z_im = ab[2].reshape(gp, 1), ab[3].reshape(gp, 1)
        bb = rowk(f_ssm_bb, [br, bi, z_re, z_im], [], [(SSM_GROUP, F32)] * 2, name="ssm_bb", tb=gp)
        return dict(lr=lr, li=li, ldt=ldt, ab=ab, br=br, bi=bi, z_re=z_re, z_im=z_im, bb=bb)

    def mixer_s5_fwd(h):
        g = W['mix_norm'][1][None]
        win, wout = gather('ssm_w_in', 0).reshape(d, d_ssm), gather('ssm_w_out', 0)
        hn = rowk(lambda hh, gg: f_rms(hh, gg), [h], [g], [(d, BF16)], name="s5_rms")[0]
        u = hmm(hn, win, name="s5_in")
        sp = ssm_params()
        bbd_re = block_diag_from(sp['bb'][0].reshape(n_grp, SSM_STATE, SSM_GROUP).transpose(0, 2, 1)).astype(BF16)
        bbd_im = block_diag_from(sp['bb'][1].reshape(n_grp, SSM_STATE, SSM_GROUP).transpose(0, 2, 1)).astype(BF16)
        cbd_re = block_diag_from(W['ssm_c_re'][0].transpose(0, 2, 1)).astype(BF16)
        cbd_im = block_diag_from(W['ssm_c_im'][0].transpose(0, 2, 1)).astype(BF16)
        ub = u.astype(BF16)
        bu_re = hmm(ub, bbd_re, name="s5_bu_re").reshape(t, gp // 128, 128)
        bu_im = hmm(ub, bbd_im, name="s5_bu_im").reshape(t, gp // 128, 128)
        a_re, a_im = sp['ab'][0].reshape(gp // 128, 128), sp['ab'][1].reshape(gp // 128, 128)
        h_re, h_im = ssm_scan_fwd(bu_re, bu_im, a_re, a_im, name="s5_scan")
        hb_re, hb_im = h_re.reshape(t, gp).astype(BF16), h_im.reshape(t, gp).astype(BF16)
        y = hmm(hb_re, cbd_re, name="s5_y_re")
        y = hmm(hb_im, cbd_im, res=y, alpha=-1.0, name="s5_y_im")
        dsk = W['ssm_d_full']
        f_act = lambda yy, uu, dd: f_gelu(yy + dd * uu)
        yg = rowk(f_act, [y, u], [dsk], [(d_ssm, BF16)], name="s5_gelu")[0]
        z = hmm(yg, wout, name="s5_out")
        f_glu = lambda zz, hh: hh + zz[:, :d] * f_sigmoid(zz[:, d:])
        h_out = rowk(f_glu, [z, h], [], [(d, F32)], name="s5_glu")[0]
        sv = dict(h=h, g=g, hn=hn, win=win, wout=wout, u=u, ub=ub, sp=sp, bbd_re=bbd_re, bbd_im=bbd_im,
                  cbd_re=cbd_re, cbd_im=cbd_im, a_re=a_re, a_im=a_im, h_re=h_re, h_im=h_im, hb_re=hb_re,
                  hb_im=hb_im, y=y, dsk=dsk, yg=yg, z=z)
        return h_out, sv

    def mixer_s5_bwd(dh, sv):
        f_glu = lambda zz: zz[:, :d] * f_sigmoid(zz[:, d:])
        dz = rowk(vjp_rows(f_glu, 1, 1), [sv['z'], dh], [], [(2 * d, BF16)], name="s5_d_glu")[0]
        grads['ssm_w_out'] = [hmm(sv['yg'], dz, ta=True, nshard=4, out3=True, out_dtype=GRAD_XFER, name="s5_d_wout")]
        dyg = hmm(dz, sv['wout'], tb=True, name="s5_d_yg")
        f_act = lambda yy, uu, dd: f_gelu(yy + dd * uu)
        dy, du1, ddsk = rowk(vjp_rows(f_act, 2, 1), [sv['y'], sv['u'], dyg], [sv['dsk']],
                             [(d_ssm, F32), (d_ssm, F32)], [(1, d_ssm)], name="s5_d_gelu")
        add_small('ssm_d', ddsk)
        dyb = dy.astype(BF16)
        dcbd_re = hmm(sv['hb_re'], dyb, ta=True, name="s5_d_c_re")
        dcbd_im = hmm(sv['hb_im'], dyb, ta=True, alpha=-1.0, name="s5_d_c_im")
        add_small('ssm_c_re', block_diag_extract(dcbd_re, n_grp).transpose(0, 2, 1)[None])
        add_small('ssm_c_im', block_diag_extract(dcbd_im, n_grp).transpose(0, 2, 1)[None])
        dh_re = hmm(dyb, sv['cbd_re'], tb=True, name="s5_d_h_re").reshape(t, gp // 128, 128)
        dh_im = hmm(dyb, sv['cbd_im'], tb=True, alpha=-1.0, name="s5_d_h_im").reshape(t, gp // 128, 128)
        hp_re = jnp.pad(sv['h_re'][:-1], ((1, 0), (0, 0), (0, 0)))
        hp_im = jnp.pad(sv['h_im'][:-1], ((1, 0), (0, 0), (0, 0)))
        g_re, g_im, da_re, da_im = ssm_scan_bwd(dh_re, dh_im, hp_re, hp_im, sv['a_re'], sv['a_im'], name="s5_d_scan")
        gb_re, gb_im = g_re.reshape(t, gp).astype(BF16), g_im.reshape(t, gp).astype(BF16)
        dbbd_re = hmm(sv['ub'], gb_re, ta=True, name="s5_d_bb_re")
        dbbd_im = hmm(sv['ub'], gb_im, ta=True, name="s5_d_bb_im")
        du = hmm(gb_re, sv['bbd_re'], tb=True, res=du1, name="s5_d_u_re")
        du = hmm(gb_im, sv['bbd_im'], tb=True, res=du, name="s5_d_u_im")
        sp = sv['sp']
        dbb_re = block_diag_extract(dbbd_re, n_grp).transpose(0, 2, 1).reshape(gp, SSM_GROUP)
        dbb_im = block_diag_extract(dbbd_im, n_grp).transpose(0, 2, 1).reshape(gp, SSM_GROUP)
        dbr, dbi, dz_re, dz_im = rowk(vjp_rows(f_ssm_bb, 4, 2),
                                      [sp['br'], sp['bi'], sp['z_re'], sp['z_im'], dbb_re, dbb_im], [],
                                      [(SSM_GROUP, F32)] * 2 + [(1, F32)] * 2, name="ssm_d_bb", tb=gp)
        add_small('ssm_b_re', dbr.reshape(W['ssm_b_re'].shape))
        add_small('ssm_b_im', dbi.reshape(W['ssm_b_im'].shape))
        dlr, dli, dldt = rowk(vjp_rows(f_ssm_ab, 3, 4),
                              [sp['lr'], sp['li'], sp['ldt'], da_re.reshape(n_grp, SSM_STATE),
                               da_im.reshape(n_grp, SSM_STATE), dz_re.reshape(n_grp, SSM_STATE),
                               dz_im.reshape(n_grp, SSM_STATE)], [],
                              [(SSM_STATE, F32)] * 2 + [(1, F32)], name="ssm_d_ab", tb=n_grp)
        add_small('ssm_lambda_re', dlr[None])
        add_small('ssm_lambda_im', dli[None])
        add_small('ssm_log_dt', dldt.reshape(1, n_grp))
        grads['ssm_w_in'] = [hmm(sv['hn'], du, ta=True, out_dtype=GRAD_XFER, name="s5_d_win").reshape(4, -1, d_ssm)]
        dhn = hmm(du, sv['win'], tb=True, name="s5_d_hn")
        dh_in, dg = rms_bwd(sv['h'], dhn, dh, sv['g'], name="s5_d_rms")
        add_small('mix_norm', dg[0], 1, depth)
        return dh_in

    def ple_fwd(h, i):
        g = W['ple_norm'][i][None]
        wpg = gather('ple_w_gate', i).reshape(d, d)
        wpp = gather('ple_w_proj', i)
        n = rowk(lambda hh, gg: f_rms(hh, gg), [h], [g], [(d, BF16)], name="ple_rms")[0]
        zg = hmm(n, wpg, name="ple_gate")
        pb = p[i, 0].astype(BF16)
        pp = hmm(pb, wpp, name="ple_proj")
        f_ple = lambda zz, pq, hh: hh + f_sigmoid(zz) * pq
        h_out = rowk(f_ple, [zg, pp, h], [], [(d, F32)], name="ple_mix")[0]
        return h_out, dict(h=h, g=g, n=n, zg=zg, pp=pp, pb=pb, wpg=wpg, wpp=wpp)

    def ple_bwd(dh, sv, i):
        f_ple = lambda zz, pq: f_sigmoid(zz) * pq
        dzg, dpp = rowk(vjp_rows(f_ple, 2, 1), [sv['zg'], sv['pp'], dh], [], [(d, BF16), (d, BF16)],
                        name="ple_d_mix")
        grads.setdefault('ple_w_proj', [None] * depth)[i] = hmm(sv['pb'], dpp, ta=True, nshard=4, out3=True,
                                                               out_dtype=GRAD_XFER, name="ple_d_wproj")
        grads.setdefault('ple_w_gate', [None] * depth)[i] = hmm(sv['n'], dzg, ta=True, out_dtype=GRAD_XFER,
                                                               name="ple_d_wgate").reshape(4, -1, d)
        dn = hmm(dzg, sv['wpg'], tb=True, name="ple_d_n")
        dh_in, dg = rms_bwd(sv['h'], dn, dh, sv['g'], name="ple_d_rms")
        add_small('ple_norm', dg[0], i, depth)
        return dh_in

    ag_order = []
    for i in range(depth):
        ag_order += [('ffn1_w_gate', i), ('ffn1_w_up', i), ('ffn1_w_down', i)]
        ag_order += [('ab_w_in', 0), ('ab_w_out', 0)] if i % 2 == 0 else [('ssm_w_in', 0), ('ssm_w_out', 0)]
        ag_order += [('ffn2_w_gate', i), ('ffn2_w_up', i), ('ffn2_w_down', i), ('ple_w_gate', i), ('ple_w_proj', i)]
    first = Comm()
    for n in SMALL_SHARDED:
        first.all_gather(W[n], nchunk=1)
    first_out = run_comm(first, name="ag_small")
    W = dict(W)
    for n, full in zip(SMALL_SHARDED, first_out):
        w = W[n]
        W[n + '_full'] = jnp.moveaxis(full, 0, -2).reshape(w.shape[1:-1] + (4 * w.shape[-1],))
    W['ssm_d_full'] = W['ssm_d_full'][None]
    queue.extend(ag_entry(n, li_) for n, li_ in ag_order)
    alone(3)
    alone(0)

    h = h0
    saved = []
    for i in range(depth):
        sv = {}
        h, sv['ffn1'] = ffn_fwd(h, 'ffn1', i)
        if i % 2 == 0:
            h, sv['mix'] = mixer_ab_fwd(h)
        else:
            h, sv['mix'] = mixer_s5_fwd(h)
        h, sv['ffn2'] = ffn_fwd(h, 'ffn2', i)
        h, sv['ple'] = ple_fwd(h, i)
        saved.append(sv)

    def f_loss(y, tg):
        e = y - tg
        part = 0.5 * jnp.sum(jnp.sum(e * e, axis=-1, keepdims=True) * (1.0 / d), axis=0, keepdims=True)
        return e * (1.0 / d), jnp.broadcast_to(part, (1, 128))
    dh, loss_part = rowk(f_loss, [h, tgt], [], [(d, F32)], [(1, 128)], name="loss")
    loss = lax.psum(loss_part[0, 0], ("x", "y", "c"))

    for i in reversed(range(depth)):
        sv = saved[i]
        dh = ple_bwd(dh, sv['ple'], i)
        dh = ffn_bwd(dh, sv['ffn2'], 'ffn2', i)
        if i % 2 == 0:
            dh = mixer_ab_bwd(dh, sv['mix'])
        else:
            dh = mixer_s5_bwd(dh, sv['mix'])
        dh = ffn_bwd(dh, sv['ffn1'], 'ffn1', i)
    grad_x = dh[None]

    small_names = [n for n in W_NAMES if n not in BIG]
    small_full = []
    for n in small_names:
        v_ = small[n]
        if isinstance(v_, list):
            v_ = jnp.stack(v_)
        full_shape = W[n].shape[:-1] + (4 * W[n].shape[-1],) if n in SMALL_SHARDED else W[n].shape
        small_full.append(v_.reshape(full_shape))
    packed = pack_flat(small_full)
    ar_got = []
    queue.append((True, lambda cm: cm.gather_all(packed), lambda outs, hd: ar_got.append(outs[hd])))

    out = {}
    todo = list(BIG)
    while todo:
        enqueue_ready()
        ready = [n for n in todo if all((n, li_) in halves for li_ in range(W[n].shape[0]))]
        if not ready:
            assert queue, todo
            flush(until=lambda: True, at_least_one=True)
            continue
        n = ready[0]
        todo.remove(n)
        nl = W[n].shape[0]
        rows, cols = int(np.prod(W[n].shape[1:-1])), W[n].shape[-1]
        a_args = (W[n].reshape(nl, rows, cols), [halves[(n, li_)][0] for li_ in range(nl)],
                  [halves[(n, li_)][1] for li_ in range(nl)], M[n].reshape(nl, rows, cols),
                  V[n].reshape(nl, rows, cols))
        if queue:
            cm, conts = take()
            res, couts = adamw_big(*a_args, comm=cm, name=f"adamw_{n}")
            for cont, hd in reversed(conts):
                cont(couts, hd)
        else:
            res = adamw_big(*a_args, name=f"adamw_{n}")
        for kind, a in zip(('grad', 'delta', 'm', 'v'), res):
            out[(kind, n)] = a.reshape(W[n].shape)
    flush()

    summed = sum_slots(ar_got[0], name="ar_small_sum")
    small_tot = unpack_flat(summed, [a.shape for a in small_full])
    g_small = {}
    for n, a in zip(small_names, small_tot):
        if n in SMALL_SHARDED:
            ns = W[n].shape[-1]
            a = lax.dynamic_slice_in_dim(a, qchip * ns, ns, axis=a.ndim - 1)
        g_small[n] = a

    pk = lambda dct: pack_flat([dct[n] for n in small_names])
    res = adamw_flat(pk(W), pk(g_small), pk(M), pk(V), name="adamw_small")
    shapes = [W[n].shape for n in small_names]
    for kind, buf in zip(('delta', 'm', 'v'), res):
        for n, a in zip(small_names, unpack_flat(buf, shapes)):
            out[(kind, n)] = a
    for n in small_names:
        out[('grad', n)] = g_small[n]

    return (loss, grad_x, *[out[('grad', n)] for n in W_NAMES], *[out[('delta', n)] for n in W_NAMES],
            *[out[('m', n)] for n in W_NAMES], *[out[('v', n)] for n in W_NAMES])
```

```python
import functools
import math

import numpy as np
import jax
import jax.numpy as jnp
from jax import lax
from jax.experimental import pallas as pl
from jax.experimental.pallas import tpu as pltpu

F32 = jnp.float32
BF16 = jnp.bfloat16
HI = lax.Precision.HIGHEST
MESH = pl.DeviceIdType.MESH

CHUNK = 64
N_LEFT = 8
BAND = (N_LEFT + 1) * CHUNK
PAD = N_LEFT * CHUNK
HEAD = 64
REL_CLIP = 128
N_REL = (CHUNK - 1) + REL_CLIP + 1
DECAY_LORA = 64
AAA_LORA = 64
GATE_LORA = 128
SSM_GROUP = 16
SSM_STATE = 64
RMS_EPS = 1e-6
GN_EPS = 64e-5
NEG_BIG = -1e30

ADAM_LR = 0.001
ADAM_B1 = 0.9
ADAM_B2 = 0.999
ADAM_EPS = 1e-08
ADAM_WD = 0.01
ADAM_STEP = 10

RW_CHUNK = 64
RW_HEADS = 16
ATT_HEADS = 2
ATT_HEADS_FWD = 4
VMEM_LIMIT = 56 * 1024 * 1024
ROW_BLOCK_BYTES = 6 * 1024 * 1024
GRAD_XFER = BF16
LOCAL_CHUNKS = 4

W_NAMES = ['ffn1_norm', 'ffn1_w_gate', 'ffn1_w_up', 'ffn1_w_down', 'mix_norm', 'ffn2_norm', 'ffn2_w_gate',
           'ffn2_w_up', 'ffn2_w_down', 'ple_norm', 'ple_w_gate', 'ple_w_proj', 'ab_w_in', 'att_q_gain',
           'att_k_gain', 'att_rel_bias', 'rwkv_mu', 'rwkv_w0', 'rwkv_w_up', 'rwkv_a0', 'rwkv_a_up',
           'rwkv_g_up', 'rwkv_k_k', 'rwkv_k_a', 'rwkv_r_k', 'rwkv_lnx_w', 'rwkv_lnx_b', 'ab_w_out',
           'ssm_w_in', 'ssm_lambda_re', 'ssm_lambda_im', 'ssm_log_dt', 'ssm_b_re', 'ssm_b_im', 'ssm_c_re',
           'ssm_c_im', 'ssm_d', 'ssm_w_out']
BIG = ['ffn1_w_gate', 'ffn1_w_up', 'ffn1_w_down', 'ffn2_w_gate', 'ffn2_w_up', 'ffn2_w_down',
       'ple_w_gate', 'ple_w_proj', 'ab_w_in', 'ab_w_out', 'ssm_w_in', 'ssm_w_out']
SMALL_SHARDED = ['rwkv_w_up', 'rwkv_a_up', 'rwkv_g_up', 'ssm_d']


def _cparams(n_axes):
    return pltpu.CompilerParams(dimension_semantics=("arbitrary",) * n_axes, vmem_limit_bytes=VMEM_LIMIT)


def _pick(n, prefs):
    for p in prefs:
        if n % p == 0:
            return p
    return n


def mm(a, b, *, ta=False, tb=False, nshard=None, out3=False, out_dtype=F32, res=None, alpha=1.0, comm=None,
       epi=None, name):
    a3, b3 = a.ndim == 3, b.ndim == 3
    if a3:
        assert not ta
        sk, m, ks = a.shape
        k = sk * ks
    elif ta:
        k, m = a.shape
    else:
        m, k = a.shape
    kshard = None
    if b3 and not tb:
        s, kb, ns = b.shape
        n = s * ns
        nshard = s
    elif b3 and tb:
        sk2, n, ks2 = b.shape
        kb = sk2 * ks2
        kshard = (sk2, ks2)
    elif tb:
        n, kb = b.shape
    else:
        kb, n = b.shape
    assert k == kb, (a.shape, b.shape, ta, tb)
    if a3:
        assert kshard is None or kshard == (sk, ks)
        kshard = (sk, ks)
    if nshard is not None:
        tn, nj = n // nshard, nshard
    else:
        assert not out3
        tn = _pick(n, (1024, 1408, 1280, 512, 640, 256, 128))
        nj = n // tn
    if kshard is not None:
        nk, tk = kshard
    else:
        tk = _pick(k, (2048, 1408, 1280, 1024, 640, 512, 256, 128))
        nk = k // tk
    tm = _pick(m, (256, 128) if epi else (512, 256, 128))
    ni = m // tm

    if a3:
        a_spec = pl.BlockSpec((None, tm, tk), lambda i, j, kk: (kk, i, 0))
    elif ta:
        a_spec = pl.BlockSpec((tk, tm), lambda i, j, kk: (kk, i))
    else:
        a_spec = pl.BlockSpec((tm, tk), lambda i, j, kk: (i, kk))
    if b3 and not tb:
        b_spec = pl.BlockSpec((None, tk, tn), lambda i, j, kk: (j, kk, 0))
    elif b3 and tb:
        b_spec = pl.BlockSpec((None, tn, tk), lambda i, j, kk: (kk, j, 0))
    elif tb:
        b_spec = pl.BlockSpec((tn, tk), lambda i, j, kk: (j, kk))
    else:
        b_spec = pl.BlockSpec((tk, tn), lambda i, j, kk: (kk, j))
    if out3:
        o_spec = pl.BlockSpec((None, tm, tn), lambda i, j, kk: (j, i, 0))
        o_shape = (nj, m, tn)
    else:
        o_spec = pl.BlockSpec((tm, tn), lambda i, j, kk: (i, j))
        o_shape = (m, n)
    has_res = res is not None
    dn = (((0 if ta else 1,), (1 if tb else 0,)), ((), ()))

    n_cin = len(comm.ins) if comm else 0
    n_cout = len(comm.outs) if comm else 0
    epi_fn, epi_extra, epi_dtypes = epi if epi else (None, [], [out_dtype])
    assert not (epi and out3)
    n_extra, n_out = len(epi_extra), len(epi_dtypes)
    n_in = 2 + has_res + n_extra

    def body(*refs):
        a_ref, b_ref = refs[0], refs[1]
        r_ref = refs[2] if has_res else None
        e_refs = refs[2 + has_res:n_in]
        c_in = refs[n_in:n_in + n_cin]
        o_refs = refs[n_in + n_cin:n_in + n_cin + n_out]
        c_out = refs[n_in + n_cin + n_out:n_in + n_cin + n_out + n_cout]
        scratch = refs[n_in + n_cin + n_out + n_cout:]
        acc_ref = scratch[0] if nk > 1 else None
        sems = scratch[1:] if nk > 1 else scratch
        i, j, kk = pl.program_id(0), pl.program_id(1), pl.program_id(2)

        if comm:
            @pl.when((i == 0) & (j == 0) & (kk == 0))
            def _():
                comm.start(c_in, c_out, *sems)

        def finish(acc):
            val = acc * alpha if alpha != 1.0 else acc
            if has_res:
                val = val + r_ref[...].astype(F32)
            vals = epi_fn(val, *[e[...] for e in e_refs]) if epi else (val,)
            for o_ref, v in zip(o_refs, vals):
                o_ref[...] = v.astype(o_ref.dtype)

        part = lax.dot_general(a_ref[...].astype(BF16), b_ref[...].astype(BF16), dn, preferred_element_type=F32)
        if nk == 1:
            finish(part)
        else:
            @pl.when(kk == 0)
            def _():
                acc_ref[...] = part

            @pl.when(kk > 0)
            def _():
                acc_ref[...] += part

            @pl.when(kk == nk - 1)
            def _():
                finish(acc_ref[...])

        if comm:
            @pl.when((i == ni - 1) & (j == nj - 1) & (kk == nk - 1))
            def _():
                comm.wait(c_in, c_out, *sems)

    in_specs = [a_spec, b_spec] + [o_spec] * (has_res + n_extra) + [ANY] * n_cin
    args = (a, b) + ((res,) if has_res else ()) + tuple(epi_extra) + (tuple(comm.ins) if comm else ())
    out_specs = [o_spec] * n_out + [ANY] * n_cout
    out_shape = [jax.ShapeDtypeStruct(o_shape, dt) for dt in epi_dtypes] + (list(comm.outs) if comm else [])
    scratch_shapes = ([pltpu.VMEM((tm, tn), F32)] if nk > 1 else []) + (comm.sem_shapes() if comm else [])
    aliases = {n_in + ii: n_out + oi for ii, oi in comm.aliases.items()} if comm else {}
    outs = pl.pallas_call(
        body, name=name, grid=(ni, nj, nk), in_specs=in_specs, out_specs=out_specs, out_shape=out_shape,
        scratch_shapes=scratch_shapes, input_output_aliases=aliases, compiler_params=_cparams(3))(*args)
    main = list(outs[:n_out]) if epi else outs[0]
    if comm:
        return main, list(outs[n_out:])
    return main


def rowk(fn, rows, consts, out_rows, out_accs=(), *, name, tb=None):
    t = rows[0].shape[0]
    nr, nc, no, na = len(rows), len(consts), len(out_rows), len(out_accs)
    if tb is None:
        per_row = sum(r.shape[1] * 4 for r in rows) + sum(n * 4 for n, _ in out_rows)
        tb = 8
        while tb * 2 <= min(t, 1024) and tb * 2 * per_row <= ROW_BLOCK_BYTES and t % (tb * 2) == 0:
            tb *= 2
    assert t % tb == 0
    nb = t // tb

    def body(*refs):
        r_in, c_in = refs[:nr], refs[nr:nr + nc]
        o_rows, o_accs = refs[nr + nc:nr + nc + no], refs[nr + nc + no:]
        outs = fn(*[r[...] for r in r_in], *[c[...] for c in c_in])
        if not isinstance(outs, (tuple, list)):
            outs = (outs,)
        assert len(outs) == no + na, (name, len(outs), no, na)
        for ref, v in zip(o_rows, outs[:no]):
            ref[...] = v.astype(ref.dtype)
        if na:
            @pl.when(pl.program_id(0) == 0)
            def _():
                for ref in o_accs:
                    ref[...] = jnp.zeros_like(ref)
            for ref, v in zip(o_accs, outs[no:]):
                ref[...] += v.astype(F32)

    in_specs = [pl.BlockSpec((tb, r.shape[1]), lambda i: (i, 0)) for r in rows]
    in_specs += [pl.BlockSpec(c.shape, lambda i, nd=c.ndim: (0,) * nd) for c in consts]
    out_specs = [pl.BlockSpec((tb, n), lambda i: (i, 0)) for n, _ in out_rows]
    out_specs += [pl.BlockSpec(s, lambda i, nd=len(s): (0,) * nd) for s in out_accs]
    out_shape = [jax.ShapeDtypeStruct((t, n), d) for n, d in out_rows]
    out_shape += [jax.ShapeDtypeStruct(s, F32) for s in out_accs]
    res = pl.pallas_call(body, name=name, grid=(nb,), in_specs=in_specs, out_specs=out_specs,
                         out_shape=out_shape, compiler_params=_cparams(1))(*rows, *consts)
    return res


def _f32(*xs):
    return [x.astype(F32) for x in xs]


def vjp_rows(f, n_rows, n_cots):
    def fn(*args):
        rows = _f32(*args[:n_rows])
        cots = _f32(*args[n_rows:n_rows + n_cots])
        consts = _f32(*args[n_rows + n_cots:])
        outs, pull = jax.vjp(f, *rows, *consts)
        if not isinstance(outs, (tuple, list)):
            cots = cots[0]
        else:
            cots = tuple(cots)
        return pull(cots)
    return fn


def hdot(x, y):
    return jnp.dot(x, y, precision=HI, preferred_element_type=F32)


def f_rms(h, g):
    return h * lax.rsqrt(jnp.mean(h * h, axis=-1, keepdims=True) + RMS_EPS) * g


def f_sigmoid(x):
    return 1.0 / (1.0 + jnp.exp(-x))


def f_swiglu(a, b):
    return a * f_sigmoid(a) * b


def f_softplus(x):
    return jnp.maximum(x, 0.0) + jnp.log(1.0 + jnp.exp(-jnp.abs(x)))


def f_gelu(x):
    return 0.5 * x * (1.0 + jnp.tanh(math.sqrt(2.0 / math.pi) * (x + 0.044715 * (x * x * x))))


def f_rwkv_pre(zs, w0, wup_p, a0, aup_p, g_up, k_k, k_a, e, et):
    d = w0.shape[1]
    r, k, v = zs[:, :d], zs[:, d:2 * d], zs[:, 2 * d:3 * d]
    xwa = zs[:, 3 * d:3 * d + DECAY_LORA + AAA_LORA]
    xg = zs[:, 3 * d + DECAY_LORA + AAA_LORA:]
    w_log = -f_softplus(-(w0 + hdot(jnp.tanh(xwa), wup_p))) - 0.5
    logw = -jnp.exp(w_log)
    a = f_sigmoid(a0 + hdot(xwa, aup_p))
    g = hdot(f_sigmoid(xg), g_up)
    kk = k * k_k
    nrm = jnp.maximum(jnp.sqrt(hdot(kk * kk, e)), 1e-12)
    kk = kk * hdot(1.0 / nrm, et)
    k2 = k * (1.0 + (a - 1.0) * k_a)
    return r, logw, k2, v, -kk, kk * a, g


def f_rwkv_post(y, r, k2, v, g, r_k, lnx_w, lnx_b, e, et):
    inv = 1.0 / HEAD
    mean = hdot(hdot(y, e) * inv, et)
    yc = y - mean
    var = hdot(yc * yc, e) * inv
    yn = yc * hdot(lax.rsqrt(var + GN_EPS), et) * lnx_w + lnx_b
    bonus = hdot(hdot(r * k2 * r_k, e), et) * v
    return (yn + bonus) * g


def f_ssm_ab(lr, li, log_dt):
    dt = jnp.exp(log_dt)
    mag = jnp.exp(lr * dt)
    ab_re, ab_im = mag * jnp.cos(li * dt), mag * jnp.sin(li * dt)
    denom = lr * lr + li * li
    z_re = ((ab_re - 1.0) * lr + ab_im * li) / denom
    z_im = (ab_im * lr - (ab_re - 1.0) * li) / denom
    return ab_re, ab_im, z_re, z_im


def f_ssm_bb(br, bi, z_re, z_im):
    return z_re * br - z_im * bi, z_re * bi + z_im * br


def _col_block(n):
    return _pick(n, (256, 128))


def ts_fwd(proj, col0, width, mu, *, name):
    t = proj.shape[0]
    cb = _col_block(width)
    assert col0 % cb == 0 and width % cb == 0
    off = col0 // cb

    def body(z_ref, mu_ref, o_ref):
        z = z_ref[...]
        row = lax.broadcasted_iota(jnp.int32, z.shape, 0)
        prev = jnp.where(row == 0, 0.0, pltpu.roll(z, 1, 0))
        o_ref[...] = z + (prev - z) * mu_ref[...]

    return pl.pallas_call(
        body, name=name, grid=(width // cb,),
        in_specs=[pl.BlockSpec((t, cb), lambda j: (0, j + off)), pl.BlockSpec((1, cb), lambda j: (0, j))],
        out_specs=pl.BlockSpec((t, cb), lambda j: (0, j)),
        out_shape=jax.ShapeDtypeStruct((t, width), F32), compiler_params=_cparams(1))(proj, mu)


def ts_bwd(proj, col0, dzs, mu, *, name):
    t, width = dzs.shape
    cb = _col_block(width)
    off = col0 // cb

    def body(z_ref, d_ref, mu_ref, dz_ref, dmu_ref):
        z, d, m = z_ref[...], d_ref[...], mu_ref[...]
        row = lax.broadcasted_iota(jnp.int32, z.shape, 0)
        prev = jnp.where(row == 0, 0.0, pltpu.roll(z, 1, 0))
        dm = d * m
        nxt = jnp.where(row == t - 1, 0.0, pltpu.roll(dm, t - 1, 0))
        dz_ref[...] = d - dm + nxt
        dmu_ref[...] = jnp.sum(d * (prev - z), axis=0, keepdims=True)

    return pl.pallas_call(
        body, name=name, grid=(width // cb,),
        in_specs=[pl.BlockSpec((t, cb), lambda j: (0, j + off)), pl.BlockSpec((t, cb), lambda j: (0, j)),
                  pl.BlockSpec((1, cb), lambda j: (0, j))],
        out_specs=[pl.BlockSpec((t, cb), lambda j: (0, j)), pl.BlockSpec((1, cb), lambda j: (0, j))],
        out_shape=[jax.ShapeDtypeStruct((t, width), F32), jax.ShapeDtypeStruct((1, width), F32)],
        compiler_params=_cparams(1))(proj, dzs, mu)


def _att_scores(qn, kb, bias, c):
    s = jnp.einsum('hqd,hkd->hqk', qn, kb, preferred_element_type=F32) + bias
    col = lax.broadcasted_iota(jnp.int32, s.shape, 2)
    s = jnp.where(col >= PAD - c * CHUNK, s, NEG_BIG)
    s = s - jnp.max(s, axis=-1, keepdims=True)
    e = jnp.exp(s)
    return e / jnp.sum(e, axis=-1, keepdims=True)


def att_fwd(qn, knp, vp, bias, *, name):
    h, t, _ = qn.shape
    hb = min(ATT_HEADS_FWD, h)
    nc = t // CHUNK

    def body(q_ref, k_ref, v_ref, b_ref, o_ref):
        c = pl.program_id(1)
        start = pl.multiple_of(c * CHUNK, CHUNK)
        kb = k_ref[:, pl.ds(start, BAND), :]
        vb = v_ref[:, pl.ds(start, BAND), :]
        p = _att_scores(q_ref[...], kb, b_ref[...], c)
        o_ref[...] = jnp.einsum('hqk,hkd->hqd', p.astype(BF16), vb, preferred_element_type=F32).astype(o_ref.dtype)

    return pl.pallas_call(
        body, name=name, grid=(h // hb, nc),
        in_specs=[pl.BlockSpec((hb, CHUNK, HEAD), lambda g, c: (g, c, 0)),
                  pl.BlockSpec((hb, t + PAD, HEAD), lambda g, c: (g, 0, 0)),
                  pl.BlockSpec((hb, t + PAD, HEAD), lambda g, c: (g, 0, 0)),
                  pl.BlockSpec((hb, CHUNK, BAND), lambda g, c: (g, 0, 0))],
        out_specs=pl.BlockSpec((hb, CHUNK, HEAD), lambda g, c: (g, c, 0)),
        out_shape=jax.ShapeDtypeStruct((h, t, HEAD), BF16), compiler_params=_cparams(2))(qn, knp, vp, bias)


def att_bwd(qn, knp, vp, bias, do, *, name):
    h, t, _ = qn.shape
    hb = ATT_HEADS
    nc = t // CHUNK

    def body(q_ref, k_ref, v_ref, b_ref, do_ref, dq_ref, dk_ref, dv_ref, db_ref):
        c = pl.program_id(1)

        @pl.when(c == 0)
        def _():
            dk_ref[...] = jnp.zeros_like(dk_ref)
            dv_ref[...] = jnp.zeros_like(dv_ref)
            db_ref[...] = jnp.zeros_like(db_ref)

        start = pl.multiple_of(c * CHUNK, CHUNK)
        qv = q_ref[...]
        kb = k_ref[:, pl.ds(start, BAND), :]
        vb = v_ref[:, pl.ds(start, BAND), :]
        p = _att_scores(qv, kb, b_ref[...], c)
        dov = do_ref[...]
        dp = jnp.einsum('hqd,hkd->hqk', dov, vb, preferred_element_type=F32)
        ds = p * (dp - jnp.sum(p * dp, axis=-1, keepdims=True))
        db_ref[...] += ds
        dsb = ds.astype(BF16)
        dq_ref[...] = jnp.einsum('hqk,hkd->hqd', dsb, kb, preferred_element_type=F32)
        dst = jnp.swapaxes(dsb, 1, 2)
        pt = jnp.swapaxes(p.astype(BF16), 1, 2)
        dk_ref[:, pl.ds(start, BAND), :] += jnp.einsum('hkq,hqd->hkd', dst, qv, preferred_element_type=F32)
        dv_ref[:, pl.ds(start, BAND), :] += jnp.einsum('hkq,hqd->hkd', pt, dov, preferred_element_type=F32)

    blk_q = pl.BlockSpec((hb, CHUNK, HEAD), lambda g, c: (g, c, 0))
    blk_k = pl.BlockSpec((hb, t + PAD, HEAD), lambda g, c: (g, 0, 0))
    blk_b = pl.BlockSpec((hb, CHUNK, BAND), lambda g, c: (g, 0, 0))
    return pl.pallas_call(
        body, name=name, grid=(h // hb, nc),
        in_specs=[blk_q, blk_k, blk_k, blk_b, blk_q],
        out_specs=[blk_q, blk_k, blk_k, blk_b],
        out_shape=[jax.ShapeDtypeStruct((h, t, HEAD), F32), jax.ShapeDtypeStruct((h, t + PAD, HEAD), F32),
                   jax.ShapeDtypeStruct((h, t + PAD, HEAD), F32), jax.ShapeDtypeStruct((h, CHUNK, BAND), F32)],
        compiler_params=_cparams(2))(qn, knp, vp, bias, do)


def _rel_index():
    i = np.arange(CHUNK)[:, None]
    j = np.arange(BAND)[None, :]
    return np.clip(i + PAD - j, -(CHUNK - 1), REL_CLIP) + (CHUNK - 1)


def _rel_onehot():
    return (jnp.asarray(_rel_index())[:, :, None] == jnp.arange(N_REL)[None, None, :]).astype(F32)


def relbias_expand(rel, *, name):
    h = rel.shape[0]
    onehot_t = jnp.swapaxes(_rel_onehot(), 1, 2)

    def body(r_ref, oh_ref, o_ref):
        o_ref[...] = hdot(r_ref[...], oh_ref[...])

    out = pl.pallas_call(
        body, name=name, grid=(CHUNK,),
        in_specs=[pl.BlockSpec((h, N_REL), lambda i: (0, 0)), pl.BlockSpec((None, N_REL, BAND), lambda i: (i, 0, 0))],
        out_specs=pl.BlockSpec((None, h, BAND), lambda i: (i, 0, 0)),
        out_shape=jax.ShapeDtypeStruct((CHUNK, h, BAND), F32), compiler_params=_cparams(1))(rel, onehot_t)
    return jnp.swapaxes(out, 0, 1)


def relbias_reduce(dbias, *, name):
    h = dbias.shape[0]
    onehot = _rel_onehot()
    dbt = jnp.swapaxes(dbias, 0, 1)

    def body(d_ref, oh_ref, o_ref):
        @pl.when(pl.program_id(0) == 0)
        def _():
            o_ref[...] = jnp.zeros_like(o_ref)
        o_ref[...] += hdot(d_ref[...], oh_ref[...])

    return pl.pallas_call(
        body, name=name, grid=(CHUNK,),
        in_specs=[pl.BlockSpec((None, h, BAND), lambda i: (i, 0, 0)),
                  pl.BlockSpec((None, BAND, N_REL), lambda i: (i, 0, 0))],
        out_specs=pl.BlockSpec((h, N_REL), lambda i: (0, 0)),
        out_shape=jax.ShapeDtypeStruct((h, N_REL), F32), compiler_params=_cparams(1))(dbt, onehot)


def _bt(x):
    return jnp.swapaxes(x, 1, 2)


_BDN = (((2,), (1,)), ((0,), (0,)))


def _bmm_exact(x, y):
    return lax.dot_general(x, y, _BDN, precision=HI, preferred_element_type=F32)


def _split_bf16(x):
    hi = x.astype(BF16)
    return hi, (x - hi.astype(F32)).astype(BF16)


def _bmm_3pass(x, y):
    xh, xl = _split_bf16(x)
    yh, yl = _split_bf16(y)
    dot = lambda p, q: lax.dot_general(p, q, _BDN, preferred_element_type=F32)
    return dot(xh, yh) + (dot(xh, yl) + dot(xl, yh))


def _make_bmm(raw):
    @jax.custom_vjp
    def f(x, y):
        return raw(x, y)

    def fwd(x, y):
        return raw(x, y), (x, y)

    def bwd(saved, dz):
        x, y = saved
        return raw(dz, _bt(y)), raw(_bt(x), dz)

    f.defvjp(fwd, bwd)
    return f


bmm = _make_bmm(_bmm_3pass)
bmm_exact = _make_bmm(_bmm_exact)


def rwkv_chunk(p0, r, lw, k, v, a, b):
    g, c, n = r.shape
    row = lax.broadcasted_iota(jnp.int32, (g, c, c), 1)
    col = lax.broadcasted_iota(jnp.int32, (g, c, c), 2)
    incl, strict = row >= col, row > col
    cs = bmm_exact(incl.astype(F32), lw)
    cs_end = cs[:, c - 1:c, :]
    e_cs = jnp.exp(cs)
    e_neg = jnp.exp(-cs)
    at = a * jnp.exp(cs - lw)
    rt = r * e_cs
    bt_, kt = b * e_neg, k * e_neg
    e_tail = jnp.exp(cs_end - cs)
    bh, kh = b * e_tail, k * e_tail
    ar = jnp.concatenate([at, rt], axis=1)
    x_b, x_k = bmm(ar, _bt(bt_)), bmm(ar, _bt(kt))
    a_ab = jnp.where(strict, x_b[:, :c], 0.0)
    a_rb = jnp.where(incl, x_b[:, c:], 0.0)
    a_ak = jnp.where(strict, x_k[:, :c], 0.0)
    a_rk = jnp.where(incl, x_k[:, c:], 0.0)
    tinv = jnp.where(row == col, 1.0, 0.0) + a_ab
    npow = bmm(a_ab, a_ab)
    for _ in range(int(math.log2(c)) - 1):
        both = bmm(jnp.concatenate([tinv, npow], axis=1), npow)
        tinv = tinv + both[:, :c]
        npow = both[:, c:]
    xp = bmm(ar, p0)
    xv = bmm(jnp.concatenate([a_ak, a_rk], axis=1), v)
    u = bmm(tinv, xp[:, :c] + xv[:, :c])
    y = xp[:, c:] + bmm(a_rb, u) + xv[:, c:]
    rown = lax.broadcasted_iota(jnp.int32, (g, n, n), 1)
    coln = lax.broadcasted_iota(jnp.int32, (g, n, n), 2)
    dg = jnp.where(rown == coln, jnp.exp(cs_end), 0.0)
    p1 = bmm(dg, p0) + bmm(_bt(jnp.concatenate([bh, kh], axis=1)), jnp.concatenate([u, v], axis=1))
    return y, p1


def rwkv_fwd(r, lw, k, v, a, b, *, name):
    h, t, n = r.shape
    g, c = min(RW_HEADS, h), RW_CHUNK
    nch = t // c

    def body(r_ref, lw_ref, k_ref, v_ref, a_ref, b_ref, y_ref, p_ref, st_ref):
        @pl.when(pl.program_id(1) == 0)
        def _():
            st_ref[...] = jnp.zeros_like(st_ref)
        p0 = st_ref[...]
        p_ref[...] = p0
        y, p1 = rwkv_chunk(p0, r_ref[...], lw_ref[...], k_ref[...], v_ref[...], a_ref[...], b_ref[...])
        y_ref[...] = y
        st_ref[...] = p1

    blk = pl.BlockSpec((g, c, n), lambda i, j: (i, j, 0))
    pblk = pl.BlockSpec((g, None, n, n), lambda i, j: (i, j, 0, 0))
    return pl.pallas_call(
        body, name=name, grid=(h // g, nch), in_specs=[blk] * 6, out_specs=[blk, pblk],
        out_shape=[jax.ShapeDtypeStruct((h, t, n), F32), jax.ShapeDtypeStruct((h, nch, n, n), F32)],
        scratch_shapes=[pltpu.VMEM((g, n, n), F32)], compiler_params=_cparams(2))(r, lw, k, v, a, b)


def rwkv_bwd(r, lw, k, v, a, b, p0s, dy, *, name):
    h, t, n = r.shape
    g, c = min(RW_HEADS, h), RW_CHUNK
    nch = t // c

    def body(r_ref, lw_ref, k_ref, v_ref, a_ref, b_ref, p_ref, dy_ref,
             dr_ref, dlw_ref, dk_ref, dv_ref, da_ref, db_ref, dp_ref):
        @pl.when(pl.program_id(1) == 0)
        def _():
            dp_ref[...] = jnp.zeros_like(dp_ref)
        _, pull = jax.vjp(rwkv_chunk, p_ref[...], r_ref[...], lw_ref[...], k_ref[...], v_ref[...],
                          a_ref[...], b_ref[...])
        dp0, dr, dlw, dk, dv, da, db = pull((dy_ref[...], dp_ref[...]))
        dr_ref[...] = dr
        dlw_ref[...] = dlw
        dk_ref[...] = dk
        dv_ref[...] = dv
        da_ref[...] = da
        db_ref[...] = db
        dp_ref[...] = dp0

    blk = pl.BlockSpec((g, c, n), lambda i, j: (i, nch - 1 - j, 0))
    pblk = pl.BlockSpec((g, None, n, n), lambda i, j: (i, nch - 1 - j, 0, 0))
    return pl.pallas_call(
        body, name=name, grid=(h // g, nch), in_specs=[blk] * 6 + [pblk, blk], out_specs=[blk] * 6,
        out_shape=[jax.ShapeDtypeStruct((h, t, n), F32)] * 6,
        scratch_shapes=[pltpu.VMEM((g, n, n), F32)], compiler_params=_cparams(2))(r, lw, k, v, a, b, p0s, dy)


def _time_block(t):
    return _pick(t, (256, 128, 64))


def ssm_scan_fwd(bu_re, bu_im, a_re, a_im, *, name):
    t, rr, ln = bu_re.shape
    tb = _time_block(t)

    def body(br_ref, bi_ref, ar_ref, ai_ref, hr_ref, hi_ref, sr_ref, si_ref):
        @pl.when(pl.program_id(0) == 0)
        def _():
            sr_ref[...] = jnp.zeros_like(sr_ref)
            si_ref[...] = jnp.zeros_like(si_ref)
        ar, ai = ar_ref[...], ai_ref[...]

        def step(i, carry):
            hr, hi = carry
            nr = ar * hr - ai * hi + br_ref[i]
            ni = ar * hi + ai * hr + bi_ref[i]
            hr_ref[i] = nr
            hi_ref[i] = ni
            return nr, ni

        hr, hi = lax.fori_loop(0, tb, step, (sr_ref[...], si_ref[...]))
        sr_ref[...] = hr
        si_ref[...] = hi

    blk = pl.BlockSpec((tb, rr, ln), lambda i: (i, 0, 0))
    cblk = pl.BlockSpec((rr, ln), lambda i: (0, 0))
    return pl.pallas_call(
        body, name=name, grid=(t // tb,), in_specs=[blk, blk, cblk, cblk], out_specs=[blk, blk],
        out_shape=[jax.ShapeDtypeStruct((t, rr, ln), F32)] * 2,
        scratch_shapes=[pltpu.VMEM((rr, ln), F32)] * 2, compiler_params=_cparams(1))(bu_re, bu_im, a_re, a_im)


def ssm_scan_bwd(dh_re, dh_im, hp_re, hp_im, a_re, a_im, *, name):
    t, rr, ln = dh_re.shape
    tb = _time_block(t)
    nb = t // tb

    def body(dr_ref, di_ref, pr_ref, pi_ref, ar_ref, ai_ref, gr_ref, gi_ref, dar_ref, dai_ref, sr_ref, si_ref):
        @pl.when(pl.program_id(0) == 0)
        def _():
            sr_ref[...] = jnp.zeros_like(sr_ref)
            si_ref[...] = jnp.zeros_like(si_ref)
            dar_ref[...] = jnp.zeros_like(dar_ref)
            dai_ref[...] = jnp.zeros_like(dai_ref)
        ar, ai = ar_ref[...], ai_ref[...]

        def step(ii, carry):
            gr, gi, dar, dai = carry
            i = tb - 1 - ii
            nr = dr_ref[i] + ar * gr + ai * gi
            ni = di_ref[i] - ai * gr + ar * gi
            gr_ref[i] = nr
            gi_ref[i] = ni
            pr, pi = pr_ref[i], pi_ref[i]
            dar = dar + nr * pr + ni * pi
            dai = dai - nr * pi + ni * pr
            return nr, ni, dar, dai

        gr, gi, dar, dai = lax.fori_loop(0, tb, step, (sr_ref[...], si_ref[...], dar_ref[...], dai_ref[...]))
        sr_ref[...] = gr
        si_ref[...] = gi
        dar_ref[...] = dar
        dai_ref[...] = dai

    blk = pl.BlockSpec((tb, rr, ln), lambda i: (nb - 1 - i, 0, 0))
    cblk = pl.BlockSpec((rr, ln), lambda i: (0, 0))
    return pl.pallas_call(
        body, name=name, grid=(nb,), in_specs=[blk] * 4 + [cblk, cblk], out_specs=[blk, blk, cblk, cblk],
        out_shape=[jax.ShapeDtypeStruct((t, rr, ln), F32)] * 2 + [jax.ShapeDtypeStruct((rr, ln), F32)] * 2,
        scratch_shapes=[pltpu.VMEM((rr, ln), F32)] * 2,
        compiler_params=_cparams(1))(dh_re, dh_im, hp_re, hp_im, a_re, a_im)


ANY = pl.BlockSpec(memory_space=pl.ANY)


def _rows(ref, lead, ch, nchunk):
    base = ref if lead is None else ref.at[lead]
    if nchunk == 1:
        return base
    n = base.shape[0] // nchunk
    return base.at[pl.ds(ch * n, n)]


class Comm:
    def __init__(self):
        self.ins, self.outs, self.ops, self.aliases = [], [], [], {}
        self.n_remote, self.n_local = 0, 0
        self.ici = False

    def _add(self, kind, src, out_shape, n_peers, nchunk, n_local=1, alias=False):
        lead_len = src.shape[1] if kind in ("scatter", "rs1") else src.shape[0]
        while lead_len % nchunk:
            nchunk //= 2
        self.ops.append((kind, len(self.ins), len(self.outs), self.n_remote, self.n_local, nchunk))
        if alias:
            self.aliases[len(self.ins)] = len(self.outs)
        self.ins.append(src)
        self.outs.append(jax.ShapeDtypeStruct(out_shape, src.dtype))
        self.n_remote += n_peers * nchunk
        self.n_local += n_local
        self.ici = self.ici or kind in ("gather", "scatter", "gather_all", "ag1", "rs2")
        return len(self.outs) - 1

    def ag1(self, w):
        return self._add("ag1", w, (4,) + w.shape, 3, 1, n_local=LOCAL_CHUNKS)

    def ag2(self, g):
        return self._add("ag2", g, g.shape, 3, 1, n_local=0, alias=True)

    def rs1(self, g4):
        s, r, c = g4.shape
        return self._add("rs1", g4, (s, r // 2, c), 4, 1, n_local=0)

    def rs2(self, h4):
        return self._add("rs2", h4, (3,) + h4.shape[1:], 3, 1, n_local=0)

    def rs3(self, s, nchunk=4):
        return self._add("rs3", s, s.shape, 1, nchunk, n_local=0)

    def all_gather(self, w, nchunk=2):
        return self._add("gather", w, (4,) + w.shape, 3, nchunk)

    def scatter(self, g4, nchunk=2):
        return self._add("scatter", g4, g4.shape, 3, nchunk)

    def swap(self, s, nchunk=8):
        return self._add("swap", s, (2,) + s.shape, 1, nchunk)

    def gather_all(self, v, nchunk=1):
        return self._add("gather_all", v, (8,) + v.shape, 7, nchunk)

    def sem_shapes(self):
        return [pltpu.SemaphoreType.DMA((self.n_remote,)), pltpu.SemaphoreType.DMA((self.n_remote,)),
                pltpu.SemaphoreType.DMA((max(self.n_local, 1),))]

    def _two_level(self, kind, src, dst, r0, l0, nchunk, x, y, c, send, recv, lsem, sends, recvs, locs):
        chips = [(1 - x, y), (x, 1 - y), (1 - x, 1 - y)]
        me, sib = 2 * x + y, (x, y, 1 - c)

        def half(ref, hc):
            n = ref.shape[0] // 2
            return ref.at[pl.ds(hc * n, n)]

        def both(k, dev, s_ref, d_send, d_recv):
            mk = functools.partial(pltpu.make_async_remote_copy, src_ref=s_ref, send_sem=send.at[k],
                                   recv_sem=recv.at[k], device_id=dev, device_id_type=MESH)
            sends.append(mk(dst_ref=d_send))
            recvs.append(mk(dst_ref=d_recv))

        if kind == "ag1":
            n = src.shape[0] // LOCAL_CHUNKS
            for j in range(LOCAL_CHUNKS):
                rows = pl.ds(j * n, n)
                locs.append(pltpu.make_async_copy(src.at[rows], dst.at[me].at[rows], lsem.at[l0 + j]))
            for pj, (px, py) in enumerate(chips):
                both(r0 + pj, (px, py, c), half(src, c), half(dst.at[me], c), half(dst.at[2 * px + py], c))
        elif kind == "ag2":
            for pj, (px, py) in enumerate(chips):
                got = dst.at[2 * px + py]
                both(r0 + pj, sib, half(got, c), half(got, c), half(got, 1 - c))
        elif kind == "rs1":
            for q in range(4):
                both(r0 + q, sib, half(src.at[q], 1 - c), dst.at[q], dst.at[q])
        elif kind == "rs2":
            for pj, (px, py) in enumerate(chips):
                both(r0 + pj, (px, py, c), src.at[2 * px + py], dst.at[pj], dst.at[pj])
        else:
            for ch in range(nchunk):
                both(r0 + ch, sib, _rows(src, None, ch, nchunk), _rows(dst, None, ch, nchunk),
                     _rows(dst, None, ch, nchunk))

    def _descs(self, c_in, c_out, send, recv, lsem):
        x, y, c = lax.axis_index("x"), lax.axis_index("y"), lax.axis_index("c")
        sends, recvs, locs = [], [], []
        for kind, ii, oi, r0, l0, nchunk in self.ops:
            src, dst = c_in[ii], c_out[oi]
            if kind in ("ag1", "ag2", "rs1", "rs2", "rs3"):
                self._two_level(kind, src, dst, r0, l0, nchunk, x, y, c, send, recv, lsem, sends, recvs, locs)
                continue
            if kind == "swap":
                me = c
                peers = [((x, y, 1 - c), 1 - c)]
            elif kind == "gather_all":
                me = 4 * x + 2 * y + c
                flips = [(dx, dy, dc) for dx in (0, 1) for dy in (0, 1) for dc in (0, 1) if dx + dy + dc]
                peers = []
                for dx, dy, dc in flips:
                    px, py, pc = (x + dx) % 2, (y + dy) % 2, (c + dc) % 2
                    peers.append(((px, py, pc), 4 * px + 2 * py + pc))
            else:
                me = 2 * x + y
                peers = [((px, py, c), 2 * px + py) for px, py in ((1 - x, y), (x, 1 - y), (1 - x, 1 - y))]
            if kind == "scatter":
                locs.append(pltpu.make_async_copy(src.at[me], dst.at[me], lsem.at[l0]))
            else:
                locs.append(pltpu.make_async_copy(src, dst.at[me], lsem.at[l0]))
            for pj, (dev, peer_slot) in enumerate(peers):
                for ch in range(nchunk):
                    k = r0 + pj * nchunk + ch
                    s_src = _rows(src, peer_slot if kind == "scatter" else None, ch, nchunk)
                    mk = functools.partial(pltpu.make_async_remote_copy, send_sem=send.at[k], recv_sem=recv.at[k],
                                           device_id=dev, device_id_type=MESH)
                    sends.append(mk(src_ref=s_src, dst_ref=_rows(dst, me, ch, nchunk)))
                    recvs.append(mk(src_ref=s_src, dst_ref=_rows(dst, peer_slot, ch, nchunk)))
        return sends, recvs, locs

    def start(self, c_in, c_out, send, recv, lsem):
        sends, _, locs = self._descs(c_in, c_out, send, recv, lsem)
        for d in locs + sends:
            d.start()

    def wait(self, c_in, c_out, send, recv, lsem):
        sends, recvs, locs = self._descs(c_in, c_out, send, recv, lsem)
        for d in recvs:
            d.wait_recv()
        for d in sends:
            d.wait_send()
        for d in locs:
            d.wait()


def run_comm(comm, *, name):
    n_cin, n_cout = len(comm.ins), len(comm.outs)

    def body(*refs):
        c_in, c_out, sems = refs[:n_cin], refs[n_cin:n_cin + n_cout], refs[n_cin + n_cout:]
        comm.start(c_in, c_out, *sems)
        comm.wait(c_in, c_out, *sems)

    outs = pl.pallas_call(
        body, name=name, in_specs=[ANY] * n_cin, out_specs=[ANY] * n_cout, out_shape=list(comm.outs),
        scratch_shapes=comm.sem_shapes(), input_output_aliases=dict(comm.aliases),
        compiler_params=pltpu.CompilerParams(has_side_effects=True))(*comm.ins)
    return list(outs)


def _core_index():
    return jnp.reshape(lax.axis_index("c"), (1,)).astype(jnp.int32)


def _chip_index():
    return jnp.reshape(2 * lax.axis_index("x") + lax.axis_index("y"), (1,)).astype(jnp.int32)


def _row_block(rows, bytes_per_row, cap=512):
    tb = 16
    while tb * 2 <= min(rows, cap) and rows % (tb * 2) == 0 and tb * 2 * bytes_per_row <= ROW_BLOCK_BYTES:
        tb *= 2
    return tb


def add_half(g4, got, *, name):
    s, r, c = g4.shape
    r2 = r // 2
    tb = _row_block(r2, c * 8)
    nb = r2 // tb

    def body(c_ref, g_ref, x_ref, o_ref):
        o_ref[...] = (g_ref[...].astype(F32) + x_ref[...].astype(F32)).astype(o_ref.dtype)

    grid_spec = pltpu.PrefetchScalarGridSpec(
        num_scalar_prefetch=1, grid=(s, nb),
        in_specs=[pl.BlockSpec((None, tb, c), lambda q, i, cr: (q, cr[0] * nb + i, 0)),
                  pl.BlockSpec((None, tb, c), lambda q, i, cr: (q, i, 0))],
        out_specs=pl.BlockSpec((None, tb, c), lambda q, i, cr: (q, i, 0)))
    return pl.pallas_call(body, name=name, grid_spec=grid_spec, out_shape=jax.ShapeDtypeStruct((s, r2, c), g4.dtype),
                          compiler_params=_cparams(2))(_core_index(), g4, got)


def sum_chips(h4, got3, *, name):
    _, r2, c = h4.shape
    tb = _row_block(r2, c * 12)
    nb = r2 // tb

    def body(q_ref, h_ref, y_ref, o_ref):
        o_ref[...] = ((h_ref[...].astype(F32) + y_ref[0].astype(F32)) + y_ref[1].astype(F32)) + y_ref[2].astype(F32)

    grid_spec = pltpu.PrefetchScalarGridSpec(
        num_scalar_prefetch=1, grid=(nb,),
        in_specs=[pl.BlockSpec((None, tb, c), lambda i, qr: (qr[0], i, 0)),
                  pl.BlockSpec((3, tb, c), lambda i, qr: (0, i, 0))],
        out_specs=pl.BlockSpec((tb, c), lambda i, qr: (i, 0)))
    return pl.pallas_call(body, name=name, grid_spec=grid_spec, out_shape=jax.ShapeDtypeStruct((r2, c), F32),
                          compiler_params=_cparams(1))(_chip_index(), h4, got3)


def adamw_big(w, mine, theirs, m, v, *, comm=None, name):
    nl, r, c = w.shape
    r2 = r // 2
    tb = _row_block(r2, c * 4 * 10, cap=256)
    nb2 = r2 // tb
    nb = 2 * nb2
    n_cin = len(comm.ins) if comm else 0
    n_cout = len(comm.outs) if comm else 0

    def body(c_ref, w_ref, m_ref, v_ref, *rest):
        g_refs, rest = rest[:2 * nl], rest[2 * nl:]
        c_in, (go_ref, d_ref, mo_ref, vo_ref), rest = rest[:n_cin], rest[n_cin:n_cin + 4], rest[n_cin + 4:]
        c_out, sems = rest[:n_cout], rest[n_cout:]
        layer, i = pl.program_id(0), pl.program_id(1)
        if comm:
            @pl.when((layer == 0) & (i == 0))
            def _():
                comm.start(c_in, c_out, *sems)

            @pl.when((layer == nl - 1) & (i == nb - 1))
            def _():
                comm.wait(c_in, c_out, *sems)
        own = (i // nb2) == c_ref[0]
        for l0 in range(nl):
            @pl.when(layer == l0)
            def _(l0=l0):
                g = jnp.where(own, g_refs[2 * l0][...], g_refs[2 * l0 + 1][...])
                d, mn, vn = _adam_math(w_ref[...], g, m_ref[...], v_ref[...])
                go_ref[...] = g
                d_ref[...] = d
                mo_ref[...] = mn
                vo_ref[...] = vn

    blk = pl.BlockSpec((None, tb, c), lambda l, i, cr: (l, i, 0))

    def half_spec(l0):
        return pl.BlockSpec((tb, c), lambda l, i, cr: (jnp.where(l == l0, i % nb2, jnp.where(l < l0, 0, nb2 - 1)), 0))

    in_specs = [blk, blk, blk]
    args = [w, m, v]
    for l0 in range(nl):
        in_specs += [half_spec(l0), half_spec(l0)]
        args += [mine[l0], theirs[l0]]
    if comm:
        in_specs += [ANY] * n_cin
        args += list(comm.ins)
    grid_spec = pltpu.PrefetchScalarGridSpec(
        num_scalar_prefetch=1, grid=(nl, nb), in_specs=in_specs, out_specs=[blk] * 4 + [ANY] * n_cout,
        scratch_shapes=comm.sem_shapes() if comm else [])
    aliases = {1 + 3 + 2 * nl + ii: 4 + oi for ii, oi in comm.aliases.items()} if comm else {}
    outs = pl.pallas_call(
        body, name=name, grid_spec=grid_spec,
        out_shape=[jax.ShapeDtypeStruct(w.shape, F32)] * 4 + (list(comm.outs) if comm else []),
        input_output_aliases=aliases, compiler_params=_cparams(2))(_core_index(), *args)
    if comm:
        return list(outs[:4]), list(outs[4:])
    return list(outs)


def sum_slots(x, *, name):
    s, r, c = x.shape
    tb = 8
    while tb * 2 <= min(r, 512) and r % (tb * 2) == 0 and tb * 2 * c * 4 * (s + 1) <= ROW_BLOCK_BYTES:
        tb *= 2

    def body(x_ref, o_ref):
        acc = x_ref[0].astype(F32)
        for i in range(1, s):
            acc = acc + x_ref[i].astype(F32)
        o_ref[...] = acc

    return pl.pallas_call(
        body, name=name, grid=(r // tb,), in_specs=[pl.BlockSpec((s, tb, c), lambda i: (0, i, 0))],
        out_specs=pl.BlockSpec((tb, c), lambda i: (i, 0)),
        out_shape=jax.ShapeDtypeStruct((r, c), F32), compiler_params=_cparams(1))(x)


def _adam_math(w, g, m, v):
    m = ADAM_B1 * m + (1.0 - ADAM_B1) * g
    v = ADAM_B2 * v + (1.0 - ADAM_B2) * (g * g)
    m_hat = m / (1.0 - ADAM_B1 ** ADAM_STEP)
    v_hat = v / (1.0 - ADAM_B2 ** ADAM_STEP)
    delta = -ADAM_LR * (m_hat / (jnp.sqrt(v_hat) + ADAM_EPS) + ADAM_WD * w)
    return delta, m, v


def adamw_pair(w, g2, m, v, *, name):
    r, c = w.shape
    tb = 8
    while tb * 2 <= min(r, 512) and r % (tb * 2) == 0 and tb * 2 * c * 4 * 9 <= 2 * ROW_BLOCK_BYTES:
        tb *= 2

    def body(w_ref, g_ref, m_ref, v_ref, go_ref, d_ref, mo_ref, vo_ref):
        g = g_ref[0] + g_ref[1]
        d, mn, vn = _adam_math(w_ref[...], g, m_ref[...], v_ref[...])
        go_ref[...] = g
        d_ref[...] = d
        mo_ref[...] = mn
        vo_ref[...] = vn

    blk = pl.BlockSpec((tb, c), lambda i: (i, 0))
    return pl.pallas_call(
        body, name=name, grid=(r // tb,), in_specs=[blk, pl.BlockSpec((2, tb, c), lambda i: (0, i, 0)), blk, blk],
        out_specs=[blk] * 4, out_shape=[jax.ShapeDtypeStruct((r, c), F32)] * 4,
        compiler_params=_cparams(1))(w, g2, m, v)


def adamw_flat(w, g, m, v, *, name):
    def fn(w_, g_, m_, v_):
        return _adam_math(w_, g_, m_, v_)
    return rowk(fn, [w, g, m, v], [], [(w.shape[1], F32)] * 3, name=name)


def to_heads(x):
    t, d = x.shape
    return x.reshape(t, d // HEAD, HEAD).transpose(1, 0, 2)


def from_heads(x):
    h, t, n = x.shape
    return x.transpose(1, 0, 2).reshape(t, h * n)


def pack_flat(arrs, lanes=128, row_mult=512):
    flat = jnp.concatenate([a.reshape(-1).astype(F32) for a in arrs])
    n = flat.shape[0]
    rows = -(-n // lanes)
    rows = -(-rows // row_mult) * row_mult
    return jnp.pad(flat, (0, rows * lanes - n)).reshape(rows, lanes)


def unpack_flat(buf, shapes):
    flat = buf.reshape(-1)
    outs, off = [], 0
    for s in shapes:
        n = int(np.prod(s))
        outs.append(flat[off:off + n].reshape(s))
        off += n
    return outs


def block_diag_from(w_gab):
    g, a, b = w_gab.shape
    eye = jnp.eye(g, dtype=w_gab.dtype)
    return (w_gab[:, :, None, :] * eye[:, None, :, None]).reshape(g * a, g * b)


def block_diag_extract(m, g):
    a, b = m.shape[0] // g, m.shape[1] // g
    eye = jnp.eye(g, dtype=m.dtype)
    return jnp.sum(m.reshape(g, a, g, b) * eye[:, None, :, None], axis=2)


def kernel(x, p, ffn1_norm, ffn1_w_gate, ffn1_w_up, ffn1_w_down, mix_norm, ffn2_norm, ffn2_w_gate, ffn2_w_up, ffn2_w_down, ple_norm, ple_w_gate, ple_w_proj, ab_w_in, att_q_gain, att_k_gain, att_rel_bias, rwkv_mu, rwkv_w0, rwkv_w_up, rwkv_a0, rwkv_a_up, rwkv_g_up, rwkv_k_k, rwkv_k_a, rwkv_r_k, rwkv_lnx_w, rwkv_lnx_b, ab_w_out, ssm_w_in, ssm_lambda_re, ssm_lambda_im, ssm_log_dt, ssm_b_re, ssm_b_im, ssm_c_re, ssm_c_im, ssm_d, ssm_w_out, loss_target, m_ffn1_norm, m_ffn1_w_gate, m_ffn1_w_up, m_ffn1_w_down, m_mix_norm, m_ffn2_norm, m_ffn2_w_gate, m_ffn2_w_up, m_ffn2_w_down, m_ple_norm, m_ple_w_gate, m_ple_w_proj, m_ab_w_in, m_att_q_gain, m_att_k_gain, m_att_rel_bias, m_rwkv_mu, m_rwkv_w0, m_rwkv_w_up, m_rwkv_a0, m_rwkv_a_up, m_rwkv_g_up, m_rwkv_k_k, m_rwkv_k_a, m_rwkv_r_k, m_rwkv_lnx_w, m_rwkv_lnx_b, m_ab_w_out, m_ssm_w_in, m_ssm_lambda_re, m_ssm_lambda_im, m_ssm_log_dt, m_ssm_b_re, m_ssm_b_im, m_ssm_c_re, m_ssm_c_im, m_ssm_d, m_ssm_w_out, v_ffn1_norm, v_ffn1_w_gate, v_ffn1_w_up, v_ffn1_w_down, v_mix_norm, v_ffn2_norm, v_ffn2_w_gate, v_ffn2_w_up, v_ffn2_w_down, v_ple_norm, v_ple_w_gate, v_ple_w_proj, v_ab_w_in, v_att_q_gain, v_att_k_gain, v_att_rel_bias, v_rwkv_mu, v_rwkv_w0, v_rwkv_w_up, v_rwkv_a0, v_rwkv_a_up, v_rwkv_g_up, v_rwkv_k_k, v_rwkv_k_a, v_rwkv_r_k, v_rwkv_lnx_w, v_rwkv_lnx_b, v_ab_w_out, v_ssm_w_in, v_ssm_lambda_re, v_ssm_lambda_im, v_ssm_log_dt, v_ssm_b_re, v_ssm_b_im, v_ssm_c_re, v_ssm_c_im, v_ssm_d, v_ssm_w_out):
    A = dict(locals())
    W = {n: A[n] for n in W_NAMES}
    return _step(A['x'], A['p'], A['loss_target'], W, {n: A['m_' + n] for n in W_NAMES},
                 {n: A['v_' + n] for n in W_NAMES})


def _step(x, p, target, W, M, V):
    assert x.shape[0] == 1
    t, d = x.shape[1], x.shape[2]
    depth = p.shape[0]
    h0 = x[0]
    tgt = target[0]
    qchip = 2 * lax.axis_index("x") + lax.axis_index("y")
    d_rw = W['rwkv_w0'].shape[1]
    d_att = W['ab_w_out'].shape[1] * 4 - d_rw
    n_h_att, n_h_rw = d_att // HEAD, d_rw // HEAD
    n_bin = 3 * d_rw + DECAY_LORA + AAA_LORA + GATE_LORA
    d_ssm = W['ssm_w_in'].shape[2]
    n_grp = d_ssm // SSM_GROUP
    gp = n_grp * SSM_STATE

    queue = []
    gathered = {}
    halves = {}
    grads = {}
    queued_grads = set()
    n_alone = [0]

    def ag_entry(name, layer):
        def add1(cm):
            return cm.ag1(W[name][layer].astype(BF16))

        def cont1(outs, hd):
            got = outs[hd]

            def cont2(outs2, hd2):
                gathered[(name, layer)] = outs2[hd2]
            queue.insert(0, (False, lambda cm: cm.ag2(got), cont2))
        return True, add1, cont1

    def rs_entry(name, layer, g4):
        shard_shape = W[name].shape[1:]
        rows, cols = int(np.prod(shard_shape[:-1])), shard_shape[-1]
        g4 = g4.reshape(4, rows, cols)

        def cont1(outs, hd):
            h4 = add_half(g4, outs[hd], name=f"rs_add_{name}")

            def cont2(outs2, hd2):
                mine = sum_chips(h4, outs2[hd2], name=f"rs_sum_{name}")

                def cont3(outs3, hd3):
                    halves[(name, layer)] = (mine, outs3[hd3])
                queue.insert(0, (False, lambda cm: cm.rs3(mine), cont3))
            queue.insert(0, (True, lambda cm: cm.rs2(h4), cont2))
        return False, (lambda cm: cm.rs1(g4)), cont1

    def enqueue_ready():
        for n in BIG:
            for li_, g4 in enumerate(grads.get(n, [])):
                if g4 is not None and (n, li_) not in queued_grads:
                    queued_grads.add((n, li_))
                    queue.append(rs_entry(n, li_, g4))

    def take(max_ici=1):
        cm, conts, n_ici = Comm(), [], 0
        while queue and (n_ici < max_ici or not queue[0][0]):
            is_ici, add, cont = queue.pop(0)
            conts.append((cont, add(cm)))
            n_ici += int(is_ici)
        return cm, conts

    def hmm(*args, **kw):
        enqueue_ready()
        if not queue:
            return mm(*args, **kw)
        cm, conts = take()
        out, couts = mm(*args, comm=cm, **kw)
        for cont, hd in reversed(conts):
            cont(couts, hd)
        return out

    def alone(max_ici=1):
        cm, conts = take(max_ici)
        assert conts
        n_alone[0] += 1
        couts = run_comm(cm, name=f"comm_alone{n_alone[0]}")
        for cont, hd in reversed(conts):
            cont(couts, hd)

    def flush(until=None, at_least_one=False):
        enqueue_ready()
        while queue and (at_least_one or not (until is not None and until())):
            at_least_one = False
            alone()
            enqueue_ready()

    def gather(name, layer):
        flush(until=lambda: (name, layer) in gathered)
        return gathered[(name, layer)]
    small = {}

    def add_small(name, val, layer=None, nl=1):
        if layer is None:
            small[name] = val
        else:
            small.setdefault(name, [None] * nl)[layer] = val

    def ffn_fwd(h, pre, i):
        g = W[pre + '_norm'][i][None]
        wg, wu, wd = gather(pre + '_w_gate', i), gather(pre + '_w_up', i), gather(pre + '_w_down', i)
        wd2 = wd.reshape(-1, d)
        n = rowk(lambda hh, gg: f_rms(hh, gg), [h], [g], [(d, BF16)], name=f"{pre}_rms")[0]
        a = hmm(n, wg, out_dtype=BF16, name=f"{pre}_gate")
        b, u = hmm(n, wu, epi=(lambda bb, aa: (bb, f_swiglu(aa.astype(F32), bb)), [a], [BF16, BF16]),
                   name=f"{pre}_up")
        h_out = hmm(u, wd2, res=h, alpha=0.5, name=f"{pre}_down")
        return h_out, dict(h=h, g=g, n=n, a=a, b=b, u=u, wg=wg, wu=wu, wd2=wd2)

    def ffn_bwd(dh, sv, pre, i):
        f = sv['a'].shape[1]
        dwd = hmm(sv['u'], dh, ta=True, alpha=0.5, out_dtype=GRAD_XFER, name=f"{pre}_d_wdown")
        def f_du(du, aa, bb):
            _, pull = jax.vjp(f_swiglu, aa.astype(F32), bb.astype(F32))
            return pull(du)
        da, db = hmm(dh, sv['wd2'], tb=True, alpha=0.5, epi=(f_du, [sv['a'], sv['b']], [BF16, BF16]),
                     name=f"{pre}_d_u")
        dwg = hmm(sv['n'], da, ta=True, nshard=4, out3=True, out_dtype=GRAD_XFER, name=f"{pre}_d_wgate")
        dwu = hmm(sv['n'], db, ta=True, nshard=4, out3=True, out_dtype=GRAD_XFER, name=f"{pre}_d_wup")
        dn = hmm(da, sv['wg'], tb=True, name=f"{pre}_d_n1")
        dn = hmm(db, sv['wu'], tb=True, res=dn, name=f"{pre}_d_n2")
        dh_in, dg = rms_bwd(sv['h'], dn, dh, sv['g'], name=f"{pre}_d_rms")
        grads.setdefault(pre + '_w_gate', [None] * depth)[i] = dwg
        grads.setdefault(pre + '_w_up', [None] * depth)[i] = dwu
        grads.setdefault(pre + '_w_down', [None] * depth)[i] = dwd.reshape(4, -1, d)
        add_small(pre + '_norm', dg[0], i, depth)
        return dh_in

    def rms_bwd(h, dn, dh_res, g, *, name):
        def fn(hh, dnn, dres, gg):
            _, pull = jax.vjp(f_rms, hh, gg)
            dh_, dg_ = pull(dnn)
            return dh_ + dres, dg_
        return rowk(fn, [h, dn, dh_res], [g], [(d, F32)], [(1, d)], name=name)

    def head_consts(nh):
        e = np.kron(np.eye(nh, dtype=np.float32), np.ones((HEAD, 1), np.float32))
        return jnp.asarray(e), jnp.asarray(e.T)

    def mixer_ab_fwd(h):
        g = W['mix_norm'][0][None]
        win = gather('ab_w_in', 0).transpose(1, 0, 2).reshape(d, -1)
        wout = gather('ab_w_out', 0).reshape(-1, d)
        hn = rowk(lambda hh, gg: f_rms(hh, gg), [h], [g], [(d, BF16)], name="mixab_rms")[0]
        proj = hmm(hn, win, name="mixab_proj")
        q2, k2, v2 = [to_heads(proj[:, j * d_att:(j + 1) * d_att]).reshape(n_h_att * t, HEAD) for j in range(3)]
        qg, kg = W['att_q_gain'], W['att_k_gain']
        f_qn = lambda qq, gg: f_rms(qq, gg) * (HEAD ** -0.5)
        qn = rowk(f_qn, [q2], [qg], [(HEAD, BF16)], name="att_qnorm")[0].reshape(n_h_att, t, HEAD)
        kn = rowk(f_rms, [k2], [kg], [(HEAD, BF16)], name="att_knorm")[0].reshape(n_h_att, t, HEAD)
        knp = jnp.pad(kn, ((0, 0), (PAD, 0), (0, 0)))
        vp = jnp.pad(v2.astype(BF16).reshape(n_h_att, t, HEAD), ((0, 0), (PAD, 0), (0, 0)))
        bias = relbias_expand(W['att_rel_bias'][0], name="att_relbias")
        o = att_fwd(qn, knp, vp, bias, name="att_fwd")
        att = from_heads(o)
        mu = W['rwkv_mu']
        zs = ts_fwd(proj, 3 * d_att, n_bin, mu, name="rwkv_shift")
        e, et = head_consts(n_h_rw)
        zpad = jnp.zeros((AAA_LORA, d_rw), F32)
        wup_p = jnp.concatenate([W['rwkv_w_up_full'], zpad], 0)
        aup_p = jnp.concatenate([zpad, W['rwkv_a_up_full']], 0)
        pre_c = [W['rwkv_w0'], wup_p, W['rwkv_a0'], aup_p, W['rwkv_g_up_full'], W['rwkv_k_k'], W['rwkv_k_a'], e, et]
        pre = rowk(f_rwkv_pre, [zs], pre_c, [(d_rw, F32)] * 7, name="rwkv_pre")
        r_, lw_, kk_, vv_, ia_, ib_, gg_ = pre
        hm = [to_heads(u_) for u_ in (r_, lw_, kk_, vv_, ia_, ib_)]
        y_h, p0s = rwkv_fwd(*hm, name="rwkv_scan")
        y = from_heads(y_h)
        post_c = [W['rwkv_r_k'].reshape(1, d_rw), W['rwkv_lnx_w'], W['rwkv_lnx_b'], e, et]
        rw = rowk(f_rwkv_post, [y, r_, kk_, vv_, gg_], post_c, [(d_rw, BF16)], name="rwkv_post")[0]
        cat = jnp.concatenate([att, rw], axis=1)
        h_out = hmm(cat, wout, res=h, name="mixab_out")
        sv = dict(h=h, g=g, hn=hn, win=win, wout=wout, proj=proj, q2=q2, k2=k2, qn=qn, knp=knp, vp=vp, bias=bias,
                  zs=zs, pre_c=pre_c, pre=pre, hm=hm, p0s=p0s, y=y, post_c=post_c, cat=cat, qg=qg, kg=kg, mu=mu)
        return h_out, sv

    def mixer_ab_bwd(dh, sv):
        dwout = hmm(sv['cat'], dh, ta=True, out_dtype=GRAD_XFER, name="mixab_d_wout")
        grads['ab_w_out'] = [dwout.reshape(4, -1, d)]
        dcat = hmm(dh, sv['wout'], tb=True, name="mixab_d_cat")
        datt, drw = dcat[:, :d_att], dcat[:, d_att:]
        r_, lw_, kk_, vv_, ia_, ib_, gg_ = sv['pre']
        post = rowk(vjp_rows(f_rwkv_post, 5, 1), [sv['y'], r_, kk_, vv_, gg_, drw], sv['post_c'],
                    [(d_rw, F32)] * 5, [(1, d_rw)] * 3 + [sv['post_c'][3].shape, sv['post_c'][4].shape],
                    name="rwkv_d_post")
        dy, dr1, dk1, dv1, dg1 = post[:5]
        add_small('rwkv_r_k', post[5].reshape(W['rwkv_r_k'].shape))
        add_small('rwkv_lnx_w', post[6])
        add_small('rwkv_lnx_b', post[7])
        dscan = rwkv_bwd(*sv['hm'], sv['p0s'], to_heads(dy), name="rwkv_d_scan")
        dr2, dlw, dk2, dv2, dia, dib = [from_heads(u_) for u_ in dscan]

        def pre_bwd(zs, dra, drb, dlw_, dka, dkb, dva, dvb, dia_, dib_, dg_, *consts):
            _, pull = jax.vjp(f_rwkv_pre, zs, *consts)
            return pull((dra + drb, dlw_, dka + dkb, dva + dvb, dia_, dib_, dg_))

        pc = sv['pre_c']
        preb = rowk(pre_bwd, [sv['zs'], dr1, dr2, dlw, dk1, dk2, dv1, dv2, dia, dib, dg1], pc,
                    [(n_bin, F32)], [c.shape for c in pc], name="rwkv_d_pre")
        dzs = preb[0]
        add_small('rwkv_w0', preb[1])
        add_small('rwkv_w_up', preb[2][:DECAY_LORA])
        add_small('rwkv_a0', preb[3])
        add_small('rwkv_a_up', preb[4][DECAY_LORA:])
        add_small('rwkv_g_up', preb[5])
        add_small('rwkv_k_k', preb[6])
        add_small('rwkv_k_a', preb[7])
        dz, dmu = ts_bwd(sv['proj'], 3 * d_att, dzs, sv['mu'], name="rwkv_d_shift")
        add_small('rwkv_mu', dmu)
        do = to_heads(datt).astype(BF16)
        dqn, dknp, dvp, dbias = att_bwd(sv['qn'], sv['knp'], sv['vp'], sv['bias'], do, name="att_bwd")
        add_small('att_rel_bias', relbias_reduce(dbias, name="att_d_relbias")[None])
        f_qn = lambda qq, gg: f_rms(qq, gg) * (HEAD ** -0.5)
        dq2, dqg = rowk(vjp_rows(f_qn, 1, 1), [sv['q2'], dqn.reshape(-1, HEAD)], [sv['qg']], [(HEAD, F32)],
                        [(1, HEAD)], name="att_d_qnorm")
        dk2_, dkg = rowk(vjp_rows(f_rms, 1, 1), [sv['k2'], dknp[:, PAD:].reshape(-1, HEAD)], [sv['kg']],
                         [(HEAD, F32)], [(1, HEAD)], name="att_d_knorm")
        add_small('att_q_gain', dqg)
        add_small('att_k_gain', dkg)
        dproj = jnp.concatenate([from_heads(dq2.reshape(n_h_att, t, HEAD)), from_heads(dk2_.reshape(n_h_att, t, HEAD)),
                                 from_heads(dvp[:, PAD:]), dz], axis=1)
        dproj = dproj.astype(BF16)
        dwin = hmm(sv['hn'], dproj, ta=True, out_dtype=GRAD_XFER, name="mixab_d_win")
        grads['ab_w_in'] = [dwin.reshape(d, 4, -1).transpose(1, 0, 2)]
        dhn = hmm(dproj, sv['win'], tb=True, name="mixab_d_hn")
        dh_in, dg = rms_bwd(sv['h'], dhn, dh, sv['g'], name="mixab_d_rms")
        add_small('mix_norm', dg[0], 0, depth)
        return dh_in

    def ssm_params():
        lr, li = W['ssm_lambda_re'][0], W['ssm_lambda_im'][0]
        ldt = W['ssm_log_dt'][0][:, None]
        ab = rowk(f_ssm_ab, [lr, li, ldt], [], [(SSM_STATE, F32)] * 4, name="ssm_ab", tb=n_grp)
        br = W['ssm_b_re'][0].reshape(gp, SSM_GROUP)
        bi = W['ssm_b_im'][0].reshape(gp, SSM_GROUP)
        z_re, z_im = ab[2].reshape(gp, 1), ab[3].reshape(gp, 1)
        bb = rowk(f_ssm_bb, [br, bi, z_re, z_im], [], [(SSM_GROUP, F32)] * 2, name="ssm_bb", tb=gp)
        return dict(lr=lr, li=li, ldt=ldt, ab=ab, br=br, bi=bi, z_re=z_re, z_im=z_im, bb=bb)

    def mixer_s5_fwd(h):
        g = W['mix_norm'][1][None]
        win, wout = gather('ssm_w_in', 0).reshape(d, d_ssm), gather('ssm_w_out', 0)
        hn = rowk(lambda hh, gg: f_rms(hh, gg), [h], [g], [(d, BF16)], name="s5_rms")[0]
        u = hmm(hn, win, name="s5_in")
        sp = ssm_params()
        bbd_re = block_diag_from(sp['bb'][0].reshape(n_grp, SSM_STATE, SSM_GROUP).transpose(0, 2, 1)).astype(BF16)
        bbd_im = block_diag_from(sp['bb'][1].reshape(n_grp, SSM_STATE, SSM_GROUP).transpose(0, 2, 1)).astype(BF16)
        cbd_re = block_diag_from(W['ssm_c_re'][0].transpose(0, 2, 1)).astype(BF16)
        cbd_im = block_diag_from(W['ssm_c_im'][0].transpose(0, 2, 1)).astype(BF16)
        ub = u.astype(BF16)
        bu_re = hmm(ub, bbd_re, name="s5_bu_re").reshape(t, gp // 128, 128)
        bu_im = hmm(ub, bbd_im, name="s5_bu_im").reshape(t, gp // 128, 128)
        a_re, a_im = sp['ab'][0].reshape(gp // 128, 128), sp['ab'][1].reshape(gp // 128, 128)
        h_re, h_im = ssm_scan_fwd(bu_re, bu_im, a_re, a_im, name="s5_scan")
        hb_re, hb_im = h_re.reshape(t, gp).astype(BF16), h_im.reshape(t, gp).astype(BF16)
        y = hmm(hb_re, cbd_re, name="s5_y_re")
        y = hmm(hb_im, cbd_im, res=y, alpha=-1.0, name="s5_y_im")
        dsk = W['ssm_d_full']
        f_act = lambda yy, uu, dd: f_gelu(yy + dd * uu)
        yg = rowk(f_act, [y, u], [dsk], [(d_ssm, BF16)], name="s5_gelu")[0]
        z = hmm(yg, wout, name="s5_out")
        f_glu = lambda zz, hh: hh + zz[:, :d] * f_sigmoid(zz[:, d:])
        h_out = rowk(f_glu, [z, h], [], [(d, F32)], name="s5_glu")[0]
        sv = dict(h=h, g=g, hn=hn, win=win, wout=wout, u=u, ub=ub, sp=sp, bbd_re=bbd_re, bbd_im=bbd_im,
                  cbd_re=cbd_re, cbd_im=cbd_im, a_re=a_re, a_im=a_im, h_re=h_re, h_im=h_im, hb_re=hb_re,
                  hb_im=hb_im, y=y, dsk=dsk, yg=yg, z=z)
        return h_out, sv

    def mixer_s5_bwd(dh, sv):
        f_glu = lambda zz: zz[:, :d] * f_sigmoid(zz[:, d:])
        dz = rowk(vjp_rows(f_glu, 1, 1), [sv['z'], dh], [], [(2 * d, BF16)], name="s5_d_glu")[0]
        grads['ssm_w_out'] = [hmm(sv['yg'], dz, ta=True, nshard=4, out3=True, out_dtype=GRAD_XFER, name="s5_d_wout")]
        dyg = hmm(dz, sv['wout'], tb=True, name="s5_d_yg")
        f_act = lambda yy, uu, dd: f_gelu(yy + dd * uu)
        dy, du1, ddsk = rowk(vjp_rows(f_act, 2, 1), [sv['y'], sv['u'], dyg], [sv['dsk']],
                             [(d_ssm, F32), (d_ssm, F32)], [(1, d_ssm)], name="s5_d_gelu")
        add_small('ssm_d', ddsk)
        dyb = dy.astype(BF16)
        dcbd_re = hmm(sv['hb_re'], dyb, ta=True, name="s5_d_c_re")
        dcbd_im = hmm(sv['hb_im'], dyb, ta=True, alpha=-1.0, name="s5_d_c_im")
        add_small('ssm_c_re', block_diag_extract(dcbd_re, n_grp).transpose(0, 2, 1)[None])
        add_small('ssm_c_im', block_diag_extract(dcbd_im, n_grp).transpose(0, 2, 1)[None])
        dh_re = hmm(dyb, sv['cbd_re'], tb=True, name="s5_d_h_re").reshape(t, gp // 128, 128)
        dh_im = hmm(dyb, sv['cbd_im'], tb=True, alpha=-1.0, name="s5_d_h_im").reshape(t, gp // 128, 128)
        hp_re = jnp.pad(sv['h_re'][:-1], ((1, 0), (0, 0), (0, 0)))
        hp_im = jnp.pad(sv['h_im'][:-1], ((1, 0), (0, 0), (0, 0)))
        g_re, g_im, da_re, da_im = ssm_scan_bwd(dh_re, dh_im, hp_re, hp_im, sv['a_re'], sv['a_im'], name="s5_d_scan")
        gb_re, gb_im = g_re.reshape(t, gp).astype(BF16), g_im.reshape(t, gp).astype(BF16)
        dbbd_re = hmm(sv['ub'], gb_re, ta=True, name="s5_d_bb_re")
        dbbd_im = hmm(sv['ub'], gb_im, ta=True, name="s5_d_bb_im")
        du = hmm(gb_re, sv['bbd_re'], tb=True, res=du1, name="s5_d_u_re")
        du = hmm(gb_im, sv['bbd_im'], tb=True, res=du, name="s5_d_u_im")
        sp = sv['sp']
        dbb_re = block_diag_extract(dbbd_re, n_grp).transpose(0, 2, 1).reshape(gp, SSM_GROUP)
        dbb_im = block_diag_extract(dbbd_im, n_grp).transpose(0, 2, 1).reshape(gp, SSM_GROUP)
        dbr, dbi, dz_re, dz_im = rowk(vjp_rows(f_ssm_bb, 4, 2),
                                      [sp['br'], sp['bi'], sp['z_re'], sp['z_im'], dbb_re, dbb_im], [],
                                      [(SSM_GROUP, F32)] * 2 + [(1, F32)] * 2, name="ssm_d_bb", tb=gp)
        add_small('ssm_b_re', dbr.reshape(W['ssm_b_re'].shape))
        add_small('ssm_b_im', dbi.reshape(W['ssm_b_im'].shape))
        dlr, dli, dldt = rowk(vjp_rows(f_ssm_ab, 3, 4),
                              [sp['lr'], sp['li'], sp['ldt'], da_re.reshape(n_grp, SSM_STATE),
                               da_im.reshape(n_grp, SSM_STATE), dz_re.reshape(n_grp, SSM_STATE),
                               dz_im.reshape(n_grp, SSM_STATE)], [],
                              [(SSM_STATE, F32)] * 2 + [(1, F32)], name="ssm_d_ab", tb=n_grp)
        add_small('ssm_lambda_re', dlr[None])
        add_small('ssm_lambda_im', dli[None])
        add_small('ssm_log_dt', dldt.reshape(1, n_grp))
        grads['ssm_w_in'] = [hmm(sv['hn'], du, ta=True, out_dtype=GRAD_XFER, name="s5_d_win").reshape(4, -1, d_ssm)]
        dhn = hmm(du, sv['win'], tb=True, name="s5_d_hn")
        dh_in, dg = rms_bwd(sv['h'], dhn, dh, sv['g'], name="s5_d_rms")
        add_small('mix_norm', dg[0], 1, depth)
        return dh_in

    def ple_fwd(h, i):
        g = W['ple_norm'][i][None]
        wpg = gather('ple_w_gate', i).reshape(d, d)
        wpp = gather('ple_w_proj', i)
        n = rowk(lambda hh, gg: f_rms(hh, gg), [h], [g], [(d, BF16)], name="ple_rms")[0]
        zg = hmm(n, wpg, name="ple_gate")
        pb = p[i, 0].astype(BF16)
        pp = hmm(pb, wpp, name="ple_proj")
        f_ple = lambda zz, pq, hh: hh + f_sigmoid(zz) * pq
        h_out = rowk(f_ple, [zg, pp, h], [], [(d, F32)], name="ple_mix")[0]
        return h_out, dict(h=h, g=g, n=n, zg=zg, pp=pp, pb=pb, wpg=wpg, wpp=wpp)

    def ple_bwd(dh, sv, i):
        f_ple = lambda zz, pq: f_sigmoid(zz) * pq
        dzg, dpp = rowk(vjp_rows(f_ple, 2, 1), [sv['zg'], sv['pp'], dh], [], [(d, BF16), (d, BF16)],
                        name="ple_d_mix")
        grads.setdefault('ple_w_proj', [None] * depth)[i] = hmm(sv['pb'], dpp, ta=True, nshard=4, out3=True,
                                                               out_dtype=GRAD_XFER, name="ple_d_wproj")
        grads.setdefault('ple_w_gate', [None] * depth)[i] = hmm(sv['n'], dzg, ta=True, out_dtype=GRAD_XFER,
                                                               name="ple_d_wgate").reshape(4, -1, d)
        dn = hmm(dzg, sv['wpg'], tb=True, name="ple_d_n")
        dh_in, dg = rms_bwd(sv['h'], dn, dh, sv['g'], name="ple_d_rms")
        add_small('ple_norm', dg[0], i, depth)
        return dh_in

    ag_order = []
    for i in range(depth):
        ag_order += [('ffn1_w_gate', i), ('ffn1_w_up', i), ('ffn1_w_down', i)]
        ag_order += [('ab_w_in', 0), ('ab_w_out', 0)] if i % 2 == 0 else [('ssm_w_in', 0), ('ssm_w_out', 0)]
        ag_order += [('ffn2_w_gate', i), ('ffn2_w_up', i), ('ffn2_w_down', i), ('ple_w_gate', i), ('ple_w_proj', i)]
    first = Comm()
    for n in SMALL_SHARDED:
        first.all_gather(W[n], nchunk=1)
    first_out = run_comm(first, name="ag_small")
    W = dict(W)
    for n, full in zip(SMALL_SHARDED, first_out):
        w = W[n]
        W[n + '_full'] = jnp.moveaxis(full, 0, -2).reshape(w.shape[1:-1] + (4 * w.shape[-1],))
    W['ssm_d_full'] = W['ssm_d_full'][None]
    queue.extend(ag_entry(n, li_) for n, li_ in ag_order)
    alone(2)
    alone(0)

    h = h0
    saved = []
    for i in range(depth):
        sv = {}
        h, sv['ffn1'] = ffn_fwd(h, 'ffn1', i)
        if i % 2 == 0:
            h, sv['mix'] = mixer_ab_fwd(h)
        else:
            h, sv['mix'] = mixer_s5_fwd(h)
        h, sv['ffn2'] = ffn_fwd(h, 'ffn2', i)
        h, sv['ple'] = ple_fwd(h, i)
        saved.append(sv)

    def f_loss(y, tg):
        e = y - tg
        part = 0.5 * jnp.sum(jnp.sum(e * e, axis=-1, keepdims=True) * (1.0 / d), axis=0, keepdims=True)
        return e * (1.0 / d), jnp.broadcast_to(part, (1, 128))
    dh, loss_part = rowk(f_loss, [h, tgt], [], [(d, F32)], [(1, 128)], name="loss")
    loss = lax.psum(loss_part[0, 0], ("x", "y", "c"))

    for i in reversed(range(depth)):
        sv = saved[i]
        dh = ple_bwd(dh, sv['ple'], i)
        dh = ffn_bwd(dh, sv['ffn2'], 'ffn2', i)
        if i % 2 == 0:
            dh = mixer_ab_bwd(dh, sv['mix'])
        else:
            dh = mixer_s5_bwd(dh, sv['mix'])
        dh = ffn_bwd(dh, sv['ffn1'], 'ffn1', i)
    grad_x = dh[None]

    small_names = [n for n in W_NAMES if n not in BIG]
    small_full = []
    for n in small_names:
        v_ = small[n]
        if isinstance(v_, list):
            v_ = jnp.stack(v_)
        full_shape = W[n].shape[:-1] + (4 * W[n].shape[-1],) if n in SMALL_SHARDED else W[n].shape
        small_full.append(v_.reshape(full_shape))
    packed = pack_flat(small_full)
    ar_got = []
    queue.append((True, lambda cm: cm.gather_all(packed), lambda outs, hd: ar_got.append(outs[hd])))

    out = {}
    todo = list(BIG)
    while todo:
        enqueue_ready()
        ready = [n for n in todo if all((n, li_) in halves for li_ in range(W[n].shape[0]))]
        if not ready:
            assert queue, todo
            flush(until=lambda: True, at_least_one=True)
            continue
        n = ready[0]
        todo.remove(n)
        nl = W[n].shape[0]
        rows, cols = int(np.prod(W[n].shape[1:-1])), W[n].shape[-1]
        a_args = (W[n].reshape(nl, rows, cols), [halves[(n, li_)][0] for li_ in range(nl)],
                  [halves[(n, li_)][1] for li_ in range(nl)], M[n].reshape(nl, rows, cols),
                  V[n].reshape(nl, rows, cols))
        if queue:
            cm, conts = take()
            res, couts = adamw_big(*a_args, comm=cm, name=f"adamw_{n}")
            for cont, hd in reversed(conts):
                cont(couts, hd)
        else:
            res = adamw_big(*a_args, name=f"adamw_{n}")
        for kind, a in zip(('grad', 'delta', 'm', 'v'), res):
            out[(kind, n)] = a.reshape(W[n].shape)
    flush()

    summed = sum_slots(ar_got[0], name="ar_small_sum")
    small_tot = unpack_flat(summed, [a.shape for a in small_full])
    g_small = {}
    for n, a in zip(small_names, small_tot):
        if n in SMALL_SHARDED:
            ns = W[n].shape[-1]
            a = lax.dynamic_slice_in_dim(a, qchip * ns, ns, axis=a.ndim - 1)
        g_small[n] = a

    pk = lambda dct: pack_flat([dct[n] for n in small_names])
    res = adamw_flat(pk(W), pk(g_small), pk(M), pk(V), name="adamw_small")
    shapes = [W[n].shape for n in small_names]
    for kind, buf in zip(('delta', 'm', 'v'), res):
        for n, a in zip(small_names, unpack_flat(buf, shapes)):
            out[(kind, n)] = a
    for n in small_names:
        out[('grad', n)] = g_small[n]

    return (loss, grad_x, *[out[('grad', n)] for n in W_NAMES], *[out[('delta', n)] for n in W_NAMES],
            *[out[('m', n)] for n in W_NAMES], *[out[('v', n)] for n in W_NAMES])
```

```python
import functools
import math

import numpy as np
import jax
import jax.numpy as jnp
from jax import lax
from jax.experimental import pallas as pl
from jax.experimental.pallas import tpu as pltpu

F32 = jnp.float32
BF16 = jnp.bfloat16
HI = lax.Precision.HIGHEST
MESH = pl.DeviceIdType.MESH

CHUNK = 64
N_LEFT = 8
BAND = (N_LEFT + 1) * CHUNK
PAD = N_LEFT * CHUNK
HEAD = 64
REL_CLIP = 128
N_REL = (CHUNK - 1) + REL_CLIP + 1
DECAY_LORA = 64
AAA_LORA = 64
GATE_LORA = 128
SSM_GROUP = 16
SSM_STATE = 64
RMS_EPS = 1e-6
GN_EPS = 64e-5
NEG_BIG = -1e30

ADAM_LR = 0.001
ADAM_B1 = 0.9
ADAM_B2 = 0.999
ADAM_EPS = 1e-08
ADAM_WD = 0.01
ADAM_STEP = 10

RW_CHUNK = 64
RW_HEADS = 16
ATT_HEADS = 2
ATT_HEADS_FWD = 4
VMEM_LIMIT = 56 * 1024 * 1024
ROW_BLOCK_BYTES = 12 * 1024 * 1024
GRAD_XFER = BF16
LOCAL_CHUNKS = 4

W_NAMES = ['ffn1_norm', 'ffn1_w_gate', 'ffn1_w_up', 'ffn1_w_down', 'mix_norm', 'ffn2_norm', 'ffn2_w_gate',
           'ffn2_w_up', 'ffn2_w_down', 'ple_norm', 'ple_w_gate', 'ple_w_proj', 'ab_w_in', 'att_q_gain',
           'att_k_gain', 'att_rel_bias', 'rwkv_mu', 'rwkv_w0', 'rwkv_w_up', 'rwkv_a0', 'rwkv_a_up',
           'rwkv_g_up', 'rwkv_k_k', 'rwkv_k_a', 'rwkv_r_k', 'rwkv_lnx_w', 'rwkv_lnx_b', 'ab_w_out',
           'ssm_w_in', 'ssm_lambda_re', 'ssm_lambda_im', 'ssm_log_dt', 'ssm_b_re', 'ssm_b_im', 'ssm_c_re',
           'ssm_c_im', 'ssm_d', 'ssm_w_out']
BIG = ['ffn1_w_gate', 'ffn1_w_up', 'ffn1_w_down', 'ffn2_w_gate', 'ffn2_w_up', 'ffn2_w_down',
       'ple_w_gate', 'ple_w_proj', 'ab_w_in', 'ab_w_out', 'ssm_w_in', 'ssm_w_out']
SMALL_SHARDED = ['rwkv_w_up', 'rwkv_a_up', 'rwkv_g_up', 'ssm_d']


def _cparams(n_axes):
    return pltpu.CompilerParams(dimension_semantics=("arbitrary",) * n_axes, vmem_limit_bytes=VMEM_LIMIT)


def _pick(n, prefs):
    for p in prefs:
        if n % p == 0:
            return p
    return n


def mm(a, b, *, ta=False, tb=False, nshard=None, out3=False, out_dtype=F32, res=None, alpha=1.0, comm=None,
       epi=None, name):
    a3, b3 = a.ndim == 3, b.ndim == 3
    if a3:
        assert not ta
        sk, m, ks = a.shape
        k = sk * ks
    elif ta:
        k, m = a.shape
    else:
        m, k = a.shape
    kshard = None
    if b3 and not tb:
        s, kb, ns = b.shape
        n = s * ns
        nshard = s
    elif b3 and tb:
        sk2, n, ks2 = b.shape
        kb = sk2 * ks2
        kshard = (sk2, ks2)
    elif tb:
        n, kb = b.shape
    else:
        kb, n = b.shape
    assert k == kb, (a.shape, b.shape, ta, tb)
    if a3:
        assert kshard is None or kshard == (sk, ks)
        kshard = (sk, ks)
    if nshard is not None:
        tn, nj = n // nshard, nshard
    else:
        assert not out3
        tn = _pick(n, (1024, 1408, 1280, 512, 640, 256, 128))
        nj = n // tn
    if kshard is not None:
        nk, tk = kshard
    else:
        tk = _pick(k, (2048, 1408, 1280, 1024, 640, 512, 256, 128))
        nk = k // tk
    tm = _pick(m, (256, 128) if epi else (512, 256, 128))
    ni = m // tm

    if a3:
        a_spec = pl.BlockSpec((None, tm, tk), lambda i, j, kk: (kk, i, 0))
    elif ta:
        a_spec = pl.BlockSpec((tk, tm), lambda i, j, kk: (kk, i))
    else:
        a_spec = pl.BlockSpec((tm, tk), lambda i, j, kk: (i, kk))
    if b3 and not tb:
        b_spec = pl.BlockSpec((None, tk, tn), lambda i, j, kk: (j, kk, 0))
    elif b3 and tb:
        b_spec = pl.BlockSpec((None, tn, tk), lambda i, j, kk: (kk, j, 0))
    elif tb:
        b_spec = pl.BlockSpec((tn, tk), lambda i, j, kk: (j, kk))
    else:
        b_spec = pl.BlockSpec((tk, tn), lambda i, j, kk: (kk, j))
    if out3:
        o_spec = pl.BlockSpec((None, tm, tn), lambda i, j, kk: (j, i, 0))
        o_shape = (nj, m, tn)
    else:
        o_spec = pl.BlockSpec((tm, tn), lambda i, j, kk: (i, j))
        o_shape = (m, n)
    has_res = res is not None
    dn = (((0 if ta else 1,), (1 if tb else 0,)), ((), ()))

    n_cin = len(comm.ins) if comm else 0
    n_cout = len(comm.outs) if comm else 0
    epi_fn, epi_extra, epi_dtypes = epi if epi else (None, [], [out_dtype])
    assert not (epi and out3)
    n_extra, n_out = len(epi_extra), len(epi_dtypes)
    n_in = 2 + has_res + n_extra

    def body(*refs):
        a_ref, b_ref = refs[0], refs[1]
        r_ref = refs[2] if has_res else None
        e_refs = refs[2 + has_res:n_in]
        c_in = refs[n_in:n_in + n_cin]
        o_refs = refs[n_in + n_cin:n_in + n_cin + n_out]
        c_out = refs[n_in + n_cin + n_out:n_in + n_cin + n_out + n_cout]
        scratch = refs[n_in + n_cin + n_out + n_cout:]
        acc_ref = scratch[0] if nk > 1 else None
        sems = scratch[1:] if nk > 1 else scratch
        i, j, kk = pl.program_id(0), pl.program_id(1), pl.program_id(2)

        if comm:
            @pl.when((i == 0) & (j == 0) & (kk == 0))
            def _():
                comm.start(c_in, c_out, *sems)

        def finish(acc):
            val = acc * alpha if alpha != 1.0 else acc
            if has_res:
                val = val + r_ref[...].astype(F32)
            vals = epi_fn(val, *[e[...] for e in e_refs]) if epi else (val,)
            for o_ref, v in zip(o_refs, vals):
                o_ref[...] = v.astype(o_ref.dtype)

        part = lax.dot_general(a_ref[...].astype(BF16), b_ref[...].astype(BF16), dn, preferred_element_type=F32)
        if nk == 1:
            finish(part)
        else:
            @pl.when(kk == 0)
            def _():
                acc_ref[...] = part

            @pl.when(kk > 0)
            def _():
                acc_ref[...] += part

            @pl.when(kk == nk - 1)
            def _():
                finish(acc_ref[...])

        if comm:
            @pl.when((i == ni - 1) & (j == nj - 1) & (kk == nk - 1))
            def _():
                comm.wait(c_in, c_out, *sems)

    in_specs = [a_spec, b_spec] + [o_spec] * (has_res + n_extra) + [ANY] * n_cin
    args = (a, b) + ((res,) if has_res else ()) + tuple(epi_extra) + (tuple(comm.ins) if comm else ())
    out_specs = [o_spec] * n_out + [ANY] * n_cout
    out_shape = [jax.ShapeDtypeStruct(o_shape, dt) for dt in epi_dtypes] + (list(comm.outs) if comm else [])
    scratch_shapes = ([pltpu.VMEM((tm, tn), F32)] if nk > 1 else []) + (comm.sem_shapes() if comm else [])
    aliases = {n_in + ii: n_out + oi for ii, oi in comm.aliases.items()} if comm else {}
    outs = pl.pallas_call(
        body, name=name, grid=(ni, nj, nk), in_specs=in_specs, out_specs=out_specs, out_shape=out_shape,
        scratch_shapes=scratch_shapes, input_output_aliases=aliases, compiler_params=_cparams(3))(*args)
    main = list(outs[:n_out]) if epi else outs[0]
    if comm:
        return main, list(outs[n_out:])
    return main


def rowk(fn, rows, consts, out_rows, out_accs=(), *, name, tb=None):
    t = rows[0].shape[0]
    nr, nc, no, na = len(rows), len(consts), len(out_rows), len(out_accs)
    if tb is None:
        per_row = sum(r.shape[1] * 4 for r in rows) + sum(n * 4 for n, _ in out_rows)
        tb = 8
        while tb * 2 <= min(t, 1024) and tb * 2 * per_row <= ROW_BLOCK_BYTES and t % (tb * 2) == 0:
            tb *= 2
    assert t % tb == 0
    nb = t // tb

    def body(*refs):
        r_in, c_in = refs[:nr], refs[nr:nr + nc]
        o_rows, o_accs = refs[nr + nc:nr + nc + no], refs[nr + nc + no:]
        outs = fn(*[r[...] for r in r_in], *[c[...] for c in c_in])
        if not isinstance(outs, (tuple, list)):
            outs = (outs,)
        assert len(outs) == no + na, (name, len(outs), no, na)
        for ref, v in zip(o_rows, outs[:no]):
            ref[...] = v.astype(ref.dtype)
        if na:
            @pl.when(pl.program_id(0) == 0)
            def _():
                for ref in o_accs:
                    ref[...] = jnp.zeros_like(ref)
            for ref, v in zip(o_accs, outs[no:]):
                ref[...] += v.astype(F32)

    in_specs = [pl.BlockSpec((tb, r.shape[1]), lambda i: (i, 0)) for r in rows]
    in_specs += [pl.BlockSpec(c.shape, lambda i, nd=c.ndim: (0,) * nd) for c in consts]
    out_specs = [pl.BlockSpec((tb, n), lambda i: (i, 0)) for n, _ in out_rows]
    out_specs += [pl.BlockSpec(s, lambda i, nd=len(s): (0,) * nd) for s in out_accs]
    out_shape = [jax.ShapeDtypeStruct((t, n), d) for n, d in out_rows]
    out_shape += [jax.ShapeDtypeStruct(s, F32) for s in out_accs]
    res = pl.pallas_call(body, name=name, grid=(nb,), in_specs=in_specs, out_specs=out_specs,
                         out_shape=out_shape, compiler_params=_cparams(1))(*rows, *consts)
    return res


def _f32(*xs):
    return [x.astype(F32) for x in xs]


def vjp_rows(f, n_rows, n_cots):
    def fn(*args):
        rows = _f32(*args[:n_rows])
        cots = _f32(*args[n_rows:n_rows + n_cots])
        consts = _f32(*args[n_rows + n_cots:])
        outs, pull = jax.vjp(f, *rows, *consts)
        if not isinstance(outs, (tuple, list)):
            cots = cots[0]
        else:
            cots = tuple(cots)
        return pull(cots)
    return fn


def hdot(x, y):
    return jnp.dot(x, y, precision=HI, preferred_element_type=F32)


def f_rms(h, g):
    return h * lax.rsqrt(jnp.mean(h * h, axis=-1, keepdims=True) + RMS_EPS) * g


def f_sigmoid(x):
    return 1.0 / (1.0 + jnp.exp(-x))


def f_swiglu(a, b):
    return a * f_sigmoid(a) * b


def f_softplus(x):
    return jnp.maximum(x, 0.0) + jnp.log(1.0 + jnp.exp(-jnp.abs(x)))


def f_gelu(x):
    return 0.5 * x * (1.0 + jnp.tanh(math.sqrt(2.0 / math.pi) * (x + 0.044715 * (x * x * x))))


def f_rwkv_pre(zs, w0, wup_p, a0, aup_p, g_up, k_k, k_a, e, et):
    d = w0.shape[1]
    r, k, v = zs[:, :d], zs[:, d:2 * d], zs[:, 2 * d:3 * d]
    xwa = zs[:, 3 * d:3 * d + DECAY_LORA + AAA_LORA]
    xg = zs[:, 3 * d + DECAY_LORA + AAA_LORA:]
    w_log = -f_softplus(-(w0 + hdot(jnp.tanh(xwa), wup_p))) - 0.5
    logw = -jnp.exp(w_log)
    a = f_sigmoid(a0 + hdot(xwa, aup_p))
    g = hdot(f_sigmoid(xg), g_up)
    kk = k * k_k
    nrm = jnp.maximum(jnp.sqrt(hdot(kk * kk, e)), 1e-12)
    kk = kk * hdot(1.0 / nrm, et)
    k2 = k * (1.0 + (a - 1.0) * k_a)
    return r, logw, k2, v, -kk, kk * a, g


def f_rwkv_post(y, r, k2, v, g, r_k, lnx_w, lnx_b, e, et):
    inv = 1.0 / HEAD
    mean = hdot(hdot(y, e) * inv, et)
    yc = y - mean
    var = hdot(yc * yc, e) * inv
    yn = yc * hdot(lax.rsqrt(var + GN_EPS), et) * lnx_w + lnx_b
    bonus = hdot(hdot(r * k2 * r_k, e), et) * v
    return (yn + bonus) * g


def f_ssm_ab(lr, li, log_dt):
    dt = jnp.exp(log_dt)
    mag = jnp.exp(lr * dt)
    ab_re, ab_im = mag * jnp.cos(li * dt), mag * jnp.sin(li * dt)
    denom = lr * lr + li * li
    z_re = ((ab_re - 1.0) * lr + ab_im * li) / denom
    z_im = (ab_im * lr - (ab_re - 1.0) * li) / denom
    return ab_re, ab_im, z_re, z_im


def f_ssm_bb(br, bi, z_re, z_im):
    return z_re * br - z_im * bi, z_re * bi + z_im * br


def _col_block(n):
    return _pick(n, (256, 128))


def ts_fwd(proj, col0, width, mu, *, name):
    t = proj.shape[0]
    cb = _col_block(width)
    assert col0 % cb == 0 and width % cb == 0
    off = col0 // cb

    def body(z_ref, mu_ref, o_ref):
        z = z_ref[...]
        row = lax.broadcasted_iota(jnp.int32, z.shape, 0)
        prev = jnp.where(row == 0, 0.0, pltpu.roll(z, 1, 0))
        o_ref[...] = z + (prev - z) * mu_ref[...]

    return pl.pallas_call(
        body, name=name, grid=(width // cb,),
        in_specs=[pl.BlockSpec((t, cb), lambda j: (0, j + off)), pl.BlockSpec((1, cb), lambda j: (0, j))],
        out_specs=pl.BlockSpec((t, cb), lambda j: (0, j)),
        out_shape=jax.ShapeDtypeStruct((t, width), F32), compiler_params=_cparams(1))(proj, mu)


def ts_bwd(proj, col0, dzs, mu, *, name):
    t, width = dzs.shape
    cb = _col_block(width)
    off = col0 // cb

    def body(z_ref, d_ref, mu_ref, dz_ref, dmu_ref):
        z, d, m = z_ref[...], d_ref[...], mu_ref[...]
        row = lax.broadcasted_iota(jnp.int32, z.shape, 0)
        prev = jnp.where(row == 0, 0.0, pltpu.roll(z, 1, 0))
        dm = d * m
        nxt = jnp.where(row == t - 1, 0.0, pltpu.roll(dm, t - 1, 0))
        dz_ref[...] = d - dm + nxt
        dmu_ref[...] = jnp.sum(d * (prev - z), axis=0, keepdims=True)

    return pl.pallas_call(
        body, name=name, grid=(width // cb,),
        in_specs=[pl.BlockSpec((t, cb), lambda j: (0, j + off)), pl.BlockSpec((t, cb), lambda j: (0, j)),
                  pl.BlockSpec((1, cb), lambda j: (0, j))],
        out_specs=[pl.BlockSpec((t, cb), lambda j: (0, j)), pl.BlockSpec((1, cb), lambda j: (0, j))],
        out_shape=[jax.ShapeDtypeStruct((t, width), F32), jax.ShapeDtypeStruct((1, width), F32)],
        compiler_params=_cparams(1))(proj, dzs, mu)


def _att_scores(qn, kb, bias, c):
    s = jnp.einsum('hqd,hkd->hqk', qn, kb, preferred_element_type=F32) + bias
    col = lax.broadcasted_iota(jnp.int32, s.shape, 2)
    s = jnp.where(col >= PAD - c * CHUNK, s, NEG_BIG)
    s = s - jnp.max(s, axis=-1, keepdims=True)
    e = jnp.exp(s)
    return e / jnp.sum(e, axis=-1, keepdims=True)


def att_fwd(qn, knp, vp, bias, *, name):
    h, t, _ = qn.shape
    hb = min(ATT_HEADS_FWD, h)
    nc = t // CHUNK

    def body(q_ref, k_ref, v_ref, b_ref, o_ref):
        c = pl.program_id(1)
        start = pl.multiple_of(c * CHUNK, CHUNK)
        kb = k_ref[:, pl.ds(start, BAND), :]
        vb = v_ref[:, pl.ds(start, BAND), :]
        p = _att_scores(q_ref[...], kb, b_ref[...], c)
        o_ref[...] = jnp.einsum('hqk,hkd->hqd', p.astype(BF16), vb, preferred_element_type=F32).astype(o_ref.dtype)

    return pl.pallas_call(
        body, name=name, grid=(h // hb, nc),
        in_specs=[pl.BlockSpec((hb, CHUNK, HEAD), lambda g, c: (g, c, 0)),
                  pl.BlockSpec((hb, t + PAD, HEAD), lambda g, c: (g, 0, 0)),
                  pl.BlockSpec((hb, t + PAD, HEAD), lambda g, c: (g, 0, 0)),
                  pl.BlockSpec((hb, CHUNK, BAND), lambda g, c: (g, 0, 0))],
        out_specs=pl.BlockSpec((hb, CHUNK, HEAD), lambda g, c: (g, c, 0)),
        out_shape=jax.ShapeDtypeStruct((h, t, HEAD), BF16), compiler_params=_cparams(2))(qn, knp, vp, bias)


def att_bwd(qn, knp, vp, bias, do, *, name):
    h, t, _ = qn.shape
    hb = ATT_HEADS
    nc = t // CHUNK

    def body(q_ref, k_ref, v_ref, b_ref, do_ref, dq_ref, dk_ref, dv_ref, db_ref):
        c = pl.program_id(1)

        @pl.when(c == 0)
        def _():
            dk_ref[...] = jnp.zeros_like(dk_ref)
            dv_ref[...] = jnp.zeros_like(dv_ref)
            db_ref[...] = jnp.zeros_like(db_ref)

        start = pl.multiple_of(c * CHUNK, CHUNK)
        qv = q_ref[...]
        kb = k_ref[:, pl.ds(start, BAND), :]
        vb = v_ref[:, pl.ds(start, BAND), :]
        p = _att_scores(qv, kb, b_ref[...], c)
        dov = do_ref[...]
        dp = jnp.einsum('hqd,hkd->hqk', dov, vb, preferred_element_type=F32)
        ds = p * (dp - jnp.sum(p * dp, axis=-1, keepdims=True))
        db_ref[...] += ds
        dsb = ds.astype(BF16)
        dq_ref[...] = jnp.einsum('hqk,hkd->hqd', dsb, kb, preferred_element_type=F32)
        dst = jnp.swapaxes(dsb, 1, 2)
        pt = jnp.swapaxes(p.astype(BF16), 1, 2)
        dk_ref[:, pl.ds(start, BAND), :] += jnp.einsum('hkq,hqd->hkd', dst, qv, preferred_element_type=F32)
        dv_ref[:, pl.ds(start, BAND), :] += jnp.einsum('hkq,hqd->hkd', pt, dov, preferred_element_type=F32)

    blk_q = pl.BlockSpec((hb, CHUNK, HEAD), lambda g, c: (g, c, 0))
    blk_k = pl.BlockSpec((hb, t + PAD, HEAD), lambda g, c: (g, 0, 0))
    blk_b = pl.BlockSpec((hb, CHUNK, BAND), lambda g, c: (g, 0, 0))
    return pl.pallas_call(
        body, name=name, grid=(h // hb, nc),
        in_specs=[blk_q, blk_k, blk_k, blk_b, blk_q],
        out_specs=[blk_q, blk_k, blk_k, blk_b],
        out_shape=[jax.ShapeDtypeStruct((h, t, HEAD), F32), jax.ShapeDtypeStruct((h, t + PAD, HEAD), F32),
                   jax.ShapeDtypeStruct((h, t + PAD, HEAD), F32), jax.ShapeDtypeStruct((h, CHUNK, BAND), F32)],
        compiler_params=_cparams(2))(qn, knp, vp, bias, do)


def _rel_index():
    i = np.arange(CHUNK)[:, None]
    j = np.arange(BAND)[None, :]
    return np.clip(i + PAD - j, -(CHUNK - 1), REL_CLIP) + (CHUNK - 1)


def _rel_onehot():
    return (jnp.asarray(_rel_index())[:, :, None] == jnp.arange(N_REL)[None, None, :]).astype(F32)


def relbias_expand(rel, *, name):
    h = rel.shape[0]
    onehot_t = jnp.swapaxes(_rel_onehot(), 1, 2)

    def body(r_ref, oh_ref, o_ref):
        o_ref[...] = hdot(r_ref[...], oh_ref[...])

    out = pl.pallas_call(
        body, name=name, grid=(CHUNK,),
        in_specs=[pl.BlockSpec((h, N_REL), lambda i: (0, 0)), pl.BlockSpec((None, N_REL, BAND), lambda i: (i, 0, 0))],
        out_specs=pl.BlockSpec((None, h, BAND), lambda i: (i, 0, 0)),
        out_shape=jax.ShapeDtypeStruct((CHUNK, h, BAND), F32), compiler_params=_cparams(1))(rel, onehot_t)
    return jnp.swapaxes(out, 0, 1)


def relbias_reduce(dbias, *, name):
    h = dbias.shape[0]
    onehot = _rel_onehot()
    dbt = jnp.swapaxes(dbias, 0, 1)

    def body(d_ref, oh_ref, o_ref):
        @pl.when(pl.program_id(0) == 0)
        def _():
            o_ref[...] = jnp.zeros_like(o_ref)
        o_ref[...] += hdot(d_ref[...], oh_ref[...])

    return pl.pallas_call(
        body, name=name, grid=(CHUNK,),
        in_specs=[pl.BlockSpec((None, h, BAND), lambda i: (i, 0, 0)),
                  pl.BlockSpec((None, BAND, N_REL), lambda i: (i, 0, 0))],
        out_specs=pl.BlockSpec((h, N_REL), lambda i: (0, 0)),
        out_shape=jax.ShapeDtypeStruct((h, N_REL), F32), compiler_params=_cparams(1))(dbt, onehot)


def _bt(x):
    return jnp.swapaxes(x, 1, 2)


_BDN = (((2,), (1,)), ((0,), (0,)))


def _bmm_exact(x, y):
    return lax.dot_general(x, y, _BDN, precision=HI, preferred_element_type=F32)


def _split_bf16(x):
    hi = x.astype(BF16)
    return hi, (x - hi.astype(F32)).astype(BF16)


def _bmm_3pass(x, y):
    xh, xl = _split_bf16(x)
    yh, yl = _split_bf16(y)
    dot = lambda p, q: lax.dot_general(p, q, _BDN, preferred_element_type=F32)
    return dot(xh, yh) + (dot(xh, yl) + dot(xl, yh))


def _make_bmm(raw):
    @jax.custom_vjp
    def f(x, y):
        return raw(x, y)

    def fwd(x, y):
        return raw(x, y), (x, y)

    def bwd(saved, dz):
        x, y = saved
        return raw(dz, _bt(y)), raw(_bt(x), dz)

    f.defvjp(fwd, bwd)
    return f


bmm = _make_bmm(_bmm_3pass)
bmm_exact = _make_bmm(_bmm_exact)


def rwkv_chunk(p0, r, lw, k, v, a, b):
    g, c, n = r.shape
    row = lax.broadcasted_iota(jnp.int32, (g, c, c), 1)
    col = lax.broadcasted_iota(jnp.int32, (g, c, c), 2)
    incl, strict = row >= col, row > col
    cs = bmm_exact(incl.astype(F32), lw)
    cs_end = cs[:, c - 1:c, :]
    e_cs = jnp.exp(cs)
    e_neg = jnp.exp(-cs)
    at = a * jnp.exp(cs - lw)
    rt = r * e_cs
    bt_, kt = b * e_neg, k * e_neg
    e_tail = jnp.exp(cs_end - cs)
    bh, kh = b * e_tail, k * e_tail
    ar = jnp.concatenate([at, rt], axis=1)
    x_b, x_k = bmm(ar, _bt(bt_)), bmm(ar, _bt(kt))
    a_ab = jnp.where(strict, x_b[:, :c], 0.0)
    a_rb = jnp.where(incl, x_b[:, c:], 0.0)
    a_ak = jnp.where(strict, x_k[:, :c], 0.0)
    a_rk = jnp.where(incl, x_k[:, c:], 0.0)
    tinv = jnp.where(row == col, 1.0, 0.0) + a_ab
    npow = bmm(a_ab, a_ab)
    for _ in range(int(math.log2(c)) - 1):
        both = bmm(jnp.concatenate([tinv, npow], axis=1), npow)
        tinv = tinv + both[:, :c]
        npow = both[:, c:]
    xp = bmm(ar, p0)
    xv = bmm(jnp.concatenate([a_ak, a_rk], axis=1), v)
    u = bmm(tinv, xp[:, :c] + xv[:, :c])
    y = xp[:, c:] + bmm(a_rb, u) + xv[:, c:]
    rown = lax.broadcasted_iota(jnp.int32, (g, n, n), 1)
    coln = lax.broadcasted_iota(jnp.int32, (g, n, n), 2)
    dg = jnp.where(rown == coln, jnp.exp(cs_end), 0.0)
    p1 = bmm(dg, p0) + bmm(_bt(jnp.concatenate([bh, kh], axis=1)), jnp.concatenate([u, v], axis=1))
    return y, p1


def rwkv_fwd(r, lw, k, v, a, b, *, name):
    h, t, n = r.shape
    g, c = min(RW_HEADS, h), RW_CHUNK
    nch = t // c

    def body(r_ref, lw_ref, k_ref, v_ref, a_ref, b_ref, y_ref, p_ref, st_ref):
        @pl.when(pl.program_id(1) == 0)
        def _():
            st_ref[...] = jnp.zeros_like(st_ref)
        p0 = st_ref[...]
        p_ref[...] = p0
        y, p1 = rwkv_chunk(p0, r_ref[...], lw_ref[...], k_ref[...], v_ref[...], a_ref[...], b_ref[...])
        y_ref[...] = y
        st_ref[...] = p1

    blk = pl.BlockSpec((g, c, n), lambda i, j: (i, j, 0))
    pblk = pl.BlockSpec((g, None, n, n), lambda i, j: (i, j, 0, 0))
    return pl.pallas_call(
        body, name=name, grid=(h // g, nch), in_specs=[blk] * 6, out_specs=[blk, pblk],
        out_shape=[jax.ShapeDtypeStruct((h, t, n), F32), jax.ShapeDtypeStruct((h, nch, n, n), F32)],
        scratch_shapes=[pltpu.VMEM((g, n, n), F32)], compiler_params=_cparams(2))(r, lw, k, v, a, b)


def rwkv_bwd(r, lw, k, v, a, b, p0s, dy, *, name):
    h, t, n = r.shape
    g, c = min(RW_HEADS, h), RW_CHUNK
    nch = t // c

    def body(r_ref, lw_ref, k_ref, v_ref, a_ref, b_ref, p_ref, dy_ref,
             dr_ref, dlw_ref, dk_ref, dv_ref, da_ref, db_ref, dp_ref):
        @pl.when(pl.program_id(1) == 0)
        def _():
            dp_ref[...] = jnp.zeros_like(dp_ref)
        _, pull = jax.vjp(rwkv_chunk, p_ref[...], r_ref[...], lw_ref[...], k_ref[...], v_ref[...],
                          a_ref[...], b_ref[...])
        dp0, dr, dlw, dk, dv, da, db = pull((dy_ref[...], dp_ref[...]))
        dr_ref[...] = dr
        dlw_ref[...] = dlw
        dk_ref[...] = dk
        dv_ref[...] = dv
        da_ref[...] = da
        db_ref[...] = db
        dp_ref[...] = dp0

    blk = pl.BlockSpec((g, c, n), lambda i, j: (i, nch - 1 - j, 0))
    pblk = pl.BlockSpec((g, None, n, n), lambda i, j: (i, nch - 1 - j, 0, 0))
    return pl.pallas_call(
        body, name=name, grid=(h // g, nch), in_specs=[blk] * 6 + [pblk, blk], out_specs=[blk] * 6,
        out_shape=[jax.ShapeDtypeStruct((h, t, n), F32)] * 6,
        scratch_shapes=[pltpu.VMEM((g, n, n), F32)], compiler_params=_cparams(2))(r, lw, k, v, a, b, p0s, dy)


def _time_block(t):
    return _pick(t, (256, 128, 64))


def ssm_scan_fwd(bu_re, bu_im, a_re, a_im, *, name):
    t, rr, ln = bu_re.shape
    tb = _time_block(t)

    def body(br_ref, bi_ref, ar_ref, ai_ref, hr_ref, hi_ref, sr_ref, si_ref):
        @pl.when(pl.program_id(0) == 0)
        def _():
            sr_ref[...] = jnp.zeros_like(sr_ref)
            si_ref[...] = jnp.zeros_like(si_ref)
        ar, ai = ar_ref[...], ai_ref[...]

        def step(i, carry):
            hr, hi = carry
            nr = ar * hr - ai * hi + br_ref[i]
            ni = ar * hi + ai * hr + bi_ref[i]
            hr_ref[i] = nr
            hi_ref[i] = ni
            return nr, ni

        hr, hi = lax.fori_loop(0, tb, step, (sr_ref[...], si_ref[...]))
        sr_ref[...] = hr
        si_ref[...] = hi

    blk = pl.BlockSpec((tb, rr, ln), lambda i: (i, 0, 0))
    cblk = pl.BlockSpec((rr, ln), lambda i: (0, 0))
    return pl.pallas_call(
        body, name=name, grid=(t // tb,), in_specs=[blk, blk, cblk, cblk], out_specs=[blk, blk],
        out_shape=[jax.ShapeDtypeStruct((t, rr, ln), F32)] * 2,
        scratch_shapes=[pltpu.VMEM((rr, ln), F32)] * 2, compiler_params=_cparams(1))(bu_re, bu_im, a_re, a_im)


def ssm_scan_bwd(dh_re, dh_im, hp_re, hp_im, a_re, a_im, *, name):
    t, rr, ln = dh_re.shape
    tb = _time_block(t)
    nb = t // tb

    def body(dr_ref, di_ref, pr_ref, pi_ref, ar_ref, ai_ref, gr_ref, gi_ref, dar_ref, dai_ref, sr_ref, si_ref):
        @pl.when(pl.program_id(0) == 0)
        def _():
            sr_ref[...] = jnp.zeros_like(sr_ref)
            si_ref[...] = jnp.zeros_like(si_ref)
            dar_ref[...] = jnp.zeros_like(dar_ref)
            dai_ref[...] = jnp.zeros_like(dai_ref)
        ar, ai = ar_ref[...], ai_ref[...]

        def step(ii, carry):
            gr, gi, dar, dai = carry
            i = tb - 1 - ii
            nr = dr_ref[i] + ar * gr + ai * gi
            ni = di_ref[i] - ai * gr + ar * gi
            gr_ref[i] = nr
            gi_ref[i] = ni
            pr, pi = pr_ref[i], pi_ref[i]
            dar = dar + nr * pr + ni * pi
            dai = dai - nr * pi + ni * pr
            return nr, ni, dar, dai

        gr, gi, dar, dai = lax.fori_loop(0, tb, step, (sr_ref[...], si_ref[...], dar_ref[...], dai_ref[...]))
        sr_ref[...] = gr
        si_ref[...] = gi
        dar_ref[...] = dar
        dai_ref[...] = dai

    blk = pl.BlockSpec((tb, rr, ln), lambda i: (nb - 1 - i, 0, 0))
    cblk = pl.BlockSpec((rr, ln), lambda i: (0, 0))
    return pl.pallas_call(
        body, name=name, grid=(nb,), in_specs=[blk] * 4 + [cblk, cblk], out_specs=[blk, blk, cblk, cblk],
        out_shape=[jax.ShapeDtypeStruct((t, rr, ln), F32)] * 2 + [jax.ShapeDtypeStruct((rr, ln), F32)] * 2,
        scratch_shapes=[pltpu.VMEM((rr, ln), F32)] * 2,
        compiler_params=_cparams(1))(dh_re, dh_im, hp_re, hp_im, a_re, a_im)


ANY = pl.BlockSpec(memory_space=pl.ANY)


def _rows(ref, lead, ch, nchunk):
    base = ref if lead is None else ref.at[lead]
    if nchunk == 1:
        return base
    n = base.shape[0] // nchunk
    return base.at[pl.ds(ch * n, n)]


class Comm:
    def __init__(self):
        self.ins, self.outs, self.ops, self.aliases = [], [], [], {}
        self.n_remote, self.n_local = 0, 0
        self.ici = False

    def _add(self, kind, src, out_shape, n_peers, nchunk, n_local=1, alias=False):
        lead_len = src.shape[1] if kind in ("scatter", "rs1") else src.shape[0]
        while lead_len % nchunk:
            nchunk //= 2
        self.ops.append((kind, len(self.ins), len(self.outs), self.n_remote, self.n_local, nchunk))
        if alias:
            self.aliases[len(self.ins)] = len(self.outs)
        self.ins.append(src)
        self.outs.append(jax.ShapeDtypeStruct(out_shape, src.dtype))
        self.n_remote += n_peers * nchunk
        self.n_local += n_local
        self.ici = self.ici or kind in ("gather", "scatter", "gather_all", "ag1", "rs2")
        return len(self.outs) - 1

    def ag1(self, w):
        return self._add("ag1", w, (4,) + w.shape, 3, 1, n_local=LOCAL_CHUNKS)

    def ag2(self, g):
        return self._add("ag2", g, g.shape, 3, 1, n_local=0, alias=True)

    def rs1(self, g4):
        s, r, c = g4.shape
        return self._add("rs1", g4, (s, r // 2, c), 4, 1, n_local=0)

    def rs2(self, h4):
        return self._add("rs2", h4, (3,) + h4.shape[1:], 3, 1, n_local=0)

    def rs3(self, s, nchunk=4):
        return self._add("rs3", s, s.shape, 1, nchunk, n_local=0)

    def all_gather(self, w, nchunk=2):
        return self._add("gather", w, (4,) + w.shape, 3, nchunk)

    def scatter(self, g4, nchunk=2):
        return self._add("scatter", g4, g4.shape, 3, nchunk)

    def swap(self, s, nchunk=8):
        return self._add("swap", s, (2,) + s.shape, 1, nchunk)

    def gather_all(self, v, nchunk=1):
        return self._add("gather_all", v, (8,) + v.shape, 7, nchunk)

    def sem_shapes(self):
        return [pltpu.SemaphoreType.DMA((self.n_remote,)), pltpu.SemaphoreType.DMA((self.n_remote,)),
                pltpu.SemaphoreType.DMA((max(self.n_local, 1),))]

    def _two_level(self, kind, src, dst, r0, l0, nchunk, x, y, c, send, recv, lsem, sends, recvs, locs):
        chips = [(1 - x, y), (x, 1 - y), (1 - x, 1 - y)]
        me, sib = 2 * x + y, (x, y, 1 - c)

        def half(ref, hc):
            n = ref.shape[0] // 2
            return ref.at[pl.ds(hc * n, n)]

        def both(k, dev, s_ref, d_send, d_recv):
            mk = functools.partial(pltpu.make_async_remote_copy, src_ref=s_ref, send_sem=send.at[k],
                                   recv_sem=recv.at[k], device_id=dev, device_id_type=MESH)
            sends.append(mk(dst_ref=d_send))
            recvs.append(mk(dst_ref=d_recv))

        if kind == "ag1":
            n = src.shape[0] // LOCAL_CHUNKS
            for j in range(LOCAL_CHUNKS):
                rows = pl.ds(j * n, n)
                locs.append(pltpu.make_async_copy(src.at[rows], dst.at[me].at[rows], lsem.at[l0 + j]))
            for pj, (px, py) in enumerate(chips):
                both(r0 + pj, (px, py, c), half(src, c), half(dst.at[me], c), half(dst.at[2 * px + py], c))
        elif kind == "ag2":
            for pj, (px, py) in enumerate(chips):
                got = dst.at[2 * px + py]
                both(r0 + pj, sib, half(got, c), half(got, c), half(got, 1 - c))
        elif kind == "rs1":
            for q in range(4):
                both(r0 + q, sib, half(src.at[q], 1 - c), dst.at[q], dst.at[q])
        elif kind == "rs2":
            for pj, (px, py) in enumerate(chips):
                both(r0 + pj, (px, py, c), src.at[2 * px + py], dst.at[pj], dst.at[pj])
        else:
            for ch in range(nchunk):
                both(r0 + ch, sib, _rows(src, None, ch, nchunk), _rows(dst, None, ch, nchunk),
                     _rows(dst, None, ch, nchunk))

    def _descs(self, c_in, c_out, send, recv, lsem):
        x, y, c = lax.axis_index("x"), lax.axis_index("y"), lax.axis_index("c")
        sends, recvs, locs = [], [], []
        for kind, ii, oi, r0, l0, nchunk in self.ops:
            src, dst = c_in[ii], c_out[oi]
            if kind in ("ag1", "ag2", "rs1", "rs2", "rs3"):
                self._two_level(kind, src, dst, r0, l0, nchunk, x, y, c, send, recv, lsem, sends, recvs, locs)
                continue
            if kind == "swap":
                me = c
                peers = [((x, y, 1 - c), 1 - c)]
            elif kind == "gather_all":
                me = 4 * x + 2 * y + c
                flips = [(dx, dy, dc) for dx in (0, 1) for dy in (0, 1) for dc in (0, 1) if dx + dy + dc]
                peers = []
                for dx, dy, dc in flips:
                    px, py, pc = (x + dx) % 2, (y + dy) % 2, (c + dc) % 2
                    peers.append(((px, py, pc), 4 * px + 2 * py + pc))
            else:
                me = 2 * x + y
                peers = [((px, py, c), 2 * px + py) for px, py in ((1 - x, y), (x, 1 - y), (1 - x, 1 - y))]
            if kind == "scatter":
                locs.append(pltpu.make_async_copy(src.at[me], dst.at[me], lsem.at[l0]))
            else:
                locs.append(pltpu.make_async_copy(src, dst.at[me], lsem.at[l0]))
            for pj, (dev, peer_slot) in enumerate(peers):
                for ch in range(nchunk):
                    k = r0 + pj * nchunk + ch
                    s_src = _rows(src, peer_slot if kind == "scatter" else None, ch, nchunk)
                    mk = functools.partial(pltpu.make_async_remote_copy, send_sem=send.at[k], recv_sem=recv.at[k],
                                           device_id=dev, device_id_type=MESH)
                    sends.append(mk(src_ref=s_src, dst_ref=_rows(dst, me, ch, nchunk)))
                    recvs.append(mk(src_ref=s_src, dst_ref=_rows(dst, peer_slot, ch, nchunk)))
        return sends, recvs, locs

    def start(self, c_in, c_out, send, recv, lsem):
        sends, _, locs = self._descs(c_in, c_out, send, recv, lsem)
        for d in locs + sends:
            d.start()

    def wait(self, c_in, c_out, send, recv, lsem):
        sends, recvs, locs = self._descs(c_in, c_out, send, recv, lsem)
        for d in recvs:
            d.wait_recv()
        for d in sends:
            d.wait_send()
        for d in locs:
            d.wait()


def run_comm(comm, *, name):
    n_cin, n_cout = len(comm.ins), len(comm.outs)

    def body(*refs):
        c_in, c_out, sems = refs[:n_cin], refs[n_cin:n_cin + n_cout], refs[n_cin + n_cout:]
        comm.start(c_in, c_out, *sems)
        comm.wait(c_in, c_out, *sems)

    outs = pl.pallas_call(
        body, name=name, in_specs=[ANY] * n_cin, out_specs=[ANY] * n_cout, out_shape=list(comm.outs),
        scratch_shapes=comm.sem_shapes(), input_output_aliases=dict(comm.aliases),
        compiler_params=pltpu.CompilerParams(has_side_effects=True))(*comm.ins)
    return list(outs)


def _core_index():
    return jnp.reshape(lax.axis_index("c"), (1,)).astype(jnp.int32)


def _chip_index():
    return jnp.reshape(2 * lax.axis_index("x") + lax.axis_index("y"), (1,)).astype(jnp.int32)


def _row_block(rows, bytes_per_row, cap=512):
    tb = 16
    while tb * 2 <= min(rows, cap) and rows % (tb * 2) == 0 and tb * 2 * bytes_per_row <= ROW_BLOCK_BYTES:
        tb *= 2
    return tb


def add_half(g4, got, *, name):
    s, r, c = g4.shape
    r2 = r // 2
    tb = _row_block(r2, c * 8)
    nb = r2 // tb

    def body(c_ref, g_ref, x_ref, o_ref):
        o_ref[...] = (g_ref[...].astype(F32) + x_ref[...].astype(F32)).astype(o_ref.dtype)

    grid_spec = pltpu.PrefetchScalarGridSpec(
        num_scalar_prefetch=1, grid=(s, nb),
        in_specs=[pl.BlockSpec((None, tb, c), lambda q, i, cr: (q, cr[0] * nb + i, 0)),
                  pl.BlockSpec((None, tb, c), lambda q, i, cr: (q, i, 0))],
        out_specs=pl.BlockSpec((None, tb, c), lambda q, i, cr: (q, i, 0)))
    return pl.pallas_call(body, name=name, grid_spec=grid_spec, out_shape=jax.ShapeDtypeStruct((s, r2, c), g4.dtype),
                          compiler_params=_cparams(2))(_core_index(), g4, got)


def sum_chips(h4, got3, *, name):
    _, r2, c = h4.shape
    tb = _row_block(r2, c * 12)
    nb = r2 // tb

    def body(q_ref, h_ref, y_ref, o_ref):
        o_ref[...] = ((h_ref[...].astype(F32) + y_ref[0].astype(F32)) + y_ref[1].astype(F32)) + y_ref[2].astype(F32)

    grid_spec = pltpu.PrefetchScalarGridSpec(
        num_scalar_prefetch=1, grid=(nb,),
        in_specs=[pl.BlockSpec((None, tb, c), lambda i, qr: (qr[0], i, 0)),
                  pl.BlockSpec((3, tb, c), lambda i, qr: (0, i, 0))],
        out_specs=pl.BlockSpec((tb, c), lambda i, qr: (i, 0)))
    return pl.pallas_call(body, name=name, grid_spec=grid_spec, out_shape=jax.ShapeDtypeStruct((r2, c), F32),
                          compiler_params=_cparams(1))(_chip_index(), h4, got3)


def adamw_big(w, mine, theirs, m, v, *, comm=None, name):
    nl, r, c = w.shape
    r2 = r // 2
    tb = _row_block(r2, c * 4 * 10, cap=256)
    nb2 = r2 // tb
    nb = 2 * nb2
    n_cin = len(comm.ins) if comm else 0
    n_cout = len(comm.outs) if comm else 0

    def body(c_ref, w_ref, m_ref, v_ref, *rest):
        g_refs, rest = rest[:2 * nl], rest[2 * nl:]
        c_in, (go_ref, d_ref, mo_ref, vo_ref), rest = rest[:n_cin], rest[n_cin:n_cin + 4], rest[n_cin + 4:]
        c_out, sems = rest[:n_cout], rest[n_cout:]
        layer, i = pl.program_id(0), pl.program_id(1)
        if comm:
            @pl.when((layer == 0) & (i == 0))
            def _():
                comm.start(c_in, c_out, *sems)

            @pl.when((layer == nl - 1) & (i == nb - 1))
            def _():
                comm.wait(c_in, c_out, *sems)
        own = (i // nb2) == c_ref[0]
        for l0 in range(nl):
            @pl.when(layer == l0)
            def _(l0=l0):
                g = jnp.where(own, g_refs[2 * l0][...], g_refs[2 * l0 + 1][...])
                d, mn, vn = _adam_math(w_ref[...], g, m_ref[...], v_ref[...])
                go_ref[...] = g
                d_ref[...] = d
                mo_ref[...] = mn
                vo_ref[...] = vn

    blk = pl.BlockSpec((None, tb, c), lambda l, i, cr: (l, i, 0))

    def half_spec(l0):
        return pl.BlockSpec((tb, c), lambda l, i, cr: (jnp.where(l == l0, i % nb2, jnp.where(l < l0, 0, nb2 - 1)), 0))

    in_specs = [blk, blk, blk]
    args = [w, m, v]
    for l0 in range(nl):
        in_specs += [half_spec(l0), half_spec(l0)]
        args += [mine[l0], theirs[l0]]
    if comm:
        in_specs += [ANY] * n_cin
        args += list(comm.ins)
    grid_spec = pltpu.PrefetchScalarGridSpec(
        num_scalar_prefetch=1, grid=(nl, nb), in_specs=in_specs, out_specs=[blk] * 4 + [ANY] * n_cout,
        scratch_shapes=comm.sem_shapes() if comm else [])
    aliases = {1 + 3 + 2 * nl + ii: 4 + oi for ii, oi in comm.aliases.items()} if comm else {}
    outs = pl.pallas_call(
        body, name=name, grid_spec=grid_spec,
        out_shape=[jax.ShapeDtypeStruct(w.shape, F32)] * 4 + (list(comm.outs) if comm else []),
        input_output_aliases=aliases, compiler_params=_cparams(2))(_core_index(), *args)
    if comm:
        return list(outs[:4]), list(outs[4:])
    return list(outs)


def sum_slots(x, *, name):
    s, r, c = x.shape
    tb = 8
    while tb * 2 <= min(r, 512) and r % (tb * 2) == 0 and tb * 2 * c * 4 * (s + 1) <= ROW_BLOCK_BYTES:
        tb *= 2

    def body(x_ref, o_ref):
        acc = x_ref[0].astype(F32)
        for i in range(1, s):
            acc = acc + x_ref[i].astype(F32)
        o_ref[...] = acc

    return pl.pallas_call(
        body, name=name, grid=(r // tb,), in_specs=[pl.BlockSpec((s, tb, c), lambda i: (0, i, 0))],
        out_specs=pl.BlockSpec((tb, c), lambda i: (i, 0)),
        out_shape=jax.ShapeDtypeStruct((r, c), F32), compiler_params=_cparams(1))(x)


def _adam_math(w, g, m, v):
    m = ADAM_B1 * m + (1.0 - ADAM_B1) * g
    v = ADAM_B2 * v + (1.0 - ADAM_B2) * (g * g)
    m_hat = m / (1.0 - ADAM_B1 ** ADAM_STEP)
    v_hat = v / (1.0 - ADAM_B2 ** ADAM_STEP)
    delta = -ADAM_LR * (m_hat / (jnp.sqrt(v_hat) + ADAM_EPS) + ADAM_WD * w)
    return delta, m, v


def adamw_pair(w, g2, m, v, *, name):
    r, c = w.shape
    tb = 8
    while tb * 2 <= min(r, 512) and r % (tb * 2) == 0 and tb * 2 * c * 4 * 9 <= 2 * ROW_BLOCK_BYTES:
        tb *= 2

    def body(w_ref, g_ref, m_ref, v_ref, go_ref, d_ref, mo_ref, vo_ref):
        g = g_ref[0] + g_ref[1]
        d, mn, vn = _adam_math(w_ref[...], g, m_ref[...], v_ref[...])
        go_ref[...] = g
        d_ref[...] = d
        mo_ref[...] = mn
        vo_ref[...] = vn

    blk = pl.BlockSpec((tb, c), lambda i: (i, 0))
    return pl.pallas_call(
        body, name=name, grid=(r // tb,), in_specs=[blk, pl.BlockSpec((2, tb, c), lambda i: (0, i, 0)), blk, blk],
        out_specs=[blk] * 4, out_shape=[jax.ShapeDtypeStruct((r, c), F32)] * 4,
        compiler_params=_cparams(1))(w, g2, m, v)


def adamw_flat(w, g, m, v, *, name):
    def fn(w_, g_, m_, v_):
        return _adam_math(w_, g_, m_, v_)
    return rowk(fn, [w, g, m, v], [], [(w.shape[1], F32)] * 3, name=name)


def to_heads(x):
    t, d = x.shape
    return x.reshape(t, d // HEAD, HEAD).transpose(1, 0, 2)


def from_heads(x):
    h, t, n = x.shape
    return x.transpose(1, 0, 2).reshape(t, h * n)


def pack_flat(arrs, lanes=128, row_mult=512):
    flat = jnp.concatenate([a.reshape(-1).astype(F32) for a in arrs])
    n = flat.shape[0]
    rows = -(-n // lanes)
    rows = -(-rows // row_mult) * row_mult
    return jnp.pad(flat, (0, rows * lanes - n)).reshape(rows, lanes)


def unpack_flat(buf, shapes):
    flat = buf.reshape(-1)
    outs, off = [], 0
    for s in shapes:
        n = int(np.prod(s))
        outs.append(flat[off:off + n].reshape(s))
        off += n
    return outs


def block_diag_from(w_gab):
    g, a, b = w_gab.shape
    eye = jnp.eye(g, dtype=w_gab.dtype)
    return (w_gab[:, :, None, :] * eye[:, None, :, None]).reshape(g * a, g * b)


def block_diag_extract(m, g):
    a, b = m.shape[0] // g, m.shape[1] // g
    eye = jnp.eye(g, dtype=m.dtype)
    return jnp.sum(m.reshape(g, a, g, b) * eye[:, None, :, None], axis=2)


def kernel(x, p, ffn1_norm, ffn1_w_gate, ffn1_w_up, ffn1_w_down, mix_norm, ffn2_norm, ffn2_w_gate, ffn2_w_up, ffn2_w_down, ple_norm, ple_w_gate, ple_w_proj, ab_w_in, att_q_gain, att_k_gain, att_rel_bias, rwkv_mu, rwkv_w0, rwkv_w_up, rwkv_a0, rwkv_a_up, rwkv_g_up, rwkv_k_k, rwkv_k_a, rwkv_r_k, rwkv_lnx_w, rwkv_lnx_b, ab_w_out, ssm_w_in, ssm_lambda_re, ssm_lambda_im, ssm_log_dt, ssm_b_re, ssm_b_im, ssm_c_re, ssm_c_im, ssm_d, ssm_w_out, loss_target, m_ffn1_norm, m_ffn1_w_gate, m_ffn1_w_up, m_ffn1_w_down, m_mix_norm, m_ffn2_norm, m_ffn2_w_gate, m_ffn2_w_up, m_ffn2_w_down, m_ple_norm, m_ple_w_gate, m_ple_w_proj, m_ab_w_in, m_att_q_gain, m_att_k_gain, m_att_rel_bias, m_rwkv_mu, m_rwkv_w0, m_rwkv_w_up, m_rwkv_a0, m_rwkv_a_up, m_rwkv_g_up, m_rwkv_k_k, m_rwkv_k_a, m_rwkv_r_k, m_rwkv_lnx_w, m_rwkv_lnx_b, m_ab_w_out, m_ssm_w_in, m_ssm_lambda_re, m_ssm_lambda_im, m_ssm_log_dt, m_ssm_b_re, m_ssm_b_im, m_ssm_c_re, m_ssm_c_im, m_ssm_d, m_ssm_w_out, v_ffn1_norm, v_ffn1_w_gate, v_ffn1_w_up, v_ffn1_w_down, v_mix_norm, v_ffn2_norm, v_ffn2_w_gate, v_ffn2_w_up, v_ffn2_w_down, v_ple_norm, v_ple_w_gate, v_ple_w_proj, v_ab_w_in, v_att_q_gain, v_att_k_gain, v_att_rel_bias, v_rwkv_mu, v_rwkv_w0, v_rwkv_w_up, v_rwkv_a0, v_rwkv_a_up, v_rwkv_g_up, v_rwkv_k_k, v_rwkv_k_a, v_rwkv_r_k, v_rwkv_lnx_w, v_rwkv_lnx_b, v_ab_w_out, v_ssm_w_in, v_ssm_lambda_re, v_ssm_lambda_im, v_ssm_log_dt, v_ssm_b_re, v_ssm_b_im, v_ssm_c_re, v_ssm_c_im, v_ssm_d, v_ssm_w_out):
    A = dict(locals())
    W = {n: A[n] for n in W_NAMES}
    return _step(A['x'], A['p'], A['loss_target'], W, {n: A['m_' + n] for n in W_NAMES},
                 {n: A['v_' + n] for n in W_NAMES})


def _step(x, p, target, W, M, V):
    assert x.shape[0] == 1
    t, d = x.shape[1], x.shape[2]
    depth = p.shape[0]
    h0 = x[0]
    tgt = target[0]
    qchip = 2 * lax.axis_index("x") + lax.axis_index("y")
    d_rw = W['rwkv_w0'].shape[1]
    d_att = W['ab_w_out'].shape[1] * 4 - d_rw
    n_h_att, n_h_rw = d_att // HEAD, d_rw // HEAD
    n_bin = 3 * d_rw + DECAY_LORA + AAA_LORA + GATE_LORA
    d_ssm = W['ssm_w_in'].shape[2]
    n_grp = d_ssm // SSM_GROUP
    gp = n_grp * SSM_STATE

    queue = []
    gathered = {}
    halves = {}
    grads = {}
    queued_grads = set()
    n_alone = [0]

    def ag_entry(name, layer):
        def add1(cm):
            return cm.ag1(W[name][layer].astype(BF16))

        def cont1(outs, hd):
            got = outs[hd]

            def cont2(outs2, hd2):
                gathered[(name, layer)] = outs2[hd2]
            queue.insert(0, (False, lambda cm: cm.ag2(got), cont2))
        return True, add1, cont1

    def rs_entry(name, layer, g4):
        shard_shape = W[name].shape[1:]
        rows, cols = int(np.prod(shard_shape[:-1])), shard_shape[-1]
        g4 = g4.reshape(4, rows, cols)

        def cont1(outs, hd):
            h4 = add_half(g4, outs[hd], name=f"rs_add_{name}")

            def cont2(outs2, hd2):
                mine = sum_chips(h4, outs2[hd2], name=f"rs_sum_{name}")

                def cont3(outs3, hd3):
                    halves[(name, layer)] = (mine, outs3[hd3])
                queue.insert(0, (False, lambda cm: cm.rs3(mine), cont3))
            queue.insert(0, (True, lambda cm: cm.rs2(h4), cont2))
        return False, (lambda cm: cm.rs1(g4)), cont1

    def enqueue_ready():
        for n in BIG:
            for li_, g4 in enumerate(grads.get(n, [])):
                if g4 is not None and (n, li_) not in queued_grads:
                    queued_grads.add((n, li_))
                    queue.append(rs_entry(n, li_, g4))

    def take(max_ici=1):
        cm, conts, n_ici = Comm(), [], 0
        while queue and (n_ici < max_ici or not queue[0][0]):
            is_ici, add, cont = queue.pop(0)
            conts.append((cont, add(cm)))
            n_ici += int(is_ici)
        return cm, conts

    def hmm(*args, **kw):
        enqueue_ready()
        if not queue:
            return mm(*args, **kw)
        cm, conts = take()
        out, couts = mm(*args, comm=cm, **kw)
        for cont, hd in reversed(conts):
            cont(couts, hd)
        return out

    def alone(max_ici=1):
        cm, conts = take(max_ici)
        assert conts
        n_alone[0] += 1
        couts = run_comm(cm, name=f"comm_alone{n_alone[0]}")
        for cont, hd in reversed(conts):
            cont(couts, hd)

    def flush(until=None, at_least_one=False):
        enqueue_ready()
        while queue and (at_least_one or not (until is not None and until())):
            at_least_one = False
            alone()
            enqueue_ready()

    def gather(name, layer):
        flush(until=lambda: (name, layer) in gathered)
        return gathered[(name, layer)]
    small = {}

    def add_small(name, val, layer=None, nl=1):
        if layer is None:
            small[name] = val
        else:
            small.setdefault(name, [None] * nl)[layer] = val

    def ffn_fwd(h, pre, i):
        g = W[pre + '_norm'][i][None]
        wg, wu, wd = gather(pre + '_w_gate', i), gather(pre + '_w_up', i), gather(pre + '_w_down', i)
        wd2 = wd.reshape(-1, d)
        n = rowk(lambda hh, gg: f_rms(hh, gg), [h], [g], [(d, BF16)], name=f"{pre}_rms")[0]
        a = hmm(n, wg, out_dtype=BF16, name=f"{pre}_gate")
        b, u = hmm(n, wu, epi=(lambda bb, aa: (bb, f_swiglu(aa.astype(F32), bb)), [a], [BF16, BF16]),
                   name=f"{pre}_up")
        h_out = hmm(u, wd2, res=h, alpha=0.5, name=f"{pre}_down")
        return h_out, dict(h=h, g=g, n=n, a=a, b=b, u=u, wg=wg, wu=wu, wd2=wd2)

    def ffn_bwd(dh, sv, pre, i):
        f = sv['a'].shape[1]
        dwd = hmm(sv['u'], dh, ta=True, alpha=0.5, out_dtype=GRAD_XFER, name=f"{pre}_d_wdown")
        def f_du(du, aa, bb):
            _, pull = jax.vjp(f_swiglu, aa.astype(F32), bb.astype(F32))
            return pull(du)
        da, db = hmm(dh, sv['wd2'], tb=True, alpha=0.5, epi=(f_du, [sv['a'], sv['b']], [BF16, BF16]),
                     name=f"{pre}_d_u")
        dwg = hmm(sv['n'], da, ta=True, nshard=4, out3=True, out_dtype=GRAD_XFER, name=f"{pre}_d_wgate")
        dwu = hmm(sv['n'], db, ta=True, nshard=4, out3=True, out_dtype=GRAD_XFER, name=f"{pre}_d_wup")
        dn = hmm(da, sv['wg'], tb=True, name=f"{pre}_d_n1")
        dn = hmm(db, sv['wu'], tb=True, res=dn, name=f"{pre}_d_n2")
        dh_in, dg = rms_bwd(sv['h'], dn, dh, sv['g'], name=f"{pre}_d_rms")
        grads.setdefault(pre + '_w_gate', [None] * depth)[i] = dwg
        grads.setdefault(pre + '_w_up', [None] * depth)[i] = dwu
        grads.setdefault(pre + '_w_down', [None] * depth)[i] = dwd.reshape(4, -1, d)
        add_small(pre + '_norm', dg[0], i, depth)
        return dh_in

    def rms_bwd(h, dn, dh_res, g, *, name):
        def fn(hh, dnn, dres, gg):
            _, pull = jax.vjp(f_rms, hh, gg)
            dh_, dg_ = pull(dnn)
            return dh_ + dres, dg_
        return rowk(fn, [h, dn, dh_res], [g], [(d, F32)], [(1, d)], name=name)

    def head_consts(nh):
        e = np.kron(np.eye(nh, dtype=np.float32), np.ones((HEAD, 1), np.float32))
        return jnp.asarray(e), jnp.asarray(e.T)

    def mixer_ab_fwd(h):
        g = W['mix_norm'][0][None]
        win = gather('ab_w_in', 0).transpose(1, 0, 2).reshape(d, -1)
        wout = gather('ab_w_out', 0).reshape(-1, d)
        hn = rowk(lambda hh, gg: f_rms(hh, gg), [h], [g], [(d, BF16)], name="mixab_rms")[0]
        proj = hmm(hn, win, name="mixab_proj")
        q2, k2, v2 = [to_heads(proj[:, j * d_att:(j + 1) * d_att]).reshape(n_h_att * t, HEAD) for j in range(3)]
        qg, kg = W['att_q_gain'], W['att_k_gain']
        f_qn = lambda qq, gg: f_rms(qq, gg) * (HEAD ** -0.5)
        qn = rowk(f_qn, [q2], [qg], [(HEAD, BF16)], name="att_qnorm")[0].reshape(n_h_att, t, HEAD)
        kn = rowk(f_rms, [k2], [kg], [(HEAD, BF16)], name="att_knorm")[0].reshape(n_h_att, t, HEAD)
        knp = jnp.pad(kn, ((0, 0), (PAD, 0), (0, 0)))
        vp = jnp.pad(v2.astype(BF16).reshape(n_h_att, t, HEAD), ((0, 0), (PAD, 0), (0, 0)))
        bias = relbias_expand(W['att_rel_bias'][0], name="att_relbias")
        o = att_fwd(qn, knp, vp, bias, name="att_fwd")
        att = from_heads(o)
        mu = W['rwkv_mu']
        zs = ts_fwd(proj, 3 * d_att, n_bin, mu, name="rwkv_shift")
        e, et = head_consts(n_h_rw)
        zpad = jnp.zeros((AAA_LORA, d_rw), F32)
        wup_p = jnp.concatenate([W['rwkv_w_up_full'], zpad], 0)
        aup_p = jnp.concatenate([zpad, W['rwkv_a_up_full']], 0)
        pre_c = [W['rwkv_w0'], wup_p, W['rwkv_a0'], aup_p, W['rwkv_g_up_full'], W['rwkv_k_k'], W['rwkv_k_a'], e, et]
        pre = rowk(f_rwkv_pre, [zs], pre_c, [(d_rw, F32)] * 7, name="rwkv_pre")
        r_, lw_, kk_, vv_, ia_, ib_, gg_ = pre
        hm = [to_heads(u_) for u_ in (r_, lw_, kk_, vv_, ia_, ib_)]
        y_h, p0s = rwkv_fwd(*hm, name="rwkv_scan")
        y = from_heads(y_h)
        post_c = [W['rwkv_r_k'].reshape(1, d_rw), W['rwkv_lnx_w'], W['rwkv_lnx_b'], e, et]
        rw = rowk(f_rwkv_post, [y, r_, kk_, vv_, gg_], post_c, [(d_rw, BF16)], name="rwkv_post")[0]
        cat = jnp.concatenate([att, rw], axis=1)
        h_out = hmm(cat, wout, res=h, name="mixab_out")
        sv = dict(h=h, g=g, hn=hn, win=win, wout=wout, proj=proj, q2=q2, k2=k2, qn=qn, knp=knp, vp=vp, bias=bias,
                  zs=zs, pre_c=pre_c, pre=pre, hm=hm, p0s=p0s, y=y, post_c=post_c, cat=cat, qg=qg, kg=kg, mu=mu)
        return h_out, sv

    def mixer_ab_bwd(dh, sv):
        dwout = hmm(sv['cat'], dh, ta=True, out_dtype=GRAD_XFER, name="mixab_d_wout")
        grads['ab_w_out'] = [dwout.reshape(4, -1, d)]
        dcat = hmm(dh, sv['wout'], tb=True, name="mixab_d_cat")
        datt, drw = dcat[:, :d_att], dcat[:, d_att:]
        r_, lw_, kk_, vv_, ia_, ib_, gg_ = sv['pre']
        post = rowk(vjp_rows(f_rwkv_post, 5, 1), [sv['y'], r_, kk_, vv_, gg_, drw], sv['post_c'],
                    [(d_rw, F32)] * 5, [(1, d_rw)] * 3 + [sv['post_c'][3].shape, sv['post_c'][4].shape],
                    name="rwkv_d_post")
        dy, dr1, dk1, dv1, dg1 = post[:5]
        add_small('rwkv_r_k', post[5].reshape(W['rwkv_r_k'].shape))
        add_small('rwkv_lnx_w', post[6])
        add_small('rwkv_lnx_b', post[7])
        dscan = rwkv_bwd(*sv['hm'], sv['p0s'], to_heads(dy), name="rwkv_d_scan")
        dr2, dlw, dk2, dv2, dia, dib = [from_heads(u_) for u_ in dscan]

        def pre_bwd(zs, dra, drb, dlw_, dka, dkb, dva, dvb, dia_, dib_, dg_, *consts):
            _, pull = jax.vjp(f_rwkv_pre, zs, *consts)
            return pull((dra + drb, dlw_, dka + dkb, dva + dvb, dia_, dib_, dg_))

        pc = sv['pre_c']
        preb = rowk(pre_bwd, [sv['zs'], dr1, dr2, dlw, dk1, dk2, dv1, dv2, dia, dib, dg1], pc,
                    [(n_bin, F32)], [c.shape for c in pc], name="rwkv_d_pre")
        dzs = preb[0]
        add_small('rwkv_w0', preb[1])
        add_small('rwkv_w_up', preb[2][:DECAY_LORA])
        add_small('rwkv_a0', preb[3])
        add_small('rwkv_a_up', preb[4][DECAY_LORA:])
        add_small('rwkv_g_up', preb[5])
        add_small('rwkv_k_k', preb[6])
        add_small('rwkv_k_a', preb[7])
        dz, dmu = ts_bwd(sv['proj'], 3 * d_att, dzs, sv['mu'], name="rwkv_d_shift")
        add_small('rwkv_mu', dmu)
        do = to_heads(datt).astype(BF16)
        dqn, dknp, dvp, dbias = att_bwd(sv['qn'], sv['knp'], sv['vp'], sv['bias'], do, name="att_bwd")
        add_small('att_rel_bias', relbias_reduce(dbias, name="att_d_relbias")[None])
        f_qn = lambda qq, gg: f_rms(qq, gg) * (HEAD ** -0.5)
        dq2, dqg = rowk(vjp_rows(f_qn, 1, 1), [sv['q2'], dqn.reshape(-1, HEAD)], [sv['qg']], [(HEAD, F32)],
                        [(1, HEAD)], name="att_d_qnorm")
        dk2_, dkg = rowk(vjp_rows(f_rms, 1, 1), [sv['k2'], dknp[:, PAD:].reshape(-1, HEAD)], [sv['kg']],
                         [(HEAD, F32)], [(1, HEAD)], name="att_d_knorm")
        add_small('att_q_gain', dqg)
        add_small('att_k_gain', dkg)
        dproj = jnp.concatenate([from_heads(dq2.reshape(n_h_att, t, HEAD)), from_heads(dk2_.reshape(n_h_att, t, HEAD)),
                                 from_heads(dvp[:, PAD:]), dz], axis=1)
        dproj = dproj.astype(BF16)
        dwin = hmm(sv['hn'], dproj, ta=True, out_dtype=GRAD_XFER, name="mixab_d_win")
        grads['ab_w_in'] = [dwin.reshape(d, 4, -1).transpose(1, 0, 2)]
        dhn = hmm(dproj, sv['win'], tb=True, name="mixab_d_hn")
        dh_in, dg = rms_bwd(sv['h'], dhn, dh, sv['g'], name="mixab_d_rms")
        add_small('mix_norm', dg[0], 0, depth)
        return dh_in

    def ssm_params():
        lr, li = W['ssm_lambda_re'][0], W['ssm_lambda_im'][0]
        ldt = W['ssm_log_dt'][0][:, None]
        ab = rowk(f_ssm_ab, [lr, li, ldt], [], [(SSM_STATE, F32)] * 4, name="ssm_ab", tb=n_grp)
        br = W['ssm_b_re'][0].reshape(gp, SSM_GROUP)
        bi = W['ssm_b_im'][0].reshape(gp, SSM_GROUP)
        z_re, z_im = ab[2].reshape(gp, 1), ab[3].reshape(gp, 1)
        bb = rowk(f_ssm_bb, [br, bi, z_re, z_im], [], [(SSM_GROUP, F32)] * 2, name="ssm_bb", tb=gp)
        return dict(lr=lr, li=li, ldt=ldt, ab=ab, br=br, bi=bi, z_re=z_re, z_im=z_im, bb=bb)

    def mixer_s5_fwd(h):
        g = W['mix_norm'][1][None]
        win, wout = gather('ssm_w_in', 0).reshape(d, d_ssm), gather('ssm_w_out', 0)
        hn = rowk(lambda hh, gg: f_rms(hh, gg), [h], [g], [(d, BF16)], name="s5_rms")[0]
        u = hmm(hn, win, name="s5_in")
        sp = ssm_params()
        bbd_re = block_diag_from(sp['bb'][0].reshape(n_grp, SSM_STATE, SSM_GROUP).transpose(0, 2, 1)).astype(BF16)
        bbd_im = block_diag_from(sp['bb'][1].reshape(n_grp, SSM_STATE, SSM_GROUP).transpose(0, 2, 1)).astype(BF16)
        cbd_re = block_diag_from(W['ssm_c_re'][0].transpose(0, 2, 1)).astype(BF16)
        cbd_im = block_diag_from(W['ssm_c_im'][0].transpose(0, 2, 1)).astype(BF16)
        ub = u.astype(BF16)
        bu_re = hmm(ub, bbd_re, name="s5_bu_re").reshape(t, gp // 128, 128)
        bu_im = hmm(ub, bbd_im, name="s5_bu_im").reshape(t, gp // 128, 128)
        a_re, a_im = sp['ab'][0].reshape(gp // 128, 128), sp['ab'][1].reshape(gp // 128, 128)
        h_re, h_im = ssm_scan_fwd(bu_re, bu_im, a_re, a_im, name="s5_scan")
        hb_re, hb_im = h_re.reshape(t, gp).astype(BF16), h_im.reshape(t, gp).astype(BF16)
        y = hmm(hb_re, cbd_re, name="s5_y_re")
        y = hmm(hb_im, cbd_im, res=y, alpha=-1.0, name="s5_y_im")
        dsk = W['ssm_d_full']
        f_act = lambda yy, uu, dd: f_gelu(yy + dd * uu)
        yg = rowk(f_act, [y, u], [dsk], [(d_ssm, BF16)], name="s5_gelu")[0]
        z = hmm(yg, wout, name="s5_out")
        f_glu = lambda zz, hh: hh + zz[:, :d] * f_sigmoid(zz[:, d:])
        h_out = rowk(f_glu, [z, h], [], [(d, F32)], name="s5_glu")[0]
        sv = dict(h=h, g=g, hn=hn, win=win, wout=wout, u=u, ub=ub, sp=sp, bbd_re=bbd_re, bbd_im=bbd_im,
                  cbd_re=cbd_re, cbd_im=cbd_im, a_re=a_re, a_im=a_im, h_re=h_re, h_im=h_im, hb_re=hb_re,
                  hb_im=hb_im, y=y, dsk=dsk, yg=yg, z=z)
        return h_out, sv

    def mixer_s5_bwd(dh, sv):
        f_glu = lambda zz: zz[:, :d] * f_sigmoid(zz[:, d:])
        dz = rowk(vjp_rows(f_glu, 1, 1), [sv['z'], dh], [], [(2 * d, BF16)], name="s5_d_glu")[0]
        grads['ssm_w_out'] = [hmm(sv['yg'], dz, ta=True, nshard=4, out3=True, out_dtype=GRAD_XFER, name="s5_d_wout")]
        dyg = hmm(dz, sv['wout'], tb=True, name="s5_d_yg")
        f_act = lambda yy, uu, dd: f_gelu(yy + dd * uu)
        dy, du1, ddsk = rowk(vjp_rows(f_act, 2, 1), [sv['y'], sv['u'], dyg], [sv['dsk']],
                             [(d_ssm, F32), (d_ssm, F32)], [(1, d_ssm)], name="s5_d_gelu")
        add_small('ssm_d', ddsk)
        dyb = dy.astype(BF16)
        dcbd_re = hmm(sv['hb_re'], dyb, ta=True, name="s5_d_c_re")
        dcbd_im = hmm(sv['hb_im'], dyb, ta=True, alpha=-1.0, name="s5_d_c_im")
        add_small('ssm_c_re', block_diag_extract(dcbd_re, n_grp).transpose(0, 2, 1)[None])
        add_small('ssm_c_im', block_diag_extract(dcbd_im, n_grp).transpose(0, 2, 1)[None])
        dh_re = hmm(dyb, sv['cbd_re'], tb=True, name="s5_d_h_re").reshape(t, gp // 128, 128)
        dh_im = hmm(dyb, sv['cbd_im'], tb=True, alpha=-1.0, name="s5_d_h_im").reshape(t, gp // 128, 128)
        hp_re = jnp.pad(sv['h_re'][:-1], ((1, 0), (0, 0), (0, 0)))
        hp_im = jnp.pad(sv['h_im'][:-1], ((1, 0), (0, 0), (0, 0)))
        g_re, g_im, da_re, da_im = ssm_scan_bwd(dh_re, dh_im, hp_re, hp_im, sv['a_re'], sv['a_im'], name="s5_d_scan")
        gb_re, gb_im = g_re.reshape(t, gp).astype(BF16), g_im.reshape(t, gp).astype(BF16)
        dbbd_re = hmm(sv['ub'], gb_re, ta=True, name="s5_d_bb_re")
        dbbd_im = hmm(sv['ub'], gb_im, ta=True, name="s5_d_bb_im")
        du = hmm(gb_re, sv['bbd_re'], tb=True, res=du1, name="s5_d_u_re")
        du = hmm(gb_im, sv['bbd_im'], tb=True, res=du, name="s5_d_u_im")
        sp = sv['sp']
        dbb_re = block_diag_extract(dbbd_re, n_grp).transpose(0, 2, 1).reshape(gp, SSM_GROUP)
        dbb_im = block_diag_extract(dbbd_im, n_grp).transpose(0, 2, 1).reshape(gp, SSM_GROUP)
        dbr, dbi, dz_re, dz_im = rowk(vjp_rows(f_ssm_bb, 4, 2),
                                      [sp['br'], sp['bi'], sp['z_re'], sp['z_im'], dbb_re, dbb_im], [],
                                      [(SSM_GROUP, F32)] * 2 + [(1, F32)] * 2, name="ssm_d_bb", tb=gp)
        add_small('ssm_b_re', dbr.reshape(W['ssm_b_re'].shape))
        add_small('ssm_b_im', dbi.reshape(W['ssm_b_im'].shape))
        dlr, dli, dldt = rowk(vjp_rows(f_ssm_ab, 3, 4),
                              [sp['lr'], sp['li'], sp['ldt'], da_re.reshape(n_grp, SSM_STATE),
                               da_im.reshape(n_grp, SSM_STATE), dz_re.reshape(n_grp, SSM_STATE),
                               dz_im.reshape(n_grp, SSM_STATE)], [],
                              [(SSM_STATE, F32)] * 2 + [(1, F32)], name="ssm_d_ab", tb=n_grp)
        add_small('ssm_lambda_re', dlr[None])
        add_small('ssm_lambda_im', dli[None])
        add_small('ssm_log_dt', dldt.reshape(1, n_grp))
        grads['ssm_w_in'] = [hmm(sv['hn'], du, ta=True, out_dtype=GRAD_XFER, name="s5_d_win").reshape(4, -1, d_ssm)]
        dhn = hmm(du, sv['win'], tb=True, name="s5_d_hn")
        dh_in, dg = rms_bwd(sv['h'], dhn, dh, sv['g'], name="s5_d_rms")
        add_small('mix_norm', dg[0], 1, depth)
        return dh_in

    def ple_fwd(h, i):
        g = W['ple_norm'][i][None]
        wpg = gather('ple_w_gate', i).reshape(d, d)
        wpp = gather('ple_w_proj', i)
        n = rowk(lambda hh, gg: f_rms(hh, gg), [h], [g], [(d, BF16)], name="ple_rms")[0]
        zg = hmm(n, wpg, name="ple_gate")
        pb = p[i, 0].astype(BF16)
        pp = hmm(pb, wpp, name="ple_proj")
        f_ple = lambda zz, pq, hh: hh + f_sigmoid(zz) * pq
        h_out = rowk(f_ple, [zg, pp, h], [], [(d, F32)], name="ple_mix")[0]
        return h_out, dict(h=h, g=g, n=n, zg=zg, pp=pp, pb=pb, wpg=wpg, wpp=wpp)

    def ple_bwd(dh, sv, i):
        f_ple = lambda zz, pq: f_sigmoid(zz) * pq
        dzg, dpp = rowk(vjp_rows(f_ple, 2, 1), [sv['zg'], sv['pp'], dh], [], [(d, BF16), (d, BF16)],
                        name="ple_d_mix")
        grads.setdefault('ple_w_proj', [None] * depth)[i] = hmm(sv['pb'], dpp, ta=True, nshard=4, out3=True,
                                                               out_dtype=GRAD_XFER, name="ple_d_wproj")
        grads.setdefault('ple_w_gate', [None] * depth)[i] = hmm(sv['n'], dzg, ta=True, out_dtype=GRAD_XFER,
                                                               name="ple_d_wgate").reshape(4, -1, d)
        dn = hmm(dzg, sv['wpg'], tb=True, name="ple_d_n")
        dh_in, dg = rms_bwd(sv['h'], dn, dh, sv['g'], name="ple_d_rms")
        add_small('ple_norm', dg[0], i, depth)
        return dh_in

    ag_order = []
    for i in range(depth):
        ag_order += [('ffn1_w_gate', i), ('ffn1_w_up', i), ('ffn1_w_down', i)]
        ag_order += [('ab_w_in', 0), ('ab_w_out', 0)] if i % 2 == 0 else [('ssm_w_in', 0), ('ssm_w_out', 0)]
        ag_order += [('ffn2_w_gate', i), ('ffn2_w_up', i), ('ffn2_w_down', i), ('ple_w_gate', i), ('ple_w_proj', i)]
    first = Comm()
    for n in SMALL_SHARDED:
        first.all_gather(W[n], nchunk=1)
    first_out = run_comm(first, name="ag_small")
    W = dict(W)
    for n, full in zip(SMALL_SHARDED, first_out):
        w = W[n]
        W[n + '_full'] = jnp.moveaxis(full, 0, -2).reshape(w.shape[1:-1] + (4 * w.shape[-1],))
    W['ssm_d_full'] = W['ssm_d_full'][None]
    queue.extend(ag_entry(n, li_) for n, li_ in ag_order)
    alone(2)
    alone(0)

    h = h0
    saved = []
    for i in range(depth):
        sv = {}
        h, sv['ffn1'] = ffn_fwd(h, 'ffn1', i)
        if i % 2 == 0:
            h, sv['mix'] = mixer_ab_fwd(h)
        else:
            h, sv['mix'] = mixer_s5_fwd(h)
        h, sv['ffn2'] = ffn_fwd(h, 'ffn2', i)
        h, sv['ple'] = ple_fwd(h, i)
        saved.append(sv)

    def f_loss(y, tg):
        e = y - tg
        part = 0.5 * jnp.sum(jnp.sum(e * e, axis=-1, keepdims=True) * (1.0 / d), axis=0, keepdims=True)
        return e * (1.0 / d), jnp.broadcast_to(part, (1, 128))
    dh, loss_part = rowk(f_loss, [h, tgt], [], [(d, F32)], [(1, 128)], name="loss")
    loss = lax.psum(loss_part[0, 0], ("x", "y", "c"))

    for i in reversed(range(depth)):
        sv = saved[i]
        dh = ple_bwd(dh, sv['ple'], i)
        dh = ffn_bwd(dh, sv['ffn2'], 'ffn2', i)
        if i % 2 == 0:
            dh = mixer_ab_bwd(dh, sv['mix'])
        else:
            dh = mixer_s5_bwd(dh, sv['mix'])
        dh = ffn_bwd(dh, sv['ffn1'], 'ffn1', i)
    grad_x = dh[None]

    small_names = [n for n in W_NAMES if n not in BIG]
    small_full = []
    for n in small_names:
        v_ = small[n]
        if isinstance(v_, list):
            v_ = jnp.stack(v_)
        full_shape = W[n].shape[:-1] + (4 * W[n].shape[-1],) if n in SMALL_SHARDED else W[n].shape
        small_full.append(v_.reshape(full_shape))
    packed = pack_flat(small_full)
    ar_got = []
    queue.append((True, lambda cm: cm.gather_all(packed), lambda outs, hd: ar_got.append(outs[hd])))

    out = {}
    todo = list(BIG)
    while todo:
        enqueue_ready()
        ready = [n for n in todo if all((n, li_) in halves for li_ in range(W[n].shape[0]))]
        if not ready:
            assert queue, todo
            flush(until=lambda: True, at_least_one=True)
            continue
        n = ready[0]
        todo.remove(n)
        nl = W[n].shape[0]
        rows, cols = int(np.prod(W[n].shape[1:-1])), W[n].shape[-1]
        a_args = (W[n].reshape(nl, rows, cols), [halves[(n, li_)][0] for li_ in range(nl)],
                  [halves[(n, li_)][1] for li_ in range(nl)], M[n].reshape(nl, rows, cols),
                  V[n].reshape(nl, rows, cols))
        if queue:
            cm, conts = take()
            res, couts = adamw_big(*a_args, comm=cm, name=f"adamw_{n}")
            for cont, hd in reversed(conts):
                cont(couts, hd)
        else:
            res = adamw_big(*a_args, name=f"adamw_{n}")
        for kind, a in zip(('grad', 'delta', 'm', 'v'), res):
            out[(kind, n)] = a.reshape(W[n].shape)
    flush()

    summed = sum_slots(ar_got[0], name="ar_small_sum")
    small_tot = unpack_flat(summed, [a.shape for a in small_full])
    g_small = {}
    for n, a in zip(small_names, small_tot):
        if n in SMALL_SHARDED:
            ns = W[n].shape[-1]
            a = lax.dynamic_slice_in_dim(a, qchip * ns, ns, axis=a.ndim - 1)
        g_small[n] = a

    pk = lambda dct: pack_flat([dct[n] for n in small_names])
    res = adamw_flat(pk(W), pk(g_small), pk(M), pk(V), name="adamw_small")
    shapes = [W[n].shape for n in small_names]
    for kind, buf in zip(('delta', 'm', 'v'), res):
        for n, a in zip(small_names, unpack_flat(buf, shapes)):
            out[(kind, n)] = a
    for n in small_names:
        out[('grad', n)] = g_small[n]

    return (loss, grad_x, *[out[('grad', n)] for n in W_NAMES], *[out[('delta', n)] for n in W_NAMES],
            *[out[('m', n)] for n in W_NAMES], *[out[('v', n)] for n in W_NAMES])
```
